```python
import jax, jax.numpy as jnp
from jax import lax
import numpy as np

D_MODEL = 2048
BATCH = 8
SEQ = 8192
DEPTH = 4

GRID_W = 64
CTX_LEN = 256
N_MIXERS = 2
HEAD_DIM = 128
N_HEADS = D_MODEL // HEAD_DIM
N_KV_HEADS = N_HEADS // 4
GROUP = N_HEADS // N_KV_HEADS
D_Q = N_HEADS * HEAD_DIM
D_KV = N_KV_HEADS * HEAD_DIM
WINDOW = 128
BLOCK = 128
ROPE_PAIRS = HEAD_DIM // 4
ROPE_BASE = 10000.0
CONV_W = 3
D_FF = ((8 * D_MODEL // 3 + 255) // 256) * 256
N_ATTN_LAYERS = (DEPTH + N_MIXERS - 1) // N_MIXERS
N_CONV_LAYERS = DEPTH // N_MIXERS
EPS = 1e-6
NEG_INF = -1e30

kernel_name = "hybrid_swa_shortconv_dit_prefix"


def rms_norm(x):
    xf = x.astype(jnp.float32)
    return (xf * lax.rsqrt(jnp.mean(xf * xf, axis=-1, keepdims=True) + EPS)).astype(x.dtype)


def modulate(x, shift, scale):
    return rms_norm(x) * (1 + scale) + shift


def adaln(cond_act, w, b):
    return jnp.split(cond_act @ w + b, 6, axis=-1)


def dwconv3(x, w):
    xp = jnp.pad(x, ((0, 0), (1, 1), (0, 0)))
    return xp[:, :-2] * w[0] + xp[:, 1:-1] * w[1] + xp[:, 2:] * w[2]


def rope_tables(n):
    rows = n // GRID_W
    row = jnp.repeat(jnp.arange(rows), GRID_W).astype(jnp.float32)
    col = jnp.tile(jnp.arange(GRID_W), rows).astype(jnp.float32)
    inv = ROPE_BASE ** (-jnp.arange(ROPE_PAIRS, dtype=jnp.float32) / ROPE_PAIRS)
    ang = jnp.stack([row[:, None] * inv, col[:, None] * inv], axis=1)
    ang = jnp.broadcast_to(ang[:, :, None, :], (n, 2, 2, ROPE_PAIRS)).reshape(n, HEAD_DIM)
    return jnp.cos(ang), jnp.sin(ang)


def apply_rope(x, cos, sin):
    xf = x.astype(jnp.float32)
    xr = xf.reshape(*x.shape[:-1], 2, 2, ROPE_PAIRS)
    rot = jnp.stack([-xr[..., 1, :], xr[..., 0, :]], axis=-2).reshape(x.shape)
    return (xf * cos[:, None, :] + rot * sin[:, None, :]).astype(x.dtype)


def band_mask(n):
    nb = n // BLOCK
    qi = jnp.arange(BLOCK)[None, :, None]
    kk = jnp.arange(3 * BLOCK)[None, None, :]
    blk = jnp.arange(nb)[:, None, None]
    kpos = blk * BLOCK - BLOCK + kk
    rel = qi - kk + BLOCK
    return (jnp.abs(rel) <= WINDOW) & (kpos >= 0) & (kpos < n)


def sink_softmax(s, sink):
    m = jnp.maximum(jnp.max(s, axis=-1, keepdims=True), sink)
    e = jnp.exp(s - m)
    return e / (jnp.sum(e, axis=-1, keepdims=True) + jnp.exp(sink - m))


def project_q(h, w_q, gain):
    q = (h @ w_q).reshape(*h.shape[:-1], N_HEADS, HEAD_DIM)
    return rms_norm(q) * gain


def project_kv(h, w_kv, gain):
    k, v = jnp.split(h @ w_kv, 2, axis=-1)
    k = rms_norm(k.reshape(*h.shape[:-1], N_KV_HEADS, HEAD_DIM)) * gain
    v = v.reshape(*h.shape[:-1], N_KV_HEADS, HEAD_DIM)
    return k, v


def attention_mixer(h_x, h_c, w_qkv, w_o, q_gain, k_gain, sink, cos, sin, mask, need_ctx):
    b, n, _ = h_x.shape
    nb = n // BLOCK
    scale = HEAD_DIM ** -0.5
    w_q, w_kv = w_qkv[:, :D_Q], w_qkv[:, D_Q:]
    q = apply_rope(project_q(h_x, w_q, q_gain), cos, sin)
    k, v = project_kv(h_x, w_kv, k_gain)
    k = apply_rope(k, cos, sin)
    kc, vc = project_kv(h_c, w_kv, k_gain)
    sink_f = sink.astype(jnp.float32).reshape(N_KV_HEADS, GROUP)

    qb = q.reshape(b, nb, BLOCK, N_KV_HEADS, GROUP, HEAD_DIM)
    pad = ((0, 0), (BLOCK, BLOCK), (0, 0), (0, 0))
    kp = jnp.pad(k, pad).reshape(b, nb + 2, BLOCK, N_KV_HEADS, HEAD_DIM)
    vp = jnp.pad(v, pad).reshape(b, nb + 2, BLOCK, N_KV_HEADS, HEAD_DIM)
    kb = jnp.concatenate([kp[:, :-2], kp[:, 1:-1], kp[:, 2:]], axis=2)
    vb = jnp.concatenate([vp[:, :-2], vp[:, 1:-1], vp[:, 2:]], axis=2)
    s_loc = jnp.einsum('bnqhgd,bnkhd->bhgnqk', qb, kb).astype(jnp.float32) * scale
    s_loc = jnp.where(mask, s_loc, NEG_INF)
    s_ctx = jnp.einsum('bnqhgd,bkhd->bhgnqk', qb, kc).astype(jnp.float32) * scale
    p = sink_softmax(jnp.concatenate([s_loc, s_ctx], axis=-1),
                     sink_f[None, :, :, None, None, None]).astype(v.dtype)
    o = (jnp.einsum('bhgnqk,bnkhd->bnqhgd', p[..., :3 * BLOCK], vb)
         + jnp.einsum('bhgnqk,bkhd->bnqhgd', p[..., 3 * BLOCK:], vc))
    out_x = o.reshape(b, n, D_Q) @ w_o

    out_c = None
    if need_ctx:
        l = h_c.shape[1]
        qc = project_q(h_c, w_q, q_gain).reshape(b, l, N_KV_HEADS, GROUP, HEAD_DIM)
        s_c = jnp.einsum('bqhgd,bkhd->bhgqk', qc, kc).astype(jnp.float32) * scale
        p_c = sink_softmax(s_c, sink_f[None, :, :, None, None]).astype(vc.dtype)
        oc = jnp.einsum('bhgqk,bkhd->bqhgd', p_c, vc)
        out_c = oc.reshape(b, l, D_Q) @ w_o
    return out_x, out_c


def short_conv_mixer(h, w_in, conv_w, w_out):
    gate_b, gate_c, val = jnp.split(h @ w_in, 3, axis=-1)
    return (gate_b * dwconv3(gate_c * val, conv_w)) @ w_out


def conv_ffn(h, w_up, conv_w, conv_b, w_down):
    gate, val = jnp.split(h @ w_up, 2, axis=-1)
    gate = dwconv3(gate, conv_w) + conv_b
    return (jax.nn.silu(gate) * val) @ w_down


def _fwd_setup_inputs(seed: int = 0) -> dict:
    key = jax.random.key(seed)
    ks = jax.random.split(key, 20)

    def nrm(k, shape, scale):
        return jax.random.normal(k, shape, jnp.float32) * scale

    d = D_MODEL
    return {
        "x": nrm(ks[0], (BATCH, SEQ, d), 1.0),
        "c": nrm(ks[1], (BATCH, d), 1.0),
        "ctx": nrm(ks[2], (BATCH, CTX_LEN, d), 1.0),
        "c_ctx": nrm(ks[3], (d,), 1.0),
        "w_ada": nrm(ks[4], (DEPTH, d, 6 * d), 0.5 * d ** -0.5),
        "b_ada": nrm(ks[5], (DEPTH, 6 * d), 0.02),
        "attn_w_qkv": nrm(ks[6], (N_ATTN_LAYERS, d, D_Q + 2 * D_KV), d ** -0.5),
        "attn_w_o": nrm(ks[7], (N_ATTN_LAYERS, D_Q, d), D_Q ** -0.5),
        "attn_q_gain": 1.0 + nrm(ks[8], (N_ATTN_LAYERS, HEAD_DIM), 0.1),
        "attn_k_gain": 1.0 + nrm(ks[9], (N_ATTN_LAYERS, HEAD_DIM), 0.1),
        "attn_sink": nrm(ks[10], (N_ATTN_LAYERS, N_HEADS), 0.5),
        "sc_w_in": nrm(ks[11], (N_CONV_LAYERS, d, 3 * d), d ** -0.5),
        "sc_conv": nrm(ks[12], (N_CONV_LAYERS, CONV_W, d), CONV_W ** -0.5),
        "sc_w_out": nrm(ks[13], (N_CONV_LAYERS, d, d), d ** -0.5),
        "ffn_w_up": nrm(ks[14], (DEPTH, d, 2 * D_FF), d ** -0.5),
        "ffn_conv": nrm(ks[15], (DEPTH, CONV_W, D_FF), CONV_W ** -0.5),
        "ffn_conv_b": nrm(ks[16], (DEPTH, D_FF), 0.02),
        "ffn_w_down": nrm(ks[17], (DEPTH, D_FF, d), D_FF ** -0.5),
    }


def _fwd_reference(x, c, ctx, c_ctx, w_ada, b_ada, attn_w_qkv, attn_w_o, attn_q_gain, attn_k_gain,
              attn_sink, sc_w_in, sc_conv, sc_w_out, ffn_w_up, ffn_conv, ffn_conv_b, ffn_w_down):
    n = x.shape[1]
    cos, sin = rope_tables(n)
    mask = band_mask(n)
    silu_c = jax.nn.silu(c)
    silu_cc = jax.nn.silu(c_ctx)

    for l in range(DEPTH):
        is_attn = (l % N_MIXERS) == 0
        j = l // N_MIXERS
        need_ctx = l < DEPTH - 1
        sh_m, sc_m, g_m, sh_f, sc_f, g_f = [t[:, None, :] for t in adaln(silu_c, w_ada[l], b_ada[l])]
        h_x = modulate(x, sh_m, sc_m)
        y_c = None
        if need_ctx or is_attn:
            csh_m, csc_m, cg_m, csh_f, csc_f, cg_f = adaln(silu_cc, w_ada[l], b_ada[l])
            h_c = modulate(ctx, csh_m, csc_m)
        if is_attn:
            y_x, y_c = attention_mixer(h_x, h_c, attn_w_qkv[j], attn_w_o[j], attn_q_gain[j],
                                       attn_k_gain[j], attn_sink[j], cos, sin, mask, need_ctx)
        else:
            y_x = short_conv_mixer(h_x, sc_w_in[j], sc_conv[j], sc_w_out[j])
            if need_ctx:
                y_c = short_conv_mixer(h_c, sc_w_in[j], sc_conv[j], sc_w_out[j])
        x = x + g_m * y_x
        x = x + g_f * conv_ffn(modulate(x, sh_f, sc_f), ffn_w_up[l], ffn_conv[l], ffn_conv_b[l], ffn_w_down[l])
        if need_ctx:
            ctx = ctx + cg_m * y_c
            ctx = ctx + cg_f * conv_ffn(modulate(ctx, csh_f, csc_f), ffn_w_up[l], ffn_conv[l],
                                        ffn_conv_b[l], ffn_w_down[l])
    return x


import jax as _jax
import jax.numpy as _jnp

TWIN_FORMAT = 'train_step'
FWD_PARAMS = ['x', 'c', 'ctx', 'c_ctx', 'w_ada', 'b_ada', 'attn_w_qkv', 'attn_w_o', 'attn_q_gain', 'attn_k_gain', 'attn_sink', 'sc_w_in', 'sc_conv', 'sc_w_out', 'ffn_w_up', 'ffn_conv', 'ffn_conv_b', 'ffn_w_down']
TWIN_WEIGHTS = ['c_ctx', 'w_ada', 'b_ada', 'attn_w_qkv', 'attn_w_o', 'attn_q_gain', 'attn_k_gain', 'attn_sink', 'sc_w_in', 'sc_conv', 'sc_w_out', 'ffn_w_up', 'ffn_conv', 'ffn_conv_b', 'ffn_w_down']
TWIN_DIFF_INPUT = 'x'
TWIN_INPUTS = ['x', 'c', 'ctx', 'c_ctx', 'w_ada', 'b_ada', 'attn_w_qkv', 'attn_w_o', 'attn_q_gain', 'attn_k_gain', 'attn_sink', 'sc_w_in', 'sc_conv', 'sc_w_out', 'ffn_w_up', 'ffn_conv', 'ffn_conv_b', 'ffn_w_down', 'loss_target', 'm_c_ctx', 'm_w_ada', 'm_b_ada', 'm_attn_w_qkv', 'm_attn_w_o', 'm_attn_q_gain', 'm_attn_k_gain', 'm_attn_sink', 'm_sc_w_in', 'm_sc_conv', 'm_sc_w_out', 'm_ffn_w_up', 'm_ffn_conv', 'm_ffn_conv_b', 'm_ffn_w_down', 'v_c_ctx', 'v_w_ada', 'v_b_ada', 'v_attn_w_qkv', 'v_attn_w_o', 'v_attn_q_gain', 'v_attn_k_gain', 'v_attn_sink', 'v_sc_w_in', 'v_sc_conv', 'v_sc_w_out', 'v_ffn_w_up', 'v_ffn_conv', 'v_ffn_conv_b', 'v_ffn_w_down']
TWIN_OUTPUTS = ['loss', 'grad_x', 'grad_c_ctx', 'grad_w_ada', 'grad_b_ada', 'grad_attn_w_qkv', 'grad_attn_w_o', 'grad_attn_q_gain', 'grad_attn_k_gain', 'grad_attn_sink', 'grad_sc_w_in', 'grad_sc_conv', 'grad_sc_w_out', 'grad_ffn_w_up', 'grad_ffn_conv', 'grad_ffn_conv_b', 'grad_ffn_w_down', 'delta_c_ctx', 'delta_w_ada', 'delta_b_ada', 'delta_attn_w_qkv', 'delta_attn_w_o', 'delta_attn_q_gain', 'delta_attn_k_gain', 'delta_attn_sink', 'delta_sc_w_in', 'delta_sc_conv', 'delta_sc_w_out', 'delta_ffn_w_up', 'delta_ffn_conv', 'delta_ffn_conv_b', 'delta_ffn_w_down', 'new_m_c_ctx', 'new_m_w_ada', 'new_m_b_ada', 'new_m_attn_w_qkv', 'new_m_attn_w_o', 'new_m_attn_q_gain', 'new_m_attn_k_gain', 'new_m_attn_sink', 'new_m_sc_w_in', 'new_m_sc_conv', 'new_m_sc_w_out', 'new_m_ffn_w_up', 'new_m_ffn_conv', 'new_m_ffn_conv_b', 'new_m_ffn_w_down', 'new_v_c_ctx', 'new_v_w_ada', 'new_v_b_ada', 'new_v_attn_w_qkv', 'new_v_attn_w_o', 'new_v_attn_q_gain', 'new_v_attn_k_gain', 'new_v_attn_sink', 'new_v_sc_w_in', 'new_v_sc_conv', 'new_v_sc_w_out', 'new_v_ffn_w_up', 'new_v_ffn_conv', 'new_v_ffn_conv_b', 'new_v_ffn_w_down']
TWIN_LEAF_KINDS = {'loss': 'loss', 'grad_x': 'grad_x', 'grad_c_ctx': 'grad_w', 'grad_w_ada': 'grad_w', 'grad_b_ada': 'grad_w', 'grad_attn_w_qkv': 'grad_w', 'grad_attn_w_o': 'grad_w', 'grad_attn_q_gain': 'grad_w', 'grad_attn_k_gain': 'grad_w', 'grad_attn_sink': 'grad_w', 'grad_sc_w_in': 'grad_w', 'grad_sc_conv': 'grad_w', 'grad_sc_w_out': 'grad_w', 'grad_ffn_w_up': 'grad_w', 'grad_ffn_conv': 'grad_w', 'grad_ffn_conv_b': 'grad_w', 'grad_ffn_w_down': 'grad_w', 'delta_c_ctx': 'delta_w', 'delta_w_ada': 'delta_w', 'delta_b_ada': 'delta_w', 'delta_attn_w_qkv': 'delta_w', 'delta_attn_w_o': 'delta_w', 'delta_attn_q_gain': 'delta_w', 'delta_attn_k_gain': 'delta_w', 'delta_attn_sink': 'delta_w', 'delta_sc_w_in': 'delta_w', 'delta_sc_conv': 'delta_w', 'delta_sc_w_out': 'delta_w', 'delta_ffn_w_up': 'delta_w', 'delta_ffn_conv': 'delta_w', 'delta_ffn_conv_b': 'delta_w', 'delta_ffn_w_down': 'delta_w', 'new_m_c_ctx': 'new_m', 'new_m_w_ada': 'new_m', 'new_m_b_ada': 'new_m', 'new_m_attn_w_qkv': 'new_m', 'new_m_attn_w_o': 'new_m', 'new_m_attn_q_gain': 'new_m', 'new_m_attn_k_gain': 'new_m', 'new_m_attn_sink': 'new_m', 'new_m_sc_w_in': 'new_m', 'new_m_sc_conv': 'new_m', 'new_m_sc_w_out': 'new_m', 'new_m_ffn_w_up': 'new_m', 'new_m_ffn_conv': 'new_m', 'new_m_ffn_conv_b': 'new_m', 'new_m_ffn_w_down': 'new_m', 'new_v_c_ctx': 'new_v', 'new_v_w_ada': 'new_v', 'new_v_b_ada': 'new_v', 'new_v_attn_w_qkv': 'new_v', 'new_v_attn_w_o': 'new_v', 'new_v_attn_q_gain': 'new_v', 'new_v_attn_k_gain': 'new_v', 'new_v_attn_sink': 'new_v', 'new_v_sc_w_in': 'new_v', 'new_v_sc_conv': 'new_v', 'new_v_sc_w_out': 'new_v', 'new_v_ffn_w_up': 'new_v', 'new_v_ffn_conv': 'new_v', 'new_v_ffn_conv_b': 'new_v', 'new_v_ffn_w_down': 'new_v'}


def _forward(args):
    return _fwd_reference(*[args[k] for k in FWD_PARAMS])


def _output_shape():
    def fwd():
        inp = _fwd_setup_inputs(0)
        return _fwd_reference(*[inp[k] for k in FWD_PARAMS])
    out = _jax.eval_shape(fwd)
    return out.shape, out.dtype

N_MICROBATCH = 1
ADAM_LR = 0.001
ADAM_B1 = 0.9
ADAM_B2 = 0.999
ADAM_EPS = 1e-08
ADAM_WD = 0.01
ADAM_STEP = 10
PER_EXAMPLE_BATCH_AXIS = {'x': 0, 'c': 0, 'ctx': 0, 'loss_target': 0}
SHARED_INPUTS = []
_WEIGHT_DTYPES = {'c_ctx': _jnp.float32, 'w_ada': _jnp.float32, 'b_ada': _jnp.float32, 'attn_w_qkv': _jnp.float32, 'attn_w_o': _jnp.float32, 'attn_q_gain': _jnp.float32, 'attn_k_gain': _jnp.float32, 'attn_sink': _jnp.float32, 'sc_w_in': _jnp.float32, 'sc_conv': _jnp.float32, 'sc_w_out': _jnp.float32, 'ffn_w_up': _jnp.float32, 'ffn_conv': _jnp.float32, 'ffn_conv_b': _jnp.float32, 'ffn_w_down': _jnp.float32}
MOMENT_SCALE = {'c_ctx': 3.622327e-01, 'w_ada': 1.532433e+00, 'b_ada': 4.418860e+00, 'attn_w_qkv': 2.049276e-01, 'attn_w_o': 1.883486e-01, 'attn_q_gain': 2.682263e-01, 'attn_k_gain': 2.653616e-01, 'attn_sink': 3.058587e-02, 'sc_w_in': 1.813598e-01, 'sc_conv': 2.687894e+00, 'sc_w_out': 1.305672e-01, 'ffn_w_up': 6.197897e-02, 'ffn_conv': 3.539836e-01, 'ffn_conv_b': 4.018614e-01, 'ffn_w_down': 7.579130e-02}


def _to_microbatches(a, axis):
    t = _jnp.moveaxis(a, axis, 0)
    t = t.reshape((N_MICROBATCH, t.shape[0] // N_MICROBATCH) + t.shape[1:])
    return _jnp.moveaxis(t, 1, axis + 1)


def setup_inputs(seed: int = 0) -> dict:
    inp = _fwd_setup_inputs(seed)
    key = _jax.random.fold_in(_jax.random.key(seed), 7919)
    shape, _ = _output_shape()
    out = dict(inp)
    out["loss_target"] = _jax.random.normal(_jax.random.fold_in(key, 0), shape, _jnp.float32)
    for i, name in enumerate(TWIN_WEIGHTS):
        w = inp[name].astype(_jnp.float32)
        if MOMENT_SCALE is None:
            s = _jnp.sqrt(_jnp.mean(_jnp.square(w)) + 1e-30)
        else:
            s = MOMENT_SCALE[name]
        km, kv = _jax.random.split(_jax.random.fold_in(key, i + 1))
        out[name] = w
        out["m_" + name] = s * _jax.random.normal(km, w.shape, _jnp.float32)
        out["v_" + name] = (s * s) * _jax.random.uniform(kv, w.shape, _jnp.float32, 0.5, 1.5)
    if N_MICROBATCH > 1:
        for name, axis in PER_EXAMPLE_BATCH_AXIS.items():
            out[name] = _to_microbatches(out[name], axis)
    return {'x': out['x'], 'c': out['c'], 'ctx': out['ctx'], 'c_ctx': out['c_ctx'], 'w_ada': out['w_ada'], 'b_ada': out['b_ada'], 'attn_w_qkv': out['attn_w_qkv'], 'attn_w_o': out['attn_w_o'], 'attn_q_gain': out['attn_q_gain'], 'attn_k_gain': out['attn_k_gain'], 'attn_sink': out['attn_sink'], 'sc_w_in': out['sc_w_in'], 'sc_conv': out['sc_conv'], 'sc_w_out': out['sc_w_out'], 'ffn_w_up': out['ffn_w_up'], 'ffn_conv': out['ffn_conv'], 'ffn_conv_b': out['ffn_conv_b'], 'ffn_w_down': out['ffn_w_down'], 'loss_target': out['loss_target'], 'm_c_ctx': out['m_c_ctx'], 'm_w_ada': out['m_w_ada'], 'm_b_ada': out['m_b_ada'], 'm_attn_w_qkv': out['m_attn_w_qkv'], 'm_attn_w_o': out['m_attn_w_o'], 'm_attn_q_gain': out['m_attn_q_gain'], 'm_attn_k_gain': out['m_attn_k_gain'], 'm_attn_sink': out['m_attn_sink'], 'm_sc_w_in': out['m_sc_w_in'], 'm_sc_conv': out['m_sc_conv'], 'm_sc_w_out': out['m_sc_w_out'], 'm_ffn_w_up': out['m_ffn_w_up'], 'm_ffn_conv': out['m_ffn_conv'], 'm_ffn_conv_b': out['m_ffn_conv_b'], 'm_ffn_w_down': out['m_ffn_w_down'], 'v_c_ctx': out['v_c_ctx'], 'v_w_ada': out['v_w_ada'], 'v_b_ada': out['v_b_ada'], 'v_attn_w_qkv': out['v_attn_w_qkv'], 'v_attn_w_o': out['v_attn_w_o'], 'v_attn_q_gain': out['v_attn_q_gain'], 'v_attn_k_gain': out['v_attn_k_gain'], 'v_attn_sink': out['v_attn_sink'], 'v_sc_w_in': out['v_sc_w_in'], 'v_sc_conv': out['v_sc_conv'], 'v_sc_w_out': out['v_sc_w_out'], 'v_ffn_w_up': out['v_ffn_w_up'], 'v_ffn_conv': out['v_ffn_conv'], 'v_ffn_conv_b': out['v_ffn_conv_b'], 'v_ffn_w_down': out['v_ffn_w_down']}


def _loss(weights, diff, rest, loss_target):
    with _jax.named_scope("forward"):
        args = {**rest, TWIN_DIFF_INPUT: diff, **{k: w.astype(_WEIGHT_DTYPES[k]) for k, w in weights.items()}}
        y = _forward(args)
    with _jax.named_scope("loss_head"):
        err = _jnp.square(y.astype(_jnp.float32) - loss_target)
        return 0.5 * _jnp.sum(_jnp.mean(err, axis=-1)) if err.ndim else 0.5 * err


def _adamw(w, g, m, v):
    m = ADAM_B1 * m + (1.0 - ADAM_B1) * g
    v = ADAM_B2 * v + (1.0 - ADAM_B2) * _jnp.square(g)
    m_hat = m / (1.0 - ADAM_B1 ** ADAM_STEP)
    v_hat = v / (1.0 - ADAM_B2 ** ADAM_STEP)
    delta = -ADAM_LR * (m_hat / (_jnp.sqrt(v_hat) + ADAM_EPS) + ADAM_WD * w)
    return delta, m, v


def reference(x, c, ctx, c_ctx, w_ada, b_ada, attn_w_qkv, attn_w_o, attn_q_gain, attn_k_gain, attn_sink, sc_w_in, sc_conv, sc_w_out, ffn_w_up, ffn_conv, ffn_conv_b, ffn_w_down, loss_target, m_c_ctx, m_w_ada, m_b_ada, m_attn_w_qkv, m_attn_w_o, m_attn_q_gain, m_attn_k_gain, m_attn_sink, m_sc_w_in, m_sc_conv, m_sc_w_out, m_ffn_w_up, m_ffn_conv, m_ffn_conv_b, m_ffn_w_down, v_c_ctx, v_w_ada, v_b_ada, v_attn_w_qkv, v_attn_w_o, v_attn_q_gain, v_attn_k_gain, v_attn_sink, v_sc_w_in, v_sc_conv, v_sc_w_out, v_ffn_w_up, v_ffn_conv, v_ffn_conv_b, v_ffn_w_down):
    given = dict(x=x, c=c, ctx=ctx, c_ctx=c_ctx, w_ada=w_ada, b_ada=b_ada, attn_w_qkv=attn_w_qkv, attn_w_o=attn_w_o, attn_q_gain=attn_q_gain, attn_k_gain=attn_k_gain, attn_sink=attn_sink, sc_w_in=sc_w_in, sc_conv=sc_conv, sc_w_out=sc_w_out, ffn_w_up=ffn_w_up, ffn_conv=ffn_conv, ffn_conv_b=ffn_conv_b, ffn_w_down=ffn_w_down, loss_target=loss_target, m_c_ctx=m_c_ctx, m_w_ada=m_w_ada, m_b_ada=m_b_ada, m_attn_w_qkv=m_attn_w_qkv, m_attn_w_o=m_attn_w_o, m_attn_q_gain=m_attn_q_gain, m_attn_k_gain=m_attn_k_gain, m_attn_sink=m_attn_sink, m_sc_w_in=m_sc_w_in, m_sc_conv=m_sc_conv, m_sc_w_out=m_sc_w_out, m_ffn_w_up=m_ffn_w_up, m_ffn_conv=m_ffn_conv, m_ffn_conv_b=m_ffn_conv_b, m_ffn_w_down=m_ffn_w_down, v_c_ctx=v_c_ctx, v_w_ada=v_w_ada, v_b_ada=v_b_ada, v_attn_w_qkv=v_attn_w_qkv, v_attn_w_o=v_attn_w_o, v_attn_q_gain=v_attn_q_gain, v_attn_k_gain=v_attn_k_gain, v_attn_sink=v_attn_sink, v_sc_w_in=v_sc_w_in, v_sc_conv=v_sc_conv, v_sc_w_out=v_sc_w_out, v_ffn_w_up=v_ffn_w_up, v_ffn_conv=v_ffn_conv, v_ffn_conv_b=v_ffn_conv_b, v_ffn_w_down=v_ffn_w_down)
    weights = {n: given[n] for n in TWIN_WEIGHTS}
    shared = {n: given[n] for n in SHARED_INPUTS}
    per_example = {n: given[n] for n in ['x', 'c', 'ctx']}
    grad_fn = _jax.value_and_grad(_loss, argnums=(0, 1))

    def one_microbatch(ex, loss_target):
        ex = dict(ex)
        diff = ex.pop(TWIN_DIFF_INPUT)
        return grad_fn(weights, diff, {**shared, **ex}, loss_target)

    if N_MICROBATCH == 1:
        loss, (grad_w, grad_x) = one_microbatch(per_example, given["loss_target"])
    else:
        def body(carry, xs):
            loss_sum, grad_sum = carry
            l_k, (gw_k, gx_k) = one_microbatch(xs[0], xs[1])
            with _jax.named_scope("update"):
                return (loss_sum + l_k, _jax.tree.map(_jnp.add, grad_sum, gw_k)), gx_k

        init = (_jnp.zeros((), _jnp.float32), _jax.tree.map(_jnp.zeros_like, weights))
        (loss, grad_w), grad_x = _jax.lax.scan(body, init, (per_example, given["loss_target"]))
    with _jax.named_scope("update"):
        delta_w, new_m, new_v = {}, {}, {}
        for n in TWIN_WEIGHTS:
            delta_w[n], new_m[n], new_v[n] = _adamw(weights[n], grad_w[n], given["m_" + n], given["v_" + n])
    return (loss, grad_x, *[grad_w[n] for n in TWIN_WEIGHTS], *[delta_w[n] for n in TWIN_WEIGHTS],
            *[new_m[n] for n in TWIN_WEIGHTS], *[new_v[n] for n in TWIN_WEIGHTS])
```

```python
import functools

import jax
import jax.numpy as jnp
from jax import lax
from jax.experimental import pallas as pl
from jax.experimental.pallas import tpu as pltpu

F32 = jnp.float32
BF16 = jnp.bfloat16
I32 = jnp.int32

N_DEV = 8
HEAD_DIM = 128
GROUP = 4
WINDOW = 128
BLK = 128
GRID_W = 64
ROPE_BASE = 10000.0
EPS = 1e-6
NEG = -1e30
HALO = 16

ADAM_LR = 0.001
ADAM_B1 = 0.9
ADAM_B2 = 0.999
ADAM_EPS = 1e-08
ADAM_WD = 0.01
ADAM_STEP = 10

V7X_VMEM_BYTES = 64 << 20
VMEM_MATMUL = 52 << 20
VMEM_ELEMENTWISE = 44 << 20

MESH = pl.DeviceIdType.MESH
ANY = pl.BlockSpec(memory_space=pl.ANY)

NT_DIMS = (((1,), (1,)), ((), ()))
TN_DIMS = (((0,), (0,)), ((), ()))


def _pick(n, cands):
    for t in cands:
        if n % t == 0:
            return t
    raise ValueError(f"no tile for {n} in {cands}")


def _cp(n_axes, vmem=VMEM_ELEMENTWISE):
    return pltpu.CompilerParams(dimension_semantics=("arbitrary",) * n_axes, vmem_limit_bytes=vmem)


def _rows(i, tm, off=0):
    return i * tm + off + lax.broadcasted_iota(I32, (tm, 1), 0)


def _my_pos():
    return lax.axis_index("x"), lax.axis_index("y"), lax.axis_index("c")


def _gather_small(x_shard, name):
    m_per, n = x_shard.shape

    def body(x_ref, out_ref, send_sems, recv_sems, local_sem):
        x, y, c = _my_pos()
        me, sibling = (x, y, c), (x, y, 1 - c)
        chips = [(1 - x, y), (x, 1 - y), (1 - x, 1 - y)]

        def rows(px, py, pc):
            return out_ref.at[pl.ds((4 * px + 2 * py + pc) * m_per, m_per), :]

        def copy(k, block, to, src=None):
            return pltpu.make_async_remote_copy(
                src_ref=rows(*block) if src is None else src, dst_ref=rows(*block),
                send_sem=send_sems.at[k], recv_sem=recv_sems.at[k], device_id=to, device_id_type=MESH)

        mine = pltpu.make_async_copy(x_ref, rows(*me), local_sem)
        mine.start()
        first = [copy(0, me, sibling, src=x_ref)]
        first += [copy(1 + j, me, (*chip, c), src=x_ref) for j, chip in enumerate(chips)]
        for cp in first:
            cp.start()
        passed = [copy(4 + j, (*chip, c), sibling) for j, chip in enumerate(chips)]
        for j, chip in enumerate(chips):
            copy(1 + j, (*chip, c), me).wait_recv()
            passed[j].start()
        copy(0, sibling, me).wait_recv()
        for j, chip in enumerate(chips):
            copy(4 + j, (*chip, 1 - c), me).wait_recv()
        for cp in first + passed:
            cp.wait_send()
        mine.wait()

    return pl.pallas_call(
        body, name=name,
        out_shape=jax.ShapeDtypeStruct((N_DEV * m_per, n), x_shard.dtype),
        in_specs=[pl.BlockSpec(memory_space=pltpu.VMEM)],
        out_specs=pl.BlockSpec(memory_space=pltpu.VMEM),
        scratch_shapes=[pltpu.SemaphoreType.DMA((7,)), pltpu.SemaphoreType.DMA((7,)), pltpu.SemaphoreType.DMA],
        compiler_params=pltpu.CompilerParams(vmem_limit_bytes=VMEM_ELEMENTWISE),
    )(x_shard)


def _gather_weights(shards, name):
    n = len(shards)

    def body(*refs):
        x_refs, o_refs = refs[:n], refs[n:2 * n]
        send_sems, recv_sems, local_sems = refs[2 * n:]
        x, y, c = _my_pos()
        me, sibling = (x, y, c), (x, y, 1 - c)
        chips = [(1 - x, y), (x, 1 - y), (1 - x, 1 - y)]

        def slot(a, px, py, pc):
            return o_refs[a].at[4 * px + 2 * py + pc]

        def copy(a, k, block, to, src=None):
            return pltpu.make_async_remote_copy(
                src_ref=slot(a, *block) if src is None else src, dst_ref=slot(a, *block),
                send_sem=send_sems.at[a, k], recv_sem=recv_sems.at[a, k], device_id=to, device_id_type=MESH)

        mine = [pltpu.make_async_copy(x_refs[a], slot(a, *me), local_sems.at[a]) for a in range(n)]
        for cp in mine:
            cp.start()
        first = []
        for a in range(n):
            first.append(copy(a, 0, me, sibling, src=x_refs[a]))
            first += [copy(a, 1 + j, me, (*chip, c), src=x_refs[a]) for j, chip in enumerate(chips)]
        for cp in first:
            cp.start()
        passed = []
        for j, chip in enumerate(chips):
            for a in range(n):
                copy(a, 1 + j, (*chip, c), me).wait_recv()
                fwd = copy(a, 4 + j, (*chip, c), sibling)
                fwd.start()
                passed.append(fwd)
        for a in range(n):
            copy(a, 0, sibling, me).wait_recv()
            for j, chip in enumerate(chips):
                copy(a, 4 + j, (*chip, 1 - c), me).wait_recv()
        for cp in first + passed:
            cp.wait_send()
        for cp in mine:
            cp.wait()

    return pl.pallas_call(
        body, name=name,
        out_shape=[jax.ShapeDtypeStruct((N_DEV,) + s.shape, s.dtype) for s in shards],
        in_specs=[ANY] * n, out_specs=[ANY] * n,
        scratch_shapes=[pltpu.SemaphoreType.DMA((n, 7)), pltpu.SemaphoreType.DMA((n, 7)),
                        pltpu.SemaphoreType.DMA((n,))],
    )(*shards)


def _rs_sibling(grads, name):
    n = len(grads)

    def body(*refs):
        g_refs, o_refs = refs[:n], refs[n:2 * n]
        send_sems, recv_sems = refs[2 * n:]
        x, y, c = _my_pos()
        cps = []
        for a in range(n):
            for t in range(4):
                cps.append(pltpu.make_async_remote_copy(
                    src_ref=g_refs[a].at[2 * t + (1 - c)], dst_ref=o_refs[a].at[t],
                    send_sem=send_sems.at[a, t], recv_sem=recv_sems.at[a, t],
                    device_id=(x, y, 1 - c), device_id_type=MESH))
        for cp in cps:
            cp.start()
        for cp in cps:
            cp.wait()

    return pl.pallas_call(
        body, name=name,
        out_shape=[jax.ShapeDtypeStruct((4,) + g.shape[1:], g.dtype) for g in grads],
        in_specs=[ANY] * n, out_specs=[ANY] * n,
        scratch_shapes=[pltpu.SemaphoreType.DMA((n, 4)), pltpu.SemaphoreType.DMA((n, 4))],
    )(*grads)


def _rs_chips(parts, name):
    n = len(parts)

    def body(*refs):
        p_refs, o_refs = refs[:n], refs[n:2 * n]
        send_sems, recv_sems = refs[2 * n:]
        x, y, c = _my_pos()
        chips = [(1 - x, y), (x, 1 - y), (1 - x, 1 - y)]
        cps = []
        for a in range(n):
            for j, (px, py) in enumerate(chips):
                cps.append(pltpu.make_async_remote_copy(
                    src_ref=p_refs[a].at[2 * px + py], dst_ref=o_refs[a].at[j],
                    send_sem=send_sems.at[a, j], recv_sem=recv_sems.at[a, j],
                    device_id=(px, py, c), device_id_type=MESH))
        for cp in cps:
            cp.start()
        for cp in cps:
            cp.wait()

    return pl.pallas_call(
        body, name=name,
        out_shape=[jax.ShapeDtypeStruct((3,) + p.shape[1:], p.dtype) for p in parts],
        in_specs=[ANY] * n, out_specs=[ANY] * n,
        scratch_shapes=[pltpu.SemaphoreType.DMA((n, 3)), pltpu.SemaphoreType.DMA((n, 3))],
    )(*parts)


def _cast_layer(w, l, name):
    _, r, c = w.shape
    tr = _pick(r, [512, 256, 128, 64, 32, 16])

    def body(w_ref, o_ref):
        o_ref[...] = w_ref[...].astype(BF16)

    return pl.pallas_call(
        body, name=name, grid=(r // tr,),
        in_specs=[pl.BlockSpec((None, tr, c), lambda i: (l, i, 0))],
        out_specs=pl.BlockSpec((tr, c), lambda i: (i, 0)),
        out_shape=jax.ShapeDtypeStruct((r, c), BF16), compiler_params=_cp(1),
    )(w)


def _chip_partial(g, recv, c_idx, name):
    _, r, c = g.shape
    tr = _pick(r, [512, 256, 128, 64, 32, 16])
    g4 = g.reshape(4, 2, r, c)

    def body(c_ref, g_ref, r_ref, o_ref):
        o_ref[...] = (g_ref[...].astype(F32) + r_ref[...].astype(F32)).astype(BF16)

    return pl.pallas_call(
        body, name=name,
        grid_spec=pltpu.PrefetchScalarGridSpec(
            num_scalar_prefetch=1, grid=(4, r // tr),
            in_specs=[pl.BlockSpec((None, None, tr, c), lambda t, i, cr: (t, cr[0], i, 0)),
                      pl.BlockSpec((None, tr, c), lambda t, i, cr: (t, i, 0))],
            out_specs=pl.BlockSpec((None, tr, c), lambda t, i, cr: (t, i, 0))),
        out_shape=jax.ShapeDtypeStruct((4, r, c), BF16), compiler_params=_cp(2),
    )(c_idx, g4, recv)


def _adam_math(w, g, m, v):
    m2 = ADAM_B1 * m + (1.0 - ADAM_B1) * g
    v2 = ADAM_B2 * v + (1.0 - ADAM_B2) * (g * g)
    m_hat = m2 / (1.0 - ADAM_B1 ** ADAM_STEP)
    v_hat = v2 / (1.0 - ADAM_B2 ** ADAM_STEP)
    delta = -ADAM_LR * (m_hat / (jnp.sqrt(v_hat) + ADAM_EPS) + ADAM_WD * w)
    return delta, m2, v2


def _adamw_reduced(part, recv, chip_idx, w, m, v, l, name):
    _, r, c = part.shape
    tr = _pick(r, [256, 128, 64, 32, 16])

    def body(t_ref, p_ref, r_ref, w_ref, m_ref, v_ref, g_out, d_out, m_out, v_out):
        g = p_ref[...].astype(F32)
        for j in range(3):
            g = g + r_ref[j].astype(F32)
        d, m2, v2 = _adam_math(w_ref[...], g, m_ref[...], v_ref[...])
        g_out[...] = g
        d_out[...] = d
        m_out[...] = m2
        v_out[...] = v2

    wspec = pl.BlockSpec((None, tr, c), lambda i, tr_: (l, i, 0))
    ospec = pl.BlockSpec((tr, c), lambda i, tr_: (i, 0))
    return pl.pallas_call(
        body, name=name,
        grid_spec=pltpu.PrefetchScalarGridSpec(
            num_scalar_prefetch=1, grid=(r // tr,),
            in_specs=[pl.BlockSpec((None, tr, c), lambda i, tr_: (tr_[0], i, 0)),
                      pl.BlockSpec((3, tr, c), lambda i, tr_: (0, i, 0)), wspec, wspec, wspec],
            out_specs=[ospec] * 4),
        out_shape=[jax.ShapeDtypeStruct((r, c), F32)] * 4, compiler_params=_cp(1),
    )(chip_idx, part, recv, w, m, v)


def _adamw_plain(w, g, m, v, name):
    def body(w_ref, g_ref, m_ref, v_ref, d_out, m_out, v_out):
        d, m2, v2 = _adam_math(w_ref[...], g_ref[...], m_ref[...], v_ref[...])
        d_out[...] = d
        m_out[...] = m2
        v_out[...] = v2

    return pl.pallas_call(
        body, name=name, out_shape=[jax.ShapeDtypeStruct(w.shape, F32)] * 3,
        compiler_params=pltpu.CompilerParams(vmem_limit_bytes=VMEM_ELEMENTWISE),
    )(w, g, m, v)


def _adamw_tiled(w, g, m, v, name):
    lyr, r, c = w.shape
    tr = _pick(r, [256, 128, 64, 32, 16, 8])

    def body(w_ref, g_ref, m_ref, v_ref, d_out, m_out, v_out):
        d, m2, v2 = _adam_math(w_ref[...], g_ref[...], m_ref[...], v_ref[...])
        d_out[...] = d
        m_out[...] = m2
        v_out[...] = v2

    spec = pl.BlockSpec((None, tr, c), lambda l, i: (l, i, 0))
    return pl.pallas_call(
        body, name=name, grid=(lyr, r // tr), in_specs=[spec] * 4, out_specs=[spec] * 3,
        out_shape=[jax.ShapeDtypeStruct(w.shape, F32)] * 3, compiler_params=_cp(2),
    )(w, g, m, v)


def _sum8(gathered, rows, name):
    def body(g_ref, o_ref):
        acc = g_ref[0:rows, :]
        for d in range(1, N_DEV):
            acc = acc + g_ref[d * rows:(d + 1) * rows, :]
        o_ref[...] = acc

    return pl.pallas_call(
        body, name=name, out_shape=jax.ShapeDtypeStruct((rows, 128), F32),
        compiler_params=pltpu.CompilerParams(vmem_limit_bytes=VMEM_ELEMENTWISE),
    )(gathered)


def _mm_nn(a, b3, *, tm, tn, out_dtype, name):
    M, K = a.shape
    nb, _, nc = b3.shape
    q = nc // tn

    def body(a_ref, b_ref, o_ref):
        o_ref[...] = jnp.dot(a_ref[...], b_ref[...], preferred_element_type=F32).astype(o_ref.dtype)

    return pl.pallas_call(
        body, name=name, grid=(nb * q, M // tm),
        in_specs=[pl.BlockSpec((tm, K), lambda j, i: (i, 0)),
                  pl.BlockSpec((None, K, tn), lambda j, i: (j // q, 0, j % q))],
        out_specs=pl.BlockSpec((tm, tn), lambda j, i: (i, j)),
        out_shape=jax.ShapeDtypeStruct((M, nb * nc), out_dtype), compiler_params=_cp(2, VMEM_MATMUL),
    )(a, b3)


def _mm_nn_resid(a, b2, x_old, mod, gate_row, n_ctx, *, tm, tn, name):
    M, K = a.shape
    N = b2.shape[1]

    def body(a_ref, b_ref, x_ref, mod_ref, y_ref, xn_ref):
        y = jnp.dot(a_ref[...], b_ref[...], preferred_element_type=F32)
        is_ctx = _rows(pl.program_id(1), tm) < n_ctx
        g = jnp.where(is_ctx, mod_ref[0, gate_row:gate_row + 1, :], mod_ref[1, gate_row:gate_row + 1, :])
        y_ref[...] = y.astype(BF16)
        xn_ref[...] = x_ref[...] + g * y

    return pl.pallas_call(
        body, name=name, grid=(N // tn, M // tm),
        in_specs=[pl.BlockSpec((tm, K), lambda j, i: (i, 0)),
                  pl.BlockSpec((K, tn), lambda j, i: (0, j)),
                  pl.BlockSpec((tm, tn), lambda j, i: (i, j)),
                  pl.BlockSpec((2, 6, tn), lambda j, i: (0, 0, j))],
        out_specs=[pl.BlockSpec((tm, tn), lambda j, i: (i, j))] * 2,
        out_shape=[jax.ShapeDtypeStruct((M, N), BF16), jax.ShapeDtypeStruct((M, N), F32)],
        compiler_params=_cp(2, VMEM_MATMUL),
    )(a, b2, x_old, mod)


def _mm_nt_acc(dy, w3, *, tm, out_dtype, name):
    M = dy.shape[0]
    nb, K, nc = w3.shape

    def body(dy_ref, w_ref, o_ref, acc_ref):
        s = pl.program_id(1)

        @pl.when(s == 0)
        def _():
            acc_ref[...] = jnp.zeros_like(acc_ref)

        acc_ref[...] += lax.dot_general(dy_ref[...], w_ref[...], NT_DIMS, preferred_element_type=F32)

        @pl.when(s == nb - 1)
        def _():
            o_ref[...] = acc_ref[...].astype(o_ref.dtype)

    return pl.pallas_call(
        body, name=name, grid=(M // tm, nb),
        in_specs=[pl.BlockSpec((tm, nc), lambda i, s: (i, s)),
                  pl.BlockSpec((None, K, nc), lambda i, s: (s, 0, 0))],
        out_specs=pl.BlockSpec((tm, K), lambda i, s: (i, 0)),
        out_shape=jax.ShapeDtypeStruct((M, K), out_dtype),
        scratch_shapes=[pltpu.VMEM((tm, K), F32)], compiler_params=_cp(2, VMEM_MATMUL),
    )(dy, w3)


def _mm_nt(dy, w2, *, tm, tn, out_dtype, name):
    M, N = dy.shape
    K = w2.shape[0]

    def body(dy_ref, w_ref, o_ref):
        o_ref[...] = lax.dot_general(dy_ref[...], w_ref[...], NT_DIMS,
                                     preferred_element_type=F32).astype(o_ref.dtype)

    return pl.pallas_call(
        body, name=name, grid=(K // tn, M // tm),
        in_specs=[pl.BlockSpec((tm, N), lambda j, i: (i, 0)),
                  pl.BlockSpec((tn, N), lambda j, i: (j, 0))],
        out_specs=pl.BlockSpec((tm, tn), lambda j, i: (i, j)),
        out_shape=jax.ShapeDtypeStruct((M, K), out_dtype), compiler_params=_cp(2, VMEM_MATMUL),
    )(dy, w2)


def _mm_tn(a, dy, *, nb, tka, tn, ts, name):
    S, Ka = a.shape
    N = dy.shape[1]
    nc = N // nb
    q = nc // tn
    nk = S // ts

    def body(a_ref, dy_ref, o_ref, acc_ref):
        k = pl.program_id(2)

        @pl.when(k == 0)
        def _():
            acc_ref[...] = jnp.zeros_like(acc_ref)

        acc_ref[...] += lax.dot_general(a_ref[...], dy_ref[...], TN_DIMS, preferred_element_type=F32)

        @pl.when(k == nk - 1)
        def _():
            o_ref[...] = acc_ref[...].astype(o_ref.dtype)

    return pl.pallas_call(
        body, name=name, grid=(nb * q, Ka // tka, nk),
        in_specs=[pl.BlockSpec((ts, tka), lambda j, ia, k: (k, ia)),
                  pl.BlockSpec((ts, tn), lambda j, ia, k: (k, j))],
        out_specs=pl.BlockSpec((None, tka, tn), lambda j, ia, k: (j // q, ia, j % q)),
        out_shape=jax.ShapeDtypeStruct((nb, Ka, nc), BF16),
        scratch_shapes=[pltpu.VMEM((tka, tn), F32)], compiler_params=_cp(3, VMEM_MATMUL),
    )(a, dy)


def _norm_mod(x, mod, row0, n_ctx, name):
    S, D = x.shape
    tm = _pick(S, [256, 128])

    def body(x_ref, mod_ref, h_ref):
        xv = x_ref[...]
        r = lax.rsqrt(jnp.mean(xv * xv, axis=-1, keepdims=True) + EPS)
        is_ctx = _rows(pl.program_id(0), tm) < n_ctx
        sh = jnp.where(is_ctx, mod_ref[0, row0:row0 + 1, :], mod_ref[1, row0:row0 + 1, :])
        sc = jnp.where(is_ctx, mod_ref[0, row0 + 1:row0 + 2, :], mod_ref[1, row0 + 1:row0 + 2, :])
        h_ref[...] = (xv * r * (1.0 + sc) + sh).astype(BF16)

    return pl.pallas_call(
        body, name=name, grid=(S // tm,),
        in_specs=[pl.BlockSpec((tm, D), lambda i: (i, 0)), pl.BlockSpec((2, 6, D), lambda i: (0, 0, 0))],
        out_specs=pl.BlockSpec((tm, D), lambda i: (i, 0)),
        out_shape=jax.ShapeDtypeStruct((S, D), BF16), compiler_params=_cp(1),
    )(x, mod)


def _norm_mod_bwd(dh, x, mod, dx_res, row0, n_ctx, name):
    S, D = x.shape
    tm = _pick(S, [256, 128])

    def body(dh_ref, x_ref, mod_ref, res_ref, dx_ref, acc_ref):
        i = pl.program_id(0)

        @pl.when(i == 0)
        def _():
            acc_ref[...] = jnp.zeros_like(acc_ref)

        xv = x_ref[...]
        dh_v = dh_ref[...].astype(F32)
        r = lax.rsqrt(jnp.mean(xv * xv, axis=-1, keepdims=True) + EPS)
        xhat = xv * r
        is_ctx = _rows(i, tm) < n_ctx
        sc = jnp.where(is_ctx, mod_ref[0, row0 + 1:row0 + 2, :], mod_ref[1, row0 + 1:row0 + 2, :])
        dxhat = dh_v * (1.0 + sc)
        dx_ref[...] = res_ref[...] + r * (dxhat - xhat * jnp.mean(dxhat * xhat, axis=-1, keepdims=True))
        dsc = dh_v * xhat
        zero = jnp.zeros_like(dh_v)
        sums = [jnp.sum(jnp.where(is_ctx, dh_v, zero), axis=0, keepdims=True),
                jnp.sum(jnp.where(is_ctx, dsc, zero), axis=0, keepdims=True),
                jnp.sum(jnp.where(is_ctx, zero, dh_v), axis=0, keepdims=True),
                jnp.sum(jnp.where(is_ctx, zero, dsc), axis=0, keepdims=True)]
        rid = lax.broadcasted_iota(I32, (8, D), 0)
        upd = jnp.zeros((8, D), F32)
        for k, s in enumerate(sums):
            upd = upd + jnp.where(rid == k, s, 0.0)
        acc_ref[...] += upd

    return pl.pallas_call(
        body, name=name, grid=(S // tm,),
        in_specs=[pl.BlockSpec((tm, D), lambda i: (i, 0)), pl.BlockSpec((tm, D), lambda i: (i, 0)),
                  pl.BlockSpec((2, 6, D), lambda i: (0, 0, 0)), pl.BlockSpec((tm, D), lambda i: (i, 0))],
        out_specs=[pl.BlockSpec((tm, D), lambda i: (i, 0)), pl.BlockSpec((8, D), lambda i: (0, 0))],
        out_shape=[jax.ShapeDtypeStruct((S, D), F32), jax.ShapeDtypeStruct((8, D), F32)],
        compiler_params=_cp(1),
    )(dh, x, mod, dx_res)


def _gate_bwd(dx, y, mod, gate_row, n_ctx, name):
    S, D = dx.shape
    tm = _pick(S, [256, 128])

    def body(dx_ref, y_ref, mod_ref, dy_ref, acc_ref):
        i = pl.program_id(0)

        @pl.when(i == 0)
        def _():
            acc_ref[...] = jnp.zeros_like(acc_ref)

        dxv = dx_ref[...]
        is_ctx = _rows(i, tm) < n_ctx
        g = jnp.where(is_ctx, mod_ref[0, gate_row:gate_row + 1, :], mod_ref[1, gate_row:gate_row + 1, :])
        dy_ref[...] = (g * dxv).astype(BF16)
        prod = dxv * y_ref[...].astype(F32)
        zero = jnp.zeros_like(prod)
        s_ctx = jnp.sum(jnp.where(is_ctx, prod, zero), axis=0, keepdims=True)
        s_lat = jnp.sum(jnp.where(is_ctx, zero, prod), axis=0, keepdims=True)
        rid = lax.broadcasted_iota(I32, (8, D), 0)
        acc_ref[...] += jnp.where(rid == 0, s_ctx, 0.0) + jnp.where(rid == 1, s_lat, 0.0)

    return pl.pallas_call(
        body, name=name, grid=(S // tm,),
        in_specs=[pl.BlockSpec((tm, D), lambda i: (i, 0)), pl.BlockSpec((tm, D), lambda i: (i, 0)),
                  pl.BlockSpec((2, 6, D), lambda i: (0, 0, 0))],
        out_specs=[pl.BlockSpec((tm, D), lambda i: (i, 0)), pl.BlockSpec((8, D), lambda i: (0, 0))],
        out_shape=[jax.ShapeDtypeStruct((S, D), BF16), jax.ShapeDtypeStruct((8, D), F32)],
        compiler_params=_cp(1),
    )(dx, y, mod)


def _loss_grad(x, target, n_ctx, name):
    S, D = x.shape
    tm = _pick(n_ctx, [256, 128])
    nct = n_ctx // tm

    def body(x_ref, t_ref, dx_ref, tot_ref, acc_ref):
        i = pl.program_id(0)

        @pl.when(i == 0)
        def _():
            acc_ref[...] = jnp.zeros_like(acc_ref)

        @pl.when(i < nct)
        def _():
            dx_ref[...] = jnp.zeros_like(dx_ref)

        @pl.when(i >= nct)
        def _():
            err = x_ref[...] - t_ref[...]
            dx_ref[...] = err * (1.0 / D)
            acc_ref[...] += jnp.sum(err * err, axis=0, keepdims=True)

        @pl.when(i == S // tm - 1)
        def _():
            tot = jnp.sum(acc_ref[...], axis=1, keepdims=True) * (0.5 / D)
            tot_ref[...] = jnp.broadcast_to(tot, tot_ref.shape)

    return pl.pallas_call(
        body, name=name, grid=(S // tm,),
        in_specs=[pl.BlockSpec((tm, D), lambda i: (i, 0)),
                  pl.BlockSpec((tm, D), lambda i: (jnp.maximum(i - nct, 0), 0))],
        out_specs=[pl.BlockSpec((tm, D), lambda i: (i, 0)), pl.BlockSpec((1, 128), lambda i: (0, 0))],
        out_shape=[jax.ShapeDtypeStruct((S, D), F32), jax.ShapeDtypeStruct((1, 128), F32)],
        scratch_shapes=[pltpu.VMEM((1, D), F32)], compiler_params=_cp(1),
    )(x, target)


def _halo_specs(tm, tc, S, col_off):
    per = tm // HALO
    last = S // HALO - 1
    return [pl.BlockSpec((HALO, tc), lambda j, i: (jnp.maximum(i * per - 1, 0), j + col_off)),
            pl.BlockSpec((tm, tc), lambda j, i: (i, j + col_off)),
            pl.BlockSpec((HALO, tc), lambda j, i: (jnp.minimum((i + 1) * per, last), j + col_off))]


def _ext(p_ref, m_ref, n_ref):
    return jnp.concatenate([p_ref[...], m_ref[...], n_ref[...]], axis=0).astype(F32)


def _links(i, tm, S, n_ctx):
    n = tm + 2 * HALO
    rid = i * tm - HALO + lax.broadcasted_iota(I32, (n, 1), 0)
    has_prev = (rid != 0) & (rid != n_ctx)
    has_next = (rid != n_ctx - 1) & (rid != S - 1)
    return has_prev, has_next


def _up(x):
    return pltpu.roll(x, 1, 0)


def _dn(x):
    return pltpu.roll(x, x.shape[0] - 1, 0)


def _conv3(x, w_ref, has_prev, has_next):
    return (w_ref[0:1, :] * jnp.where(has_prev, _up(x), 0.0) + w_ref[1:2, :] * x
            + w_ref[2:3, :] * jnp.where(has_next, _dn(x), 0.0))


def _conv3_t(d, w_ref, has_prev, has_next):
    return (w_ref[0:1, :] * jnp.where(has_next, _dn(d), 0.0) + w_ref[1:2, :] * d
            + w_ref[2:3, :] * jnp.where(has_prev, _up(d), 0.0))


def _conv3_wgrad(d, x, has_prev, has_next, extra=None):
    c = slice(HALO, d.shape[0] - HALO)
    taps = [jnp.where(has_prev, _up(x), 0.0), x, jnp.where(has_next, _dn(x), 0.0)]
    sums = [jnp.sum((d * t)[c], axis=0, keepdims=True) for t in taps]
    if extra is not None:
        sums.append(jnp.sum(extra[c], axis=0, keepdims=True))
    rid = lax.broadcasted_iota(I32, (8, d.shape[1]), 0)
    upd = jnp.zeros((8, d.shape[1]), F32)
    for k, s in enumerate(sums):
        upd = upd + jnp.where(rid == k, s, 0.0)
    return upd


def _sigmoid(x):
    return 1.0 / (1.0 + jnp.exp(-x))


def _ffn_act(u, conv_w, conv_b, l, n_ctx, name):
    S, F2 = u.shape
    F = F2 // 2
    tm = _pick(S, [384, 256, 128])
    tc = _pick(F, [1408, 512, 256, 128])
    nj = F // tc

    def body(gp, gm, gn, v_ref, w_ref, b_ref, a_ref):
        has_prev, has_next = _links(pl.program_id(1), tm, S, n_ctx)
        gc = _conv3(_ext(gp, gm, gn), w_ref, has_prev, has_next)[HALO:HALO + tm] + b_ref[...]
        a_ref[...] = (gc * _sigmoid(gc) * v_ref[...].astype(F32)).astype(BF16)

    return pl.pallas_call(
        body, name=name, grid=(nj, S // tm),
        in_specs=_halo_specs(tm, tc, S, 0) + [
            pl.BlockSpec((tm, tc), lambda j, i: (i, j + nj)),
            pl.BlockSpec((None, 3, tc), lambda j, i: (l, 0, j)),
            pl.BlockSpec((None, 1, tc), lambda j, i: (l, 0, j))],
        out_specs=pl.BlockSpec((tm, tc), lambda j, i: (i, j)),
        out_shape=jax.ShapeDtypeStruct((S, F), BF16), compiler_params=_cp(2),
    )(u, u, u, u, conv_w, conv_b)


def _ffn_act_bwd(u, da, conv_w, conv_b, l, n_ctx, name):
    S, F2 = u.shape
    F = F2 // 2
    tm = _pick(S, [384, 256, 128])
    tc = _pick(F, [1408, 512, 256, 128])
    nj = F // tc

    def body(gp, gm, gn, vp, vm, vn, dp, dm, dn_, w_ref, b_ref, dg_ref, dv_ref, acc_ref):
        i = pl.program_id(1)

        @pl.when(i == 0)
        def _():
            acc_ref[...] = jnp.zeros_like(acc_ref)

        has_prev, has_next = _links(i, tm, S, n_ctx)
        g = _ext(gp, gm, gn)
        val = _ext(vp, vm, vn)
        d_a = _ext(dp, dm, dn_)
        gc = _conv3(g, w_ref, has_prev, has_next) + b_ref[...]
        sg = _sigmoid(gc)
        dgc = d_a * val * (sg * (1.0 + gc * (1.0 - sg)))
        c = slice(HALO, HALO + tm)
        dv_ref[...] = (d_a * gc * sg)[c].astype(BF16)
        dg_ref[...] = _conv3_t(dgc, w_ref, has_prev, has_next)[c].astype(BF16)
        acc_ref[...] += _conv3_wgrad(dgc, g, has_prev, has_next, extra=dgc)

    return pl.pallas_call(
        body, name=name, grid=(nj, S // tm),
        in_specs=_halo_specs(tm, tc, S, 0) + _halo_specs(tm, tc, S, nj) + _halo_specs(tm, tc, S, 0) + [
            pl.BlockSpec((None, 3, tc), lambda j, i: (l, 0, j)),
            pl.BlockSpec((None, 1, tc), lambda j, i: (l, 0, j))],
        out_specs=[pl.BlockSpec((tm, tc), lambda j, i: (i, j))] * 2 + [pl.BlockSpec((8, tc), lambda j, i: (0, j))],
        out_shape=[jax.ShapeDtypeStruct((S, F), BF16)] * 2 + [jax.ShapeDtypeStruct((8, F), F32)],
        compiler_params=_cp(2),
    )(u, u, u, u, u, u, da, da, da, conv_w, conv_b)


def _sc_act(u, conv_w, l, n_ctx, name):
    S, D3 = u.shape
    D = D3 // 3
    tm = _pick(S, [384, 256, 128])
    tc = _pick(D, [1024, 512, 256, 128])
    nj = D // tc

    def body(b_ref, cp, cm, cn, vp, vm, vn, w_ref, z_ref):
        has_prev, has_next = _links(pl.program_id(1), tm, S, n_ctx)
        t = _ext(cp, cm, cn) * _ext(vp, vm, vn)
        cv = _conv3(t, w_ref, has_prev, has_next)[HALO:HALO + tm]
        z_ref[...] = (b_ref[...].astype(F32) * cv).astype(BF16)

    return pl.pallas_call(
        body, name=name, grid=(nj, S // tm),
        in_specs=[pl.BlockSpec((tm, tc), lambda j, i: (i, j))] + _halo_specs(tm, tc, S, nj)
        + _halo_specs(tm, tc, S, 2 * nj) + [pl.BlockSpec((None, 3, tc), lambda j, i: (l, 0, j))],
        out_specs=pl.BlockSpec((tm, tc), lambda j, i: (i, j)),
        out_shape=jax.ShapeDtypeStruct((S, D), BF16), compiler_params=_cp(2),
    )(u, u, u, u, u, u, u, conv_w)


def _sc_act_bwd(u, dz, conv_w, l, n_ctx, name):
    S, D3 = u.shape
    D = D3 // 3
    tm = _pick(S, [384, 256, 128])
    tc = _pick(D, [1024, 512, 256, 128])
    nj = D // tc

    def body(bp, bm, bn, cp, cm, cn, vp, vm, vn, zp, zm, zn, w_ref, db_ref, dc_ref, dv_ref, acc_ref):
        i = pl.program_id(1)

        @pl.when(i == 0)
        def _():
            acc_ref[...] = jnp.zeros_like(acc_ref)

        has_prev, has_next = _links(i, tm, S, n_ctx)
        gb = _ext(bp, bm, bn)
        gcv = _ext(cp, cm, cn)
        val = _ext(vp, vm, vn)
        d_z = _ext(zp, zm, zn)
        t = gcv * val
        c = slice(HALO, HALO + tm)
        db_ref[...] = (d_z * _conv3(t, w_ref, has_prev, has_next))[c].astype(BF16)
        dcv = d_z * gb
        dt = _conv3_t(dcv, w_ref, has_prev, has_next)
        dc_ref[...] = (dt * val)[c].astype(BF16)
        dv_ref[...] = (dt * gcv)[c].astype(BF16)
        acc_ref[...] += _conv3_wgrad(dcv, t, has_prev, has_next)

    return pl.pallas_call(
        body, name=name, grid=(nj, S // tm),
        in_specs=_halo_specs(tm, tc, S, 0) + _halo_specs(tm, tc, S, nj) + _halo_specs(tm, tc, S, 2 * nj)
        + _halo_specs(tm, tc, S, 0) + [pl.BlockSpec((None, 3, tc), lambda j, i: (l, 0, j))],
        out_specs=[pl.BlockSpec((tm, tc), lambda j, i: (i, j))] * 3 + [pl.BlockSpec((8, tc), lambda j, i: (0, j))],
        out_shape=[jax.ShapeDtypeStruct((S, D), BF16)] * 3 + [jax.ShapeDtypeStruct((8, D), F32)],
        compiler_params=_cp(2),
    )(u, u, u, u, u, u, u, u, u, dz, dz, dz, conv_w)


def _rope_tables(T, n_ctx):
    rows = T // GRID_W
    pairs = HEAD_DIM // 4
    row = jnp.repeat(jnp.arange(rows), GRID_W).astype(F32)
    col = jnp.tile(jnp.arange(GRID_W), rows).astype(F32)
    inv = ROPE_BASE ** (-jnp.arange(pairs, dtype=F32) / pairs)
    ang = jnp.concatenate([row[:, None] * inv, row[:, None] * inv, col[:, None] * inv, col[:, None] * inv], axis=1)
    cos, sin = jnp.cos(ang), jnp.sin(ang)
    first = (jnp.arange(HEAD_DIM) % (2 * pairs)) < pairs
    sin_a = jnp.where(first, -sin, 0.0)
    sin_b = jnp.where(first, 0.0, sin)
    pad = jnp.zeros((n_ctx, HEAD_DIM), F32)
    return (jnp.concatenate([pad + 1.0, cos], axis=0), jnp.concatenate([pad, sin_a], axis=0),
            jnp.concatenate([pad, sin_b], axis=0))


def _qk_prep(qkv, tabs, gains, l, n_q, n_kv, name):
    S = qkv.shape[0]
    W = qkv.shape[1]
    tm = _pick(S, [256, 128])
    cos, sin_a, sin_b = tabs

    def body(x_ref, cos_ref, sa_ref, sb_ref, g_ref, q_ref, k_ref):
        cs, sa, sb = cos_ref[...], sa_ref[...], sb_ref[...]
        for h in range(n_q + n_kv):
            xv = x_ref[:, h * 128:(h + 1) * 128].astype(F32)
            r = lax.rsqrt(jnp.mean(xv * xv, axis=-1, keepdims=True) + EPS)
            gain = g_ref[0:1, :] if h < n_q else g_ref[1:2, :]
            y = xv * r * gain
            out = (y * cs + pltpu.roll(y, 96, 1) * sa + pltpu.roll(y, 32, 1) * sb).astype(BF16)
            if h < n_q:
                q_ref[:, h * 128:(h + 1) * 128] = out
            else:
                k_ref[:, (h - n_q) * 128:(h - n_q + 1) * 128] = out

    tspec = pl.BlockSpec((tm, 128), lambda i: (i, 0))
    return pl.pallas_call(
        body, name=name, grid=(S // tm,),
        in_specs=[pl.BlockSpec((tm, W), lambda i: (i, 0)), tspec, tspec, tspec,
                  pl.BlockSpec((None, 2, 128), lambda i: (l, 0, 0))],
        out_specs=[pl.BlockSpec((tm, n_q * 128), lambda i: (i, 0)), pl.BlockSpec((tm, n_kv * 128), lambda i: (i, 0))],
        out_shape=[jax.ShapeDtypeStruct((S, n_q * 128), BF16), jax.ShapeDtypeStruct((S, n_kv * 128), BF16)],
        compiler_params=_cp(1),
    )(qkv, cos, sin_a, sin_b, gains)


def _qk_prep_bwd(dq, dk, dv, qkv, tabs, gains, l, n_q, n_kv, name):
    S, W = qkv.shape
    tm = _pick(S, [256, 128])
    cos, sin_a, sin_b = tabs

    def body(dq_ref, dk_ref, dv_ref, x_ref, cos_ref, sa_ref, sb_ref, g_ref, o_ref, acc_ref):
        @pl.when(pl.program_id(0) == 0)
        def _():
            acc_ref[...] = jnp.zeros_like(acc_ref)

        cs, sa, sb = cos_ref[...], sa_ref[...], sb_ref[...]
        dgq = jnp.zeros((1, 128), F32)
        dgk = jnp.zeros((1, 128), F32)
        for h in range(n_q + n_kv):
            if h < n_q:
                d_out = dq_ref[:, h * 128:(h + 1) * 128]
                gain = g_ref[0:1, :]
            else:
                d_out = dk_ref[:, (h - n_q) * 128:(h - n_q + 1) * 128]
                gain = g_ref[1:2, :]
            dy = d_out * cs + pltpu.roll(d_out * sa, 32, 1) + pltpu.roll(d_out * sb, 96, 1)
            xv = x_ref[:, h * 128:(h + 1) * 128].astype(F32)
            r = lax.rsqrt(jnp.mean(xv * xv, axis=-1, keepdims=True) + EPS)
            xhat = xv * r
            dg = jnp.sum(dy * xhat, axis=0, keepdims=True)
            if h < n_q:
                dgq = dgq + dg
            else:
                dgk = dgk + dg
            dxhat = dy * gain
            dx = r * (dxhat - xhat * jnp.mean(dxhat * xhat, axis=-1, keepdims=True))
            o_ref[:, h * 128:(h + 1) * 128] = dx.astype(BF16)
        v0 = (n_q + n_kv) * 128
        o_ref[:, v0:] = dv_ref[...].astype(BF16)
        rid = lax.broadcasted_iota(I32, (8, 128), 0)
        acc_ref[...] += jnp.where(rid == 0, dgq, 0.0) + jnp.where(rid == 1, dgk, 0.0)

    tspec = pl.BlockSpec((tm, 128), lambda i: (i, 0))
    return pl.pallas_call(
        body, name=name, grid=(S // tm,),
        in_specs=[pl.BlockSpec((tm, n_q * 128), lambda i: (i, 0)), pl.BlockSpec((tm, n_kv * 128), lambda i: (i, 0)),
                  pl.BlockSpec((tm, n_kv * 128), lambda i: (i, 0)), pl.BlockSpec((tm, W), lambda i: (i, 0)),
                  tspec, tspec, tspec, pl.BlockSpec((None, 2, 128), lambda i: (l, 0, 0))],
        out_specs=[pl.BlockSpec((tm, W), lambda i: (i, 0)), pl.BlockSpec((8, 128), lambda i: (0, 0))],
        out_shape=[jax.ShapeDtypeStruct((S, W), BF16), jax.ShapeDtypeStruct((8, 128), F32)],
        compiler_params=_cp(1),
    )(dq, dk, dv, qkv, cos, sin_a, sin_b, gains)


def _band_specs(width, col, nb, n_ctx):
    return [pl.BlockSpec((BLK, width), lambda i: (jnp.maximum(i - 1, 0), col)),
            pl.BlockSpec((BLK, width), lambda i: (i, col)),
            pl.BlockSpec((BLK, width), lambda i: (jnp.minimum(i + 1, nb - 1), col)),
            pl.BlockSpec((n_ctx, width), lambda i: (0, col))]


def _q_side_mask(i, S, n_ctx):
    shape = (GROUP * BLK, 3 * BLK + n_ctx)
    a = lax.broadcasted_iota(I32, shape, 0) & (BLK - 1)
    kk = lax.broadcasted_iota(I32, shape, 1)
    rq = i * BLK + a
    rk = (i - 1) * BLK + kk
    band = (rq >= n_ctx) & (rk >= n_ctx) & (rk < S) & (jnp.abs(rq - rk) <= WINDOW)
    return (kk >= 3 * BLK) | band


def _stack_heads(ref, g):
    return jnp.concatenate([ref[:, (GROUP * g + hh) * 128:(GROUP * g + hh + 1) * 128] for hh in range(GROUP)], axis=0)


def _stack_cols(ref, g):
    return jnp.concatenate([ref[:, GROUP * g + hh:GROUP * g + hh + 1] for hh in range(GROUP)], axis=0)


def _sink_col(sink_ref, l, g):
    return jnp.concatenate([jnp.full((BLK, 1), sink_ref[l, GROUP * g + hh], F32) for hh in range(GROUP)], axis=0)


def _attn_fwd(q, k, qkv, sink, l, n_ctx, name):
    S, DQ = q.shape
    DK = k.shape[1]
    n_kv = DK // 128
    nb = S // BLK
    vcol = (DQ + DK) // DK
    scale = HEAD_DIM ** -0.5

    def body(sink_ref, q_ref, kp, kc, kn, kx, vp, vc, vn, vx, o_ref, lse_ref):
        i = pl.program_id(0)
        mask = _q_side_mask(i, S, n_ctx)
        lane = lax.broadcasted_iota(I32, (BLK, 128), 1)
        lse_tile = jnp.zeros((BLK, 128), F32)
        for g in range(n_kv):
            sl = slice(g * 128, (g + 1) * 128)
            kcat = jnp.concatenate([kp[:, sl], kc[:, sl], kn[:, sl], kx[:, sl]], axis=0)
            vcat = jnp.concatenate([vp[:, sl], vc[:, sl], vn[:, sl], vx[:, sl]], axis=0)
            s = lax.dot_general(_stack_heads(q_ref, g), kcat, NT_DIMS, preferred_element_type=F32) * scale
            s = jnp.where(mask, s, NEG)
            sk = _sink_col(sink_ref, l, g)
            m = jnp.maximum(jnp.max(s, axis=1, keepdims=True), sk)
            e = jnp.exp(s - m)
            den = jnp.sum(e, axis=1, keepdims=True) + jnp.exp(sk - m)
            p = (e / den).astype(BF16)
            o = jnp.dot(p, vcat, preferred_element_type=F32)
            lse = m + jnp.log(den)
            for hh in range(GROUP):
                h = GROUP * g + hh
                o_ref[:, h * 128:(h + 1) * 128] = o[hh * BLK:(hh + 1) * BLK].astype(BF16)
                lse_tile = jnp.where(lane == h, lse[hh * BLK:(hh + 1) * BLK], lse_tile)
        lse_ref[...] = lse_tile

    return pl.pallas_call(
        body, name=name, grid=(nb,),
        in_specs=[pl.BlockSpec(memory_space=pltpu.SMEM), pl.BlockSpec((BLK, DQ), lambda i: (i, 0))]
        + _band_specs(DK, 0, nb, n_ctx) + _band_specs(DK, vcol, nb, n_ctx),
        out_specs=[pl.BlockSpec((BLK, DQ), lambda i: (i, 0)), pl.BlockSpec((BLK, 128), lambda i: (i, 0))],
        out_shape=[jax.ShapeDtypeStruct((S, DQ), BF16), jax.ShapeDtypeStruct((S, 128), F32)],
        compiler_params=_cp(1),
    )(sink, q, k, k, k, k, qkv, qkv, qkv, qkv)


def _attn_bwd_q(q, k, qkv, o, do, lse, sink, l, n_ctx, name):
    S, DQ = q.shape
    DK = k.shape[1]
    n_kv = DK // 128
    nb = S // BLK
    vcol = (DQ + DK) // DK
    scale = HEAD_DIM ** -0.5

    def body(sink_ref, q_ref, kp, kc, kn, kx, vp, vc, vn, vx, o_ref, do_ref, lse_ref,
             dq_ref, delta_ref, dkx_ref, dvx_ref, dsink_ref):
        i = pl.program_id(0)

        @pl.when(i == 0)
        def _():
            dkx_ref[...] = jnp.zeros_like(dkx_ref)
            dvx_ref[...] = jnp.zeros_like(dvx_ref)
            dsink_ref[...] = jnp.zeros_like(dsink_ref)

        mask = _q_side_mask(i, S, n_ctx)
        lane = lax.broadcasted_iota(I32, (BLK, 128), 1)
        lane8 = lax.broadcasted_iota(I32, (8, 128), 1)
        row8 = lax.broadcasted_iota(I32, (8, 128), 0)
        delta_tile = jnp.zeros((BLK, 128), F32)
        dsink_upd = jnp.zeros((8, 128), F32)
        for g in range(n_kv):
            sl = slice(g * 128, (g + 1) * 128)
            kcat = jnp.concatenate([kp[:, sl], kc[:, sl], kn[:, sl], kx[:, sl]], axis=0)
            vcat = jnp.concatenate([vp[:, sl], vc[:, sl], vn[:, sl], vx[:, sl]], axis=0)
            qg = _stack_heads(q_ref, g)
            dog = _stack_heads(do_ref, g)
            delta = jnp.sum(dog.astype(F32) * _stack_heads(o_ref, g).astype(F32), axis=1, keepdims=True)
            lse_g = _stack_cols(lse_ref, g)
            s = lax.dot_general(qg, kcat, NT_DIMS, preferred_element_type=F32) * scale
            p = jnp.exp(jnp.where(mask, s - lse_g, NEG))
            dp = lax.dot_general(dog, vcat, NT_DIMS, preferred_element_type=F32)
            ds = (p * (dp - delta) * scale).astype(BF16)
            dqg = jnp.dot(ds, kcat, preferred_element_type=F32)
            dkx_ref[:, sl] += lax.dot_general(ds[:, 3 * BLK:], qg, TN_DIMS, preferred_element_type=F32)
            dvx_ref[:, sl] += lax.dot_general(p.astype(BF16)[:, 3 * BLK:], dog, TN_DIMS, preferred_element_type=F32)
            dsk = -jnp.exp(_sink_col(sink_ref, l, g) - lse_g) * delta
            for hh in range(GROUP):
                h = GROUP * g + hh
                rs = slice(hh * BLK, (hh + 1) * BLK)
                dq_ref[:, h * 128:(h + 1) * 128] = dqg[rs]
                delta_tile = jnp.where(lane == h, delta[rs], delta_tile)
                tot = jnp.sum(dsk[rs], axis=0, keepdims=True)
                dsink_upd = dsink_upd + jnp.where((lane8 == h) & (row8 == 0), tot, 0.0)
        delta_ref[...] = delta_tile
        dsink_ref[...] += dsink_upd

    blk = pl.BlockSpec((BLK, DQ), lambda i: (i, 0))
    stat = pl.BlockSpec((BLK, 128), lambda i: (i, 0))
    return pl.pallas_call(
        body, name=name, grid=(nb,),
        in_specs=[pl.BlockSpec(memory_space=pltpu.SMEM), blk] + _band_specs(DK, 0, nb, n_ctx)
        + _band_specs(DK, vcol, nb, n_ctx) + [blk, blk, stat],
        out_specs=[blk, stat, pl.BlockSpec((n_ctx, DK), lambda i: (0, 0)), pl.BlockSpec((n_ctx, DK), lambda i: (0, 0)),
                   pl.BlockSpec((8, 128), lambda i: (0, 0))],
        out_shape=[jax.ShapeDtypeStruct((S, DQ), F32), jax.ShapeDtypeStruct((S, 128), F32),
                   jax.ShapeDtypeStruct((n_ctx, DK), F32), jax.ShapeDtypeStruct((n_ctx, DK), F32),
                   jax.ShapeDtypeStruct((8, 128), F32)],
        compiler_params=_cp(1),
    )(sink, q, k, k, k, k, qkv, qkv, qkv, qkv, o, do, lse)


def _attn_bwd_kv(q, k, qkv, do, lse, delta, dkx, dvx, n_ctx, name):
    S, DQ = q.shape
    DK = k.shape[1]
    n_kv = DK // 128
    nb = S // BLK
    nctx_b = n_ctx // BLK
    vcol = (DQ + DK) // DK
    scale = HEAD_DIM ** -0.5

    def three(width):
        return [pl.BlockSpec((BLK, width), lambda j: (jnp.maximum(j - 1, 0), 0)),
                pl.BlockSpec((BLK, width), lambda j: (j, 0)),
                pl.BlockSpec((BLK, width), lambda j: (jnp.minimum(j + 1, nb - 1), 0))]

    def body(k_ref, v_ref, qp, qc, qn, dop, doc, don, lp, lc, ln, dlp, dlc, dln, dkx_ref, dvx_ref, dk_ref, dv_ref):
        j = pl.program_id(0)

        @pl.when(j < nctx_b)
        def _():
            dk_ref[...] = dkx_ref[...]
            dv_ref[...] = dvx_ref[...]

        @pl.when(j >= nctx_b)
        def _():
            shape = (3 * GROUP * BLK, BLK)
            t = lax.broadcasted_iota(I32, shape, 0)
            rq = (j - 1 + t // (GROUP * BLK)) * BLK + (t & (BLK - 1))
            rk = j * BLK + lax.broadcasted_iota(I32, shape, 1)
            valid = (rq >= n_ctx) & (rq < S) & (jnp.abs(rq - rk) <= WINDOW)
            for g in range(n_kv):
                sl = slice(g * 128, (g + 1) * 128)
                qcat = jnp.concatenate([_stack_heads(r, g) for r in (qp, qc, qn)], axis=0)
                docat = jnp.concatenate([_stack_heads(r, g) for r in (dop, doc, don)], axis=0)
                lse_c = jnp.concatenate([_stack_cols(r, g) for r in (lp, lc, ln)], axis=0)
                delta_c = jnp.concatenate([_stack_cols(r, g) for r in (dlp, dlc, dln)], axis=0)
                s = lax.dot_general(qcat, k_ref[:, sl], NT_DIMS, preferred_element_type=F32) * scale
                p = jnp.exp(jnp.where(valid, s - lse_c, NEG))
                dp = lax.dot_general(docat, v_ref[:, sl], NT_DIMS, preferred_element_type=F32)
                ds = (p * (dp - delta_c) * scale).astype(BF16)
                dk_ref[:, sl] = lax.dot_general(ds, qcat, TN_DIMS, preferred_element_type=F32)
                dv_ref[:, sl] = lax.dot_general(p.astype(BF16), docat, TN_DIMS, preferred_element_type=F32)

    cspec = pl.BlockSpec((BLK, DK), lambda j: (jnp.minimum(j, nctx_b - 1), 0))
    return pl.pallas_call(
        body, name=name, grid=(nb,),
        in_specs=[pl.BlockSpec((BLK, DK), lambda j: (j, 0)), pl.BlockSpec((BLK, DK), lambda j: (j, vcol))]
        + three(DQ) + three(DQ) + three(128) + three(128) + [cspec, cspec],
        out_specs=[pl.BlockSpec((BLK, DK), lambda j: (j, 0))] * 2,
        out_shape=[jax.ShapeDtypeStruct((S, DK), F32)] * 2, compiler_params=_cp(1),
    )(k, qkv, q, q, q, do, do, do, lse, lse, lse, delta, delta, delta, dkx, dvx)


def _ada_fwd(cond, w_ada, b_cols, name):
    lyr, D, C = w_ada.shape
    tc = _pick(C, [512, 384, 256, 128])

    def body(c_ref, w_ref, b_ref, o_ref):
        cv = c_ref[...]
        act = cv * _sigmoid(cv)
        o_ref[...] = jnp.dot(act, w_ref[...], preferred_element_type=F32,
                             precision=lax.Precision.HIGHEST) + b_ref[...]

    return pl.pallas_call(
        body, name=name, grid=(lyr, C // tc),
        in_specs=[pl.BlockSpec((16, D), lambda l, j: (0, 0)),
                  pl.BlockSpec((None, D, tc), lambda l, j: (l, 0, j)),
                  pl.BlockSpec((None, 1, tc), lambda l, j: (l, 0, j))],
        out_specs=pl.BlockSpec((None, 16, tc), lambda l, j: (l, 0, j)),
        out_shape=jax.ShapeDtypeStruct((lyr, 16, C), F32), compiler_params=_cp(2),
    )(cond, w_ada, b_cols)


def _ada_bwd(cond, d_out, w_ada, name):
    lyr, D, C = w_ada.shape
    tc = _pick(C, [512, 384, 256, 128])

    def body(c_ref, d_ref, w_ref, gw_ref, dc_ref):
        @pl.when((pl.program_id(0) == 0) & (pl.program_id(1) == 0))
        def _():
            dc_ref[...] = jnp.zeros_like(dc_ref)

        cv = c_ref[...]
        act = cv * _sigmoid(cv)
        dv = d_ref[...]
        gw_ref[...] = lax.dot_general(act, dv, TN_DIMS, preferred_element_type=F32, precision=lax.Precision.HIGHEST)
        dc_ref[...] += lax.dot_general(dv, w_ref[...], NT_DIMS, preferred_element_type=F32,
                                       precision=lax.Precision.HIGHEST)

    return pl.pallas_call(
        body, name=name, grid=(lyr, C // tc),
        in_specs=[pl.BlockSpec((16, D), lambda l, j: (0, 0)),
                  pl.BlockSpec((None, 16, tc), lambda l, j: (l, 0, j)),
                  pl.BlockSpec((None, D, tc), lambda l, j: (l, 0, j))],
        out_specs=[pl.BlockSpec((None, D, tc), lambda l, j: (l, 0, j)), pl.BlockSpec((16, D), lambda l, j: (0, 0))],
        out_shape=[jax.ShapeDtypeStruct((lyr, D, C), F32), jax.ShapeDtypeStruct((16, D), F32)],
        compiler_params=_cp(2),
    )(cond, d_out, w_ada)


def _sum_rows(d_rows, name):
    lyr, r, C = d_rows.shape

    def body(d_ref, o_ref):
        o_ref[...] = jnp.sum(d_ref[...], axis=0, keepdims=True)

    return pl.pallas_call(
        body, name=name, grid=(lyr,),
        in_specs=[pl.BlockSpec((None, r, C), lambda l: (l, 0, 0))],
        out_specs=pl.BlockSpec((None, 1, C), lambda l: (l, 0, 0)),
        out_shape=jax.ShapeDtypeStruct((lyr, 1, C), F32), compiler_params=_cp(1),
    )(d_rows)


def _cctx_grad(gathered, c_ctx_row, name):
    D = gathered.shape[1]

    def body(g_ref, c_ref, o_ref):
        acc = g_ref[0:16, :]
        for d in range(1, N_DEV):
            acc = acc + g_ref[16 * d:16 * (d + 1), :]
        cv = c_ref[...]
        sg = _sigmoid(cv)
        o_ref[...] = acc[8:16] * (sg * (1.0 + cv * (1.0 - sg)))

    return pl.pallas_call(
        body, name=name, out_shape=jax.ShapeDtypeStruct((8, D), F32),
        compiler_params=pltpu.CompilerParams(vmem_limit_bytes=VMEM_ELEMENTWISE),
    )(gathered, c_ctx_row)


def _pad_rows(a, rows):
    return jnp.concatenate([a, jnp.zeros((rows - a.shape[0],) + a.shape[1:], a.dtype)], axis=0)


def kernel(x, c, ctx, c_ctx, w_ada, b_ada, attn_w_qkv, attn_w_o, attn_q_gain, attn_k_gain, attn_sink, sc_w_in, sc_conv, sc_w_out, ffn_w_up, ffn_conv, ffn_conv_b, ffn_w_down, loss_target, m_c_ctx, m_w_ada, m_b_ada, m_attn_w_qkv, m_attn_w_o, m_attn_q_gain, m_attn_k_gain, m_attn_sink, m_sc_w_in, m_sc_conv, m_sc_w_out, m_ffn_w_up, m_ffn_conv, m_ffn_conv_b, m_ffn_w_down, v_c_ctx, v_w_ada, v_b_ada, v_attn_w_qkv, v_attn_w_o, v_attn_q_gain, v_attn_k_gain, v_attn_sink, v_sc_w_in, v_sc_conv, v_sc_w_out, v_ffn_w_up, v_ffn_conv, v_ffn_conv_b, v_ffn_w_down):
    T, D = x.shape[1], x.shape[2]
    L = ctx.shape[1]
    S = L + T
    depth = w_ada.shape[0]
    F = ffn_conv_b.shape[1]
    n_q = D // HEAD_DIM
    n_kv = n_q // GROUP
    ada_c = w_ada.shape[2]
    assert L % BLK == 0 and T % BLK == 0 and ada_c * N_DEV == 6 * D

    px, py, pc = _my_pos()
    me = 4 * px + 2 * py + pc
    c_idx = jnp.reshape(pc, (1,)).astype(I32)
    chip_idx = jnp.reshape(2 * px + py, (1,)).astype(I32)

    tm_mm = _pick(S, [768, 704, 384, 256, 128])
    ts_tn = _pick(S, [2112, 1056, 768, 384, 256, 128])

    c_all = _gather_small(_pad_rows(c, 8), "gather_cond")
    cond = jnp.concatenate([c_all[0::8], c_ctx[None, :], jnp.zeros((7, D), F32)], axis=0)
    b_cols = lax.dynamic_slice_in_dim(b_ada, me * ada_c, ada_c, axis=1)[:, None, :]
    ada_mine = _ada_fwd(cond, w_ada, b_cols, "ada_fwd")
    ada_all = _gather_small(ada_mine.reshape(depth * 16, ada_c), "gather_ada")
    ada_all = ada_all.reshape(N_DEV, depth, 16, ada_c)
    ada_rows = jnp.transpose(ada_all, (1, 2, 0, 3)).reshape(depth, 16, 6, D)
    mod_lat = lax.dynamic_index_in_dim(ada_rows, me, axis=1, keepdims=False)
    mods = jnp.stack([ada_rows[:, 8], mod_lat], axis=1)

    def layer_weights(l):
        j = l // 2
        if l % 2 == 0:
            pair = [_cast_layer(attn_w_qkv, j, f"cast_qkv{l}"), _cast_layer(attn_w_o, j, f"cast_wo{l}")]
        else:
            pair = [_cast_layer(sc_w_in, j, f"cast_scin{l}"), _cast_layer(sc_w_out, j, f"cast_scout{l}")]
        pair += [_cast_layer(ffn_w_up, l, f"cast_up{l}"), _cast_layer(ffn_w_down, l, f"cast_down{l}")]
        return _gather_weights(pair, f"gather_weights{l}")

    gathered = [layer_weights(l) for l in range(depth)]
    tabs = _rope_tables(T, L)
    gains = jnp.stack([attn_q_gain, attn_k_gain], axis=1)
    conv_b3 = ffn_conv_b[:, None, :]
    sc_conv_all = _gather_small(_pad_rows(sc_conv.reshape(-1, sc_conv.shape[2]), 8), "gather_scconv")
    ffn_conv_all = _gather_small(_pad_rows(ffn_conv.reshape(-1, ffn_conv.shape[2]), 16), "gather_ffnconv")
    n_sc = sc_conv.shape[0]
    sc_conv_full = jnp.transpose(sc_conv_all.reshape(N_DEV, 8, -1)[:, :n_sc * 3], (1, 0, 2)).reshape(n_sc, 3, D)
    ffn_conv_full = jnp.transpose(ffn_conv_all.reshape(N_DEV, 16, -1)[:, :depth * 3], (1, 0, 2)).reshape(depth, 3, F)

    xs = jnp.concatenate([ctx[0], x[0]], axis=0)
    saved = []
    for l in range(depth):
        j = l // 2
        w_a, w_b, w_up, w_down = gathered[l]
        mod = mods[l]
        h = _norm_mod(xs, mod, 0, L, f"norm_m{l}")
        if l % 2 == 0:
            qkv = _mm_nn(h, w_a, tm=tm_mm, tn=w_a.shape[2], out_dtype=BF16, name=f"qkv{l}")
            qr, kr = _qk_prep(qkv, tabs, gains, j, n_q, n_kv, f"qk_prep{l}")
            z, lse = _attn_fwd(qr, kr, qkv, attn_sink, j, L, f"attn{l}")
            mix = (qkv, qr, kr, lse)
        else:
            u = _mm_nn(h, w_a, tm=tm_mm, tn=w_a.shape[2], out_dtype=BF16, name=f"scin{l}")
            z = _sc_act(u, sc_conv_full, j, L, f"sc_act{l}")
            mix = (u,)
        y_m, x1 = _mm_nn_resid(z, w_b.reshape(D, D), xs, mod, 2, L, tm=tm_mm, tn=_pick(D, [1024, 512]), name=f"mixout{l}")
        h2 = _norm_mod(x1, mod, 3, L, f"norm_f{l}")
        u_f = _mm_nn(h2, w_up, tm=tm_mm, tn=w_up.shape[2], out_dtype=BF16, name=f"up{l}")
        a_f = _ffn_act(u_f, ffn_conv_full, conv_b3, l, L, f"ffn_act{l}")
        y_f, x2 = _mm_nn_resid(a_f, w_down.reshape(F, D), x1, mod, 5, L, tm=tm_mm, tn=_pick(D, [512]), name=f"down{l}")
        saved.append((xs, h, mix, z, y_m, x1, h2, u_f, a_f, y_f))
        xs = x2

    dx, sq = _loss_grad(xs, loss_target[0], L, "loss")
    loss = lax.psum(sq[0, 0], ("x", "y", "c"))

    dmods = [None] * depth
    g_conv_b, g_ffn_conv, g_sc_conv = [None] * depth, [None] * depth, [None] * n_sc
    g_gain, g_sink = [None] * (depth - n_sc), [None] * (depth - n_sc)
    grads_big = [None] * depth
    for l in reversed(range(depth)):
        j = l // 2
        w_a, w_b, w_up, w_down = gathered[l]
        mod = mods[l]
        x0, h, mix, z, y_m, x1, h2, u_f, a_f, y_f = saved[l]
        dy, s_gf = _gate_bwd(dx, y_f, mod, 5, L, f"gate_f_bwd{l}")
        da = _mm_nt(dy, w_down.reshape(F, D), tm=tm_mm, tn=_pick(F, [1408, 512]), out_dtype=BF16, name=f"down_dgrad{l}")
        gw_down = _mm_tn(a_f, dy, nb=1, tka=_pick(F, [1408, 512]), tn=_pick(D, [1024, 512]), ts=ts_tn, name=f"down_wgrad{l}")
        dgate, dval, s_conv = _ffn_act_bwd(u_f, da, ffn_conv_full, conv_b3, l, L, f"ffn_act_bwd{l}")
        du = jnp.concatenate([dgate, dval], axis=1)
        dh2 = _mm_nt_acc(du, w_up, tm=tm_mm, out_dtype=F32, name=f"up_dgrad{l}")
        gw_up = _mm_tn(h2, du, nb=N_DEV, tka=_pick(D, [1024, 512]), tn=w_up.shape[2], ts=ts_tn, name=f"up_wgrad{l}")
        dx1, s_nf = _norm_mod_bwd(dh2, x1, mod, dx, 3, L, f"norm_f_bwd{l}")
        g_ffn_conv[l], g_conv_b[l] = s_conv[0:3], s_conv[3]
        dy, s_gm = _gate_bwd(dx1, y_m, mod, 2, L, f"gate_m_bwd{l}")
        dz = _mm_nt(dy, w_b.reshape(D, D), tm=tm_mm, tn=_pick(D, [1024, 512]), out_dtype=BF16, name=f"mixout_dgrad{l}")
        gw_b = _mm_tn(z, dy, nb=1, tka=_pick(D, [1024, 512]), tn=_pick(D, [1024, 512]), ts=ts_tn, name=f"mixout_wgrad{l}")
        if l % 2 == 0:
            qkv, qr, kr, lse = mix
            dq, delta, dkx, dvx, s_sink = _attn_bwd_q(qr, kr, qkv, z, dz, lse, attn_sink, j, L, f"attn_bwd_q{l}")
            dk, dv = _attn_bwd_kv(qr, kr, qkv, dz, lse, delta, dkx, dvx, L, f"attn_bwd_kv{l}")
            du_m, s_gain = _qk_prep_bwd(dq, dk, dv, qkv, tabs, gains, j, n_q, n_kv, f"qk_prep_bwd{l}")
            g_gain[j], g_sink[j] = s_gain[0:2], s_sink[0]
        else:
            (u,) = mix
            d_gb, d_gc, d_val, s_scconv = _sc_act_bwd(u, dz, sc_conv_full, j, L, f"sc_act_bwd{l}")
            du_m = jnp.concatenate([d_gb, d_gc, d_val], axis=1)
            g_sc_conv[j] = s_scconv[0:3]
        dh = _mm_nt_acc(du_m, w_a, tm=tm_mm, out_dtype=F32, name=f"mixin_dgrad{l}")
        gw_a = _mm_tn(h, du_m, nb=N_DEV, tka=_pick(D, [1024, 512]), tn=w_a.shape[2], ts=ts_tn, name=f"mixin_wgrad{l}")
        dx, s_nm = _norm_mod_bwd(dh, x0, mod, dx1, 0, L, f"norm_m_bwd{l}")
        dmods[l] = jnp.stack([jnp.stack([s_nm[2 * k], s_nm[2 * k + 1], s_gm[k], s_nf[2 * k], s_nf[2 * k + 1], s_gf[k]])
                              for k in range(2)])
        grads_big[l] = [gw_a, gw_b.reshape(N_DEV, -1, D), gw_up, gw_down.reshape(N_DEV, -1, D)]

    grad_x = dx[L:][None]

    big_w = {"qkv": (attn_w_qkv, m_attn_w_qkv, v_attn_w_qkv), "wo": (attn_w_o, m_attn_w_o, v_attn_w_o),
             "scin": (sc_w_in, m_sc_w_in, v_sc_w_in), "scout": (sc_w_out, m_sc_w_out, v_sc_w_out),
             "up": (ffn_w_up, m_ffn_w_up, v_ffn_w_up), "down": (ffn_w_down, m_ffn_w_down, v_ffn_w_down)}
    big_out = {k: [] for k in big_w}
    for l in range(depth):
        j = l // 2
        names = (["qkv", "wo"] if l % 2 == 0 else ["scin", "scout"]) + ["up", "down"]
        idxs = [j, j, l, l]
        recv_a = _rs_sibling(grads_big[l], f"rs_sibling{l}")
        parts = [_chip_partial(g, r, c_idx, f"chip_partial{l}_{n}") for g, r, n in zip(grads_big[l], recv_a, names)]
        recv_b = _rs_chips(parts, f"rs_chips{l}")
        for n, li, p, r in zip(names, idxs, parts, recv_b):
            w, m, v = big_w[n]
            big_out[n].append(_adamw_reduced(p, r, chip_idx, w, m, v, li, f"adamw_{n}{l}"))
    big_res = {k: [jnp.stack([o[t] for o in outs]) for t in range(4)] for k, outs in big_out.items()}

    n_attn = depth - n_sc
    pack = [jnp.stack(dmods)[:, 0].reshape(-1, 128), jnp.stack(dmods)[:, 1].reshape(-1, 128),
            jnp.stack(g_gain).reshape(-1, 128), jnp.stack(g_sink),
            jnp.stack(g_conv_b).reshape(-1, 128), jnp.stack(g_ffn_conv).reshape(-1, 128),
            jnp.stack(g_sc_conv).reshape(-1, 128)]
    used = [p.shape[0] for p in pack]
    pack = [_pad_rows(p, -(-p.shape[0] // 8) * 8) for p in pack]
    sizes = [p.shape[0] for p in pack]
    flat = jnp.concatenate(pack, axis=0)
    rows = flat.shape[0]
    small_all = _gather_small(flat, "gather_small_grads")
    small_sum = _sum8(small_all, rows, "sum_small_grads")
    offs = [sum(sizes[:k]) for k in range(len(sizes))]
    seg = lambda a, k: a[offs[k]:offs[k] + used[k]]
    dmod_ctx = seg(small_sum, 0).reshape(depth, 6 * D)
    dmod_lat = small_all.reshape(N_DEV, rows, 128)[:, offs[1]:offs[1] + used[1]].reshape(N_DEV, depth, 6 * D)
    g_gain_sum = seg(small_sum, 2).reshape(n_attn, 2, 128)
    g_sink_sum = seg(small_sum, 3)[:n_attn, :n_q]
    g_conv_b_sum = seg(small_sum, 4).reshape(depth, F)
    g_ffn_conv_sum = seg(small_sum, 5).reshape(depth, 3, F)
    g_sc_conv_sum = seg(small_sum, 6).reshape(n_sc, 3, D)

    d_rows = jnp.concatenate([jnp.transpose(dmod_lat, (1, 0, 2)), dmod_ctx[:, None, :],
                              jnp.zeros((depth, 7, 6 * D), F32)], axis=1)
    d_cols = lax.dynamic_slice_in_dim(d_rows, me * ada_c, ada_c, axis=2)
    g_w_ada, dcond_part = _ada_bwd(cond, d_cols, w_ada, "ada_bwd")
    dcond_all = _gather_small(dcond_part, "gather_dcond")
    g_c_ctx = _cctx_grad(dcond_all, jnp.broadcast_to(c_ctx[None, :], (8, D)), "cctx_grad")[0]
    g_b_ada = _sum_rows(d_rows, "b_ada_grad")[:, 0]

    def small_adam(w, g, m, v, name):
        w2 = w.reshape(-1, w.shape[-1])
        d, m2, v2 = _adamw_plain(w2, g.reshape(w2.shape), m.reshape(w2.shape), v.reshape(w2.shape), name)
        return g.reshape(w.shape), d.reshape(w.shape), m2.reshape(w.shape), v2.reshape(w.shape)

    g_sc_conv_mine = lax.dynamic_slice_in_dim(g_sc_conv_sum, me * sc_conv.shape[2], sc_conv.shape[2], axis=2)
    g_ffn_conv_mine = lax.dynamic_slice_in_dim(g_ffn_conv_sum, me * ffn_conv.shape[2], ffn_conv.shape[2], axis=2)
    res = {
        "c_ctx": small_adam(c_ctx[None, :], g_c_ctx[None, :], m_c_ctx[None, :], v_c_ctx[None, :], "adamw_c_ctx"),
        "b_ada": small_adam(b_ada, g_b_ada, m_b_ada, v_b_ada, "adamw_b_ada"),
        "attn_q_gain": small_adam(attn_q_gain, g_gain_sum[:, 0], m_attn_q_gain, v_attn_q_gain, "adamw_q_gain"),
        "attn_k_gain": small_adam(attn_k_gain, g_gain_sum[:, 1], m_attn_k_gain, v_attn_k_gain, "adamw_k_gain"),
        "attn_sink": small_adam(attn_sink, g_sink_sum, m_attn_sink, v_attn_sink, "adamw_sink"),
        "sc_conv": small_adam(sc_conv, g_sc_conv_mine, m_sc_conv, v_sc_conv, "adamw_sc_conv"),
        "ffn_conv": small_adam(ffn_conv, g_ffn_conv_mine, m_ffn_conv, v_ffn_conv, "adamw_ffn_conv"),
        "ffn_conv_b": small_adam(ffn_conv_b, g_conv_b_sum, m_ffn_conv_b, v_ffn_conv_b, "adamw_conv_b"),
    }
    res["c_ctx"] = tuple(t[0] for t in res["c_ctx"])
    res["w_ada"] = (g_w_ada,) + tuple(_adamw_tiled(w_ada, g_w_ada, m_w_ada, v_w_ada, "adamw_w_ada"))
    res["attn_w_qkv"], res["attn_w_o"] = big_res["qkv"], big_res["wo"]
    res["sc_w_in"], res["sc_w_out"] = big_res["scin"], big_res["scout"]
    res["ffn_w_up"], res["ffn_w_down"] = big_res["up"], big_res["down"]

    order = ["c_ctx", "w_ada", "b_ada", "attn_w_qkv", "attn_w_o", "attn_q_gain", "attn_k_gain", "attn_sink",
             "sc_w_in", "sc_conv", "sc_w_out", "ffn_w_up", "ffn_conv", "ffn_conv_b", "ffn_w_down"]
    outs = [loss, grad_x]
    for t in range(4):
        outs += [res[n][t] for n in order]
    return tuple(outs)
```

```python
import functools

import jax
import jax.numpy as jnp
from jax import lax
from jax.experimental import pallas as pl
from jax.experimental.pallas import tpu as pltpu

F32 = jnp.float32
BF16 = jnp.bfloat16
I32 = jnp.int32

N_DEV = 8
HEAD_DIM = 128
GROUP = 4
WINDOW = 128
BLK = 128
GRID_W = 64
ROPE_BASE = 10000.0
EPS = 1e-6
NEG = -1e30
HALO = 16

ADAM_LR = 0.001
ADAM_B1 = 0.9
ADAM_B2 = 0.999
ADAM_EPS = 1e-08
ADAM_WD = 0.01
ADAM_STEP = 10

V7X_VMEM_BYTES = 64 << 20
VMEM_MATMUL = 52 << 20
VMEM_ELEMENTWISE = 44 << 20

MESH = pl.DeviceIdType.MESH
ANY = pl.BlockSpec(memory_space=pl.ANY)
HBM = pl.BlockSpec(memory_space=pltpu.HBM)
SEM = pl.BlockSpec(memory_space=pltpu.SEMAPHORE)
EFFECT = pltpu.SideEffectType.DATAFLOW_SIDE_EFFECTING

NT_DIMS = (((1,), (1,)), ((), ()))
TN_DIMS = (((0,), (0,)), ((), ()))


def _pick(n, cands):
    for t in cands:
        if n % t == 0:
            return t
    raise ValueError(f"no tile for {n} in {cands}")


def _cp(n_axes, vmem=VMEM_ELEMENTWISE):
    return pltpu.CompilerParams(dimension_semantics=("arbitrary",) * n_axes, vmem_limit_bytes=vmem)


def _rows(i, tm, off=0):
    return i * tm + off + lax.broadcasted_iota(I32, (tm, 1), 0)


def _my_pos():
    return lax.axis_index("x"), lax.axis_index("y"), lax.axis_index("c")


def _gather_small(x_shard, name):
    m_per, n = x_shard.shape

    def body(x_ref, out_ref, send_sems, recv_sems, local_sem):
        x, y, c = _my_pos()
        me, sibling = (x, y, c), (x, y, 1 - c)
        chips = [(1 - x, y), (x, 1 - y), (1 - x, 1 - y)]

        def rows(px, py, pc):
            return out_ref.at[pl.ds((4 * px + 2 * py + pc) * m_per, m_per), :]

        def copy(k, block, to, src=None):
            return pltpu.make_async_remote_copy(
                src_ref=rows(*block) if src is None else src, dst_ref=rows(*block),
                send_sem=send_sems.at[k], recv_sem=recv_sems.at[k], device_id=to, device_id_type=MESH)

        mine = pltpu.make_async_copy(x_ref, rows(*me), local_sem)
        mine.start()
        first = [copy(0, me, sibling, src=x_ref)]
        first += [copy(1 + j, me, (*chip, c), src=x_ref) for j, chip in enumerate(chips)]
        for cp in first:
            cp.start()
        passed = [copy(4 + j, (*chip, c), sibling) for j, chip in enumerate(chips)]
        for j, chip in enumerate(chips):
            copy(1 + j, (*chip, c), me).wait_recv()
            passed[j].start()
        copy(0, sibling, me).wait_recv()
        for j, chip in enumerate(chips):
            copy(4 + j, (*chip, 1 - c), me).wait_recv()
        for cp in first + passed:
            cp.wait_send()
        mine.wait()

    return pl.pallas_call(
        body, name=name,
        out_shape=jax.ShapeDtypeStruct((N_DEV * m_per, n), x_shard.dtype),
        in_specs=[pl.BlockSpec(memory_space=pltpu.VMEM)],
        out_specs=pl.BlockSpec(memory_space=pltpu.VMEM),
        scratch_shapes=[pltpu.SemaphoreType.DMA((7,)), pltpu.SemaphoreType.DMA((7,)), pltpu.SemaphoreType.DMA],
        compiler_params=pltpu.CompilerParams(vmem_limit_bytes=VMEM_ELEMENTWISE),
    )(x_shard)


def _peer(k):
    x, y, c = _my_pos()
    b = k + 1
    return ((1 - x) if b & 4 else x, (1 - y) if b & 2 else y, (1 - c) if b & 1 else c)


def _slot(p):
    return 4 * p[0] + 2 * p[1] + p[2]


def _in_hbm(a):
    return pltpu.with_memory_space_constraint(a, pltpu.HBM)


def _gather_start(lands, name):
    n = len(lands)

    def body(*refs):
        l_refs, send_sems, recv_sems = refs[:n], refs[n], refs[n + 1]
        token = refs[2 * n + 2]
        me = _slot(_my_pos())
        for a in range(n):
            for k in range(7):
                pltpu.make_async_remote_copy(
                    src_ref=l_refs[a].at[me], dst_ref=l_refs[a].at[me],
                    send_sem=send_sems.at[7 * a + k], recv_sem=recv_sems.at[7 * a + k],
                    device_id=_peer(k), device_id_type=MESH).start()
        token[...] = jnp.zeros_like(token)

    out = pl.pallas_call(
        body, name=name,
        out_shape=(pltpu.SemaphoreType.DMA((7 * n,)), pltpu.SemaphoreType.DMA((7 * n,)),
                   *[pltpu.HBM(a.shape, a.dtype) for a in lands], jax.ShapeDtypeStruct((8, 128), F32)),
        in_specs=[HBM] * n,
        out_specs=(SEM, SEM, *[HBM] * n, pl.BlockSpec(memory_space=pltpu.VMEM)),
        input_output_aliases={a: 2 + a for a in range(n)},
        compiler_params=pltpu.CompilerParams(has_side_effects=EFFECT),
    )(*[_in_hbm(a) for a in lands])
    return out[0], out[1], list(out[2:2 + n]), out[2 + n]


def _gather_wait(lands, send_sems, recv_sems, after, name):
    n = len(lands)

    def body(*refs):
        l_refs, ss, rs = refs[:n], refs[n], refs[n + 1]
        me = _slot(_my_pos())
        for a in range(n):
            for k in range(7):
                cp = pltpu.make_async_remote_copy(
                    src_ref=l_refs[a].at[me], dst_ref=l_refs[a].at[_slot(_peer(k))],
                    send_sem=ss.at[7 * a + k], recv_sem=rs.at[7 * a + k], device_id=_peer(k), device_id_type=MESH)
                cp.wait_send()
                cp.wait_recv()

    out = pl.pallas_call(
        body, name=name,
        out_shape=tuple(pltpu.HBM(a.shape, a.dtype) for a in lands),
        in_specs=[HBM] * n + [SEM, SEM, ANY], out_specs=tuple([HBM] * n),
        input_output_aliases={a: a for a in range(n)},
        compiler_params=pltpu.CompilerParams(has_side_effects=EFFECT),
    )(*lands, send_sems, recv_sems, after)
    return list(out)


def _rs_start(grads, name):
    n = len(grads)

    def body(*refs):
        g_refs, z_refs, send_sems, recv_sems = refs[:n], refs[n:2 * n], refs[2 * n], refs[2 * n + 1]
        token = refs[4 * n + 2]
        for a in range(n):
            for k in range(7):
                pltpu.make_async_remote_copy(
                    src_ref=g_refs[a].at[_slot(_peer(k))], dst_ref=z_refs[a].at[k],
                    send_sem=send_sems.at[7 * a + k], recv_sem=recv_sems.at[7 * a + k],
                    device_id=_peer(k), device_id_type=MESH).start()
        token[...] = jnp.zeros_like(token)

    zones = [lax.empty((7,) + g.shape[1:], g.dtype) for g in grads]
    out = pl.pallas_call(
        body, name=name,
        out_shape=(pltpu.SemaphoreType.DMA((7 * n,)), pltpu.SemaphoreType.DMA((7 * n,)),
                   *[pltpu.HBM(a.shape, a.dtype) for a in grads], *[pltpu.HBM(z.shape, z.dtype) for z in zones],
                   jax.ShapeDtypeStruct((8, 128), F32)),
        in_specs=[HBM] * (2 * n),
        out_specs=(SEM, SEM, *[HBM] * (2 * n), pl.BlockSpec(memory_space=pltpu.VMEM)),
        input_output_aliases={a: 2 + a for a in range(2 * n)},
        compiler_params=pltpu.CompilerParams(has_side_effects=EFFECT),
    )(*[_in_hbm(a) for a in grads], *[_in_hbm(z) for z in zones])
    return out[0], out[1], list(out[2:2 + n]), list(out[2 + n:2 + 2 * n]), out[2 + 2 * n]


def _rs_wait(grads, zones, send_sems, recv_sems, after, name):
    n = len(grads)

    def body(*refs):
        g_refs, z_refs, ss, rs = refs[:n], refs[n:2 * n], refs[2 * n], refs[2 * n + 1]
        for a in range(n):
            for k in range(7):
                cp = pltpu.make_async_remote_copy(
                    src_ref=g_refs[a].at[_slot(_peer(k))], dst_ref=z_refs[a].at[k],
                    send_sem=ss.at[7 * a + k], recv_sem=rs.at[7 * a + k], device_id=_peer(k), device_id_type=MESH)
                cp.wait_send()
                cp.wait_recv()

    out = pl.pallas_call(
        body, name=name,
        out_shape=tuple(pltpu.HBM(a.shape, a.dtype) for a in list(grads) + list(zones)),
        in_specs=[HBM] * (2 * n) + [SEM, SEM, ANY], out_specs=tuple([HBM] * (2 * n)),
        input_output_aliases={a: a for a in range(2 * n)},
        compiler_params=pltpu.CompilerParams(has_side_effects=EFFECT),
    )(*grads, *zones, send_sems, recv_sems, after)
    return list(out[:n]), list(out[n:])


def _cast_layer(w, l, me_idx, name):
    _, r, c = w.shape
    tr = _pick(r, [512, 256, 128, 64, 32, 16])

    def body(me_ref, w_ref, o_ref):
        o_ref[...] = w_ref[...].astype(BF16)

    return pl.pallas_call(
        body, name=name,
        grid_spec=pltpu.PrefetchScalarGridSpec(
            num_scalar_prefetch=1, grid=(r // tr,),
            in_specs=[pl.BlockSpec((None, tr, c), lambda i, me_ref: (l, i, 0))],
            out_specs=pl.BlockSpec((None, tr, c), lambda i, me_ref: (me_ref[0], i, 0))),
        out_shape=jax.ShapeDtypeStruct((N_DEV, r, c), BF16), compiler_params=_cp(1),
    )(me_idx, w)


def _adam_math(w, g, m, v):
    m2 = ADAM_B1 * m + (1.0 - ADAM_B1) * g
    v2 = ADAM_B2 * v + (1.0 - ADAM_B2) * (g * g)
    m_hat = m2 / (1.0 - ADAM_B1 ** ADAM_STEP)
    v_hat = v2 / (1.0 - ADAM_B2 ** ADAM_STEP)
    delta = -ADAM_LR * (m_hat / (jnp.sqrt(v_hat) + ADAM_EPS) + ADAM_WD * w)
    return delta, m2, v2


def _adamw_reduced(own, zone, me_idx, w, m, v, l, name):
    _, r, c = own.shape
    tr = _pick(r, [256, 128, 64, 32, 16])

    def body(me_ref, p_ref, z_ref, w_ref, m_ref, v_ref, g_out, d_out, m_out, v_out):
        g = p_ref[...].astype(F32)
        for k in range(7):
            g = g + z_ref[k].astype(F32)
        d, m2, v2 = _adam_math(w_ref[...], g, m_ref[...], v_ref[...])
        g_out[...] = g
        d_out[...] = d
        m_out[...] = m2
        v_out[...] = v2

    wspec = pl.BlockSpec((None, tr, c), lambda i, me_ref: (l, i, 0))
    ospec = pl.BlockSpec((tr, c), lambda i, me_ref: (i, 0))
    return pl.pallas_call(
        body, name=name,
        grid_spec=pltpu.PrefetchScalarGridSpec(
            num_scalar_prefetch=1, grid=(r // tr,),
            in_specs=[pl.BlockSpec((None, tr, c), lambda i, me_ref: (me_ref[0], i, 0)),
                      pl.BlockSpec((7, tr, c), lambda i, me_ref: (0, i, 0)), wspec, wspec, wspec],
            out_specs=[ospec] * 4),
        out_shape=[jax.ShapeDtypeStruct((r, c), F32)] * 4, compiler_params=_cp(1),
    )(me_idx, own, zone, w, m, v)


def _adamw_plain(w, g, m, v, name):
    def body(w_ref, g_ref, m_ref, v_ref, d_out, m_out, v_out):
        d, m2, v2 = _adam_math(w_ref[...], g_ref[...], m_ref[...], v_ref[...])
        d_out[...] = d
        m_out[...] = m2
        v_out[...] = v2

    return pl.pallas_call(
        body, name=name, out_shape=[jax.ShapeDtypeStruct(w.shape, F32)] * 3,
        compiler_params=pltpu.CompilerParams(vmem_limit_bytes=VMEM_ELEMENTWISE),
    )(w, g, m, v)


def _adamw_tiled(w, g, m, v, name):
    lyr, r, c = w.shape
    tr = _pick(r, [256, 128, 64, 32, 16, 8])

    def body(w_ref, g_ref, m_ref, v_ref, d_out, m_out, v_out):
        d, m2, v2 = _adam_math(w_ref[...], g_ref[...], m_ref[...], v_ref[...])
        d_out[...] = d
        m_out[...] = m2
        v_out[...] = v2

    spec = pl.BlockSpec((None, tr, c), lambda l, i: (l, i, 0))
    return pl.pallas_call(
        body, name=name, grid=(lyr, r // tr), in_specs=[spec] * 4, out_specs=[spec] * 3,
        out_shape=[jax.ShapeDtypeStruct(w.shape, F32)] * 3, compiler_params=_cp(2),
    )(w, g, m, v)


def _sum8(gathered, rows, name):
    def body(g_ref, o_ref):
        acc = g_ref[0:rows, :]
        for d in range(1, N_DEV):
            acc = acc + g_ref[d * rows:(d + 1) * rows, :]
        o_ref[...] = acc

    return pl.pallas_call(
        body, name=name, out_shape=jax.ShapeDtypeStruct((rows, 128), F32),
        compiler_params=pltpu.CompilerParams(vmem_limit_bytes=VMEM_ELEMENTWISE),
    )(gathered)


def _mm_nn(a, b3, *, tm, tn, out_dtype, name):
    M, K = a.shape
    nb, _, nc = b3.shape
    q = nc // tn

    def body(a_ref, b_ref, o_ref):
        o_ref[...] = jnp.dot(a_ref[...], b_ref[...], preferred_element_type=F32).astype(o_ref.dtype)

    return pl.pallas_call(
        body, name=name, grid=(nb * q, M // tm),
        in_specs=[pl.BlockSpec((tm, K), lambda j, i: (i, 0)),
                  pl.BlockSpec((None, K, tn), lambda j, i: (j // q, 0, j % q))],
        out_specs=pl.BlockSpec((tm, tn), lambda j, i: (i, j)),
        out_shape=jax.ShapeDtypeStruct((M, nb * nc), out_dtype), compiler_params=_cp(2, VMEM_MATMUL),
    )(a, b3)


def _mm_nn_resid(a, b2, x_old, mod, gate_row, n_ctx, *, tm, tn, name):
    M, K = a.shape
    N = b2.shape[1]

    def body(a_ref, b_ref, x_ref, mod_ref, y_ref, xn_ref):
        y = jnp.dot(a_ref[...], b_ref[...], preferred_element_type=F32)
        is_ctx = _rows(pl.program_id(1), tm) < n_ctx
        g = jnp.where(is_ctx, mod_ref[0, gate_row:gate_row + 1, :], mod_ref[1, gate_row:gate_row + 1, :])
        y_ref[...] = y.astype(BF16)
        xn_ref[...] = x_ref[...] + g * y

    return pl.pallas_call(
        body, name=name, grid=(N // tn, M // tm),
        in_specs=[pl.BlockSpec((tm, K), lambda j, i: (i, 0)),
                  pl.BlockSpec((K, tn), lambda j, i: (0, j)),
                  pl.BlockSpec((tm, tn), lambda j, i: (i, j)),
                  pl.BlockSpec((2, 6, tn), lambda j, i: (0, 0, j))],
        out_specs=[pl.BlockSpec((tm, tn), lambda j, i: (i, j))] * 2,
        out_shape=[jax.ShapeDtypeStruct((M, N), BF16), jax.ShapeDtypeStruct((M, N), F32)],
        compiler_params=_cp(2, VMEM_MATMUL),
    )(a, b2, x_old, mod)


def _mm_nt_acc(dy, w3, *, tm, out_dtype, name):
    M = dy.shape[0]
    nb, K, nc = w3.shape

    def body(dy_ref, w_ref, o_ref, acc_ref):
        s = pl.program_id(1)

        @pl.when(s == 0)
        def _():
            acc_ref[...] = jnp.zeros_like(acc_ref)

        acc_ref[...] += lax.dot_general(dy_ref[...], w_ref[...], NT_DIMS, preferred_element_type=F32)

        @pl.when(s == nb - 1)
        def _():
            o_ref[...] = acc_ref[...].astype(o_ref.dtype)

    return pl.pallas_call(
        body, name=name, grid=(M // tm, nb),
        in_specs=[pl.BlockSpec((tm, nc), lambda i, s: (i, s)),
                  pl.BlockSpec((None, K, nc), lambda i, s: (s, 0, 0))],
        out_specs=pl.BlockSpec((tm, K), lambda i, s: (i, 0)),
        out_shape=jax.ShapeDtypeStruct((M, K), out_dtype),
        scratch_shapes=[pltpu.VMEM((tm, K), F32)], compiler_params=_cp(2, VMEM_MATMUL),
    )(dy, w3)


def _mm_nt(dy, w2, *, tm, tn, out_dtype, name):
    M, N = dy.shape
    K = w2.shape[0]

    def body(dy_ref, w_ref, o_ref):
        o_ref[...] = lax.dot_general(dy_ref[...], w_ref[...], NT_DIMS,
                                     preferred_element_type=F32).astype(o_ref.dtype)

    return pl.pallas_call(
        body, name=name, grid=(K // tn, M // tm),
        in_specs=[pl.BlockSpec((tm, N), lambda j, i: (i, 0)),
                  pl.BlockSpec((tn, N), lambda j, i: (j, 0))],
        out_specs=pl.BlockSpec((tm, tn), lambda j, i: (i, j)),
        out_shape=jax.ShapeDtypeStruct((M, K), out_dtype), compiler_params=_cp(2, VMEM_MATMUL),
    )(dy, w2)


def _mm_tn(a, dy, *, nb, tka, tn, ts, name):
    S, Ka = a.shape
    N = dy.shape[1]
    nc = N // nb
    q = nc // tn
    nk = S // ts

    def body(a_ref, dy_ref, o_ref, acc_ref):
        k = pl.program_id(2)

        @pl.when(k == 0)
        def _():
            acc_ref[...] = jnp.zeros_like(acc_ref)

        acc_ref[...] += lax.dot_general(a_ref[...], dy_ref[...], TN_DIMS, preferred_element_type=F32)

        @pl.when(k == nk - 1)
        def _():
            o_ref[...] = acc_ref[...].astype(o_ref.dtype)

    return pl.pallas_call(
        body, name=name, grid=(nb * q, Ka // tka, nk),
        in_specs=[pl.BlockSpec((ts, tka), lambda j, ia, k: (k, ia)),
                  pl.BlockSpec((ts, tn), lambda j, ia, k: (k, j))],
        out_specs=pl.BlockSpec((None, tka, tn), lambda j, ia, k: (j // q, ia, j % q)),
        out_shape=jax.ShapeDtypeStruct((nb, Ka, nc), BF16),
        scratch_shapes=[pltpu.VMEM((tka, tn), F32)], compiler_params=_cp(3, VMEM_MATMUL),
    )(a, dy)


def _norm_mod(x, mod, row0, n_ctx, name):
    S, D = x.shape
    tm = _pick(S, [256, 128])

    def body(x_ref, mod_ref, h_ref):
        xv = x_ref[...]
        r = lax.rsqrt(jnp.mean(xv * xv, axis=-1, keepdims=True) + EPS)
        is_ctx = _rows(pl.program_id(0), tm) < n_ctx
        sh = jnp.where(is_ctx, mod_ref[0, row0:row0 + 1, :], mod_ref[1, row0:row0 + 1, :])
        sc = jnp.where(is_ctx, mod_ref[0, row0 + 1:row0 + 2, :], mod_ref[1, row0 + 1:row0 + 2, :])
        h_ref[...] = (xv * r * (1.0 + sc) + sh).astype(BF16)

    return pl.pallas_call(
        body, name=name, grid=(S // tm,),
        in_specs=[pl.BlockSpec((tm, D), lambda i: (i, 0)), pl.BlockSpec((2, 6, D), lambda i: (0, 0, 0))],
        out_specs=pl.BlockSpec((tm, D), lambda i: (i, 0)),
        out_shape=jax.ShapeDtypeStruct((S, D), BF16), compiler_params=_cp(1),
    )(x, mod)


def _norm_mod_bwd(dh, x, mod, dx_res, row0, n_ctx, name):
    S, D = x.shape
    tm = _pick(S, [256, 128])

    def body(dh_ref, x_ref, mod_ref, res_ref, dx_ref, acc_ref):
        i = pl.program_id(0)

        @pl.when(i == 0)
        def _():
            acc_ref[...] = jnp.zeros_like(acc_ref)

        xv = x_ref[...]
        dh_v = dh_ref[...].astype(F32)
        r = lax.rsqrt(jnp.mean(xv * xv, axis=-1, keepdims=True) + EPS)
        xhat = xv * r
        is_ctx = _rows(i, tm) < n_ctx
        sc = jnp.where(is_ctx, mod_ref[0, row0 + 1:row0 + 2, :], mod_ref[1, row0 + 1:row0 + 2, :])
        dxhat = dh_v * (1.0 + sc)
        dx_ref[...] = res_ref[...] + r * (dxhat - xhat * jnp.mean(dxhat * xhat, axis=-1, keepdims=True))
        dsc = dh_v * xhat
        zero = jnp.zeros_like(dh_v)
        sums = [jnp.sum(jnp.where(is_ctx, dh_v, zero), axis=0, keepdims=True),
                jnp.sum(jnp.where(is_ctx, dsc, zero), axis=0, keepdims=True),
                jnp.sum(jnp.where(is_ctx, zero, dh_v), axis=0, keepdims=True),
                jnp.sum(jnp.where(is_ctx, zero, dsc), axis=0, keepdims=True)]
        rid = lax.broadcasted_iota(I32, (8, D), 0)
        upd = jnp.zeros((8, D), F32)
        for k, s in enumerate(sums):
            upd = upd + jnp.where(rid == k, s, 0.0)
        acc_ref[...] += upd

    return pl.pallas_call(
        body, name=name, grid=(S // tm,),
        in_specs=[pl.BlockSpec((tm, D), lambda i: (i, 0)), pl.BlockSpec((tm, D), lambda i: (i, 0)),
                  pl.BlockSpec((2, 6, D), lambda i: (0, 0, 0)), pl.BlockSpec((tm, D), lambda i: (i, 0))],
        out_specs=[pl.BlockSpec((tm, D), lambda i: (i, 0)), pl.BlockSpec((8, D), lambda i: (0, 0))],
        out_shape=[jax.ShapeDtypeStruct((S, D), F32), jax.ShapeDtypeStruct((8, D), F32)],
        compiler_params=_cp(1),
    )(dh, x, mod, dx_res)


def _gate_bwd(dx, y, mod, gate_row, n_ctx, name):
    S, D = dx.shape
    tm = _pick(S, [256, 128])

    def body(dx_ref, y_ref, mod_ref, dy_ref, acc_ref):
        i = pl.program_id(0)

        @pl.when(i == 0)
        def _():
            acc_ref[...] = jnp.zeros_like(acc_ref)

        dxv = dx_ref[...]
        is_ctx = _rows(i, tm) < n_ctx
        g = jnp.where(is_ctx, mod_ref[0, gate_row:gate_row + 1, :], mod_ref[1, gate_row:gate_row + 1, :])
        dy_ref[...] = (g * dxv).astype(BF16)
        prod = dxv * y_ref[...].astype(F32)
        zero = jnp.zeros_like(prod)
        s_ctx = jnp.sum(jnp.where(is_ctx, prod, zero), axis=0, keepdims=True)
        s_lat = jnp.sum(jnp.where(is_ctx, zero, prod), axis=0, keepdims=True)
        rid = lax.broadcasted_iota(I32, (8, D), 0)
        acc_ref[...] += jnp.where(rid == 0, s_ctx, 0.0) + jnp.where(rid == 1, s_lat, 0.0)

    return pl.pallas_call(
        body, name=name, grid=(S // tm,),
        in_specs=[pl.BlockSpec((tm, D), lambda i: (i, 0)), pl.BlockSpec((tm, D), lambda i: (i, 0)),
                  pl.BlockSpec((2, 6, D), lambda i: (0, 0, 0))],
        out_specs=[pl.BlockSpec((tm, D), lambda i: (i, 0)), pl.BlockSpec((8, D), lambda i: (0, 0))],
        out_shape=[jax.ShapeDtypeStruct((S, D), BF16), jax.ShapeDtypeStruct((8, D), F32)],
        compiler_params=_cp(1),
    )(dx, y, mod)


def _loss_grad(x, target, n_ctx, name):
    S, D = x.shape
    tm = _pick(n_ctx, [256, 128])
    nct = n_ctx // tm

    def body(x_ref, t_ref, dx_ref, tot_ref, acc_ref):
        i = pl.program_id(0)

        @pl.when(i == 0)
        def _():
            acc_ref[...] = jnp.zeros_like(acc_ref)

        @pl.when(i < nct)
        def _():
            dx_ref[...] = jnp.zeros_like(dx_ref)

        @pl.when(i >= nct)
        def _():
            err = x_ref[...] - t_ref[...]
            dx_ref[...] = err * (1.0 / D)
            acc_ref[...] += jnp.sum(err * err, axis=0, keepdims=True)

        @pl.when(i == S // tm - 1)
        def _():
            tot = jnp.sum(acc_ref[...], axis=1, keepdims=True) * (0.5 / D)
            tot_ref[...] = jnp.broadcast_to(tot, tot_ref.shape)

    return pl.pallas_call(
        body, name=name, grid=(S // tm,),
        in_specs=[pl.BlockSpec((tm, D), lambda i: (i, 0)),
                  pl.BlockSpec((tm, D), lambda i: (jnp.maximum(i - nct, 0), 0))],
        out_specs=[pl.BlockSpec((tm, D), lambda i: (i, 0)), pl.BlockSpec((1, 128), lambda i: (0, 0))],
        out_shape=[jax.ShapeDtypeStruct((S, D), F32), jax.ShapeDtypeStruct((1, 128), F32)],
        scratch_shapes=[pltpu.VMEM((1, D), F32)], compiler_params=_cp(1),
    )(x, target)


def _halo_specs(tm, tc, S, col_off):
    per = tm // HALO
    last = S // HALO - 1
    return [pl.BlockSpec((HALO, tc), lambda j, i: (jnp.maximum(i * per - 1, 0), j + col_off)),
            pl.BlockSpec((tm, tc), lambda j, i: (i, j + col_off)),
            pl.BlockSpec((HALO, tc), lambda j, i: (jnp.minimum((i + 1) * per, last), j + col_off))]


def _ext(p_ref, m_ref, n_ref):
    return jnp.concatenate([p_ref[...], m_ref[...], n_ref[...]], axis=0).astype(F32)


def _links(i, tm, S, n_ctx):
    n = tm + 2 * HALO
    rid = i * tm - HALO + lax.broadcasted_iota(I32, (n, 1), 0)
    has_prev = (rid != 0) & (rid != n_ctx)
    has_next = (rid != n_ctx - 1) & (rid != S - 1)
    return has_prev, has_next


def _up(x):
    return pltpu.roll(x, 1, 0)


def _dn(x):
    return pltpu.roll(x, x.shape[0] - 1, 0)


def _conv3(x, w_ref, has_prev, has_next):
    return (w_ref[0:1, :] * jnp.where(has_prev, _up(x), 0.0) + w_ref[1:2, :] * x
            + w_ref[2:3, :] * jnp.where(has_next, _dn(x), 0.0))


def _conv3_t(d, w_ref, has_prev, has_next):
    return (w_ref[0:1, :] * jnp.where(has_next, _dn(d), 0.0) + w_ref[1:2, :] * d
            + w_ref[2:3, :] * jnp.where(has_prev, _up(d), 0.0))


def _conv3_wgrad(d, x, has_prev, has_next, extra=None):
    c = slice(HALO, d.shape[0] - HALO)
    taps = [jnp.where(has_prev, _up(x), 0.0), x, jnp.where(has_next, _dn(x), 0.0)]
    sums = [jnp.sum((d * t)[c], axis=0, keepdims=True) for t in taps]
    if extra is not None:
        sums.append(jnp.sum(extra[c], axis=0, keepdims=True))
    rid = lax.broadcasted_iota(I32, (8, d.shape[1]), 0)
    upd = jnp.zeros((8, d.shape[1]), F32)
    for k, s in enumerate(sums):
        upd = upd + jnp.where(rid == k, s, 0.0)
    return upd


def _sigmoid(x):
    return 1.0 / (1.0 + jnp.exp(-x))


def _ffn_act(u, conv_w, conv_b, l, n_ctx, name):
    S, F2 = u.shape
    F = F2 // 2
    tm = _pick(S, [384, 256, 128])
    tc = _pick(F, [1408, 512, 256, 128])
    nj = F // tc

    def body(gp, gm, gn, v_ref, w_ref, b_ref, a_ref):
        has_prev, has_next = _links(pl.program_id(1), tm, S, n_ctx)
        gc = _conv3(_ext(gp, gm, gn), w_ref, has_prev, has_next)[HALO:HALO + tm] + b_ref[...]
        a_ref[...] = (gc * _sigmoid(gc) * v_ref[...].astype(F32)).astype(BF16)

    return pl.pallas_call(
        body, name=name, grid=(nj, S // tm),
        in_specs=_halo_specs(tm, tc, S, 0) + [
            pl.BlockSpec((tm, tc), lambda j, i: (i, j + nj)),
            pl.BlockSpec((None, 3, tc), lambda j, i: (l, 0, j)),
            pl.BlockSpec((None, 1, tc), lambda j, i: (l, 0, j))],
        out_specs=pl.BlockSpec((tm, tc), lambda j, i: (i, j)),
        out_shape=jax.ShapeDtypeStruct((S, F), BF16), compiler_params=_cp(2),
    )(u, u, u, u, conv_w, conv_b)


def _ffn_act_bwd(u, da, conv_w, conv_b, l, n_ctx, name):
    S, F2 = u.shape
    F = F2 // 2
    tm = _pick(S, [384, 256, 128])
    tc = _pick(F, [1408, 512, 256, 128])
    nj = F // tc

    def body(gp, gm, gn, vp, vm, vn, dp, dm, dn_, w_ref, b_ref, dg_ref, dv_ref, acc_ref):
        i = pl.program_id(1)

        @pl.when(i == 0)
        def _():
            acc_ref[...] = jnp.zeros_like(acc_ref)

        has_prev, has_next = _links(i, tm, S, n_ctx)
        g = _ext(gp, gm, gn)
        val = _ext(vp, vm, vn)
        d_a = _ext(dp, dm, dn_)
        gc = _conv3(g, w_ref, has_prev, has_next) + b_ref[...]
        sg = _sigmoid(gc)
        dgc = d_a * val * (sg * (1.0 + gc * (1.0 - sg)))
        c = slice(HALO, HALO + tm)
        dv_ref[...] = (d_a * gc * sg)[c].astype(BF16)
        dg_ref[...] = _conv3_t(dgc, w_ref, has_prev, has_next)[c].astype(BF16)
        acc_ref[...] += _conv3_wgrad(dgc, g, has_prev, has_next, extra=dgc)

    return pl.pallas_call(
        body, name=name, grid=(nj, S // tm),
        in_specs=_halo_specs(tm, tc, S, 0) + _halo_specs(tm, tc, S, nj) + _halo_specs(tm, tc, S, 0) + [
            pl.BlockSpec((None, 3, tc), lambda j, i: (l, 0, j)),
            pl.BlockSpec((None, 1, tc), lambda j, i: (l, 0, j))],
        out_specs=[pl.BlockSpec((tm, tc), lambda j, i: (i, j))] * 2 + [pl.BlockSpec((8, tc), lambda j, i: (0, j))],
        out_shape=[jax.ShapeDtypeStruct((S, F), BF16)] * 2 + [jax.ShapeDtypeStruct((8, F), F32)],
        compiler_params=_cp(2),
    )(u, u, u, u, u, u, da, da, da, conv_w, conv_b)


def _sc_act(u, conv_w, l, n_ctx, name):
    S, D3 = u.shape
    D = D3 // 3
    tm = _pick(S, [384, 256, 128])
    tc = _pick(D, [1024, 512, 256, 128])
    nj = D // tc

    def body(b_ref, cp, cm, cn, vp, vm, vn, w_ref, z_ref):
        has_prev, has_next = _links(pl.program_id(1), tm, S, n_ctx)
        t = _ext(cp, cm, cn) * _ext(vp, vm, vn)
        cv = _conv3(t, w_ref, has_prev, has_next)[HALO:HALO + tm]
        z_ref[...] = (b_ref[...].astype(F32) * cv).astype(BF16)

    return pl.pallas_call(
        body, name=name, grid=(nj, S // tm),
        in_specs=[pl.BlockSpec((tm, tc), lambda j, i: (i, j))] + _halo_specs(tm, tc, S, nj)
        + _halo_specs(tm, tc, S, 2 * nj) + [pl.BlockSpec((None, 3, tc), lambda j, i: (l, 0, j))],
        out_specs=pl.BlockSpec((tm, tc), lambda j, i: (i, j)),
        out_shape=jax.ShapeDtypeStruct((S, D), BF16), compiler_params=_cp(2),
    )(u, u, u, u, u, u, u, conv_w)


def _sc_act_bwd(u, dz, conv_w, l, n_ctx, name):
    S, D3 = u.shape
    D = D3 // 3
    tm = _pick(S, [384, 256, 128])
    tc = _pick(D, [1024, 512, 256, 128])
    nj = D // tc

    def body(bp, bm, bn, cp, cm, cn, vp, vm, vn, zp, zm, zn, w_ref, db_ref, dc_ref, dv_ref, acc_ref):
        i = pl.program_id(1)

        @pl.when(i == 0)
        def _():
            acc_ref[...] = jnp.zeros_like(acc_ref)

        has_prev, has_next = _links(i, tm, S, n_ctx)
        gb = _ext(bp, bm, bn)
        gcv = _ext(cp, cm, cn)
        val = _ext(vp, vm, vn)
        d_z = _ext(zp, zm, zn)
        t = gcv * val
        c = slice(HALO, HALO + tm)
        db_ref[...] = (d_z * _conv3(t, w_ref, has_prev, has_next))[c].astype(BF16)
        dcv = d_z * gb
        dt = _conv3_t(dcv, w_ref, has_prev, has_next)
        dc_ref[...] = (dt * val)[c].astype(BF16)
        dv_ref[...] = (dt * gcv)[c].astype(BF16)
        acc_ref[...] += _conv3_wgrad(dcv, t, has_prev, has_next)

    return pl.pallas_call(
        body, name=name, grid=(nj, S // tm),
        in_specs=_halo_specs(tm, tc, S, 0) + _halo_specs(tm, tc, S, nj) + _halo_specs(tm, tc, S, 2 * nj)
        + _halo_specs(tm, tc, S, 0) + [pl.BlockSpec((None, 3, tc), lambda j, i: (l, 0, j))],
        out_specs=[pl.BlockSpec((tm, tc), lambda j, i: (i, j))] * 3 + [pl.BlockSpec((8, tc), lambda j, i: (0, j))],
        out_shape=[jax.ShapeDtypeStruct((S, D), BF16)] * 3 + [jax.ShapeDtypeStruct((8, D), F32)],
        compiler_params=_cp(2),
    )(u, u, u, u, u, u, u, u, u, dz, dz, dz, conv_w)


def _rope_tables(T, n_ctx):
    rows = T // GRID_W
    pairs = HEAD_DIM // 4
    row = jnp.repeat(jnp.arange(rows), GRID_W).astype(F32)
    col = jnp.tile(jnp.arange(GRID_W), rows).astype(F32)
    inv = ROPE_BASE ** (-jnp.arange(pairs, dtype=F32) / pairs)
    ang = jnp.concatenate([row[:, None] * inv, row[:, None] * inv, col[:, None] * inv, col[:, None] * inv], axis=1)
    cos, sin = jnp.cos(ang), jnp.sin(ang)
    first = (jnp.arange(HEAD_DIM) % (2 * pairs)) < pairs
    sin_a = jnp.where(first, -sin, 0.0)
    sin_b = jnp.where(first, 0.0, sin)
    pad = jnp.zeros((n_ctx, HEAD_DIM), F32)
    return (jnp.concatenate([pad + 1.0, cos], axis=0), jnp.concatenate([pad, sin_a], axis=0),
            jnp.concatenate([pad, sin_b], axis=0))


def _qk_prep(qkv, tabs, gains, l, n_q, n_kv, name):
    S = qkv.shape[0]
    W = qkv.shape[1]
    tm = _pick(S, [256, 128])
    cos, sin_a, sin_b = tabs

    def body(x_ref, cos_ref, sa_ref, sb_ref, g_ref, q_ref, k_ref):
        cs, sa, sb = cos_ref[...], sa_ref[...], sb_ref[...]
        for h in range(n_q + n_kv):
            xv = x_ref[:, h * 128:(h + 1) * 128].astype(F32)
            r = lax.rsqrt(jnp.mean(xv * xv, axis=-1, keepdims=True) + EPS)
            gain = g_ref[0:1, :] if h < n_q else g_ref[1:2, :]
            y = xv * r * gain
            out = (y * cs + pltpu.roll(y, 96, 1) * sa + pltpu.roll(y, 32, 1) * sb).astype(BF16)
            if h < n_q:
                q_ref[:, h * 128:(h + 1) * 128] = out
            else:
                k_ref[:, (h - n_q) * 128:(h - n_q + 1) * 128] = out

    tspec = pl.BlockSpec((tm, 128), lambda i: (i, 0))
    return pl.pallas_call(
        body, name=name, grid=(S // tm,),
        in_specs=[pl.BlockSpec((tm, W), lambda i: (i, 0)), tspec, tspec, tspec,
                  pl.BlockSpec((None, 2, 128), lambda i: (l, 0, 0))],
        out_specs=[pl.BlockSpec((tm, n_q * 128), lambda i: (i, 0)), pl.BlockSpec((tm, n_kv * 128), lambda i: (i, 0))],
        out_shape=[jax.ShapeDtypeStruct((S, n_q * 128), BF16), jax.ShapeDtypeStruct((S, n_kv * 128), BF16)],
        compiler_params=_cp(1),
    )(qkv, cos, sin_a, sin_b, gains)


def _qk_prep_bwd(dq, dk, dv, qkv, tabs, gains, l, n_q, n_kv, name):
    S, W = qkv.shape
    tm = _pick(S, [256, 128])
    cos, sin_a, sin_b = tabs

    def body(dq_ref, dk_ref, dv_ref, x_ref, cos_ref, sa_ref, sb_ref, g_ref, o_ref, acc_ref):
        @pl.when(pl.program_id(0) == 0)
        def _():
            acc_ref[...] = jnp.zeros_like(acc_ref)

        cs, sa, sb = cos_ref[...], sa_ref[...], sb_ref[...]
        dgq = jnp.zeros((1, 128), F32)
        dgk = jnp.zeros((1, 128), F32)
        for h in range(n_q + n_kv):
            if h < n_q:
                d_out = dq_ref[:, h * 128:(h + 1) * 128]
                gain = g_ref[0:1, :]
            else:
                d_out = dk_ref[:, (h - n_q) * 128:(h - n_q + 1) * 128]
                gain = g_ref[1:2, :]
            dy = d_out * cs + pltpu.roll(d_out * sa, 32, 1) + pltpu.roll(d_out * sb, 96, 1)
            xv = x_ref[:, h * 128:(h + 1) * 128].astype(F32)
            r = lax.rsqrt(jnp.mean(xv * xv, axis=-1, keepdims=True) + EPS)
            xhat = xv * r
            dg = jnp.sum(dy * xhat, axis=0, keepdims=True)
            if h < n_q:
                dgq = dgq + dg
            else:
                dgk = dgk + dg
            dxhat = dy * gain
            dx = r * (dxhat - xhat * jnp.mean(dxhat * xhat, axis=-1, keepdims=True))
            o_ref[:, h * 128:(h + 1) * 128] = dx.astype(BF16)
        v0 = (n_q + n_kv) * 128
        o_ref[:, v0:] = dv_ref[...].astype(BF16)
        rid = lax.broadcasted_iota(I32, (8, 128), 0)
        acc_ref[...] += jnp.where(rid == 0, dgq, 0.0) + jnp.where(rid == 1, dgk, 0.0)

    tspec = pl.BlockSpec((tm, 128), lambda i: (i, 0))
    return pl.pallas_call(
        body, name=name, grid=(S // tm,),
        in_specs=[pl.BlockSpec((tm, n_q * 128), lambda i: (i, 0)), pl.BlockSpec((tm, n_kv * 128), lambda i: (i, 0)),
                  pl.BlockSpec((tm, n_kv * 128), lambda i: (i, 0)), pl.BlockSpec((tm, W), lambda i: (i, 0)),
                  tspec, tspec, tspec, pl.BlockSpec((None, 2, 128), lambda i: (l, 0, 0))],
        out_specs=[pl.BlockSpec((tm, W), lambda i: (i, 0)), pl.BlockSpec((8, 128), lambda i: (0, 0))],
        out_shape=[jax.ShapeDtypeStruct((S, W), BF16), jax.ShapeDtypeStruct((8, 128), F32)],
        compiler_params=_cp(1),
    )(dq, dk, dv, qkv, cos, sin_a, sin_b, gains)


def _band_specs(width, col, nb, n_ctx):
    return [pl.BlockSpec((BLK, width), lambda i: (jnp.maximum(i - 1, 0), col)),
            pl.BlockSpec((BLK, width), lambda i: (i, col)),
            pl.BlockSpec((BLK, width), lambda i: (jnp.minimum(i + 1, nb - 1), col)),
            pl.BlockSpec((n_ctx, width), lambda i: (0, col))]


def _q_side_mask(i, S, n_ctx):
    shape = (GROUP * BLK, 3 * BLK + n_ctx)
    a = lax.broadcasted_iota(I32, shape, 0) & (BLK - 1)
    kk = lax.broadcasted_iota(I32, shape, 1)
    rq = i * BLK + a
    rk = (i - 1) * BLK + kk
    band = (rq >= n_ctx) & (rk >= n_ctx) & (rk < S) & (jnp.abs(rq - rk) <= WINDOW)
    return (kk >= 3 * BLK) | band


def _stack_heads(ref, g):
    return jnp.concatenate([ref[:, (GROUP * g + hh) * 128:(GROUP * g + hh + 1) * 128] for hh in range(GROUP)], axis=0)


def _stack_cols(ref, g):
    return jnp.concatenate([ref[:, GROUP * g + hh:GROUP * g + hh + 1] for hh in range(GROUP)], axis=0)


def _sink_col(sink_ref, l, g):
    return jnp.concatenate([jnp.full((BLK, 1), sink_ref[l, GROUP * g + hh], F32) for hh in range(GROUP)], axis=0)


def _attn_fwd(q, k, qkv, sink, l, n_ctx, name):
    S, DQ = q.shape
    DK = k.shape[1]
    n_kv = DK // 128
    nb = S // BLK
    vcol = (DQ + DK) // DK
    scale = HEAD_DIM ** -0.5

    def body(sink_ref, q_ref, kp, kc, kn, kx, vp, vc, vn, vx, o_ref, lse_ref):
        i = pl.program_id(0)
        mask = _q_side_mask(i, S, n_ctx)
        lane = lax.broadcasted_iota(I32, (BLK, 128), 1)
        lse_tile = jnp.zeros((BLK, 128), F32)
        for g in range(n_kv):
            sl = slice(g * 128, (g + 1) * 128)
            kcat = jnp.concatenate([kp[:, sl], kc[:, sl], kn[:, sl], kx[:, sl]], axis=0)
            vcat = jnp.concatenate([vp[:, sl], vc[:, sl], vn[:, sl], vx[:, sl]], axis=0)
            s = lax.dot_general(_stack_heads(q_ref, g), kcat, NT_DIMS, preferred_element_type=F32) * scale
            s = jnp.where(mask, s, NEG)
            sk = _sink_col(sink_ref, l, g)
            m = jnp.maximum(jnp.max(s, axis=1, keepdims=True), sk)
            e = jnp.exp(s - m)
            den = jnp.sum(e, axis=1, keepdims=True) + jnp.exp(sk - m)
            p = (e / den).astype(BF16)
            o = jnp.dot(p, vcat, preferred_element_type=F32)
            lse = m + jnp.log(den)
            for hh in range(GROUP):
                h = GROUP * g + hh
                o_ref[:, h * 128:(h + 1) * 128] = o[hh * BLK:(hh + 1) * BLK].astype(BF16)
                lse_tile = jnp.where(lane == h, lse[hh * BLK:(hh + 1) * BLK], lse_tile)
        lse_ref[...] = lse_tile

    return pl.pallas_call(
        body, name=name, grid=(nb,),
        in_specs=[pl.BlockSpec(memory_space=pltpu.SMEM), pl.BlockSpec((BLK, DQ), lambda i: (i, 0))]
        + _band_specs(DK, 0, nb, n_ctx) + _band_specs(DK, vcol, nb, n_ctx),
        out_specs=[pl.BlockSpec((BLK, DQ), lambda i: (i, 0)), pl.BlockSpec((BLK, 128), lambda i: (i, 0))],
        out_shape=[jax.ShapeDtypeStruct((S, DQ), BF16), jax.ShapeDtypeStruct((S, 128), F32)],
        compiler_params=_cp(1),
    )(sink, q, k, k, k, k, qkv, qkv, qkv, qkv)


def _attn_bwd_q(q, k, qkv, o, do, lse, sink, l, n_ctx, name):
    S, DQ = q.shape
    DK = k.shape[1]
    n_kv = DK // 128
    nb = S // BLK
    vcol = (DQ + DK) // DK
    scale = HEAD_DIM ** -0.5

    def body(sink_ref, q_ref, kp, kc, kn, kx, vp, vc, vn, vx, o_ref, do_ref, lse_ref,
             dq_ref, delta_ref, dkx_ref, dvx_ref, dsink_ref):
        i = pl.program_id(0)

        @pl.when(i == 0)
        def _():
            dkx_ref[...] = jnp.zeros_like(dkx_ref)
            dvx_ref[...] = jnp.zeros_like(dvx_ref)
            dsink_ref[...] = jnp.zeros_like(dsink_ref)

        mask = _q_side_mask(i, S, n_ctx)
        lane = lax.broadcasted_iota(I32, (BLK, 128), 1)
        lane8 = lax.broadcasted_iota(I32, (8, 128), 1)
        row8 = lax.broadcasted_iota(I32, (8, 128), 0)
        delta_tile = jnp.zeros((BLK, 128), F32)
        dsink_upd = jnp.zeros((8, 128), F32)
        for g in range(n_kv):
            sl = slice(g * 128, (g + 1) * 128)
            kcat = jnp.concatenate([kp[:, sl], kc[:, sl], kn[:, sl], kx[:, sl]], axis=0)
            vcat = jnp.concatenate([vp[:, sl], vc[:, sl], vn[:, sl], vx[:, sl]], axis=0)
            qg = _stack_heads(q_ref, g)
            dog = _stack_heads(do_ref, g)
            delta = jnp.sum(dog.astype(F32) * _stack_heads(o_ref, g).astype(F32), axis=1, keepdims=True)
            lse_g = _stack_cols(lse_ref, g)
            s = lax.dot_general(qg, kcat, NT_DIMS, preferred_element_type=F32) * scale
            p = jnp.exp(jnp.where(mask, s - lse_g, NEG))
            dp = lax.dot_general(dog, vcat, NT_DIMS, preferred_element_type=F32)
            ds = (p * (dp - delta) * scale).astype(BF16)
            dqg = jnp.dot(ds, kcat, preferred_element_type=F32)
            dkx_ref[:, sl] += lax.dot_general(ds[:, 3 * BLK:], qg, TN_DIMS, preferred_element_type=F32)
            dvx_ref[:, sl] += lax.dot_general(p.astype(BF16)[:, 3 * BLK:], dog, TN_DIMS, preferred_element_type=F32)
            dsk = -jnp.exp(_sink_col(sink_ref, l, g) - lse_g) * delta
            for hh in range(GROUP):
                h = GROUP * g + hh
                rs = slice(hh * BLK, (hh + 1) * BLK)
                dq_ref[:, h * 128:(h + 1) * 128] = dqg[rs]
                delta_tile = jnp.where(lane == h, delta[rs], delta_tile)
                tot = jnp.sum(dsk[rs], axis=0, keepdims=True)
                dsink_upd = dsink_upd + jnp.where((lane8 == h) & (row8 == 0), tot, 0.0)
        delta_ref[...] = delta_tile
        dsink_ref[...] += dsink_upd

    blk = pl.BlockSpec((BLK, DQ), lambda i: (i, 0))
    stat = pl.BlockSpec((BLK, 128), lambda i: (i, 0))
    return pl.pallas_call(
        body, name=name, grid=(nb,),
        in_specs=[pl.BlockSpec(memory_space=pltpu.SMEM), blk] + _band_specs(DK, 0, nb, n_ctx)
        + _band_specs(DK, vcol, nb, n_ctx) + [blk, blk, stat],
        out_specs=[blk, stat, pl.BlockSpec((n_ctx, DK), lambda i: (0, 0)), pl.BlockSpec((n_ctx, DK), lambda i: (0, 0)),
                   pl.BlockSpec((8, 128), lambda i: (0, 0))],
        out_shape=[jax.ShapeDtypeStruct((S, DQ), F32), jax.ShapeDtypeStruct((S, 128), F32),
                   jax.ShapeDtypeStruct((n_ctx, DK), F32), jax.ShapeDtypeStruct((n_ctx, DK), F32),
                   jax.ShapeDtypeStruct((8, 128), F32)],
        compiler_params=_cp(1),
    )(sink, q, k, k, k, k, qkv, qkv, qkv, qkv, o, do, lse)


def _attn_bwd_kv(q, k, qkv, do, lse, delta, dkx, dvx, n_ctx, name):
    S, DQ = q.shape
    DK = k.shape[1]
    n_kv = DK // 128
    nb = S // BLK
    nctx_b = n_ctx // BLK
    vcol = (DQ + DK) // DK
    scale = HEAD_DIM ** -0.5

    def three(width):
        return [pl.BlockSpec((BLK, width), lambda j: (jnp.maximum(j - 1, 0), 0)),
                pl.BlockSpec((BLK, width), lambda j: (j, 0)),
                pl.BlockSpec((BLK, width), lambda j: (jnp.minimum(j + 1, nb - 1), 0))]

    def body(k_ref, v_ref, qp, qc, qn, dop, doc, don, lp, lc, ln, dlp, dlc, dln, dkx_ref, dvx_ref, dk_ref, dv_ref):
        j = pl.program_id(0)

        @pl.when(j < nctx_b)
        def _():
            dk_ref[...] = dkx_ref[...]
            dv_ref[...] = dvx_ref[...]

        @pl.when(j >= nctx_b)
        def _():
            shape = (3 * GROUP * BLK, BLK)
            t = lax.broadcasted_iota(I32, shape, 0)
            rq = (j - 1 + t // (GROUP * BLK)) * BLK + (t & (BLK - 1))
            rk = j * BLK + lax.broadcasted_iota(I32, shape, 1)
            valid = (rq >= n_ctx) & (rq < S) & (jnp.abs(rq - rk) <= WINDOW)
            for g in range(n_kv):
                sl = slice(g * 128, (g + 1) * 128)
                qcat = jnp.concatenate([_stack_heads(r, g) for r in (qp, qc, qn)], axis=0)
                docat = jnp.concatenate([_stack_heads(r, g) for r in (dop, doc, don)], axis=0)
                lse_c = jnp.concatenate([_stack_cols(r, g) for r in (lp, lc, ln)], axis=0)
                delta_c = jnp.concatenate([_stack_cols(r, g) for r in (dlp, dlc, dln)], axis=0)
                s = lax.dot_general(qcat, k_ref[:, sl], NT_DIMS, preferred_element_type=F32) * scale
                p = jnp.exp(jnp.where(valid, s - lse_c, NEG))
                dp = lax.dot_general(docat, v_ref[:, sl], NT_DIMS, preferred_element_type=F32)
                ds = (p * (dp - delta_c) * scale).astype(BF16)
                dk_ref[:, sl] = lax.dot_general(ds, qcat, TN_DIMS, preferred_element_type=F32)
                dv_ref[:, sl] = lax.dot_general(p.astype(BF16), docat, TN_DIMS, preferred_element_type=F32)

    cspec = pl.BlockSpec((BLK, DK), lambda j: (jnp.minimum(j, nctx_b - 1), 0))
    return pl.pallas_call(
        body, name=name, grid=(nb,),
        in_specs=[pl.BlockSpec((BLK, DK), lambda j: (j, 0)), pl.BlockSpec((BLK, DK), lambda j: (j, vcol))]
        + three(DQ) + three(DQ) + three(128) + three(128) + [cspec, cspec],
        out_specs=[pl.BlockSpec((BLK, DK), lambda j: (j, 0))] * 2,
        out_shape=[jax.ShapeDtypeStruct((S, DK), F32)] * 2, compiler_params=_cp(1),
    )(k, qkv, q, q, q, do, do, do, lse, lse, lse, delta, delta, delta, dkx, dvx)


def _ada_fwd(cond, w_ada, b_cols, name):
    lyr, D, C = w_ada.shape
    tc = _pick(C, [512, 384, 256, 128])

    def body(c_ref, w_ref, b_ref, o_ref):
        cv = c_ref[...]
        act = cv * _sigmoid(cv)
        o_ref[...] = jnp.dot(act, w_ref[...], preferred_element_type=F32,
                             precision=lax.Precision.HIGHEST) + b_ref[...]

    return pl.pallas_call(
        body, name=name, grid=(lyr, C // tc),
        in_specs=[pl.BlockSpec((16, D), lambda l, j: (0, 0)),
                  pl.BlockSpec((None, D, tc), lambda l, j: (l, 0, j)),
                  pl.BlockSpec((None, 1, tc), lambda l, j: (l, 0, j))],
        out_specs=pl.BlockSpec((None, 16, tc), lambda l, j: (l, 0, j)),
        out_shape=jax.ShapeDtypeStruct((lyr, 16, C), F32), compiler_params=_cp(2),
    )(cond, w_ada, b_cols)


def _ada_bwd(cond, d_out, w_ada, name):
    lyr, D, C = w_ada.shape
    tc = _pick(C, [512, 384, 256, 128])

    def body(c_ref, d_ref, w_ref, gw_ref, dc_ref):
        @pl.when((pl.program_id(0) == 0) & (pl.program_id(1) == 0))
        def _():
            dc_ref[...] = jnp.zeros_like(dc_ref)

        cv = c_ref[...]
        act = cv * _sigmoid(cv)
        dv = d_ref[...]
        gw_ref[...] = lax.dot_general(act, dv, TN_DIMS, preferred_element_type=F32, precision=lax.Precision.HIGHEST)
        dc_ref[...] += lax.dot_general(dv, w_ref[...], NT_DIMS, preferred_element_type=F32,
                                       precision=lax.Precision.HIGHEST)

    return pl.pallas_call(
        body, name=name, grid=(lyr, C // tc),
        in_specs=[pl.BlockSpec((16, D), lambda l, j: (0, 0)),
                  pl.BlockSpec((None, 16, tc), lambda l, j: (l, 0, j)),
                  pl.BlockSpec((None, D, tc), lambda l, j: (l, 0, j))],
        out_specs=[pl.BlockSpec((None, D, tc), lambda l, j: (l, 0, j)), pl.BlockSpec((16, D), lambda l, j: (0, 0))],
        out_shape=[jax.ShapeDtypeStruct((lyr, D, C), F32), jax.ShapeDtypeStruct((16, D), F32)],
        compiler_params=_cp(2),
    )(cond, d_out, w_ada)


def _sum_rows(d_rows, name):
    lyr, r, C = d_rows.shape

    def body(d_ref, o_ref):
        o_ref[...] = jnp.sum(d_ref[...], axis=0, keepdims=True)

    return pl.pallas_call(
        body, name=name, grid=(lyr,),
        in_specs=[pl.BlockSpec((None, r, C), lambda l: (l, 0, 0))],
        out_specs=pl.BlockSpec((None, 1, C), lambda l: (l, 0, 0)),
        out_shape=jax.ShapeDtypeStruct((lyr, 1, C), F32), compiler_params=_cp(1),
    )(d_rows)


def _cctx_grad(gathered, c_ctx_row, name):
    D = gathered.shape[1]

    def body(g_ref, c_ref, o_ref):
        acc = g_ref[0:16, :]
        for d in range(1, N_DEV):
            acc = acc + g_ref[16 * d:16 * (d + 1), :]
        cv = c_ref[...]
        sg = _sigmoid(cv)
        o_ref[...] = acc[8:16] * (sg * (1.0 + cv * (1.0 - sg)))

    return pl.pallas_call(
        body, name=name, out_shape=jax.ShapeDtypeStruct((8, D), F32),
        compiler_params=pltpu.CompilerParams(vmem_limit_bytes=VMEM_ELEMENTWISE),
    )(gathered, c_ctx_row)


def _pad_rows(a, rows):
    return jnp.concatenate([a, jnp.zeros((rows - a.shape[0],) + a.shape[1:], a.dtype)], axis=0)


def kernel(x, c, ctx, c_ctx, w_ada, b_ada, attn_w_qkv, attn_w_o, attn_q_gain, attn_k_gain, attn_sink, sc_w_in, sc_conv, sc_w_out, ffn_w_up, ffn_conv, ffn_conv_b, ffn_w_down, loss_target, m_c_ctx, m_w_ada, m_b_ada, m_attn_w_qkv, m_attn_w_o, m_attn_q_gain, m_attn_k_gain, m_attn_sink, m_sc_w_in, m_sc_conv, m_sc_w_out, m_ffn_w_up, m_ffn_conv, m_ffn_conv_b, m_ffn_w_down, v_c_ctx, v_w_ada, v_b_ada, v_attn_w_qkv, v_attn_w_o, v_attn_q_gain, v_attn_k_gain, v_attn_sink, v_sc_w_in, v_sc_conv, v_sc_w_out, v_ffn_w_up, v_ffn_conv, v_ffn_conv_b, v_ffn_w_down):
    T, D = x.shape[1], x.shape[2]
    L = ctx.shape[1]
    S = L + T
    depth = w_ada.shape[0]
    F = ffn_conv_b.shape[1]
    n_q = D // HEAD_DIM
    n_kv = n_q // GROUP
    ada_c = w_ada.shape[2]
    assert L % BLK == 0 and T % BLK == 0 and ada_c * N_DEV == 6 * D

    px, py, pc = _my_pos()
    me = 4 * px + 2 * py + pc
    me_idx = jnp.reshape(me, (1,)).astype(I32)

    tm_mm = _pick(S, [768, 704, 384, 256, 128])
    ts_tn = _pick(S, [2112, 1056, 768, 384, 256, 128])

    def start_layer_weights(l):
        j = l // 2
        if l % 2 == 0:
            pair = [_cast_layer(attn_w_qkv, j, me_idx, f"cast_qkv{l}"), _cast_layer(attn_w_o, j, me_idx, f"cast_wo{l}")]
        else:
            pair = [_cast_layer(sc_w_in, j, me_idx, f"cast_scin{l}"), _cast_layer(sc_w_out, j, me_idx, f"cast_scout{l}")]
        pair += [_cast_layer(ffn_w_up, l, me_idx, f"cast_up{l}"), _cast_layer(ffn_w_down, l, me_idx, f"cast_down{l}")]
        return _gather_start(pair, f"gather_start{l}")

    in_flight = [start_layer_weights(l) for l in range(depth)]
    started = sum(t[3][0, 0] for t in in_flight)

    c_all = _gather_small(_pad_rows(c + started, 8), "gather_cond")
    cond = jnp.concatenate([c_all[0::8], c_ctx[None, :], jnp.zeros((7, D), F32)], axis=0)
    b_cols = lax.dynamic_slice_in_dim(b_ada, me * ada_c, ada_c, axis=1)[:, None, :]
    ada_mine = _ada_fwd(cond, w_ada, b_cols, "ada_fwd")
    ada_all = _gather_small(ada_mine.reshape(depth * 16, ada_c), "gather_ada")
    ada_all = ada_all.reshape(N_DEV, depth, 16, ada_c)
    ada_rows = jnp.transpose(ada_all, (1, 2, 0, 3)).reshape(depth, 16, 6, D)
    mod_lat = lax.dynamic_index_in_dim(ada_rows, me, axis=1, keepdims=False)
    mods = jnp.stack([ada_rows[:, 8], mod_lat], axis=1)

    gathered = [None] * depth
    tabs = _rope_tables(T, L)
    gains = jnp.stack([attn_q_gain, attn_k_gain], axis=1)
    conv_b3 = ffn_conv_b[:, None, :]
    sc_conv_all = _gather_small(_pad_rows(sc_conv.reshape(-1, sc_conv.shape[2]), 8), "gather_scconv")
    ffn_conv_all = _gather_small(_pad_rows(ffn_conv.reshape(-1, ffn_conv.shape[2]), 16), "gather_ffnconv")
    n_sc = sc_conv.shape[0]
    sc_conv_full = jnp.transpose(sc_conv_all.reshape(N_DEV, 8, -1)[:, :n_sc * 3], (1, 0, 2)).reshape(n_sc, 3, D)
    ffn_conv_full = jnp.transpose(ffn_conv_all.reshape(N_DEV, 16, -1)[:, :depth * 3], (1, 0, 2)).reshape(depth, 3, F)

    xs = jnp.concatenate([ctx[0], x[0]], axis=0)
    saved = []
    for l in range(depth):
        j = l // 2
        send_sems, recv_sems, lands, _ = in_flight[l]
        gathered[l] = _gather_wait(lands, send_sems, recv_sems, mods if l == 0 else xs, f"gather_wait{l}")
        w_a, w_b, w_up, w_down = gathered[l]
        mod = mods[l]
        h = _norm_mod(xs, mod, 0, L, f"norm_m{l}")
        if l % 2 == 0:
            qkv = _mm_nn(h, w_a, tm=tm_mm, tn=w_a.shape[2], out_dtype=BF16, name=f"qkv{l}")
            qr, kr = _qk_prep(qkv, tabs, gains, j, n_q, n_kv, f"qk_prep{l}")
            z, lse = _attn_fwd(qr, kr, qkv, attn_sink, j, L, f"attn{l}")
            mix = (qkv, qr, kr, lse)
        else:
            u = _mm_nn(h, w_a, tm=tm_mm, tn=w_a.shape[2], out_dtype=BF16, name=f"scin{l}")
            z = _sc_act(u, sc_conv_full, j, L, f"sc_act{l}")
            mix = (u,)
        y_m, x1 = _mm_nn_resid(z, w_b.reshape(D, D), xs, mod, 2, L, tm=tm_mm, tn=_pick(D, [1024, 512]), name=f"mixout{l}")
        h2 = _norm_mod(x1, mod, 3, L, f"norm_f{l}")
        u_f = _mm_nn(h2, w_up, tm=tm_mm, tn=w_up.shape[2], out_dtype=BF16, name=f"up{l}")
        a_f = _ffn_act(u_f, ffn_conv_full, conv_b3, l, L, f"ffn_act{l}")
        y_f, x2 = _mm_nn_resid(a_f, w_down.reshape(F, D), x1, mod, 5, L, tm=tm_mm, tn=_pick(D, [512]), name=f"down{l}")
        saved.append((xs, h, mix, z, y_m, x1, h2, u_f, a_f, y_f))
        xs = x2

    dx, sq = _loss_grad(xs, loss_target[0], L, "loss")
    loss = lax.psum(sq[0, 0], ("x", "y", "c"))

    dmods = [None] * depth
    g_conv_b, g_ffn_conv, g_sc_conv = [None] * depth, [None] * depth, [None] * n_sc
    g_gain, g_sink = [None] * (depth - n_sc), [None] * (depth - n_sc)
    rs_flight = [None] * depth
    sent = jnp.zeros((), F32)
    for l in reversed(range(depth)):
        j = l // 2
        w_a, w_b, w_up, w_down = gathered[l]
        mod = mods[l] + sent
        x0, h, mix, z, y_m, x1, h2, u_f, a_f, y_f = saved[l]
        dy, s_gf = _gate_bwd(dx, y_f, mod, 5, L, f"gate_f_bwd{l}")
        da = _mm_nt(dy, w_down.reshape(F, D), tm=tm_mm, tn=_pick(F, [1408, 512]), out_dtype=BF16, name=f"down_dgrad{l}")
        gw_down = _mm_tn(a_f, dy, nb=1, tka=_pick(F, [1408, 512]), tn=_pick(D, [1024, 512]), ts=ts_tn, name=f"down_wgrad{l}")
        dgate, dval, s_conv = _ffn_act_bwd(u_f, da, ffn_conv_full, conv_b3, l, L, f"ffn_act_bwd{l}")
        du = jnp.concatenate([dgate, dval], axis=1)
        dh2 = _mm_nt_acc(du, w_up, tm=tm_mm, out_dtype=F32, name=f"up_dgrad{l}")
        gw_up = _mm_tn(h2, du, nb=N_DEV, tka=_pick(D, [1024, 512]), tn=w_up.shape[2], ts=ts_tn, name=f"up_wgrad{l}")
        dx1, s_nf = _norm_mod_bwd(dh2, x1, mod, dx, 3, L, f"norm_f_bwd{l}")
        g_ffn_conv[l], g_conv_b[l] = s_conv[0:3], s_conv[3]
        rs_ffn = _rs_start([gw_up, gw_down.reshape(N_DEV, -1, D)], f"rs_start_ffn{l}")
        mod = mods[l] + rs_ffn[4][0, 0]
        dy, s_gm = _gate_bwd(dx1, y_m, mod, 2, L, f"gate_m_bwd{l}")
        dz = _mm_nt(dy, w_b.reshape(D, D), tm=tm_mm, tn=_pick(D, [1024, 512]), out_dtype=BF16, name=f"mixout_dgrad{l}")
        gw_b = _mm_tn(z, dy, nb=1, tka=_pick(D, [1024, 512]), tn=_pick(D, [1024, 512]), ts=ts_tn, name=f"mixout_wgrad{l}")
        if l % 2 == 0:
            qkv, qr, kr, lse = mix
            dq, delta, dkx, dvx, s_sink = _attn_bwd_q(qr, kr, qkv, z, dz, lse, attn_sink, j, L, f"attn_bwd_q{l}")
            dk, dv = _attn_bwd_kv(qr, kr, qkv, dz, lse, delta, dkx, dvx, L, f"attn_bwd_kv{l}")
            du_m, s_gain = _qk_prep_bwd(dq, dk, dv, qkv, tabs, gains, j, n_q, n_kv, f"qk_prep_bwd{l}")
            g_gain[j], g_sink[j] = s_gain[0:2], s_sink[0]
        else:
            (u,) = mix
            d_gb, d_gc, d_val, s_scconv = _sc_act_bwd(u, dz, sc_conv_full, j, L, f"sc_act_bwd{l}")
            du_m = jnp.concatenate([d_gb, d_gc, d_val], axis=1)
            g_sc_conv[j] = s_scconv[0:3]
        dh = _mm_nt_acc(du_m, w_a, tm=tm_mm, out_dtype=F32, name=f"mixin_dgrad{l}")
        gw_a = _mm_tn(h, du_m, nb=N_DEV, tka=_pick(D, [1024, 512]), tn=w_a.shape[2], ts=ts_tn, name=f"mixin_wgrad{l}")
        dx, s_nm = _norm_mod_bwd(dh, x0, mod, dx1, 0, L, f"norm_m_bwd{l}")
        dmods[l] = jnp.stack([jnp.stack([s_nm[2 * k], s_nm[2 * k + 1], s_gm[k], s_nf[2 * k], s_nf[2 * k + 1], s_gf[k]])
                              for k in range(2)])
        rs_mix = _rs_start([gw_a, gw_b.reshape(N_DEV, -1, D)], f"rs_start_mix{l}")
        sent = rs_mix[4][0, 0]
        rs_flight[l] = (rs_mix, rs_ffn)

    grad_x = dx[L:][None]

    big_w = {"qkv": (attn_w_qkv, m_attn_w_qkv, v_attn_w_qkv), "wo": (attn_w_o, m_attn_w_o, v_attn_w_o),
             "scin": (sc_w_in, m_sc_w_in, v_sc_w_in), "scout": (sc_w_out, m_sc_w_out, v_sc_w_out),
             "up": (ffn_w_up, m_ffn_w_up, v_ffn_w_up), "down": (ffn_w_down, m_ffn_w_down, v_ffn_w_down)}
    big_out = {k: [] for k in big_w}
    for l in reversed(range(depth)):
        j = l // 2
        groups = [(["qkv", "wo"] if l % 2 == 0 else ["scin", "scout"], [j, j]), (["up", "down"], [l, l])]
        for (names, idxs), flight, tag in reversed(list(zip(groups, rs_flight[l], ("mix", "ffn")))):
            send_sems, recv_sems, own, zones, _ = flight
            own, zones = _rs_wait(own, zones, send_sems, recv_sems, dx, f"rs_wait_{tag}{l}")
            for n, li, p, z in zip(names, idxs, own, zones):
                w, m, v = big_w[n]
                big_out[n].insert(0, _adamw_reduced(p, z, me_idx, w, m, v, li, f"adamw_{n}{l}"))
    big_res = {k: [jnp.stack([o[t] for o in outs]) for t in range(4)] for k, outs in big_out.items()}

    n_attn = depth - n_sc
    pack = [jnp.stack(dmods)[:, 0].reshape(-1, 128), jnp.stack(dmods)[:, 1].reshape(-1, 128),
            jnp.stack(g_gain).reshape(-1, 128), jnp.stack(g_sink),
            jnp.stack(g_conv_b).reshape(-1, 128), jnp.stack(g_ffn_conv).reshape(-1, 128),
            jnp.stack(g_sc_conv).reshape(-1, 128)]
    used = [p.shape[0] for p in pack]
    pack = [_pad_rows(p, -(-p.shape[0] // 8) * 8) for p in pack]
    sizes = [p.shape[0] for p in pack]
    flat = jnp.concatenate(pack, axis=0)
    rows = flat.shape[0]
    small_all = _gather_small(flat, "gather_small_grads")
    small_sum = _sum8(small_all, rows, "sum_small_grads")
    offs = [sum(sizes[:k]) for k in range(len(sizes))]
    seg = lambda a, k: a[offs[k]:offs[k] + used[k]]
    dmod_ctx = seg(small_sum, 0).reshape(depth, 6 * D)
    dmod_lat = small_all.reshape(N_DEV, rows, 128)[:, offs[1]:offs[1] + used[1]].reshape(N_DEV, depth, 6 * D)
    g_gain_sum = seg(small_sum, 2).reshape(n_attn, 2, 128)
    g_sink_sum = seg(small_sum, 3)[:n_attn, :n_q]
    g_conv_b_sum = seg(small_sum, 4).reshape(depth, F)
    g_ffn_conv_sum = seg(small_sum, 5).reshape(depth, 3, F)
    g_sc_conv_sum = seg(small_sum, 6).reshape(n_sc, 3, D)

    d_rows = jnp.concatenate([jnp.transpose(dmod_lat, (1, 0, 2)), dmod_ctx[:, None, :],
                              jnp.zeros((depth, 7, 6 * D), F32)], axis=1)
    d_cols = lax.dynamic_slice_in_dim(d_rows, me * ada_c, ada_c, axis=2)
    g_w_ada, dcond_part = _ada_bwd(cond, d_cols, w_ada, "ada_bwd")
    dcond_all = _gather_small(dcond_part, "gather_dcond")
    g_c_ctx = _cctx_grad(dcond_all, jnp.broadcast_to(c_ctx[None, :], (8, D)), "cctx_grad")[0]
    g_b_ada = _sum_rows(d_rows, "b_ada_grad")[:, 0]

    def small_adam(w, g, m, v, name):
        w2 = w.reshape(-1, w.shape[-1])
        d, m2, v2 = _adamw_plain(w2, g.reshape(w2.shape), m.reshape(w2.shape), v.reshape(w2.shape), name)
        return g.reshape(w.shape), d.reshape(w.shape), m2.reshape(w.shape), v2.reshape(w.shape)

    g_sc_conv_mine = lax.dynamic_slice_in_dim(g_sc_conv_sum, me * sc_conv.shape[2], sc_conv.shape[2], axis=2)
    g_ffn_conv_mine = lax.dynamic_slice_in_dim(g_ffn_conv_sum, me * ffn_conv.shape[2], ffn_conv.shape[2], axis=2)
    res = {
        "c_ctx": small_adam(c_ctx[None, :], g_c_ctx[None, :], m_c_ctx[None, :], v_c_ctx[None, :], "adamw_c_ctx"),
        "b_ada": small_adam(b_ada, g_b_ada, m_b_ada, v_b_ada, "adamw_b_ada"),
        "attn_q_gain": small_adam(attn_q_gain, g_gain_sum[:, 0], m_attn_q_gain, v_attn_q_gain, "adamw_q_gain"),
        "attn_k_gain": small_adam(attn_k_gain, g_gain_sum[:, 1], m_attn_k_gain, v_attn_k_gain, "adamw_k_gain"),
        "attn_sink": small_adam(attn_sink, g_sink_sum, m_attn_sink, v_attn_sink, "adamw_sink"),
        "sc_conv": small_adam(sc_conv, g_sc_conv_mine, m_sc_conv, v_sc_conv, "adamw_sc_conv"),
        "ffn_conv": small_adam(ffn_conv, g_ffn_conv_mine, m_ffn_conv, v_ffn_conv, "adamw_ffn_conv"),
        "ffn_conv_b": small_adam(ffn_conv_b, g_conv_b_sum, m_ffn_conv_b, v_ffn_conv_b, "adamw_conv_b"),
    }
    res["c_ctx"] = tuple(t[0] for t in res["c_ctx"])
    res["w_ada"] = (g_w_ada,) + tuple(_adamw_tiled(w_ada, g_w_ada, m_w_ada, v_w_ada, "adamw_w_ada"))
    res["attn_w_qkv"], res["attn_w_o"] = big_res["qkv"], big_res["wo"]
    res["sc_w_in"], res["sc_w_out"] = big_res["scin"], big_res["scout"]
    res["ffn_w_up"], res["ffn_w_down"] = big_res["up"], big_res["down"]

    order = ["c_ctx", "w_ada", "b_ada", "attn_w_qkv", "attn_w_o", "attn_q_gain", "attn_k_gain", "attn_sink",
             "sc_w_in", "sc_conv", "sc_w_out", "ffn_w_up", "ffn_conv", "ffn_conv_b", "ffn_w_down"]
    outs = [loss, grad_x]
    for t in range(4):
        outs += [res[n][t] for n in order]
    return tuple(outs)
```

```python
import functools

import jax
import jax.numpy as jnp
from jax import lax
from jax.experimental import pallas as pl
from jax.experimental.pallas import tpu as pltpu

F32 = jnp.float32
BF16 = jnp.bfloat16
I32 = jnp.int32

N_DEV = 8
HEAD_DIM = 128
GROUP = 4
WINDOW = 128
BLK = 128
GRID_W = 64
ROPE_BASE = 10000.0
EPS = 1e-6
NEG = -1e30
HALO = 16

ADAM_LR = 0.001
ADAM_B1 = 0.9
ADAM_B2 = 0.999
ADAM_EPS = 1e-08
ADAM_WD = 0.01
ADAM_STEP = 10

V7X_VMEM_BYTES = 64 << 20
VMEM_MATMUL = 52 << 20
VMEM_ELEMENTWISE = 44 << 20

MESH = pl.DeviceIdType.MESH
ANY = pl.BlockSpec(memory_space=pl.ANY)
HBM = pl.BlockSpec(memory_space=pltpu.HBM)
SEM = pl.BlockSpec(memory_space=pltpu.SEMAPHORE)
EFFECT = pltpu.SideEffectType.DATAFLOW_SIDE_EFFECTING

NT_DIMS = (((1,), (1,)), ((), ()))
TN_DIMS = (((0,), (0,)), ((), ()))


def _pick(n, cands):
    for t in cands:
        if n % t == 0:
            return t
    raise ValueError(f"no tile for {n} in {cands}")


def _cp(n_axes, vmem=VMEM_ELEMENTWISE):
    return pltpu.CompilerParams(dimension_semantics=("arbitrary",) * n_axes, vmem_limit_bytes=vmem)


def _rows(i, tm, off=0):
    return i * tm + off + lax.broadcasted_iota(I32, (tm, 1), 0)


def _my_pos():
    return lax.axis_index("x"), lax.axis_index("y"), lax.axis_index("c")


def _gather_small(x_shard, name):
    m_per, n = x_shard.shape

    def body(x_ref, out_ref, send_sems, recv_sems, local_sem):
        x, y, c = _my_pos()
        me, sibling = (x, y, c), (x, y, 1 - c)
        chips = [(1 - x, y), (x, 1 - y), (1 - x, 1 - y)]

        def rows(px, py, pc):
            return out_ref.at[pl.ds((4 * px + 2 * py + pc) * m_per, m_per), :]

        def copy(k, block, to, src=None):
            return pltpu.make_async_remote_copy(
                src_ref=rows(*block) if src is None else src, dst_ref=rows(*block),
                send_sem=send_sems.at[k], recv_sem=recv_sems.at[k], device_id=to, device_id_type=MESH)

        mine = pltpu.make_async_copy(x_ref, rows(*me), local_sem)
        mine.start()
        first = [copy(0, me, sibling, src=x_ref)]
        first += [copy(1 + j, me, (*chip, c), src=x_ref) for j, chip in enumerate(chips)]
        for cp in first:
            cp.start()
        passed = [copy(4 + j, (*chip, c), sibling) for j, chip in enumerate(chips)]
        for j, chip in enumerate(chips):
            copy(1 + j, (*chip, c), me).wait_recv()
            passed[j].start()
        copy(0, sibling, me).wait_recv()
        for j, chip in enumerate(chips):
            copy(4 + j, (*chip, 1 - c), me).wait_recv()
        for cp in first + passed:
            cp.wait_send()
        mine.wait()

    return pl.pallas_call(
        body, name=name,
        out_shape=jax.ShapeDtypeStruct((N_DEV * m_per, n), x_shard.dtype),
        in_specs=[pl.BlockSpec(memory_space=pltpu.VMEM)],
        out_specs=pl.BlockSpec(memory_space=pltpu.VMEM),
        scratch_shapes=[pltpu.SemaphoreType.DMA((7,)), pltpu.SemaphoreType.DMA((7,)), pltpu.SemaphoreType.DMA],
        compiler_params=pltpu.CompilerParams(vmem_limit_bytes=VMEM_ELEMENTWISE),
    )(x_shard)


def _peer(k):
    x, y, c = _my_pos()
    b = k + 1
    return ((1 - x) if b & 4 else x, (1 - y) if b & 2 else y, (1 - c) if b & 1 else c)


def _slot(p):
    return 4 * p[0] + 2 * p[1] + p[2]


def _in_hbm(a):
    return pltpu.with_memory_space_constraint(a, pltpu.HBM)


def _gather_start(lands, after, name):
    n = len(lands)

    def body(*refs):
        l_refs, send_sems, recv_sems = refs[:n], refs[n + len(after)], refs[n + len(after) + 1]
        token = refs[2 * n + len(after) + 2]
        me = _slot(_my_pos())
        for a in range(n):
            for k in range(7):
                pltpu.make_async_remote_copy(
                    src_ref=l_refs[a].at[me], dst_ref=l_refs[a].at[me],
                    send_sem=send_sems.at[7 * a + k], recv_sem=recv_sems.at[7 * a + k],
                    device_id=_peer(k), device_id_type=MESH).start()
        token[...] = jnp.zeros_like(token)

    out = pl.pallas_call(
        body, name=name,
        out_shape=(pltpu.SemaphoreType.DMA((7 * n,)), pltpu.SemaphoreType.DMA((7 * n,)),
                   *[pltpu.HBM(a.shape, a.dtype) for a in lands], jax.ShapeDtypeStruct((8, 128), F32)),
        in_specs=[HBM] * n + [ANY] * len(after),
        out_specs=(SEM, SEM, *[HBM] * n, pl.BlockSpec(memory_space=pltpu.VMEM)),
        input_output_aliases={a: 2 + a for a in range(n)},
        compiler_params=pltpu.CompilerParams(has_side_effects=EFFECT),
    )(*[_in_hbm(a) for a in lands], *after)
    return out[0], out[1], list(out[2:2 + n]), out[2 + n]


def _gather_wait(lands, send_sems, recv_sems, after, name):
    n = len(lands)

    def body(*refs):
        l_refs, ss, rs = refs[:n], refs[n], refs[n + 1]
        me = _slot(_my_pos())
        for a in range(n):
            for k in range(7):
                cp = pltpu.make_async_remote_copy(
                    src_ref=l_refs[a].at[me], dst_ref=l_refs[a].at[_slot(_peer(k))],
                    send_sem=ss.at[7 * a + k], recv_sem=rs.at[7 * a + k], device_id=_peer(k), device_id_type=MESH)
                cp.wait_send()
                cp.wait_recv()

    out = pl.pallas_call(
        body, name=name,
        out_shape=tuple(pltpu.HBM(a.shape, a.dtype) for a in lands),
        in_specs=[HBM] * n + [SEM, SEM, ANY], out_specs=tuple([HBM] * n),
        input_output_aliases={a: a for a in range(n)},
        compiler_params=pltpu.CompilerParams(has_side_effects=EFFECT),
    )(*lands, send_sems, recv_sems, after)
    return list(out)


def _rs_start(grads, name):
    n = len(grads)

    def body(*refs):
        g_refs, z_refs, send_sems, recv_sems = refs[:n], refs[n:2 * n], refs[2 * n], refs[2 * n + 1]
        token = refs[4 * n + 2]
        for a in range(n):
            for k in range(7):
                pltpu.make_async_remote_copy(
                    src_ref=g_refs[a].at[_slot(_peer(k))], dst_ref=z_refs[a].at[k],
                    send_sem=send_sems.at[7 * a + k], recv_sem=recv_sems.at[7 * a + k],
                    device_id=_peer(k), device_id_type=MESH).start()
        token[...] = jnp.zeros_like(token)

    zones = [lax.empty((7,) + g.shape[1:], g.dtype) for g in grads]
    out = pl.pallas_call(
        body, name=name,
        out_shape=(pltpu.SemaphoreType.DMA((7 * n,)), pltpu.SemaphoreType.DMA((7 * n,)),
                   *[pltpu.HBM(a.shape, a.dtype) for a in grads], *[pltpu.HBM(z.shape, z.dtype) for z in zones],
                   jax.ShapeDtypeStruct((8, 128), F32)),
        in_specs=[HBM] * (2 * n),
        out_specs=(SEM, SEM, *[HBM] * (2 * n), pl.BlockSpec(memory_space=pltpu.VMEM)),
        input_output_aliases={a: 2 + a for a in range(2 * n)},
        compiler_params=pltpu.CompilerParams(has_side_effects=EFFECT),
    )(*[_in_hbm(a) for a in grads], *[_in_hbm(z) for z in zones])
    return out[0], out[1], list(out[2:2 + n]), list(out[2 + n:2 + 2 * n]), out[2 + 2 * n]


def _rs_wait(grads, zones, send_sems, recv_sems, after, name):
    n = len(grads)

    def body(*refs):
        g_refs, z_refs, ss, rs = refs[:n], refs[n:2 * n], refs[2 * n], refs[2 * n + 1]
        for a in range(n):
            for k in range(7):
                cp = pltpu.make_async_remote_copy(
                    src_ref=g_refs[a].at[_slot(_peer(k))], dst_ref=z_refs[a].at[k],
                    send_sem=ss.at[7 * a + k], recv_sem=rs.at[7 * a + k], device_id=_peer(k), device_id_type=MESH)
                cp.wait_send()
                cp.wait_recv()

    out = pl.pallas_call(
        body, name=name,
        out_shape=tuple(pltpu.HBM(a.shape, a.dtype) for a in list(grads) + list(zones)),
        in_specs=[HBM] * (2 * n) + [SEM, SEM, ANY], out_specs=tuple([HBM] * (2 * n)),
        input_output_aliases={a: a for a in range(2 * n)},
        compiler_params=pltpu.CompilerParams(has_side_effects=EFFECT),
    )(*grads, *zones, send_sems, recv_sems, after)
    return list(out[:n]), list(out[n:])


def _cast_layer(w, l, me_idx, name):
    _, r, c = w.shape
    tr = _pick(r, [512, 256, 128, 64, 32, 16])

    def body(me_ref, w_ref, o_ref):
        o_ref[...] = w_ref[...].astype(BF16)

    return pl.pallas_call(
        body, name=name,
        grid_spec=pltpu.PrefetchScalarGridSpec(
            num_scalar_prefetch=1, grid=(r // tr,),
            in_specs=[pl.BlockSpec((None, tr, c), lambda i, me_ref: (l, i, 0))],
            out_specs=pl.BlockSpec((None, tr, c), lambda i, me_ref: (me_ref[0], i, 0))),
        out_shape=jax.ShapeDtypeStruct((N_DEV, r, c), BF16), compiler_params=_cp(1),
    )(me_idx, w)


def _adam_math(w, g, m, v):
    m2 = ADAM_B1 * m + (1.0 - ADAM_B1) * g
    v2 = ADAM_B2 * v + (1.0 - ADAM_B2) * (g * g)
    m_hat = m2 / (1.0 - ADAM_B1 ** ADAM_STEP)
    v_hat = v2 / (1.0 - ADAM_B2 ** ADAM_STEP)
    delta = -ADAM_LR * (m_hat / (jnp.sqrt(v_hat) + ADAM_EPS) + ADAM_WD * w)
    return delta, m2, v2


def _adamw_reduced(own, zone, me_idx, w, m, v, l, name):
    _, r, c = own.shape
    tr = _pick(r, [256, 128, 64, 32, 16])

    def body(me_ref, p_ref, z_ref, w_ref, m_ref, v_ref, g_out, d_out, m_out, v_out):
        g = p_ref[...].astype(F32)
        for k in range(7):
            g = g + z_ref[k].astype(F32)
        d, m2, v2 = _adam_math(w_ref[...], g, m_ref[...], v_ref[...])
        g_out[...] = g
        d_out[...] = d
        m_out[...] = m2
        v_out[...] = v2

    wspec = pl.BlockSpec((None, tr, c), lambda i, me_ref: (l, i, 0))
    ospec = pl.BlockSpec((tr, c), lambda i, me_ref: (i, 0))
    return pl.pallas_call(
        body, name=name,
        grid_spec=pltpu.PrefetchScalarGridSpec(
            num_scalar_prefetch=1, grid=(r // tr,),
            in_specs=[pl.BlockSpec((None, tr, c), lambda i, me_ref: (me_ref[0], i, 0)),
                      pl.BlockSpec((7, tr, c), lambda i, me_ref: (0, i, 0)), wspec, wspec, wspec],
            out_specs=[ospec] * 4),
        out_shape=[jax.ShapeDtypeStruct((r, c), F32)] * 4, compiler_params=_cp(1),
    )(me_idx, own, zone, w, m, v)


def _adamw_plain(w, g, m, v, name):
    def body(w_ref, g_ref, m_ref, v_ref, d_out, m_out, v_out):
        d, m2, v2 = _adam_math(w_ref[...], g_ref[...], m_ref[...], v_ref[...])
        d_out[...] = d
        m_out[...] = m2
        v_out[...] = v2

    return pl.pallas_call(
        body, name=name, out_shape=[jax.ShapeDtypeStruct(w.shape, F32)] * 3,
        compiler_params=pltpu.CompilerParams(vmem_limit_bytes=VMEM_ELEMENTWISE),
    )(w, g, m, v)


def _adamw_tiled(w, g, m, v, name):
    lyr, r, c = w.shape
    tr = _pick(r, [256, 128, 64, 32, 16, 8])

    def body(w_ref, g_ref, m_ref, v_ref, d_out, m_out, v_out):
        d, m2, v2 = _adam_math(w_ref[...], g_ref[...], m_ref[...], v_ref[...])
        d_out[...] = d
        m_out[...] = m2
        v_out[...] = v2

    spec = pl.BlockSpec((None, tr, c), lambda l, i: (l, i, 0))
    return pl.pallas_call(
        body, name=name, grid=(lyr, r // tr), in_specs=[spec] * 4, out_specs=[spec] * 3,
        out_shape=[jax.ShapeDtypeStruct(w.shape, F32)] * 3, compiler_params=_cp(2),
    )(w, g, m, v)


def _sum8(gathered, rows, name):
    def body(g_ref, o_ref):
        acc = g_ref[0:rows, :]
        for d in range(1, N_DEV):
            acc = acc + g_ref[d * rows:(d + 1) * rows, :]
        o_ref[...] = acc

    return pl.pallas_call(
        body, name=name, out_shape=jax.ShapeDtypeStruct((rows, 128), F32),
        compiler_params=pltpu.CompilerParams(vmem_limit_bytes=VMEM_ELEMENTWISE),
    )(gathered)


def _mm_nn(a, b3, *, tm, tn, out_dtype, name):
    M, K = a.shape
    nb, _, nc = b3.shape
    q = nc // tn

    def body(a_ref, b_ref, o_ref):
        o_ref[...] = jnp.dot(a_ref[...], b_ref[...], preferred_element_type=F32).astype(o_ref.dtype)

    return pl.pallas_call(
        body, name=name, grid=(nb * q, M // tm),
        in_specs=[pl.BlockSpec((tm, K), lambda j, i: (i, 0)),
                  pl.BlockSpec((None, K, tn), lambda j, i: (j // q, 0, j % q))],
        out_specs=pl.BlockSpec((tm, tn), lambda j, i: (i, j)),
        out_shape=jax.ShapeDtypeStruct((M, nb * nc), out_dtype), compiler_params=_cp(2, VMEM_MATMUL),
    )(a, b3)


def _mm_nn_resid(a, b2, x_old, mod, gate_row, n_ctx, *, tm, tn, name):
    M, K = a.shape
    N = b2.shape[1]

    def body(a_ref, b_ref, x_ref, mod_ref, y_ref, xn_ref):
        y = jnp.dot(a_ref[...], b_ref[...], preferred_element_type=F32)
        is_ctx = _rows(pl.program_id(1), tm) < n_ctx
        g = jnp.where(is_ctx, mod_ref[0, gate_row:gate_row + 1, :], mod_ref[1, gate_row:gate_row + 1, :])
        y_ref[...] = y.astype(BF16)
        xn_ref[...] = x_ref[...] + g * y

    return pl.pallas_call(
        body, name=name, grid=(N // tn, M // tm),
        in_specs=[pl.BlockSpec((tm, K), lambda j, i: (i, 0)),
                  pl.BlockSpec((K, tn), lambda j, i: (0, j)),
                  pl.BlockSpec((tm, tn), lambda j, i: (i, j)),
                  pl.BlockSpec((2, 6, tn), lambda j, i: (0, 0, j))],
        out_specs=[pl.BlockSpec((tm, tn), lambda j, i: (i, j))] * 2,
        out_shape=[jax.ShapeDtypeStruct((M, N), BF16), jax.ShapeDtypeStruct((M, N), F32)],
        compiler_params=_cp(2, VMEM_MATMUL),
    )(a, b2, x_old, mod)


def _mm_nt_acc(parts, w3, *, tm, out_dtype, name):
    n_parts = len(parts)
    M = parts[0].shape[0]
    nb, K, nc = w3.shape
    per = nb // n_parts

    def body(*refs):
        dy_refs, w_ref = refs[:n_parts], refs[n_parts]
        o_ref, acc_ref = refs[n_parts + 1], refs[n_parts + 2]
        s = pl.program_id(1)

        @pl.when(s == 0)
        def _():
            acc_ref[...] = jnp.zeros_like(acc_ref)

        for p in range(n_parts):
            @pl.when(s // per == p)
            def _(p=p):
                acc_ref[...] += lax.dot_general(dy_refs[p][...], w_ref[...], NT_DIMS, preferred_element_type=F32)

        @pl.when(s == nb - 1)
        def _():
            o_ref[...] = acc_ref[...].astype(o_ref.dtype)

    def part_spec(p):
        return pl.BlockSpec((tm, nc), lambda i, s: (i, jnp.clip(s - p * per, 0, per - 1)))

    return pl.pallas_call(
        body, name=name, grid=(M // tm, nb),
        in_specs=[part_spec(p) for p in range(n_parts)] + [pl.BlockSpec((None, K, nc), lambda i, s: (s, 0, 0))],
        out_specs=pl.BlockSpec((tm, K), lambda i, s: (i, 0)),
        out_shape=jax.ShapeDtypeStruct((M, K), out_dtype),
        scratch_shapes=[pltpu.VMEM((tm, K), F32)], compiler_params=_cp(2, VMEM_MATMUL),
    )(*parts, w3)


def _mm_nt(dy, w2, *, tm, tn, out_dtype, name):
    M, N = dy.shape
    K = w2.shape[0]

    def body(dy_ref, w_ref, o_ref):
        o_ref[...] = lax.dot_general(dy_ref[...], w_ref[...], NT_DIMS,
                                     preferred_element_type=F32).astype(o_ref.dtype)

    return pl.pallas_call(
        body, name=name, grid=(K // tn, M // tm),
        in_specs=[pl.BlockSpec((tm, N), lambda j, i: (i, 0)),
                  pl.BlockSpec((tn, N), lambda j, i: (j, 0))],
        out_specs=pl.BlockSpec((tm, tn), lambda j, i: (i, j)),
        out_shape=jax.ShapeDtypeStruct((M, K), out_dtype), compiler_params=_cp(2, VMEM_MATMUL),
    )(dy, w2)


def _mm_tn(a, parts, *, nb, tka, tn, ts, name):
    n_parts = len(parts)
    S, Ka = a.shape
    N = sum(p.shape[1] for p in parts)
    nc = N // nb
    q = nc // tn
    nk = S // ts
    per = nb * q // n_parts

    def body(a_ref, *rest):
        dy_refs, o_ref, acc_ref = rest[:n_parts], rest[n_parts], rest[n_parts + 1]
        j = pl.program_id(0)
        k = pl.program_id(2)

        @pl.when(k == 0)
        def _():
            acc_ref[...] = jnp.zeros_like(acc_ref)

        for p in range(n_parts):
            @pl.when(j // per == p)
            def _(p=p):
                acc_ref[...] += lax.dot_general(a_ref[...], dy_refs[p][...], TN_DIMS, preferred_element_type=F32)

        @pl.when(k == nk - 1)
        def _():
            o_ref[...] = acc_ref[...].astype(o_ref.dtype)

    def part_spec(p):
        return pl.BlockSpec((ts, tn), lambda j, ia, k: (jnp.where(j // per == p, k, 0),
                                                        jnp.clip(j - p * per, 0, per - 1)))

    return pl.pallas_call(
        body, name=name, grid=(nb * q, Ka // tka, nk),
        in_specs=[pl.BlockSpec((ts, tka), lambda j, ia, k: (k, ia))] + [part_spec(p) for p in range(n_parts)],
        out_specs=pl.BlockSpec((None, tka, tn), lambda j, ia, k: (j // q, ia, j % q)),
        out_shape=jax.ShapeDtypeStruct((nb, Ka, nc), BF16),
        scratch_shapes=[pltpu.VMEM((tka, tn), F32)], compiler_params=_cp(3, VMEM_MATMUL),
    )(a, *parts)


def _norm_mod(x, mod, row0, n_ctx, name):
    S, D = x.shape
    tm = _pick(S, [256, 128])

    def body(x_ref, mod_ref, h_ref):
        xv = x_ref[...]
        r = lax.rsqrt(jnp.mean(xv * xv, axis=-1, keepdims=True) + EPS)
        is_ctx = _rows(pl.program_id(0), tm) < n_ctx
        sh = jnp.where(is_ctx, mod_ref[0, row0:row0 + 1, :], mod_ref[1, row0:row0 + 1, :])
        sc = jnp.where(is_ctx, mod_ref[0, row0 + 1:row0 + 2, :], mod_ref[1, row0 + 1:row0 + 2, :])
        h_ref[...] = (xv * r * (1.0 + sc) + sh).astype(BF16)

    return pl.pallas_call(
        body, name=name, grid=(S // tm,),
        in_specs=[pl.BlockSpec((tm, D), lambda i: (i, 0)), pl.BlockSpec((2, 6, D), lambda i: (0, 0, 0))],
        out_specs=pl.BlockSpec((tm, D), lambda i: (i, 0)),
        out_shape=jax.ShapeDtypeStruct((S, D), BF16), compiler_params=_cp(1),
    )(x, mod)


def _norm_mod_bwd(dh, x, mod, dx_res, row0, n_ctx, name):
    S, D = x.shape
    tm = _pick(S, [256, 128])

    def body(dh_ref, x_ref, mod_ref, res_ref, dx_ref, acc_ref):
        i = pl.program_id(0)

        @pl.when(i == 0)
        def _():
            acc_ref[...] = jnp.zeros_like(acc_ref)

        xv = x_ref[...]
        dh_v = dh_ref[...].astype(F32)
        r = lax.rsqrt(jnp.mean(xv * xv, axis=-1, keepdims=True) + EPS)
        xhat = xv * r
        is_ctx = _rows(i, tm) < n_ctx
        sc = jnp.where(is_ctx, mod_ref[0, row0 + 1:row0 + 2, :], mod_ref[1, row0 + 1:row0 + 2, :])
        dxhat = dh_v * (1.0 + sc)
        dx_ref[...] = res_ref[...] + r * (dxhat - xhat * jnp.mean(dxhat * xhat, axis=-1, keepdims=True))
        dsc = dh_v * xhat
        zero = jnp.zeros_like(dh_v)
        sums = [jnp.sum(jnp.where(is_ctx, dh_v, zero), axis=0, keepdims=True),
                jnp.sum(jnp.where(is_ctx, dsc, zero), axis=0, keepdims=True),
                jnp.sum(jnp.where(is_ctx, zero, dh_v), axis=0, keepdims=True),
                jnp.sum(jnp.where(is_ctx, zero, dsc), axis=0, keepdims=True)]
        rid = lax.broadcasted_iota(I32, (8, D), 0)
        upd = jnp.zeros((8, D), F32)
        for k, s in enumerate(sums):
            upd = upd + jnp.where(rid == k, s, 0.0)
        acc_ref[...] += upd

    return pl.pallas_call(
        body, name=name, grid=(S // tm,),
        in_specs=[pl.BlockSpec((tm, D), lambda i: (i, 0)), pl.BlockSpec((tm, D), lambda i: (i, 0)),
                  pl.BlockSpec((2, 6, D), lambda i: (0, 0, 0)), pl.BlockSpec((tm, D), lambda i: (i, 0))],
        out_specs=[pl.BlockSpec((tm, D), lambda i: (i, 0)), pl.BlockSpec((8, D), lambda i: (0, 0))],
        out_shape=[jax.ShapeDtypeStruct((S, D), F32), jax.ShapeDtypeStruct((8, D), F32)],
        compiler_params=_cp(1),
    )(dh, x, mod, dx_res)


def _gate_bwd(dx, y, mod, gate_row, n_ctx, name):
    S, D = dx.shape
    tm = _pick(S, [256, 128])

    def body(dx_ref, y_ref, mod_ref, dy_ref, acc_ref):
        i = pl.program_id(0)

        @pl.when(i == 0)
        def _():
            acc_ref[...] = jnp.zeros_like(acc_ref)

        dxv = dx_ref[...]
        is_ctx = _rows(i, tm) < n_ctx
        g = jnp.where(is_ctx, mod_ref[0, gate_row:gate_row + 1, :], mod_ref[1, gate_row:gate_row + 1, :])
        dy_ref[...] = (g * dxv).astype(BF16)
        prod = dxv * y_ref[...].astype(F32)
        zero = jnp.zeros_like(prod)
        s_ctx = jnp.sum(jnp.where(is_ctx, prod, zero), axis=0, keepdims=True)
        s_lat = jnp.sum(jnp.where(is_ctx, zero, prod), axis=0, keepdims=True)
        rid = lax.broadcasted_iota(I32, (8, D), 0)
        acc_ref[...] += jnp.where(rid == 0, s_ctx, 0.0) + jnp.where(rid == 1, s_lat, 0.0)

    return pl.pallas_call(
        body, name=name, grid=(S // tm,),
        in_specs=[pl.BlockSpec((tm, D), lambda i: (i, 0)), pl.BlockSpec((tm, D), lambda i: (i, 0)),
                  pl.BlockSpec((2, 6, D), lambda i: (0, 0, 0))],
        out_specs=[pl.BlockSpec((tm, D), lambda i: (i, 0)), pl.BlockSpec((8, D), lambda i: (0, 0))],
        out_shape=[jax.ShapeDtypeStruct((S, D), BF16), jax.ShapeDtypeStruct((8, D), F32)],
        compiler_params=_cp(1),
    )(dx, y, mod)


def _loss_grad(x, target, n_ctx, name):
    S, D = x.shape
    tm = _pick(n_ctx, [256, 128])
    nct = n_ctx // tm

    def body(x_ref, t_ref, dx_ref, tot_ref, acc_ref):
        i = pl.program_id(0)

        @pl.when(i == 0)
        def _():
            acc_ref[...] = jnp.zeros_like(acc_ref)

        @pl.when(i < nct)
        def _():
            dx_ref[...] = jnp.zeros_like(dx_ref)

        @pl.when(i >= nct)
        def _():
            err = x_ref[...] - t_ref[...]
            dx_ref[...] = err * (1.0 / D)
            acc_ref[...] += jnp.sum(err * err, axis=0, keepdims=True)

        @pl.when(i == S // tm - 1)
        def _():
            tot = jnp.sum(acc_ref[...], axis=1, keepdims=True) * (0.5 / D)
            tot_ref[...] = jnp.broadcast_to(tot, tot_ref.shape)

    return pl.pallas_call(
        body, name=name, grid=(S // tm,),
        in_specs=[pl.BlockSpec((tm, D), lambda i: (i, 0)),
                  pl.BlockSpec((tm, D), lambda i: (jnp.maximum(i - nct, 0), 0))],
        out_specs=[pl.BlockSpec((tm, D), lambda i: (i, 0)), pl.BlockSpec((1, 128), lambda i: (0, 0))],
        out_shape=[jax.ShapeDtypeStruct((S, D), F32), jax.ShapeDtypeStruct((1, 128), F32)],
        scratch_shapes=[pltpu.VMEM((1, D), F32)], compiler_params=_cp(1),
    )(x, target)


def _halo_specs(tm, tc, S, col_off):
    per = tm // HALO
    last = S // HALO - 1
    return [pl.BlockSpec((HALO, tc), lambda j, i: (jnp.maximum(i * per - 1, 0), j + col_off)),
            pl.BlockSpec((tm, tc), lambda j, i: (i, j + col_off)),
            pl.BlockSpec((HALO, tc), lambda j, i: (jnp.minimum((i + 1) * per, last), j + col_off))]


def _ext(p_ref, m_ref, n_ref):
    return jnp.concatenate([p_ref[...], m_ref[...], n_ref[...]], axis=0).astype(F32)


def _links(i, tm, S, n_ctx):
    n = tm + 2 * HALO
    rid = i * tm - HALO + lax.broadcasted_iota(I32, (n, 1), 0)
    has_prev = (rid != 0) & (rid != n_ctx)
    has_next = (rid != n_ctx - 1) & (rid != S - 1)
    return has_prev, has_next


def _up(x):
    return pltpu.roll(x, 1, 0)


def _dn(x):
    return pltpu.roll(x, x.shape[0] - 1, 0)


def _conv3(x, w_ref, has_prev, has_next):
    return (w_ref[0:1, :] * jnp.where(has_prev, _up(x), 0.0) + w_ref[1:2, :] * x
            + w_ref[2:3, :] * jnp.where(has_next, _dn(x), 0.0))


def _conv3_t(d, w_ref, has_prev, has_next):
    return (w_ref[0:1, :] * jnp.where(has_next, _dn(d), 0.0) + w_ref[1:2, :] * d
            + w_ref[2:3, :] * jnp.where(has_prev, _up(d), 0.0))


def _conv3_wgrad(d, x, has_prev, has_next, extra=None):
    c = slice(HALO, d.shape[0] - HALO)
    taps = [jnp.where(has_prev, _up(x), 0.0), x, jnp.where(has_next, _dn(x), 0.0)]
    sums = [jnp.sum((d * t)[c], axis=0, keepdims=True) for t in taps]
    if extra is not None:
        sums.append(jnp.sum(extra[c], axis=0, keepdims=True))
    rid = lax.broadcasted_iota(I32, (8, d.shape[1]), 0)
    upd = jnp.zeros((8, d.shape[1]), F32)
    for k, s in enumerate(sums):
        upd = upd + jnp.where(rid == k, s, 0.0)
    return upd


def _sigmoid(x):
    return 1.0 / (1.0 + jnp.exp(-x))


def _ffn_act(u, conv_w, conv_b, l, n_ctx, name):
    S, F2 = u.shape
    F = F2 // 2
    tm = _pick(S, [384, 256, 128])
    tc = _pick(F, [1408, 512, 256, 128])
    nj = F // tc

    def body(gp, gm, gn, v_ref, w_ref, b_ref, a_ref):
        has_prev, has_next = _links(pl.program_id(1), tm, S, n_ctx)
        gc = _conv3(_ext(gp, gm, gn), w_ref, has_prev, has_next)[HALO:HALO + tm] + b_ref[...]
        a_ref[...] = (gc * _sigmoid(gc) * v_ref[...].astype(F32)).astype(BF16)

    return pl.pallas_call(
        body, name=name, grid=(nj, S // tm),
        in_specs=_halo_specs(tm, tc, S, 0) + [
            pl.BlockSpec((tm, tc), lambda j, i: (i, j + nj)),
            pl.BlockSpec((None, 3, tc), lambda j, i: (l, 0, j)),
            pl.BlockSpec((None, 1, tc), lambda j, i: (l, 0, j))],
        out_specs=pl.BlockSpec((tm, tc), lambda j, i: (i, j)),
        out_shape=jax.ShapeDtypeStruct((S, F), BF16), compiler_params=_cp(2),
    )(u, u, u, u, conv_w, conv_b)


def _ffn_act_bwd(u, da, conv_w, conv_b, l, n_ctx, name):
    S, F2 = u.shape
    F = F2 // 2
    tm = _pick(S, [384, 256, 128])
    tc = _pick(F, [1408, 512, 256, 128])
    nj = F // tc

    def body(gp, gm, gn, vp, vm, vn, dp, dm, dn_, w_ref, b_ref, dg_ref, dv_ref, acc_ref):
        i = pl.program_id(1)

        @pl.when(i == 0)
        def _():
            acc_ref[...] = jnp.zeros_like(acc_ref)

        has_prev, has_next = _links(i, tm, S, n_ctx)
        g = _ext(gp, gm, gn)
        val = _ext(vp, vm, vn)
        d_a = _ext(dp, dm, dn_)
        gc = _conv3(g, w_ref, has_prev, has_next) + b_ref[...]
        sg = _sigmoid(gc)
        dgc = d_a * val * (sg * (1.0 + gc * (1.0 - sg)))
        c = slice(HALO, HALO + tm)
        dv_ref[...] = (d_a * gc * sg)[c].astype(BF16)
        dg_ref[...] = _conv3_t(dgc, w_ref, has_prev, has_next)[c].astype(BF16)
        acc_ref[...] += _conv3_wgrad(dgc, g, has_prev, has_next, extra=dgc)

    return pl.pallas_call(
        body, name=name, grid=(nj, S // tm),
        in_specs=_halo_specs(tm, tc, S, 0) + _halo_specs(tm, tc, S, nj) + _halo_specs(tm, tc, S, 0) + [
            pl.BlockSpec((None, 3, tc), lambda j, i: (l, 0, j)),
            pl.BlockSpec((None, 1, tc), lambda j, i: (l, 0, j))],
        out_specs=[pl.BlockSpec((tm, tc), lambda j, i: (i, j))] * 2 + [pl.BlockSpec((8, tc), lambda j, i: (0, j))],
        out_shape=[jax.ShapeDtypeStruct((S, F), BF16)] * 2 + [jax.ShapeDtypeStruct((8, F), F32)],
        compiler_params=_cp(2),
    )(u, u, u, u, u, u, da, da, da, conv_w, conv_b)


def _sc_act(u, conv_w, l, n_ctx, name):
    S, D3 = u.shape
    D = D3 // 3
    tm = _pick(S, [384, 256, 128])
    tc = _pick(D, [1024, 512, 256, 128])
    nj = D // tc

    def body(b_ref, cp, cm, cn, vp, vm, vn, w_ref, z_ref):
        has_prev, has_next = _links(pl.program_id(1), tm, S, n_ctx)
        t = _ext(cp, cm, cn) * _ext(vp, vm, vn)
        cv = _conv3(t, w_ref, has_prev, has_next)[HALO:HALO + tm]
        z_ref[...] = (b_ref[...].astype(F32) * cv).astype(BF16)

    return pl.pallas_call(
        body, name=name, grid=(nj, S // tm),
        in_specs=[pl.BlockSpec((tm, tc), lambda j, i: (i, j))] + _halo_specs(tm, tc, S, nj)
        + _halo_specs(tm, tc, S, 2 * nj) + [pl.BlockSpec((None, 3, tc), lambda j, i: (l, 0, j))],
        out_specs=pl.BlockSpec((tm, tc), lambda j, i: (i, j)),
        out_shape=jax.ShapeDtypeStruct((S, D), BF16), compiler_params=_cp(2),
    )(u, u, u, u, u, u, u, conv_w)


def _sc_act_bwd(u, dz, conv_w, l, n_ctx, name):
    S, D3 = u.shape
    D = D3 // 3
    tm = _pick(S, [384, 256, 128])
    tc = _pick(D, [1024, 512, 256, 128])
    nj = D // tc

    def body(bp, bm, bn, cp, cm, cn, vp, vm, vn, zp, zm, zn, w_ref, db_ref, dc_ref, dv_ref, acc_ref):
        i = pl.program_id(1)

        @pl.when(i == 0)
        def _():
            acc_ref[...] = jnp.zeros_like(acc_ref)

        has_prev, has_next = _links(i, tm, S, n_ctx)
        gb = _ext(bp, bm, bn)
        gcv = _ext(cp, cm, cn)
        val = _ext(vp, vm, vn)
        d_z = _ext(zp, zm, zn)
        t = gcv * val
        c = slice(HALO, HALO + tm)
        db_ref[...] = (d_z * _conv3(t, w_ref, has_prev, has_next))[c].astype(BF16)
        dcv = d_z * gb
        dt = _conv3_t(dcv, w_ref, has_prev, has_next)
        dc_ref[...] = (dt * val)[c].astype(BF16)
        dv_ref[...] = (dt * gcv)[c].astype(BF16)
        acc_ref[...] += _conv3_wgrad(dcv, t, has_prev, has_next)

    return pl.pallas_call(
        body, name=name, grid=(nj, S // tm),
        in_specs=_halo_specs(tm, tc, S, 0) + _halo_specs(tm, tc, S, nj) + _halo_specs(tm, tc, S, 2 * nj)
        + _halo_specs(tm, tc, S, 0) + [pl.BlockSpec((None, 3, tc), lambda j, i: (l, 0, j))],
        out_specs=[pl.BlockSpec((tm, tc), lambda j, i: (i, j))] * 3 + [pl.BlockSpec((8, tc), lambda j, i: (0, j))],
        out_shape=[jax.ShapeDtypeStruct((S, D), BF16)] * 3 + [jax.ShapeDtypeStruct((8, D), F32)],
        compiler_params=_cp(2),
    )(u, u, u, u, u, u, u, u, u, dz, dz, dz, conv_w)


def _rope_tables(T, n_ctx):
    rows = T // GRID_W
    pairs = HEAD_DIM // 4
    row = jnp.repeat(jnp.arange(rows), GRID_W).astype(F32)
    col = jnp.tile(jnp.arange(GRID_W), rows).astype(F32)
    inv = ROPE_BASE ** (-jnp.arange(pairs, dtype=F32) / pairs)
    ang = jnp.concatenate([row[:, None] * inv, row[:, None] * inv, col[:, None] * inv, col[:, None] * inv], axis=1)
    cos, sin = jnp.cos(ang), jnp.sin(ang)
    first = (jnp.arange(HEAD_DIM) % (2 * pairs)) < pairs
    sin_a = jnp.where(first, -sin, 0.0)
    sin_b = jnp.where(first, 0.0, sin)
    pad = jnp.zeros((n_ctx, HEAD_DIM), F32)
    return (jnp.concatenate([pad + 1.0, cos], axis=0), jnp.concatenate([pad, sin_a], axis=0),
            jnp.concatenate([pad, sin_b], axis=0))


def _qk_prep(qkv, tabs, gains, l, n_q, n_kv, name):
    S = qkv.shape[0]
    W = qkv.shape[1]
    tm = _pick(S, [256, 128])
    cos, sin_a, sin_b = tabs

    def body(x_ref, cos_ref, sa_ref, sb_ref, g_ref, q_ref, k_ref):
        cs, sa, sb = cos_ref[...], sa_ref[...], sb_ref[...]
        for h in range(n_q + n_kv):
            xv = x_ref[:, h * 128:(h + 1) * 128].astype(F32)
            r = lax.rsqrt(jnp.mean(xv * xv, axis=-1, keepdims=True) + EPS)
            gain = g_ref[0:1, :] if h < n_q else g_ref[1:2, :]
            y = xv * r * gain
            out = (y * cs + pltpu.roll(y, 96, 1) * sa + pltpu.roll(y, 32, 1) * sb).astype(BF16)
            if h < n_q:
                q_ref[:, h * 128:(h + 1) * 128] = out
            else:
                k_ref[:, (h - n_q) * 128:(h - n_q + 1) * 128] = out

    tspec = pl.BlockSpec((tm, 128), lambda i: (i, 0))
    return pl.pallas_call(
        body, name=name, grid=(S // tm,),
        in_specs=[pl.BlockSpec((tm, W), lambda i: (i, 0)), tspec, tspec, tspec,
                  pl.BlockSpec((None, 2, 128), lambda i: (l, 0, 0))],
        out_specs=[pl.BlockSpec((tm, n_q * 128), lambda i: (i, 0)), pl.BlockSpec((tm, n_kv * 128), lambda i: (i, 0))],
        out_shape=[jax.ShapeDtypeStruct((S, n_q * 128), BF16), jax.ShapeDtypeStruct((S, n_kv * 128), BF16)],
        compiler_params=_cp(1),
    )(qkv, cos, sin_a, sin_b, gains)


def _qk_prep_bwd(dq, dk, dv, qkv, tabs, gains, l, n_q, n_kv, name):
    S, W = qkv.shape
    tm = _pick(S, [256, 128])
    cos, sin_a, sin_b = tabs

    def body(dq_ref, dk_ref, dv_ref, x_ref, cos_ref, sa_ref, sb_ref, g_ref, o_ref, acc_ref):
        @pl.when(pl.program_id(0) == 0)
        def _():
            acc_ref[...] = jnp.zeros_like(acc_ref)

        cs, sa, sb = cos_ref[...], sa_ref[...], sb_ref[...]
        dgq = jnp.zeros((1, 128), F32)
        dgk = jnp.zeros((1, 128), F32)
        for h in range(n_q + n_kv):
            if h < n_q:
                d_out = dq_ref[:, h * 128:(h + 1) * 128]
                gain = g_ref[0:1, :]
            else:
                d_out = dk_ref[:, (h - n_q) * 128:(h - n_q + 1) * 128]
                gain = g_ref[1:2, :]
            dy = d_out * cs + pltpu.roll(d_out * sa, 32, 1) + pltpu.roll(d_out * sb, 96, 1)
            xv = x_ref[:, h * 128:(h + 1) * 128].astype(F32)
            r = lax.rsqrt(jnp.mean(xv * xv, axis=-1, keepdims=True) + EPS)
            xhat = xv * r
            dg = jnp.sum(dy * xhat, axis=0, keepdims=True)
            if h < n_q:
                dgq = dgq + dg
            else:
                dgk = dgk + dg
            dxhat = dy * gain
            dx = r * (dxhat - xhat * jnp.mean(dxhat * xhat, axis=-1, keepdims=True))
            o_ref[:, h * 128:(h + 1) * 128] = dx.astype(BF16)
        v0 = (n_q + n_kv) * 128
        o_ref[:, v0:] = dv_ref[...].astype(BF16)
        rid = lax.broadcasted_iota(I32, (8, 128), 0)
        acc_ref[...] += jnp.where(rid == 0, dgq, 0.0) + jnp.where(rid == 1, dgk, 0.0)

    tspec = pl.BlockSpec((tm, 128), lambda i: (i, 0))
    return pl.pallas_call(
        body, name=name, grid=(S // tm,),
        in_specs=[pl.BlockSpec((tm, n_q * 128), lambda i: (i, 0)), pl.BlockSpec((tm, n_kv * 128), lambda i: (i, 0)),
                  pl.BlockSpec((tm, n_kv * 128), lambda i: (i, 0)), pl.BlockSpec((tm, W), lambda i: (i, 0)),
                  tspec, tspec, tspec, pl.BlockSpec((None, 2, 128), lambda i: (l, 0, 0))],
        out_specs=[pl.BlockSpec((tm, W), lambda i: (i, 0)), pl.BlockSpec((8, 128), lambda i: (0, 0))],
        out_shape=[jax.ShapeDtypeStruct((S, W), BF16), jax.ShapeDtypeStruct((8, 128), F32)],
        compiler_params=_cp(1),
    )(dq, dk, dv, qkv, cos, sin_a, sin_b, gains)


def _band_specs(width, col, nb, n_ctx):
    return [pl.BlockSpec((BLK, width), lambda i: (jnp.maximum(i - 1, 0), col)),
            pl.BlockSpec((BLK, width), lambda i: (i, col)),
            pl.BlockSpec((BLK, width), lambda i: (jnp.minimum(i + 1, nb - 1), col)),
            pl.BlockSpec((n_ctx, width), lambda i: (0, col))]


def _q_side_mask(i, S, n_ctx):
    shape = (GROUP * BLK, 3 * BLK + n_ctx)
    a = lax.broadcasted_iota(I32, shape, 0) & (BLK - 1)
    kk = lax.broadcasted_iota(I32, shape, 1)
    rq = i * BLK + a
    rk = (i - 1) * BLK + kk
    band = (rq >= n_ctx) & (rk >= n_ctx) & (rk < S) & (jnp.abs(rq - rk) <= WINDOW)
    return (kk >= 3 * BLK) | band


def _stack_heads(ref, g):
    return jnp.concatenate([ref[:, (GROUP * g + hh) * 128:(GROUP * g + hh + 1) * 128] for hh in range(GROUP)], axis=0)


def _stack_cols(ref, g):
    return jnp.concatenate([ref[:, GROUP * g + hh:GROUP * g + hh + 1] for hh in range(GROUP)], axis=0)


def _sink_col(sink_ref, l, g):
    return jnp.concatenate([jnp.full((BLK, 1), sink_ref[l, GROUP * g + hh], F32) for hh in range(GROUP)], axis=0)


def _attn_fwd(q, k, qkv, sink, l, n_ctx, name):
    S, DQ = q.shape
    DK = k.shape[1]
    n_kv = DK // 128
    nb = S // BLK
    vcol = (DQ + DK) // DK
    scale = HEAD_DIM ** -0.5

    def body(sink_ref, q_ref, kp, kc, kn, kx, vp, vc, vn, vx, o_ref, lse_ref):
        i = pl.program_id(0)
        mask = _q_side_mask(i, S, n_ctx)
        lane = lax.broadcasted_iota(I32, (BLK, 128), 1)
        lse_tile = jnp.zeros((BLK, 128), F32)
        for g in range(n_kv):
            sl = slice(g * 128, (g + 1) * 128)
            kcat = jnp.concatenate([kp[:, sl], kc[:, sl], kn[:, sl], kx[:, sl]], axis=0)
            vcat = jnp.concatenate([vp[:, sl], vc[:, sl], vn[:, sl], vx[:, sl]], axis=0)
            s = lax.dot_general(_stack_heads(q_ref, g), kcat, NT_DIMS, preferred_element_type=F32) * scale
            s = jnp.where(mask, s, NEG)
            sk = _sink_col(sink_ref, l, g)
            m = jnp.maximum(jnp.max(s, axis=1, keepdims=True), sk)
            e = jnp.exp(s - m)
            den = jnp.sum(e, axis=1, keepdims=True) + jnp.exp(sk - m)
            p = (e / den).astype(BF16)
            o = jnp.dot(p, vcat, preferred_element_type=F32)
            lse = m + jnp.log(den)
            for hh in range(GROUP):
                h = GROUP * g + hh
                o_ref[:, h * 128:(h + 1) * 128] = o[hh * BLK:(hh + 1) * BLK].astype(BF16)
                lse_tile = jnp.where(lane == h, lse[hh * BLK:(hh + 1) * BLK], lse_tile)
        lse_ref[...] = lse_tile

    return pl.pallas_call(
        body, name=name, grid=(nb,),
        in_specs=[pl.BlockSpec(memory_space=pltpu.SMEM), pl.BlockSpec((BLK, DQ), lambda i: (i, 0))]
        + _band_specs(DK, 0, nb, n_ctx) + _band_specs(DK, vcol, nb, n_ctx),
        out_specs=[pl.BlockSpec((BLK, DQ), lambda i: (i, 0)), pl.BlockSpec((BLK, 128), lambda i: (i, 0))],
        out_shape=[jax.ShapeDtypeStruct((S, DQ), BF16), jax.ShapeDtypeStruct((S, 128), F32)],
        compiler_params=_cp(1),
    )(sink, q, k, k, k, k, qkv, qkv, qkv, qkv)


def _attn_bwd_q(q, k, qkv, o, do, lse, sink, l, n_ctx, name):
    S, DQ = q.shape
    DK = k.shape[1]
    n_kv = DK // 128
    nb = S // BLK
    vcol = (DQ + DK) // DK
    scale = HEAD_DIM ** -0.5

    def body(sink_ref, q_ref, kp, kc, kn, kx, vp, vc, vn, vx, o_ref, do_ref, lse_ref,
             dq_ref, delta_ref, dkx_ref, dvx_ref, dsink_ref):
        i = pl.program_id(0)

        @pl.when(i == 0)
        def _():
            dkx_ref[...] = jnp.zeros_like(dkx_ref)
            dvx_ref[...] = jnp.zeros_like(dvx_ref)
            dsink_ref[...] = jnp.zeros_like(dsink_ref)

        mask = _q_side_mask(i, S, n_ctx)
        lane = lax.broadcasted_iota(I32, (BLK, 128), 1)
        lane8 = lax.broadcasted_iota(I32, (8, 128), 1)
        row8 = lax.broadcasted_iota(I32, (8, 128), 0)
        delta_tile = jnp.zeros((BLK, 128), F32)
        dsink_upd = jnp.zeros((8, 128), F32)
        for g in range(n_kv):
            sl = slice(g * 128, (g + 1) * 128)
            kcat = jnp.concatenate([kp[:, sl], kc[:, sl], kn[:, sl], kx[:, sl]], axis=0)
            vcat = jnp.concatenate([vp[:, sl], vc[:, sl], vn[:, sl], vx[:, sl]], axis=0)
            qg = _stack_heads(q_ref, g)
            dog = _stack_heads(do_ref, g)
            delta = jnp.sum(dog.astype(F32) * _stack_heads(o_ref, g).astype(F32), axis=1, keepdims=True)
            lse_g = _stack_cols(lse_ref, g)
            s = lax.dot_general(qg, kcat, NT_DIMS, preferred_element_type=F32) * scale
            p = jnp.exp(jnp.where(mask, s - lse_g, NEG))
            dp = lax.dot_general(dog, vcat, NT_DIMS, preferred_element_type=F32)
            ds = (p * (dp - delta) * scale).astype(BF16)
            dqg = jnp.dot(ds, kcat, preferred_element_type=F32)
            dkx_ref[:, sl] += lax.dot_general(ds[:, 3 * BLK:], qg, TN_DIMS, preferred_element_type=F32)
            dvx_ref[:, sl] += lax.dot_general(p.astype(BF16)[:, 3 * BLK:], dog, TN_DIMS, preferred_element_type=F32)
            dsk = -jnp.exp(_sink_col(sink_ref, l, g) - lse_g) * delta
            for hh in range(GROUP):
                h = GROUP * g + hh
                rs = slice(hh * BLK, (hh + 1) * BLK)
                dq_ref[:, h * 128:(h + 1) * 128] = dqg[rs]
                delta_tile = jnp.where(lane == h, delta[rs], delta_tile)
                tot = jnp.sum(dsk[rs], axis=0, keepdims=True)
                dsink_upd = dsink_upd + jnp.where((lane8 == h) & (row8 == 0), tot, 0.0)
        delta_ref[...] = delta_tile
        dsink_ref[...] += dsink_upd

    blk = pl.BlockSpec((BLK, DQ), lambda i: (i, 0))
    stat = pl.BlockSpec((BLK, 128), lambda i: (i, 0))
    return pl.pallas_call(
        body, name=name, grid=(nb,),
        in_specs=[pl.BlockSpec(memory_space=pltpu.SMEM), blk] + _band_specs(DK, 0, nb, n_ctx)
        + _band_specs(DK, vcol, nb, n_ctx) + [blk, blk, stat],
        out_specs=[blk, stat, pl.BlockSpec((n_ctx, DK), lambda i: (0, 0)), pl.BlockSpec((n_ctx, DK), lambda i: (0, 0)),
                   pl.BlockSpec((8, 128), lambda i: (0, 0))],
        out_shape=[jax.ShapeDtypeStruct((S, DQ), F32), jax.ShapeDtypeStruct((S, 128), F32),
                   jax.ShapeDtypeStruct((n_ctx, DK), F32), jax.ShapeDtypeStruct((n_ctx, DK), F32),
                   jax.ShapeDtypeStruct((8, 128), F32)],
        compiler_params=_cp(1),
    )(sink, q, k, k, k, k, qkv, qkv, qkv, qkv, o, do, lse)


def _attn_bwd_kv(q, k, qkv, do, lse, delta, dkx, dvx, n_ctx, name):
    S, DQ = q.shape
    DK = k.shape[1]
    n_kv = DK // 128
    nb = S // BLK
    nctx_b = n_ctx // BLK
    vcol = (DQ + DK) // DK
    scale = HEAD_DIM ** -0.5

    def three(width):
        return [pl.BlockSpec((BLK, width), lambda j: (jnp.maximum(j - 1, 0), 0)),
                pl.BlockSpec((BLK, width), lambda j: (j, 0)),
                pl.BlockSpec((BLK, width), lambda j: (jnp.minimum(j + 1, nb - 1), 0))]

    def body(k_ref, v_ref, qp, qc, qn, dop, doc, don, lp, lc, ln, dlp, dlc, dln, dkx_ref, dvx_ref, dk_ref, dv_ref):
        j = pl.program_id(0)

        @pl.when(j < nctx_b)
        def _():
            dk_ref[...] = dkx_ref[...]
            dv_ref[...] = dvx_ref[...]

        @pl.when(j >= nctx_b)
        def _():
            shape = (3 * GROUP * BLK, BLK)
            t = lax.broadcasted_iota(I32, shape, 0)
            rq = (j - 1 + t // (GROUP * BLK)) * BLK + (t & (BLK - 1))
            rk = j * BLK + lax.broadcasted_iota(I32, shape, 1)
            valid = (rq >= n_ctx) & (rq < S) & (jnp.abs(rq - rk) <= WINDOW)
            for g in range(n_kv):
                sl = slice(g * 128, (g + 1) * 128)
                qcat = jnp.concatenate([_stack_heads(r, g) for r in (qp, qc, qn)], axis=0)
                docat = jnp.concatenate([_stack_heads(r, g) for r in (dop, doc, don)], axis=0)
                lse_c = jnp.concatenate([_stack_cols(r, g) for r in (lp, lc, ln)], axis=0)
                delta_c = jnp.concatenate([_stack_cols(r, g) for r in (dlp, dlc, dln)], axis=0)
                s = lax.dot_general(qcat, k_ref[:, sl], NT_DIMS, preferred_element_type=F32) * scale
                p = jnp.exp(jnp.where(valid, s - lse_c, NEG))
                dp = lax.dot_general(docat, v_ref[:, sl], NT_DIMS, preferred_element_type=F32)
                ds = (p * (dp - delta_c) * scale).astype(BF16)
                dk_ref[:, sl] = lax.dot_general(ds, qcat, TN_DIMS, preferred_element_type=F32)
                dv_ref[:, sl] = lax.dot_general(p.astype(BF16), docat, TN_DIMS, preferred_element_type=F32)

    cspec = pl.BlockSpec((BLK, DK), lambda j: (jnp.minimum(j, nctx_b - 1), 0))
    return pl.pallas_call(
        body, name=name, grid=(nb,),
        in_specs=[pl.BlockSpec((BLK, DK), lambda j: (j, 0)), pl.BlockSpec((BLK, DK), lambda j: (j, vcol))]
        + three(DQ) + three(DQ) + three(128) + three(128) + [cspec, cspec],
        out_specs=[pl.BlockSpec((BLK, DK), lambda j: (j, 0))] * 2,
        out_shape=[jax.ShapeDtypeStruct((S, DK), F32)] * 2, compiler_params=_cp(1),
    )(k, qkv, q, q, q, do, do, do, lse, lse, lse, delta, delta, delta, dkx, dvx)


def _ada_fwd(cond, w_ada, b_cols, name):
    lyr, D, C = w_ada.shape
    tc = _pick(C, [512, 384, 256, 128])

    def body(c_ref, w_ref, b_ref, o_ref):
        cv = c_ref[...]
        act = cv * _sigmoid(cv)
        o_ref[...] = jnp.dot(act, w_ref[...], preferred_element_type=F32,
                             precision=lax.Precision.HIGHEST) + b_ref[...]

    return pl.pallas_call(
        body, name=name, grid=(lyr, C // tc),
        in_specs=[pl.BlockSpec((16, D), lambda l, j: (0, 0)),
                  pl.BlockSpec((None, D, tc), lambda l, j: (l, 0, j)),
                  pl.BlockSpec((None, 1, tc), lambda l, j: (l, 0, j))],
        out_specs=pl.BlockSpec((None, 16, tc), lambda l, j: (l, 0, j)),
        out_shape=jax.ShapeDtypeStruct((lyr, 16, C), F32), compiler_params=_cp(2),
    )(cond, w_ada, b_cols)


def _ada_bwd(cond, d_out, w_ada, name):
    lyr, D, C = w_ada.shape
    tc = _pick(C, [512, 384, 256, 128])

    def body(c_ref, d_ref, w_ref, gw_ref, dc_ref):
        @pl.when((pl.program_id(0) == 0) & (pl.program_id(1) == 0))
        def _():
            dc_ref[...] = jnp.zeros_like(dc_ref)

        cv = c_ref[...]
        act = cv * _sigmoid(cv)
        dv = d_ref[...]
        gw_ref[...] = lax.dot_general(act, dv, TN_DIMS, preferred_element_type=F32, precision=lax.Precision.HIGHEST)
        dc_ref[...] += lax.dot_general(dv, w_ref[...], NT_DIMS, preferred_element_type=F32,
                                       precision=lax.Precision.HIGHEST)

    return pl.pallas_call(
        body, name=name, grid=(lyr, C // tc),
        in_specs=[pl.BlockSpec((16, D), lambda l, j: (0, 0)),
                  pl.BlockSpec((None, 16, tc), lambda l, j: (l, 0, j)),
                  pl.BlockSpec((None, D, tc), lambda l, j: (l, 0, j))],
        out_specs=[pl.BlockSpec((None, D, tc), lambda l, j: (l, 0, j)), pl.BlockSpec((16, D), lambda l, j: (0, 0))],
        out_shape=[jax.ShapeDtypeStruct((lyr, D, C), F32), jax.ShapeDtypeStruct((16, D), F32)],
        compiler_params=_cp(2),
    )(cond, d_out, w_ada)


def _sum_rows(d_rows, name):
    lyr, r, C = d_rows.shape

    def body(d_ref, o_ref):
        o_ref[...] = jnp.sum(d_ref[...], axis=0, keepdims=True)

    return pl.pallas_call(
        body, name=name, grid=(lyr,),
        in_specs=[pl.BlockSpec((None, r, C), lambda l: (l, 0, 0))],
        out_specs=pl.BlockSpec((None, 1, C), lambda l: (l, 0, 0)),
        out_shape=jax.ShapeDtypeStruct((lyr, 1, C), F32), compiler_params=_cp(1),
    )(d_rows)


def _cctx_grad(gathered, c_ctx_row, name):
    D = gathered.shape[1]

    def body(g_ref, c_ref, o_ref):
        acc = g_ref[0:16, :]
        for d in range(1, N_DEV):
            acc = acc + g_ref[16 * d:16 * (d + 1), :]
        cv = c_ref[...]
        sg = _sigmoid(cv)
        o_ref[...] = acc[8:16] * (sg * (1.0 + cv * (1.0 - sg)))

    return pl.pallas_call(
        body, name=name, out_shape=jax.ShapeDtypeStruct((8, D), F32),
        compiler_params=pltpu.CompilerParams(vmem_limit_bytes=VMEM_ELEMENTWISE),
    )(gathered, c_ctx_row)


def _pad_rows(a, rows):
    return jnp.concatenate([a, jnp.zeros((rows - a.shape[0],) + a.shape[1:], a.dtype)], axis=0)


def kernel(x, c, ctx, c_ctx, w_ada, b_ada, attn_w_qkv, attn_w_o, attn_q_gain, attn_k_gain, attn_sink, sc_w_in, sc_conv, sc_w_out, ffn_w_up, ffn_conv, ffn_conv_b, ffn_w_down, loss_target, m_c_ctx, m_w_ada, m_b_ada, m_attn_w_qkv, m_attn_w_o, m_attn_q_gain, m_attn_k_gain, m_attn_sink, m_sc_w_in, m_sc_conv, m_sc_w_out, m_ffn_w_up, m_ffn_conv, m_ffn_conv_b, m_ffn_w_down, v_c_ctx, v_w_ada, v_b_ada, v_attn_w_qkv, v_attn_w_o, v_attn_q_gain, v_attn_k_gain, v_attn_sink, v_sc_w_in, v_sc_conv, v_sc_w_out, v_ffn_w_up, v_ffn_conv, v_ffn_conv_b, v_ffn_w_down):
    T, D = x.shape[1], x.shape[2]
    L = ctx.shape[1]
    S = L + T
    depth = w_ada.shape[0]
    F = ffn_conv_b.shape[1]
    n_q = D // HEAD_DIM
    n_kv = n_q // GROUP
    ada_c = w_ada.shape[2]
    assert L % BLK == 0 and T % BLK == 0 and ada_c * N_DEV == 6 * D

    px, py, pc = _my_pos()
    me = 4 * px + 2 * py + pc
    me_idx = jnp.reshape(me, (1,)).astype(I32)

    tm_mm = _pick(S, [768, 704, 384, 256, 128])
    ts_tn = _pick(S, [2112, 1056, 768, 384, 256, 128])

    c_all = _gather_small(_pad_rows(c, 8), "gather_cond")
    cond = jnp.concatenate([c_all[0::8], c_ctx[None, :], jnp.zeros((7, D), F32)], axis=0)
    b_cols = lax.dynamic_slice_in_dim(b_ada, me * ada_c, ada_c, axis=1)[:, None, :]
    ada_mine = _ada_fwd(cond, w_ada, b_cols, "ada_fwd")
    ada_all = _gather_small(ada_mine.reshape(depth * 16, ada_c), "gather_ada")
    ada_all = ada_all.reshape(N_DEV, depth, 16, ada_c)
    ada_rows = jnp.transpose(ada_all, (1, 2, 0, 3)).reshape(depth, 16, 6, D)
    mod_lat = lax.dynamic_index_in_dim(ada_rows, me, axis=1, keepdims=False)
    mods = jnp.stack([ada_rows[:, 8], mod_lat], axis=1)

    gathered = [None] * depth
    tabs = _rope_tables(T, L)
    gains = jnp.stack([attn_q_gain, attn_k_gain], axis=1)
    conv_b3 = ffn_conv_b[:, None, :]
    sc_conv_all = _gather_small(_pad_rows(sc_conv.reshape(-1, sc_conv.shape[2]), 8), "gather_scconv")
    ffn_conv_all = _gather_small(_pad_rows(ffn_conv.reshape(-1, ffn_conv.shape[2]), 16), "gather_ffnconv")
    n_sc = sc_conv.shape[0]
    sc_conv_full = jnp.transpose(sc_conv_all.reshape(N_DEV, 8, -1)[:, :n_sc * 3], (1, 0, 2)).reshape(n_sc, 3, D)
    ffn_conv_full = jnp.transpose(ffn_conv_all.reshape(N_DEV, 16, -1)[:, :depth * 3], (1, 0, 2)).reshape(depth, 3, F)

    def start_weights(ws, l, tag):
        lands = [_cast_layer(w, j, me_idx, f"cast_{tag}{k}_{l}") for k, (w, j) in enumerate(ws)]
        return _gather_start(lands, [mods, sc_conv_full, ffn_conv_full], f"gather_start_{tag}{l}")

    in_flight = []
    for l in range(depth):
        mix_ws = [(attn_w_qkv, l // 2), (attn_w_o, l // 2)] if l % 2 == 0 else [(sc_w_in, l // 2), (sc_w_out, l // 2)]
        in_flight.append((start_weights(mix_ws, l, "mix"), start_weights([(ffn_w_up, l), (ffn_w_down, l)], l, "ffn")))
    mods = mods + sum(t[3][0, 0] for pair in in_flight for t in pair)

    xs = jnp.concatenate([ctx[0], x[0]], axis=0)
    saved = []
    for l in range(depth):
        j = l // 2
        mod = mods[l]
        h = _norm_mod(xs, mod, 0, L, f"norm_m{l}")
        send_sems, recv_sems, lands, _ = in_flight[l][0]
        w_a, w_b = _gather_wait(lands, send_sems, recv_sems, h, f"gather_wait_mix{l}")
        if l % 2 == 0:
            qkv = _mm_nn(h, w_a, tm=tm_mm, tn=w_a.shape[2], out_dtype=BF16, name=f"qkv{l}")
            qr, kr = _qk_prep(qkv, tabs, gains, j, n_q, n_kv, f"qk_prep{l}")
            z, lse = _attn_fwd(qr, kr, qkv, attn_sink, j, L, f"attn{l}")
            mix = (qkv, qr, kr, lse)
        else:
            u = _mm_nn(h, w_a, tm=tm_mm, tn=w_a.shape[2], out_dtype=BF16, name=f"scin{l}")
            z = _sc_act(u, sc_conv_full, j, L, f"sc_act{l}")
            mix = (u,)
        y_m, x1 = _mm_nn_resid(z, w_b.reshape(D, D), xs, mod, 2, L, tm=tm_mm, tn=_pick(D, [1024, 512]), name=f"mixout{l}")
        h2 = _norm_mod(x1, mod, 3, L, f"norm_f{l}")
        send_sems, recv_sems, lands, _ = in_flight[l][1]
        w_up, w_down = _gather_wait(lands, send_sems, recv_sems, h2, f"gather_wait_ffn{l}")
        gathered[l] = (w_a, w_b, w_up, w_down)
        u_f = _mm_nn(h2, w_up, tm=tm_mm, tn=w_up.shape[2], out_dtype=BF16, name=f"up{l}")
        a_f = _ffn_act(u_f, ffn_conv_full, conv_b3, l, L, f"ffn_act{l}")
        y_f, x2 = _mm_nn_resid(a_f, w_down.reshape(F, D), x1, mod, 5, L, tm=tm_mm, tn=_pick(D, [512]), name=f"down{l}")
        saved.append((xs, h, mix, z, y_m, x1, h2, u_f, a_f, y_f))
        xs = x2

    dx, sq = _loss_grad(xs, loss_target[0], L, "loss")
    loss = lax.psum(sq[0, 0], ("x", "y", "c"))

    dmods = [None] * depth
    g_conv_b, g_ffn_conv, g_sc_conv = [None] * depth, [None] * depth, [None] * n_sc
    g_gain, g_sink = [None] * (depth - n_sc), [None] * (depth - n_sc)
    rs_flight = [None] * depth
    sent = jnp.zeros((), F32)
    for l in reversed(range(depth)):
        j = l // 2
        w_a, w_b, w_up, w_down = gathered[l]
        mod = mods[l] + sent
        x0, h, mix, z, y_m, x1, h2, u_f, a_f, y_f = saved[l]
        dy, s_gf = _gate_bwd(dx, y_f, mod, 5, L, f"gate_f_bwd{l}")
        da = _mm_nt(dy, w_down.reshape(F, D), tm=tm_mm, tn=_pick(F, [1408, 512]), out_dtype=BF16, name=f"down_dgrad{l}")
        gw_down = _mm_tn(a_f, [dy], nb=1, tka=_pick(F, [1408, 512]), tn=_pick(D, [1024, 512]), ts=ts_tn, name=f"down_wgrad{l}")
        dgate, dval, s_conv = _ffn_act_bwd(u_f, da, ffn_conv_full, conv_b3, l, L, f"ffn_act_bwd{l}")
        dh2 = _mm_nt_acc([dgate, dval], w_up, tm=tm_mm, out_dtype=F32, name=f"up_dgrad{l}")
        gw_up = _mm_tn(h2, [dgate, dval], nb=N_DEV, tka=_pick(D, [1024, 512]), tn=w_up.shape[2], ts=ts_tn, name=f"up_wgrad{l}")
        dx1, s_nf = _norm_mod_bwd(dh2, x1, mod, dx, 3, L, f"norm_f_bwd{l}")
        g_ffn_conv[l], g_conv_b[l] = s_conv[0:3], s_conv[3]
        rs_ffn = _rs_start([gw_up, gw_down.reshape(N_DEV, -1, D)], f"rs_start_ffn{l}")
        mod = mods[l] + rs_ffn[4][0, 0]
        dy, s_gm = _gate_bwd(dx1, y_m, mod, 2, L, f"gate_m_bwd{l}")
        dz = _mm_nt(dy, w_b.reshape(D, D), tm=tm_mm, tn=_pick(D, [1024, 512]), out_dtype=BF16, name=f"mixout_dgrad{l}")
        gw_b = _mm_tn(z, [dy], nb=1, tka=_pick(D, [1024, 512]), tn=_pick(D, [1024, 512]), ts=ts_tn, name=f"mixout_wgrad{l}")
        if l % 2 == 0:
            qkv, qr, kr, lse = mix
            dq, delta, dkx, dvx, s_sink = _attn_bwd_q(qr, kr, qkv, z, dz, lse, attn_sink, j, L, f"attn_bwd_q{l}")
            dk, dv = _attn_bwd_kv(qr, kr, qkv, dz, lse, delta, dkx, dvx, L, f"attn_bwd_kv{l}")
            du_m, s_gain = _qk_prep_bwd(dq, dk, dv, qkv, tabs, gains, j, n_q, n_kv, f"qk_prep_bwd{l}")
            g_gain[j], g_sink[j] = s_gain[0:2], s_sink[0]
        else:
            (u,) = mix
            d_gb, d_gc, d_val, s_scconv = _sc_act_bwd(u, dz, sc_conv_full, j, L, f"sc_act_bwd{l}")
            du_m = jnp.concatenate([d_gb, d_gc, d_val], axis=1)
            g_sc_conv[j] = s_scconv[0:3]
        dh = _mm_nt_acc([du_m], w_a, tm=tm_mm, out_dtype=F32, name=f"mixin_dgrad{l}")
        gw_a = _mm_tn(h, [du_m], nb=N_DEV, tka=_pick(D, [1024, 512]), tn=w_a.shape[2], ts=ts_tn, name=f"mixin_wgrad{l}")
        dx, s_nm = _norm_mod_bwd(dh, x0, mod, dx1, 0, L, f"norm_m_bwd{l}")
        dmods[l] = jnp.stack([jnp.stack([s_nm[2 * k], s_nm[2 * k + 1], s_gm[k], s_nf[2 * k], s_nf[2 * k + 1], s_gf[k]])
                              for k in range(2)])
        rs_mix = _rs_start([gw_a, gw_b.reshape(N_DEV, -1, D)], f"rs_start_mix{l}")
        sent = rs_mix[4][0, 0]
        rs_flight[l] = (rs_mix, rs_ffn)

    grad_x = dx[L:][None]

    big_w = {"qkv": (attn_w_qkv, m_attn_w_qkv, v_attn_w_qkv), "wo": (attn_w_o, m_attn_w_o, v_attn_w_o),
             "scin": (sc_w_in, m_sc_w_in, v_sc_w_in), "scout": (sc_w_out, m_sc_w_out, v_sc_w_out),
             "up": (ffn_w_up, m_ffn_w_up, v_ffn_w_up), "down": (ffn_w_down, m_ffn_w_down, v_ffn_w_down)}
    big_out = {k: [] for k in big_w}
    for l in reversed(range(depth)):
        j = l // 2
        groups = [(["qkv", "wo"] if l % 2 == 0 else ["scin", "scout"], [j, j]), (["up", "down"], [l, l])]
        for (names, idxs), flight, tag in reversed(list(zip(groups, rs_flight[l], ("mix", "ffn")))):
            send_sems, recv_sems, own, zones, _ = flight
            own, zones = _rs_wait(own, zones, send_sems, recv_sems, dx, f"rs_wait_{tag}{l}")
            for n, li, p, z in zip(names, idxs, own, zones):
                w, m, v = big_w[n]
                big_out[n].insert(0, _adamw_reduced(p, z, me_idx, w, m, v, li, f"adamw_{n}{l}"))
    big_res = {k: [jnp.stack([o[t] for o in outs]) for t in range(4)] for k, outs in big_out.items()}

    n_attn = depth - n_sc
    pack = [jnp.stack(dmods)[:, 0].reshape(-1, 128), jnp.stack(dmods)[:, 1].reshape(-1, 128),
            jnp.stack(g_gain).reshape(-1, 128), jnp.stack(g_sink),
            jnp.stack(g_conv_b).reshape(-1, 128), jnp.stack(g_ffn_conv).reshape(-1, 128),
            jnp.stack(g_sc_conv).reshape(-1, 128)]
    used = [p.shape[0] for p in pack]
    pack = [_pad_rows(p, -(-p.shape[0] // 8) * 8) for p in pack]
    sizes = [p.shape[0] for p in pack]
    flat = jnp.concatenate(pack, axis=0)
    rows = flat.shape[0]
    small_all = _gather_small(flat, "gather_small_grads")
    small_sum = _sum8(small_all, rows, "sum_small_grads")
    offs = [sum(sizes[:k]) for k in range(len(sizes))]
    seg = lambda a, k: a[offs[k]:offs[k] + used[k]]
    dmod_ctx = seg(small_sum, 0).reshape(depth, 6 * D)
    dmod_lat = small_all.reshape(N_DEV, rows, 128)[:, offs[1]:offs[1] + used[1]].reshape(N_DEV, depth, 6 * D)
    g_gain_sum = seg(small_sum, 2).reshape(n_attn, 2, 128)
    g_sink_sum = seg(small_sum, 3)[:n_attn, :n_q]
    g_conv_b_sum = seg(small_sum, 4).reshape(depth, F)
    g_ffn_conv_sum = seg(small_sum, 5).reshape(depth, 3, F)
    g_sc_conv_sum = seg(small_sum, 6).reshape(n_sc, 3, D)

    d_rows = jnp.concatenate([jnp.transpose(dmod_lat, (1, 0, 2)), dmod_ctx[:, None, :],
                              jnp.zeros((depth, 7, 6 * D), F32)], axis=1)
    d_cols = lax.dynamic_slice_in_dim(d_rows, me * ada_c, ada_c, axis=2)
    g_w_ada, dcond_part = _ada_bwd(cond, d_cols, w_ada, "ada_bwd")
    dcond_all = _gather_small(dcond_part, "gather_dcond")
    g_c_ctx = _cctx_grad(dcond_all, jnp.broadcast_to(c_ctx[None, :], (8, D)), "cctx_grad")[0]
    g_b_ada = _sum_rows(d_rows, "b_ada_grad")[:, 0]

    def small_adam(w, g, m, v, name):
        w2 = w.reshape(-1, w.shape[-1])
        d, m2, v2 = _adamw_plain(w2, g.reshape(w2.shape), m.reshape(w2.shape), v.reshape(w2.shape), name)
        return g.reshape(w.shape), d.reshape(w.shape), m2.reshape(w.shape), v2.reshape(w.shape)

    g_sc_conv_mine = lax.dynamic_slice_in_dim(g_sc_conv_sum, me * sc_conv.shape[2], sc_conv.shape[2], axis=2)
    g_ffn_conv_mine = lax.dynamic_slice_in_dim(g_ffn_conv_sum, me * ffn_conv.shape[2], ffn_conv.shape[2], axis=2)
    res = {
        "c_ctx": small_adam(c_ctx[None, :], g_c_ctx[None, :], m_c_ctx[None, :], v_c_ctx[None, :], "adamw_c_ctx"),
        "b_ada": small_adam(b_ada, g_b_ada, m_b_ada, v_b_ada, "adamw_b_ada"),
        "attn_q_gain": small_adam(attn_q_gain, g_gain_sum[:, 0], m_attn_q_gain, v_attn_q_gain, "adamw_q_gain"),
        "attn_k_gain": small_adam(attn_k_gain, g_gain_sum[:, 1], m_attn_k_gain, v_attn_k_gain, "adamw_k_gain"),
        "attn_sink": small_adam(attn_sink, g_sink_sum, m_attn_sink, v_attn_sink, "adamw_sink"),
        "sc_conv": small_adam(sc_conv, g_sc_conv_mine, m_sc_conv, v_sc_conv, "adamw_sc_conv"),
        "ffn_conv": small_adam(ffn_conv, g_ffn_conv_mine, m_ffn_conv, v_ffn_conv, "adamw_ffn_conv"),
        "ffn_conv_b": small_adam(ffn_conv_b, g_conv_b_sum, m_ffn_conv_b, v_ffn_conv_b, "adamw_conv_b"),
    }
    res["c_ctx"] = tuple(t[0] for t in res["c_ctx"])
    res["w_ada"] = (g_w_ada,) + tuple(_adamw_tiled(w_ada, g_w_ada, m_w_ada, v_w_ada, "adamw_w_ada"))
    res["attn_w_qkv"], res["attn_w_o"] = big_res["qkv"], big_res["wo"]
    res["sc_w_in"], res["sc_w_out"] = big_res["scin"], big_res["scout"]
    res["ffn_w_up"], res["ffn_w_down"] = big_res["up"], big_res["down"]

    order = ["c_ctx", "w_ada", "b_ada", "attn_w_qkv", "attn_w_o", "attn_q_gain", "attn_k_gain", "attn_sink",
             "sc_w_in", "sc_conv", "sc_w_out", "ffn_w_up", "ffn_conv", "ffn_conv_b", "ffn_w_down"]
    outs = [loss, grad_x]
    for t in range(4):
        outs += [res[n][t] for n in order]
    return tuple(outs)
```

```python
import functools

import jax
import jax.numpy as jnp
from jax import lax
from jax.experimental import pallas as pl
from jax.experimental.pallas import tpu as pltpu

F32 = jnp.float32
BF16 = jnp.bfloat16
I32 = jnp.int32

N_DEV = 8
HEAD_DIM = 128
GROUP = 4
WINDOW = 128
BLK = 128
GRID_W = 64
ROPE_BASE = 10000.0
EPS = 1e-6
NEG = -1e30
HALO = 16

ADAM_LR = 0.001
ADAM_B1 = 0.9
ADAM_B2 = 0.999
ADAM_EPS = 1e-08
ADAM_WD = 0.01
ADAM_STEP = 10

V7X_VMEM_BYTES = 64 << 20
VMEM_MATMUL = 52 << 20
VMEM_ELEMENTWISE = 44 << 20

MESH = pl.DeviceIdType.MESH
ANY = pl.BlockSpec(memory_space=pl.ANY)
HBM = pl.BlockSpec(memory_space=pltpu.HBM)
SEM = pl.BlockSpec(memory_space=pltpu.SEMAPHORE)
EFFECT = pltpu.SideEffectType.DATAFLOW_SIDE_EFFECTING

NT_DIMS = (((1,), (1,)), ((), ()))
TN_DIMS = (((0,), (0,)), ((), ()))


def _pick(n, cands):
    for t in cands:
        if n % t == 0:
            return t
    raise ValueError(f"no tile for {n} in {cands}")


def _cp(n_axes, vmem=VMEM_ELEMENTWISE):
    return pltpu.CompilerParams(dimension_semantics=("arbitrary",) * n_axes, vmem_limit_bytes=vmem)


def _rows(i, tm, off=0):
    return i * tm + off + lax.broadcasted_iota(I32, (tm, 1), 0)


def _my_pos():
    return lax.axis_index("x"), lax.axis_index("y"), lax.axis_index("c")


def _gather_small(x_shard, name):
    m_per, n = x_shard.shape

    def body(x_ref, out_ref, send_sems, recv_sems, local_sem):
        x, y, c = _my_pos()
        me, sibling = (x, y, c), (x, y, 1 - c)
        chips = [(1 - x, y), (x, 1 - y), (1 - x, 1 - y)]

        def rows(px, py, pc):
            return out_ref.at[pl.ds((4 * px + 2 * py + pc) * m_per, m_per), :]

        def copy(k, block, to, src=None):
            return pltpu.make_async_remote_copy(
                src_ref=rows(*block) if src is None else src, dst_ref=rows(*block),
                send_sem=send_sems.at[k], recv_sem=recv_sems.at[k], device_id=to, device_id_type=MESH)

        mine = pltpu.make_async_copy(x_ref, rows(*me), local_sem)
        mine.start()
        first = [copy(0, me, sibling, src=x_ref)]
        first += [copy(1 + j, me, (*chip, c), src=x_ref) for j, chip in enumerate(chips)]
        for cp in first:
            cp.start()
        passed = [copy(4 + j, (*chip, c), sibling) for j, chip in enumerate(chips)]
        for j, chip in enumerate(chips):
            copy(1 + j, (*chip, c), me).wait_recv()
            passed[j].start()
        copy(0, sibling, me).wait_recv()
        for j, chip in enumerate(chips):
            copy(4 + j, (*chip, 1 - c), me).wait_recv()
        for cp in first + passed:
            cp.wait_send()
        mine.wait()

    return pl.pallas_call(
        body, name=name,
        out_shape=jax.ShapeDtypeStruct((N_DEV * m_per, n), x_shard.dtype),
        in_specs=[pl.BlockSpec(memory_space=pltpu.VMEM)],
        out_specs=pl.BlockSpec(memory_space=pltpu.VMEM),
        scratch_shapes=[pltpu.SemaphoreType.DMA((7,)), pltpu.SemaphoreType.DMA((7,)), pltpu.SemaphoreType.DMA],
        compiler_params=pltpu.CompilerParams(vmem_limit_bytes=VMEM_ELEMENTWISE),
    )(x_shard)


def _peer(k):
    x, y, c = _my_pos()
    b = k + 1
    return ((1 - x) if b & 4 else x, (1 - y) if b & 2 else y, (1 - c) if b & 1 else c)


def _slot(p):
    return 4 * p[0] + 2 * p[1] + p[2]


def _in_hbm(a):
    return pltpu.with_memory_space_constraint(a, pltpu.HBM)


def _gather_start(lands, after, name):
    n = len(lands)

    def body(*refs):
        l_refs, send_sems, recv_sems = refs[:n], refs[n + len(after)], refs[n + len(after) + 1]
        token = refs[2 * n + len(after) + 2]
        me = _slot(_my_pos())
        for a in range(n):
            for k in range(7):
                pltpu.make_async_remote_copy(
                    src_ref=l_refs[a].at[me], dst_ref=l_refs[a].at[me],
                    send_sem=send_sems.at[7 * a + k], recv_sem=recv_sems.at[7 * a + k],
                    device_id=_peer(k), device_id_type=MESH).start()
        token[...] = jnp.zeros_like(token)

    out = pl.pallas_call(
        body, name=name,
        out_shape=(pltpu.SemaphoreType.DMA((7 * n,)), pltpu.SemaphoreType.DMA((7 * n,)),
                   *[pltpu.HBM(a.shape, a.dtype) for a in lands], jax.ShapeDtypeStruct((8, 128), F32)),
        in_specs=[HBM] * n + [ANY] * len(after),
        out_specs=(SEM, SEM, *[HBM] * n, pl.BlockSpec(memory_space=pltpu.VMEM)),
        input_output_aliases={a: 2 + a for a in range(n)},
        compiler_params=pltpu.CompilerParams(has_side_effects=EFFECT),
    )(*[_in_hbm(a) for a in lands], *after)
    return out[0], out[1], list(out[2:2 + n]), out[2 + n]


def _gather_wait(lands, send_sems, recv_sems, after, name):
    n = len(lands)

    def body(*refs):
        l_refs, ss, rs = refs[:n], refs[n], refs[n + 1]
        me = _slot(_my_pos())
        for a in range(n):
            for k in range(7):
                cp = pltpu.make_async_remote_copy(
                    src_ref=l_refs[a].at[me], dst_ref=l_refs[a].at[_slot(_peer(k))],
                    send_sem=ss.at[7 * a + k], recv_sem=rs.at[7 * a + k], device_id=_peer(k), device_id_type=MESH)
                cp.wait_send()
                cp.wait_recv()

    out = pl.pallas_call(
        body, name=name,
        out_shape=tuple(pltpu.HBM(a.shape, a.dtype) for a in lands),
        in_specs=[HBM] * n + [SEM, SEM, ANY], out_specs=tuple([HBM] * n),
        input_output_aliases={a: a for a in range(n)},
        compiler_params=pltpu.CompilerParams(has_side_effects=EFFECT),
    )(*lands, send_sems, recv_sems, after)
    return list(out)


def _rs_start(grads, name):
    n = len(grads)

    def body(*refs):
        g_refs, z_refs, send_sems, recv_sems = refs[:n], refs[n:2 * n], refs[2 * n], refs[2 * n + 1]
        token = refs[4 * n + 2]
        for a in range(n):
            for k in range(7):
                pltpu.make_async_remote_copy(
                    src_ref=g_refs[a].at[_slot(_peer(k))], dst_ref=z_refs[a].at[k],
                    send_sem=send_sems.at[7 * a + k], recv_sem=recv_sems.at[7 * a + k],
                    device_id=_peer(k), device_id_type=MESH).start()
        token[...] = jnp.zeros_like(token)

    zones = [lax.empty((7,) + g.shape[1:], g.dtype) for g in grads]
    out = pl.pallas_call(
        body, name=name,
        out_shape=(pltpu.SemaphoreType.DMA((7 * n,)), pltpu.SemaphoreType.DMA((7 * n,)),
                   *[pltpu.HBM(a.shape, a.dtype) for a in grads], *[pltpu.HBM(z.shape, z.dtype) for z in zones],
                   jax.ShapeDtypeStruct((8, 128), F32)),
        in_specs=[HBM] * (2 * n),
        out_specs=(SEM, SEM, *[HBM] * (2 * n), pl.BlockSpec(memory_space=pltpu.VMEM)),
        input_output_aliases={a: 2 + a for a in range(2 * n)},
        compiler_params=pltpu.CompilerParams(has_side_effects=EFFECT),
    )(*[_in_hbm(a) for a in grads], *[_in_hbm(z) for z in zones])
    return out[0], out[1], list(out[2:2 + n]), list(out[2 + n:2 + 2 * n]), out[2 + 2 * n]


def _rs_wait(grads, zones, send_sems, recv_sems, after, name):
    n = len(grads)

    def body(*refs):
        g_refs, z_refs, ss, rs = refs[:n], refs[n:2 * n], refs[2 * n], refs[2 * n + 1]
        for a in range(n):
            for k in range(7):
                cp = pltpu.make_async_remote_copy(
                    src_ref=g_refs[a].at[_slot(_peer(k))], dst_ref=z_refs[a].at[k],
                    send_sem=ss.at[7 * a + k], recv_sem=rs.at[7 * a + k], device_id=_peer(k), device_id_type=MESH)
                cp.wait_send()
                cp.wait_recv()

    out = pl.pallas_call(
        body, name=name,
        out_shape=tuple(pltpu.HBM(a.shape, a.dtype) for a in list(grads) + list(zones)),
        in_specs=[HBM] * (2 * n) + [SEM, SEM, ANY], out_specs=tuple([HBM] * (2 * n)),
        input_output_aliases={a: a for a in range(2 * n)},
        compiler_params=pltpu.CompilerParams(has_side_effects=EFFECT),
    )(*grads, *zones, send_sems, recv_sems, after)
    return list(out[:n]), list(out[n:])


def _cast_layer(w, l, me_idx, name):
    _, r, c = w.shape
    tr = _pick(r, [512, 256, 128, 64, 32, 16])

    def body(me_ref, w_ref, o_ref):
        o_ref[...] = w_ref[...].astype(BF16)

    return pl.pallas_call(
        body, name=name,
        grid_spec=pltpu.PrefetchScalarGridSpec(
            num_scalar_prefetch=1, grid=(r // tr,),
            in_specs=[pl.BlockSpec((None, tr, c), lambda i, me_ref: (l, i, 0))],
            out_specs=pl.BlockSpec((None, tr, c), lambda i, me_ref: (me_ref[0], i, 0))),
        out_shape=jax.ShapeDtypeStruct((N_DEV, r, c), BF16), compiler_params=_cp(1),
    )(me_idx, w)


def _adam_math(w, g, m, v):
    m2 = ADAM_B1 * m + (1.0 - ADAM_B1) * g
    v2 = ADAM_B2 * v + (1.0 - ADAM_B2) * (g * g)
    m_hat = m2 / (1.0 - ADAM_B1 ** ADAM_STEP)
    v_hat = v2 / (1.0 - ADAM_B2 ** ADAM_STEP)
    delta = -ADAM_LR * (m_hat / (jnp.sqrt(v_hat) + ADAM_EPS) + ADAM_WD * w)
    return delta, m2, v2


def _adamw_reduced(own, zone, me_idx, w, m, v, l, name):
    _, r, c = own.shape
    tr = _pick(r, [256, 128, 64, 32, 16])

    def body(me_ref, p_ref, z_ref, w_ref, m_ref, v_ref, g_out, d_out, m_out, v_out):
        g = p_ref[...].astype(F32)
        for k in range(7):
            g = g + z_ref[k].astype(F32)
        d, m2, v2 = _adam_math(w_ref[...], g, m_ref[...], v_ref[...])
        g_out[...] = g
        d_out[...] = d
        m_out[...] = m2
        v_out[...] = v2

    wspec = pl.BlockSpec((None, tr, c), lambda i, me_ref: (l, i, 0))
    ospec = pl.BlockSpec((tr, c), lambda i, me_ref: (i, 0))
    return pl.pallas_call(
        body, name=name,
        grid_spec=pltpu.PrefetchScalarGridSpec(
            num_scalar_prefetch=1, grid=(r // tr,),
            in_specs=[pl.BlockSpec((None, tr, c), lambda i, me_ref: (me_ref[0], i, 0)),
                      pl.BlockSpec((7, tr, c), lambda i, me_ref: (0, i, 0)), wspec, wspec, wspec],
            out_specs=[ospec] * 4),
        out_shape=[jax.ShapeDtypeStruct((r, c), F32)] * 4, compiler_params=_cp(1),
    )(me_idx, own, zone, w, m, v)


def _adamw_plain(w, g, m, v, name):
    def body(w_ref, g_ref, m_ref, v_ref, d_out, m_out, v_out):
        d, m2, v2 = _adam_math(w_ref[...], g_ref[...], m_ref[...], v_ref[...])
        d_out[...] = d
        m_out[...] = m2
        v_out[...] = v2

    return pl.pallas_call(
        body, name=name, out_shape=[jax.ShapeDtypeStruct(w.shape, F32)] * 3,
        compiler_params=pltpu.CompilerParams(vmem_limit_bytes=VMEM_ELEMENTWISE),
    )(w, g, m, v)


def _adamw_tiled(w, g, m, v, name):
    lyr, r, c = w.shape
    tr = _pick(r, [256, 128, 64, 32, 16, 8])

    def body(w_ref, g_ref, m_ref, v_ref, d_out, m_out, v_out):
        d, m2, v2 = _adam_math(w_ref[...], g_ref[...], m_ref[...], v_ref[...])
        d_out[...] = d
        m_out[...] = m2
        v_out[...] = v2

    spec = pl.BlockSpec((None, tr, c), lambda l, i: (l, i, 0))
    return pl.pallas_call(
        body, name=name, grid=(lyr, r // tr), in_specs=[spec] * 4, out_specs=[spec] * 3,
        out_shape=[jax.ShapeDtypeStruct(w.shape, F32)] * 3, compiler_params=_cp(2),
    )(w, g, m, v)


def _sum8(gathered, rows, name):
    def body(g_ref, o_ref):
        acc = g_ref[0:rows, :]
        for d in range(1, N_DEV):
            acc = acc + g_ref[d * rows:(d + 1) * rows, :]
        o_ref[...] = acc

    return pl.pallas_call(
        body, name=name, out_shape=jax.ShapeDtypeStruct((rows, 128), F32),
        compiler_params=pltpu.CompilerParams(vmem_limit_bytes=VMEM_ELEMENTWISE),
    )(gathered)


MXU_COLS = 256


def _pairable(nb, nc):
    return nb % 2 == 0 and nc % MXU_COLS == MXU_COLS // 2 and nc > MXU_COLS // 2


def _mm_nn(a, b3, *, tm, tn, out_dtype, name):
    M, K = a.shape
    nb, _, nc = b3.shape
    q = nc // tn

    if _pairable(nb, nc) and tn == nc:
        cut = nc - 128

        def pair_body(a_ref, b_ref, o_ref):
            av = a_ref[...]
            mid = jnp.concatenate([b_ref[0, :, cut:nc], b_ref[1, :, 0:128]], axis=1)
            o_ref[:, 0:cut] = jnp.dot(av, b_ref[0, :, 0:cut], preferred_element_type=F32).astype(o_ref.dtype)
            o_ref[:, cut:nc + 128] = jnp.dot(av, mid, preferred_element_type=F32).astype(o_ref.dtype)
            o_ref[:, nc + 128:2 * nc] = jnp.dot(av, b_ref[1, :, 128:nc], preferred_element_type=F32).astype(o_ref.dtype)

        return pl.pallas_call(
            pair_body, name=name, grid=(nb // 2, M // tm),
            in_specs=[pl.BlockSpec((tm, K), lambda j, i: (i, 0)),
                      pl.BlockSpec((2, K, nc), lambda j, i: (j, 0, 0))],
            out_specs=pl.BlockSpec((tm, 2 * nc), lambda j, i: (i, j)),
            out_shape=jax.ShapeDtypeStruct((M, nb * nc), out_dtype), compiler_params=_cp(2, VMEM_MATMUL),
        )(a, b3)

    def body(a_ref, b_ref, o_ref):
        o_ref[...] = jnp.dot(a_ref[...], b_ref[...], preferred_element_type=F32).astype(o_ref.dtype)

    return pl.pallas_call(
        body, name=name, grid=(nb * q, M // tm),
        in_specs=[pl.BlockSpec((tm, K), lambda j, i: (i, 0)),
                  pl.BlockSpec((None, K, tn), lambda j, i: (j // q, 0, j % q))],
        out_specs=pl.BlockSpec((tm, tn), lambda j, i: (i, j)),
        out_shape=jax.ShapeDtypeStruct((M, nb * nc), out_dtype), compiler_params=_cp(2, VMEM_MATMUL),
    )(a, b3)


def _mm_nn_resid(a, b2, x_old, mod, gate_row, n_ctx, *, tm, tn, name):
    M, K = a.shape
    N = b2.shape[1]

    def body(a_ref, b_ref, x_ref, mod_ref, y_ref, xn_ref):
        y = jnp.dot(a_ref[...], b_ref[...], preferred_element_type=F32)
        is_ctx = _rows(pl.program_id(1), tm) < n_ctx
        g = jnp.where(is_ctx, mod_ref[0, gate_row:gate_row + 1, :], mod_ref[1, gate_row:gate_row + 1, :])
        y_ref[...] = y.astype(BF16)
        xn_ref[...] = x_ref[...] + g * y

    return pl.pallas_call(
        body, name=name, grid=(N // tn, M // tm),
        in_specs=[pl.BlockSpec((tm, K), lambda j, i: (i, 0)),
                  pl.BlockSpec((K, tn), lambda j, i: (0, j)),
                  pl.BlockSpec((tm, tn), lambda j, i: (i, j)),
                  pl.BlockSpec((2, 6, tn), lambda j, i: (0, 0, j))],
        out_specs=[pl.BlockSpec((tm, tn), lambda j, i: (i, j))] * 2,
        out_shape=[jax.ShapeDtypeStruct((M, N), BF16), jax.ShapeDtypeStruct((M, N), F32)],
        compiler_params=_cp(2, VMEM_MATMUL),
    )(a, b2, x_old, mod)


def _mm_nt_acc(parts, w3, *, tm, out_dtype, name):
    n_parts = len(parts)
    M = parts[0].shape[0]
    nb, K, nc = w3.shape
    pair = _pairable(nb // n_parts, nc)
    g = 2 if pair else 1
    steps, per = nb // g, nb // n_parts // g
    cut = nc - 128

    def nt(d, w):
        return lax.dot_general(d, w, NT_DIMS, preferred_element_type=F32)

    def contribution(d_ref, w_ref):
        if not pair:
            return nt(d_ref[...], w_ref[0])
        mid = jnp.concatenate([w_ref[0, :, cut:nc], w_ref[1, :, 0:128]], axis=1)
        return (nt(d_ref[:, 0:cut], w_ref[0, :, 0:cut]) + nt(d_ref[:, cut:nc + 128], mid)
                + nt(d_ref[:, nc + 128:2 * nc], w_ref[1, :, 128:nc]))

    def body(*refs):
        dy_refs, w_ref = refs[:n_parts], refs[n_parts]
        o_ref, acc_ref = refs[n_parts + 1], refs[n_parts + 2]
        s = pl.program_id(1)

        @pl.when(s == 0)
        def _():
            acc_ref[...] = jnp.zeros_like(acc_ref)

        for p in range(n_parts):
            @pl.when(s // per == p)
            def _(p=p):
                acc_ref[...] += contribution(dy_refs[p], w_ref)

        @pl.when(s == steps - 1)
        def _():
            o_ref[...] = acc_ref[...].astype(o_ref.dtype)

    def part_spec(p):
        return pl.BlockSpec((tm, g * nc), lambda i, s: (i, jnp.clip(s - p * per, 0, per - 1)))

    return pl.pallas_call(
        body, name=name, grid=(M // tm, steps),
        in_specs=[part_spec(p) for p in range(n_parts)] + [pl.BlockSpec((g, K, nc), lambda i, s: (s, 0, 0))],
        out_specs=pl.BlockSpec((tm, K), lambda i, s: (i, 0)),
        out_shape=jax.ShapeDtypeStruct((M, K), out_dtype),
        scratch_shapes=[pltpu.VMEM((tm, K), F32)], compiler_params=_cp(2, VMEM_MATMUL),
    )(*parts, w3)


def _mm_nt(dy, w2, *, tm, tn, out_dtype, name):
    M, N = dy.shape
    K = w2.shape[0]

    def body(dy_ref, w_ref, o_ref):
        o_ref[...] = lax.dot_general(dy_ref[...], w_ref[...], NT_DIMS,
                                     preferred_element_type=F32).astype(o_ref.dtype)

    return pl.pallas_call(
        body, name=name, grid=(K // tn, M // tm),
        in_specs=[pl.BlockSpec((tm, N), lambda j, i: (i, 0)),
                  pl.BlockSpec((tn, N), lambda j, i: (j, 0))],
        out_specs=pl.BlockSpec((tm, tn), lambda j, i: (i, j)),
        out_shape=jax.ShapeDtypeStruct((M, K), out_dtype), compiler_params=_cp(2, VMEM_MATMUL),
    )(dy, w2)


def _mm_tn(a, parts, *, nb, tka, tn, ts, name):
    n_parts = len(parts)
    S, Ka = a.shape
    N = sum(p.shape[1] for p in parts)
    nc = N // nb
    q = nc // tn
    nk = S // ts
    g = 2 if (tn == nc and n_parts == 1 and _pairable(nb, nc)) else 1
    per = nb * q // n_parts // g

    def body(a_ref, *rest):
        dy_refs, o_ref, acc_ref = rest[:n_parts], rest[n_parts], rest[n_parts + 1]
        j = pl.program_id(0)
        k = pl.program_id(2)

        @pl.when(k == 0)
        def _():
            acc_ref[...] = jnp.zeros_like(acc_ref)

        for p in range(n_parts):
            @pl.when(j // per == p)
            def _(p=p):
                acc_ref[...] += lax.dot_general(a_ref[...], dy_refs[p][...], TN_DIMS, preferred_element_type=F32)

        @pl.when(k == nk - 1)
        def _():
            for t in range(g):
                o_ref[t] = acc_ref[:, t * tn:(t + 1) * tn].astype(o_ref.dtype)

    def part_spec(p):
        return pl.BlockSpec((ts, g * tn), lambda j, ia, k: (jnp.where(j // per == p, k, 0),
                                                            jnp.clip(j - p * per, 0, per - 1)))

    return pl.pallas_call(
        body, name=name, grid=(nb * q // g, Ka // tka, nk),
        in_specs=[pl.BlockSpec((ts, tka), lambda j, ia, k: (k, ia))] + [part_spec(p) for p in range(n_parts)],
        out_specs=pl.BlockSpec((g, tka, tn), lambda j, ia, k: (j // q, ia, j % q)),
        out_shape=jax.ShapeDtypeStruct((nb, Ka, nc), BF16),
        scratch_shapes=[pltpu.VMEM((tka, g * tn), F32)], compiler_params=_cp(3, VMEM_MATMUL),
    )(a, *parts)


def _norm_mod(x, mod, row0, n_ctx, name):
    S, D = x.shape
    tm = _pick(S, [256, 128])

    def body(x_ref, mod_ref, h_ref):
        xv = x_ref[...]
        r = lax.rsqrt(jnp.mean(xv * xv, axis=-1, keepdims=True) + EPS)
        is_ctx = _rows(pl.program_id(0), tm) < n_ctx
        sh = jnp.where(is_ctx, mod_ref[0, row0:row0 + 1, :], mod_ref[1, row0:row0 + 1, :])
        sc = jnp.where(is_ctx, mod_ref[0, row0 + 1:row0 + 2, :], mod_ref[1, row0 + 1:row0 + 2, :])
        h_ref[...] = (xv * r * (1.0 + sc) + sh).astype(BF16)

    return pl.pallas_call(
        body, name=name, grid=(S // tm,),
        in_specs=[pl.BlockSpec((tm, D), lambda i: (i, 0)), pl.BlockSpec((2, 6, D), lambda i: (0, 0, 0))],
        out_specs=pl.BlockSpec((tm, D), lambda i: (i, 0)),
        out_shape=jax.ShapeDtypeStruct((S, D), BF16), compiler_params=_cp(1),
    )(x, mod)


def _norm_mod_bwd(dh, x, mod, dx_res, row0, n_ctx, name):
    S, D = x.shape
    tm = _pick(S, [256, 128])

    def body(dh_ref, x_ref, mod_ref, res_ref, dx_ref, acc_ref):
        i = pl.program_id(0)

        @pl.when(i == 0)
        def _():
            acc_ref[...] = jnp.zeros_like(acc_ref)

        xv = x_ref[...]
        dh_v = dh_ref[...].astype(F32)
        r = lax.rsqrt(jnp.mean(xv * xv, axis=-1, keepdims=True) + EPS)
        xhat = xv * r
        is_ctx = _rows(i, tm) < n_ctx
        sc = jnp.where(is_ctx, mod_ref[0, row0 + 1:row0 + 2, :], mod_ref[1, row0 + 1:row0 + 2, :])
        dxhat = dh_v * (1.0 + sc)
        dx_ref[...] = res_ref[...] + r * (dxhat - xhat * jnp.mean(dxhat * xhat, axis=-1, keepdims=True))
        dsc = dh_v * xhat
        zero = jnp.zeros_like(dh_v)
        sums = [jnp.sum(jnp.where(is_ctx, dh_v, zero), axis=0, keepdims=True),
                jnp.sum(jnp.where(is_ctx, dsc, zero), axis=0, keepdims=True),
                jnp.sum(jnp.where(is_ctx, zero, dh_v), axis=0, keepdims=True),
                jnp.sum(jnp.where(is_ctx, zero, dsc), axis=0, keepdims=True)]
        rid = lax.broadcasted_iota(I32, (8, D), 0)
        upd = jnp.zeros((8, D), F32)
        for k, s in enumerate(sums):
            upd = upd + jnp.where(rid == k, s, 0.0)
        acc_ref[...] += upd

    return pl.pallas_call(
        body, name=name, grid=(S // tm,),
        in_specs=[pl.BlockSpec((tm, D), lambda i: (i, 0)), pl.BlockSpec((tm, D), lambda i: (i, 0)),
                  pl.BlockSpec((2, 6, D), lambda i: (0, 0, 0)), pl.BlockSpec((tm, D), lambda i: (i, 0))],
        out_specs=[pl.BlockSpec((tm, D), lambda i: (i, 0)), pl.BlockSpec((8, D), lambda i: (0, 0))],
        out_shape=[jax.ShapeDtypeStruct((S, D), F32), jax.ShapeDtypeStruct((8, D), F32)],
        compiler_params=_cp(1),
    )(dh, x, mod, dx_res)


def _gate_bwd(dx, y, mod, gate_row, n_ctx, name):
    S, D = dx.shape
    tm = _pick(S, [256, 128])

    def body(dx_ref, y_ref, mod_ref, dy_ref, acc_ref):
        i = pl.program_id(0)

        @pl.when(i == 0)
        def _():
            acc_ref[...] = jnp.zeros_like(acc_ref)

        dxv = dx_ref[...]
        is_ctx = _rows(i, tm) < n_ctx
        g = jnp.where(is_ctx, mod_ref[0, gate_row:gate_row + 1, :], mod_ref[1, gate_row:gate_row + 1, :])
        dy_ref[...] = (g * dxv).astype(BF16)
        prod = dxv * y_ref[...].astype(F32)
        zero = jnp.zeros_like(prod)
        s_ctx = jnp.sum(jnp.where(is_ctx, prod, zero), axis=0, keepdims=True)
        s_lat = jnp.sum(jnp.where(is_ctx, zero, prod), axis=0, keepdims=True)
        rid = lax.broadcasted_iota(I32, (8, D), 0)
        acc_ref[...] += jnp.where(rid == 0, s_ctx, 0.0) + jnp.where(rid == 1, s_lat, 0.0)

    return pl.pallas_call(
        body, name=name, grid=(S // tm,),
        in_specs=[pl.BlockSpec((tm, D), lambda i: (i, 0)), pl.BlockSpec((tm, D), lambda i: (i, 0)),
                  pl.BlockSpec((2, 6, D), lambda i: (0, 0, 0))],
        out_specs=[pl.BlockSpec((tm, D), lambda i: (i, 0)), pl.BlockSpec((8, D), lambda i: (0, 0))],
        out_shape=[jax.ShapeDtypeStruct((S, D), BF16), jax.ShapeDtypeStruct((8, D), F32)],
        compiler_params=_cp(1),
    )(dx, y, mod)


def _loss_grad(x, target, n_ctx, name):
    S, D = x.shape
    tm = _pick(n_ctx, [256, 128])
    nct = n_ctx // tm

    def body(x_ref, t_ref, dx_ref, tot_ref, acc_ref):
        i = pl.program_id(0)

        @pl.when(i == 0)
        def _():
            acc_ref[...] = jnp.zeros_like(acc_ref)

        @pl.when(i < nct)
        def _():
            dx_ref[...] = jnp.zeros_like(dx_ref)

        @pl.when(i >= nct)
        def _():
            err = x_ref[...] - t_ref[...]
            dx_ref[...] = err * (1.0 / D)
            acc_ref[...] += jnp.sum(err * err, axis=0, keepdims=True)

        @pl.when(i == S // tm - 1)
        def _():
            tot = jnp.sum(acc_ref[...], axis=1, keepdims=True) * (0.5 / D)
            tot_ref[...] = jnp.broadcast_to(tot, tot_ref.shape)

    return pl.pallas_call(
        body, name=name, grid=(S // tm,),
        in_specs=[pl.BlockSpec((tm, D), lambda i: (i, 0)),
                  pl.BlockSpec((tm, D), lambda i: (jnp.maximum(i - nct, 0), 0))],
        out_specs=[pl.BlockSpec((tm, D), lambda i: (i, 0)), pl.BlockSpec((1, 128), lambda i: (0, 0))],
        out_shape=[jax.ShapeDtypeStruct((S, D), F32), jax.ShapeDtypeStruct((1, 128), F32)],
        scratch_shapes=[pltpu.VMEM((1, D), F32)], compiler_params=_cp(1),
    )(x, target)


def _halo_specs(tm, tc, S, col_off):
    per = tm // HALO
    last = S // HALO - 1
    return [pl.BlockSpec((HALO, tc), lambda j, i: (jnp.maximum(i * per - 1, 0), j + col_off)),
            pl.BlockSpec((tm, tc), lambda j, i: (i, j + col_off)),
            pl.BlockSpec((HALO, tc), lambda j, i: (jnp.minimum((i + 1) * per, last), j + col_off))]


def _ext(p_ref, m_ref, n_ref):
    return jnp.concatenate([p_ref[...], m_ref[...], n_ref[...]], axis=0).astype(F32)


def _links(i, tm, S, n_ctx):
    n = tm + 2 * HALO
    rid = i * tm - HALO + lax.broadcasted_iota(I32, (n, 1), 0)
    has_prev = (rid != 0) & (rid != n_ctx)
    has_next = (rid != n_ctx - 1) & (rid != S - 1)
    return has_prev, has_next


def _up(x):
    return pltpu.roll(x, 1, 0)


def _dn(x):
    return pltpu.roll(x, x.shape[0] - 1, 0)


def _conv3(x, w_ref, has_prev, has_next):
    return (w_ref[0:1, :] * jnp.where(has_prev, _up(x), 0.0) + w_ref[1:2, :] * x
            + w_ref[2:3, :] * jnp.where(has_next, _dn(x), 0.0))


def _conv3_t(d, w_ref, has_prev, has_next):
    return (w_ref[0:1, :] * jnp.where(has_next, _dn(d), 0.0) + w_ref[1:2, :] * d
            + w_ref[2:3, :] * jnp.where(has_prev, _up(d), 0.0))


def _conv3_wgrad(d, x, has_prev, has_next, extra=None):
    c = slice(HALO, d.shape[0] - HALO)
    taps = [jnp.where(has_prev, _up(x), 0.0), x, jnp.where(has_next, _dn(x), 0.0)]
    sums = [jnp.sum((d * t)[c], axis=0, keepdims=True) for t in taps]
    if extra is not None:
        sums.append(jnp.sum(extra[c], axis=0, keepdims=True))
    rid = lax.broadcasted_iota(I32, (8, d.shape[1]), 0)
    upd = jnp.zeros((8, d.shape[1]), F32)
    for k, s in enumerate(sums):
        upd = upd + jnp.where(rid == k, s, 0.0)
    return upd


def _sigmoid(x):
    return 1.0 / (1.0 + jnp.exp(-x))


def _ffn_act(u, conv_w, conv_b, l, n_ctx, name):
    S, F2 = u.shape
    F = F2 // 2
    tm = _pick(S, [384, 256, 128])
    tc = _pick(F, [1408, 512, 256, 128])
    nj = F // tc

    def body(gp, gm, gn, v_ref, w_ref, b_ref, a_ref):
        has_prev, has_next = _links(pl.program_id(1), tm, S, n_ctx)
        gc = _conv3(_ext(gp, gm, gn), w_ref, has_prev, has_next)[HALO:HALO + tm] + b_ref[...]
        a_ref[...] = (gc * _sigmoid(gc) * v_ref[...].astype(F32)).astype(BF16)

    return pl.pallas_call(
        body, name=name, grid=(nj, S // tm),
        in_specs=_halo_specs(tm, tc, S, 0) + [
            pl.BlockSpec((tm, tc), lambda j, i: (i, j + nj)),
            pl.BlockSpec((None, 3, tc), lambda j, i: (l, 0, j)),
            pl.BlockSpec((None, 1, tc), lambda j, i: (l, 0, j))],
        out_specs=pl.BlockSpec((tm, tc), lambda j, i: (i, j)),
        out_shape=jax.ShapeDtypeStruct((S, F), BF16), compiler_params=_cp(2),
    )(u, u, u, u, conv_w, conv_b)


def _ffn_act_bwd(u, da, conv_w, conv_b, l, n_ctx, name):
    S, F2 = u.shape
    F = F2 // 2
    tm = _pick(S, [384, 256, 128])
    tc = _pick(F, [1408, 512, 256, 128])
    nj = F // tc

    def body(gp, gm, gn, vp, vm, vn, dp, dm, dn_, w_ref, b_ref, dg_ref, dv_ref, acc_ref):
        i = pl.program_id(1)

        @pl.when(i == 0)
        def _():
            acc_ref[...] = jnp.zeros_like(acc_ref)

        has_prev, has_next = _links(i, tm, S, n_ctx)
        g = _ext(gp, gm, gn)
        val = _ext(vp, vm, vn)
        d_a = _ext(dp, dm, dn_)
        gc = _conv3(g, w_ref, has_prev, has_next) + b_ref[...]
        sg = _sigmoid(gc)
        dgc = d_a * val * (sg * (1.0 + gc * (1.0 - sg)))
        c = slice(HALO, HALO + tm)
        dv_ref[...] = (d_a * gc * sg)[c].astype(BF16)
        dg_ref[...] = _conv3_t(dgc, w_ref, has_prev, has_next)[c].astype(BF16)
        acc_ref[...] += _conv3_wgrad(dgc, g, has_prev, has_next, extra=dgc)

    return pl.pallas_call(
        body, name=name, grid=(nj, S // tm),
        in_specs=_halo_specs(tm, tc, S, 0) + _halo_specs(tm, tc, S, nj) + _halo_specs(tm, tc, S, 0) + [
            pl.BlockSpec((None, 3, tc), lambda j, i: (l, 0, j)),
            pl.BlockSpec((None, 1, tc), lambda j, i: (l, 0, j))],
        out_specs=[pl.BlockSpec((tm, tc), lambda j, i: (i, j))] * 2 + [pl.BlockSpec((8, tc), lambda j, i: (0, j))],
        out_shape=[jax.ShapeDtypeStruct((S, F), BF16)] * 2 + [jax.ShapeDtypeStruct((8, F), F32)],
        compiler_params=_cp(2),
    )(u, u, u, u, u, u, da, da, da, conv_w, conv_b)


def _sc_act(u, conv_w, l, n_ctx, name):
    S, D3 = u.shape
    D = D3 // 3
    tm = _pick(S, [384, 256, 128])
    tc = _pick(D, [1024, 512, 256, 128])
    nj = D // tc

    def body(b_ref, cp, cm, cn, vp, vm, vn, w_ref, z_ref):
        has_prev, has_next = _links(pl.program_id(1), tm, S, n_ctx)
        t = _ext(cp, cm, cn) * _ext(vp, vm, vn)
        cv = _conv3(t, w_ref, has_prev, has_next)[HALO:HALO + tm]
        z_ref[...] = (b_ref[...].astype(F32) * cv).astype(BF16)

    return pl.pallas_call(
        body, name=name, grid=(nj, S // tm),
        in_specs=[pl.BlockSpec((tm, tc), lambda j, i: (i, j))] + _halo_specs(tm, tc, S, nj)
        + _halo_specs(tm, tc, S, 2 * nj) + [pl.BlockSpec((None, 3, tc), lambda j, i: (l, 0, j))],
        out_specs=pl.BlockSpec((tm, tc), lambda j, i: (i, j)),
        out_shape=jax.ShapeDtypeStruct((S, D), BF16), compiler_params=_cp(2),
    )(u, u, u, u, u, u, u, conv_w)


def _sc_act_bwd(u, dz, conv_w, l, n_ctx, name):
    S, D3 = u.shape
    D = D3 // 3
    tm = _pick(S, [384, 256, 128])
    tc = _pick(D, [1024, 512, 256, 128])
    nj = D // tc

    def body(bp, bm, bn, cp, cm, cn, vp, vm, vn, zp, zm, zn, w_ref, db_ref, dc_ref, dv_ref, acc_ref):
        i = pl.program_id(1)

        @pl.when(i == 0)
        def _():
            acc_ref[...] = jnp.zeros_like(acc_ref)

        has_prev, has_next = _links(i, tm, S, n_ctx)
        gb = _ext(bp, bm, bn)
        gcv = _ext(cp, cm, cn)
        val = _ext(vp, vm, vn)
        d_z = _ext(zp, zm, zn)
        t = gcv * val
        c = slice(HALO, HALO + tm)
        db_ref[...] = (d_z * _conv3(t, w_ref, has_prev, has_next))[c].astype(BF16)
        dcv = d_z * gb
        dt = _conv3_t(dcv, w_ref, has_prev, has_next)
        dc_ref[...] = (dt * val)[c].astype(BF16)
        dv_ref[...] = (dt * gcv)[c].astype(BF16)
        acc_ref[...] += _conv3_wgrad(dcv, t, has_prev, has_next)

    return pl.pallas_call(
        body, name=name, grid=(nj, S // tm),
        in_specs=_halo_specs(tm, tc, S, 0) + _halo_specs(tm, tc, S, nj) + _halo_specs(tm, tc, S, 2 * nj)
        + _halo_specs(tm, tc, S, 0) + [pl.BlockSpec((None, 3, tc), lambda j, i: (l, 0, j))],
        out_specs=[pl.BlockSpec((tm, tc), lambda j, i: (i, j))] * 3 + [pl.BlockSpec((8, tc), lambda j, i: (0, j))],
        out_shape=[jax.ShapeDtypeStruct((S, D), BF16)] * 3 + [jax.ShapeDtypeStruct((8, D), F32)],
        compiler_params=_cp(2),
    )(u, u, u, u, u, u, u, u, u, dz, dz, dz, conv_w)


def _rope_tables(T, n_ctx):
    rows = T // GRID_W
    pairs = HEAD_DIM // 4
    row = jnp.repeat(jnp.arange(rows), GRID_W).astype(F32)
    col = jnp.tile(jnp.arange(GRID_W), rows).astype(F32)
    inv = ROPE_BASE ** (-jnp.arange(pairs, dtype=F32) / pairs)
    ang = jnp.concatenate([row[:, None] * inv, row[:, None] * inv, col[:, None] * inv, col[:, None] * inv], axis=1)
    cos, sin = jnp.cos(ang), jnp.sin(ang)
    first = (jnp.arange(HEAD_DIM) % (2 * pairs)) < pairs
    sin_a = jnp.where(first, -sin, 0.0)
    sin_b = jnp.where(first, 0.0, sin)
    pad = jnp.zeros((n_ctx, HEAD_DIM), F32)
    return (jnp.concatenate([pad + 1.0, cos], axis=0), jnp.concatenate([pad, sin_a], axis=0),
            jnp.concatenate([pad, sin_b], axis=0))


def _qk_prep(qkv, tabs, gains, l, n_q, n_kv, name):
    S = qkv.shape[0]
    W = qkv.shape[1]
    tm = _pick(S, [256, 128])
    cos, sin_a, sin_b = tabs

    def body(x_ref, cos_ref, sa_ref, sb_ref, g_ref, q_ref, k_ref):
        cs, sa, sb = cos_ref[...], sa_ref[...], sb_ref[...]
        for h in range(n_q + n_kv):
            xv = x_ref[:, h * 128:(h + 1) * 128].astype(F32)
            r = lax.rsqrt(jnp.mean(xv * xv, axis=-1, keepdims=True) + EPS)
            gain = g_ref[0:1, :] if h < n_q else g_ref[1:2, :]
            y = xv * r * gain
            out = (y * cs + pltpu.roll(y, 96, 1) * sa + pltpu.roll(y, 32, 1) * sb).astype(BF16)
            if h < n_q:
                q_ref[:, h * 128:(h + 1) * 128] = out
            else:
                k_ref[:, (h - n_q) * 128:(h - n_q + 1) * 128] = out

    tspec = pl.BlockSpec((tm, 128), lambda i: (i, 0))
    return pl.pallas_call(
        body, name=name, grid=(S // tm,),
        in_specs=[pl.BlockSpec((tm, W), lambda i: (i, 0)), tspec, tspec, tspec,
                  pl.BlockSpec((None, 2, 128), lambda i: (l, 0, 0))],
        out_specs=[pl.BlockSpec((tm, n_q * 128), lambda i: (i, 0)), pl.BlockSpec((tm, n_kv * 128), lambda i: (i, 0))],
        out_shape=[jax.ShapeDtypeStruct((S, n_q * 128), BF16), jax.ShapeDtypeStruct((S, n_kv * 128), BF16)],
        compiler_params=_cp(1),
    )(qkv, cos, sin_a, sin_b, gains)


def _qk_prep_bwd(dq, dk, dv, qkv, tabs, gains, l, n_q, n_kv, name):
    S, W = qkv.shape
    tm = _pick(S, [256, 128])
    cos, sin_a, sin_b = tabs

    def body(dq_ref, dk_ref, dv_ref, x_ref, cos_ref, sa_ref, sb_ref, g_ref, o_ref, acc_ref):
        @pl.when(pl.program_id(0) == 0)
        def _():
            acc_ref[...] = jnp.zeros_like(acc_ref)

        cs, sa, sb = cos_ref[...], sa_ref[...], sb_ref[...]
        dgq = jnp.zeros((1, 128), F32)
        dgk = jnp.zeros((1, 128), F32)
        for h in range(n_q + n_kv):
            if h < n_q:
                d_out = dq_ref[:, h * 128:(h + 1) * 128]
                gain = g_ref[0:1, :]
            else:
                d_out = dk_ref[:, (h - n_q) * 128:(h - n_q + 1) * 128]
                gain = g_ref[1:2, :]
            dy = d_out * cs + pltpu.roll(d_out * sa, 32, 1) + pltpu.roll(d_out * sb, 96, 1)
            xv = x_ref[:, h * 128:(h + 1) * 128].astype(F32)
            r = lax.rsqrt(jnp.mean(xv * xv, axis=-1, keepdims=True) + EPS)
            xhat = xv * r
            dg = jnp.sum(dy * xhat, axis=0, keepdims=True)
            if h < n_q:
                dgq = dgq + dg
            else:
                dgk = dgk + dg
            dxhat = dy * gain
            dx = r * (dxhat - xhat * jnp.mean(dxhat * xhat, axis=-1, keepdims=True))
            o_ref[:, h * 128:(h + 1) * 128] = dx.astype(BF16)
        v0 = (n_q + n_kv) * 128
        o_ref[:, v0:] = dv_ref[...].astype(BF16)
        rid = lax.broadcasted_iota(I32, (8, 128), 0)
        acc_ref[...] += jnp.where(rid == 0, dgq, 0.0) + jnp.where(rid == 1, dgk, 0.0)

    tspec = pl.BlockSpec((tm, 128), lambda i: (i, 0))
    return pl.pallas_call(
        body, name=name, grid=(S // tm,),
        in_specs=[pl.BlockSpec((tm, n_q * 128), lambda i: (i, 0)), pl.BlockSpec((tm, n_kv * 128), lambda i: (i, 0)),
                  pl.BlockSpec((tm, n_kv * 128), lambda i: (i, 0)), pl.BlockSpec((tm, W), lambda i: (i, 0)),
                  tspec, tspec, tspec, pl.BlockSpec((None, 2, 128), lambda i: (l, 0, 0))],
        out_specs=[pl.BlockSpec((tm, W), lambda i: (i, 0)), pl.BlockSpec((8, 128), lambda i: (0, 0))],
        out_shape=[jax.ShapeDtypeStruct((S, W), BF16), jax.ShapeDtypeStruct((8, 128), F32)],
        compiler_params=_cp(1),
    )(dq, dk, dv, qkv, cos, sin_a, sin_b, gains)


def _band_specs(width, col, nb, n_ctx):
    return [pl.BlockSpec((BLK, width), lambda i: (jnp.maximum(i - 1, 0), col)),
            pl.BlockSpec((BLK, width), lambda i: (i, col)),
            pl.BlockSpec((BLK, width), lambda i: (jnp.minimum(i + 1, nb - 1), col)),
            pl.BlockSpec((n_ctx, width), lambda i: (0, col))]


def _q_side_mask(i, S, n_ctx):
    shape = (GROUP * BLK, 3 * BLK + n_ctx)
    a = lax.broadcasted_iota(I32, shape, 0) & (BLK - 1)
    kk = lax.broadcasted_iota(I32, shape, 1)
    rq = i * BLK + a
    rk = (i - 1) * BLK + kk
    band = (rq >= n_ctx) & (rk >= n_ctx) & (rk < S) & (jnp.abs(rq - rk) <= WINDOW)
    return (kk >= 3 * BLK) | band


def _stack_heads(ref, g):
    return jnp.concatenate([ref[:, (GROUP * g + hh) * 128:(GROUP * g + hh + 1) * 128] for hh in range(GROUP)], axis=0)


def _stack_cols(ref, g):
    return jnp.concatenate([ref[:, GROUP * g + hh:GROUP * g + hh + 1] for hh in range(GROUP)], axis=0)


def _sink_col(sink_ref, l, g):
    return jnp.concatenate([jnp.full((BLK, 1), sink_ref[l, GROUP * g + hh], F32) for hh in range(GROUP)], axis=0)


def _attn_fwd(q, k, qkv, sink, l, n_ctx, name):
    S, DQ = q.shape
    DK = k.shape[1]
    n_kv = DK // 128
    nb = S // BLK
    vcol = (DQ + DK) // DK
    scale = HEAD_DIM ** -0.5

    def body(sink_ref, q_ref, kp, kc, kn, kx, vp, vc, vn, vx, o_ref, lse_ref):
        i = pl.program_id(0)
        mask = _q_side_mask(i, S, n_ctx)
        lane = lax.broadcasted_iota(I32, (BLK, 128), 1)
        lse_tile = jnp.zeros((BLK, 128), F32)
        outs = []
        for g in range(n_kv):
            sl = slice(g * 128, (g + 1) * 128)
            kcat = jnp.concatenate([kp[:, sl], kc[:, sl], kn[:, sl], kx[:, sl]], axis=0)
            vcat = jnp.concatenate([vp[:, sl], vc[:, sl], vn[:, sl], vx[:, sl]], axis=0)
            s = lax.dot_general(_stack_heads(q_ref, g), kcat, NT_DIMS, preferred_element_type=F32) * scale
            s = jnp.where(mask, s, NEG)
            sk = _sink_col(sink_ref, l, g)
            m = jnp.maximum(jnp.max(s, axis=1, keepdims=True), sk)
            e = jnp.exp(s - m)
            den = jnp.sum(e, axis=1, keepdims=True) + jnp.exp(sk - m)
            p = (e / den).astype(BF16)
            o = jnp.dot(p, vcat, preferred_element_type=F32)
            lse = m + jnp.log(den)
            for hh in range(GROUP):
                h = GROUP * g + hh
                outs.append(o[hh * BLK:(hh + 1) * BLK].astype(BF16))
                lse_tile = jnp.where(lane == h, lse[hh * BLK:(hh + 1) * BLK], lse_tile)
        o_ref[...] = jnp.concatenate(outs, axis=1)
        lse_ref[...] = lse_tile

    return pl.pallas_call(
        body, name=name, grid=(nb,),
        in_specs=[pl.BlockSpec(memory_space=pltpu.SMEM), pl.BlockSpec((BLK, DQ), lambda i: (i, 0))]
        + _band_specs(DK, 0, nb, n_ctx) + _band_specs(DK, vcol, nb, n_ctx),
        out_specs=[pl.BlockSpec((BLK, DQ), lambda i: (i, 0)), pl.BlockSpec((BLK, 128), lambda i: (i, 0))],
        out_shape=[jax.ShapeDtypeStruct((S, DQ), BF16), jax.ShapeDtypeStruct((S, 128), F32)],
        compiler_params=_cp(1),
    )(sink, q, k, k, k, k, qkv, qkv, qkv, qkv)


def _attn_bwd_q(q, k, qkv, o, do, lse, sink, l, n_ctx, name):
    S, DQ = q.shape
    DK = k.shape[1]
    n_kv = DK // 128
    nb = S // BLK
    vcol = (DQ + DK) // DK
    scale = HEAD_DIM ** -0.5

    def body(sink_ref, q_ref, kp, kc, kn, kx, vp, vc, vn, vx, o_ref, do_ref, lse_ref,
             dq_ref, delta_ref, dkx_ref, dvx_ref, dsink_ref):
        i = pl.program_id(0)

        @pl.when(i == 0)
        def _():
            dkx_ref[...] = jnp.zeros_like(dkx_ref)
            dvx_ref[...] = jnp.zeros_like(dvx_ref)
            dsink_ref[...] = jnp.zeros_like(dsink_ref)

        mask = _q_side_mask(i, S, n_ctx)
        lane = lax.broadcasted_iota(I32, (BLK, 128), 1)
        lane8 = lax.broadcasted_iota(I32, (8, 128), 1)
        row8 = lax.broadcasted_iota(I32, (8, 128), 0)
        delta_tile = jnp.zeros((BLK, 128), F32)
        dsink_upd = jnp.zeros((8, 128), F32)
        dqs, dkx_upd, dvx_upd = [], [], []
        for g in range(n_kv):
            sl = slice(g * 128, (g + 1) * 128)
            kcat = jnp.concatenate([kp[:, sl], kc[:, sl], kn[:, sl], kx[:, sl]], axis=0)
            vcat = jnp.concatenate([vp[:, sl], vc[:, sl], vn[:, sl], vx[:, sl]], axis=0)
            qg = _stack_heads(q_ref, g)
            dog = _stack_heads(do_ref, g)
            delta = jnp.sum(dog.astype(F32) * _stack_heads(o_ref, g).astype(F32), axis=1, keepdims=True)
            lse_g = _stack_cols(lse_ref, g)
            s = lax.dot_general(qg, kcat, NT_DIMS, preferred_element_type=F32) * scale
            p = jnp.exp(jnp.where(mask, s - lse_g, NEG))
            dp = lax.dot_general(dog, vcat, NT_DIMS, preferred_element_type=F32)
            ds = (p * (dp - delta) * scale).astype(BF16)
            dqg = jnp.dot(ds, kcat, preferred_element_type=F32)
            dkx_upd.append(lax.dot_general(ds[:, 3 * BLK:], qg, TN_DIMS, preferred_element_type=F32))
            dvx_upd.append(lax.dot_general(p.astype(BF16)[:, 3 * BLK:], dog, TN_DIMS, preferred_element_type=F32))
            dsk = -jnp.exp(_sink_col(sink_ref, l, g) - lse_g) * delta
            for hh in range(GROUP):
                h = GROUP * g + hh
                rs = slice(hh * BLK, (hh + 1) * BLK)
                dqs.append(dqg[rs])
                delta_tile = jnp.where(lane == h, delta[rs], delta_tile)
                tot = jnp.sum(dsk[rs], axis=0, keepdims=True)
                dsink_upd = dsink_upd + jnp.where((lane8 == h) & (row8 == 0), tot, 0.0)
        dq_ref[...] = jnp.concatenate(dqs, axis=1)
        dkx_ref[...] += jnp.concatenate(dkx_upd, axis=1)
        dvx_ref[...] += jnp.concatenate(dvx_upd, axis=1)
        delta_ref[...] = delta_tile
        dsink_ref[...] += dsink_upd

    blk = pl.BlockSpec((BLK, DQ), lambda i: (i, 0))
    stat = pl.BlockSpec((BLK, 128), lambda i: (i, 0))
    return pl.pallas_call(
        body, name=name, grid=(nb,),
        in_specs=[pl.BlockSpec(memory_space=pltpu.SMEM), blk] + _band_specs(DK, 0, nb, n_ctx)
        + _band_specs(DK, vcol, nb, n_ctx) + [blk, blk, stat],
        out_specs=[blk, stat, pl.BlockSpec((n_ctx, DK), lambda i: (0, 0)), pl.BlockSpec((n_ctx, DK), lambda i: (0, 0)),
                   pl.BlockSpec((8, 128), lambda i: (0, 0))],
        out_shape=[jax.ShapeDtypeStruct((S, DQ), F32), jax.ShapeDtypeStruct((S, 128), F32),
                   jax.ShapeDtypeStruct((n_ctx, DK), F32), jax.ShapeDtypeStruct((n_ctx, DK), F32),
                   jax.ShapeDtypeStruct((8, 128), F32)],
        compiler_params=_cp(1),
    )(sink, q, k, k, k, k, qkv, qkv, qkv, qkv, o, do, lse)


def _attn_bwd_kv(q, k, qkv, do, lse, delta, dkx, dvx, n_ctx, name):
    S, DQ = q.shape
    DK = k.shape[1]
    n_kv = DK // 128
    nb = S // BLK
    nctx_b = n_ctx // BLK
    vcol = (DQ + DK) // DK
    scale = HEAD_DIM ** -0.5

    def three(width):
        return [pl.BlockSpec((BLK, width), lambda j: (jnp.maximum(j - 1, 0), 0)),
                pl.BlockSpec((BLK, width), lambda j: (j, 0)),
                pl.BlockSpec((BLK, width), lambda j: (jnp.minimum(j + 1, nb - 1), 0))]

    def body(k_ref, v_ref, qp, qc, qn, dop, doc, don, lp, lc, ln, dlp, dlc, dln, dkx_ref, dvx_ref, dk_ref, dv_ref):
        j = pl.program_id(0)

        @pl.when(j < nctx_b)
        def _():
            dk_ref[...] = dkx_ref[...]
            dv_ref[...] = dvx_ref[...]

        @pl.when(j >= nctx_b)
        def _():
            shape = (3 * GROUP * BLK, BLK)
            t = lax.broadcasted_iota(I32, shape, 0)
            rq = (j - 1 + t // (GROUP * BLK)) * BLK + (t & (BLK - 1))
            rk = j * BLK + lax.broadcasted_iota(I32, shape, 1)
            valid = (rq >= n_ctx) & (rq < S) & (jnp.abs(rq - rk) <= WINDOW)
            dks, dvs = [], []
            for g in range(n_kv):
                sl = slice(g * 128, (g + 1) * 128)
                qcat = jnp.concatenate([_stack_heads(r, g) for r in (qp, qc, qn)], axis=0)
                docat = jnp.concatenate([_stack_heads(r, g) for r in (dop, doc, don)], axis=0)
                lse_c = jnp.concatenate([_stack_cols(r, g) for r in (lp, lc, ln)], axis=0)
                delta_c = jnp.concatenate([_stack_cols(r, g) for r in (dlp, dlc, dln)], axis=0)
                s = lax.dot_general(qcat, k_ref[:, sl], NT_DIMS, preferred_element_type=F32) * scale
                p = jnp.exp(jnp.where(valid, s - lse_c, NEG))
                dp = lax.dot_general(docat, v_ref[:, sl], NT_DIMS, preferred_element_type=F32)
                ds = (p * (dp - delta_c) * scale).astype(BF16)
                dks.append(lax.dot_general(ds, qcat, TN_DIMS, preferred_element_type=F32))
                dvs.append(lax.dot_general(p.astype(BF16), docat, TN_DIMS, preferred_element_type=F32))
            dk_ref[...] = jnp.concatenate(dks, axis=1)
            dv_ref[...] = jnp.concatenate(dvs, axis=1)

    cspec = pl.BlockSpec((BLK, DK), lambda j: (jnp.minimum(j, nctx_b - 1), 0))
    return pl.pallas_call(
        body, name=name, grid=(nb,),
        in_specs=[pl.BlockSpec((BLK, DK), lambda j: (j, 0)), pl.BlockSpec((BLK, DK), lambda j: (j, vcol))]
        + three(DQ) + three(DQ) + three(128) + three(128) + [cspec, cspec],
        out_specs=[pl.BlockSpec((BLK, DK), lambda j: (j, 0))] * 2,
        out_shape=[jax.ShapeDtypeStruct((S, DK), F32)] * 2, compiler_params=_cp(1),
    )(k, qkv, q, q, q, do, do, do, lse, lse, lse, delta, delta, delta, dkx, dvx)


def _ada_fwd(cond, w_ada, b_cols, name):
    lyr, D, C = w_ada.shape
    tc = _pick(C, [512, 384, 256, 128])

    def body(c_ref, w_ref, b_ref, o_ref):
        cv = c_ref[...]
        act = cv * _sigmoid(cv)
        o_ref[...] = jnp.dot(act, w_ref[...], preferred_element_type=F32,
                             precision=lax.Precision.HIGHEST) + b_ref[...]

    return pl.pallas_call(
        body, name=name, grid=(lyr, C // tc),
        in_specs=[pl.BlockSpec((16, D), lambda l, j: (0, 0)),
                  pl.BlockSpec((None, D, tc), lambda l, j: (l, 0, j)),
                  pl.BlockSpec((None, 1, tc), lambda l, j: (l, 0, j))],
        out_specs=pl.BlockSpec((None, 16, tc), lambda l, j: (l, 0, j)),
        out_shape=jax.ShapeDtypeStruct((lyr, 16, C), F32), compiler_params=_cp(2),
    )(cond, w_ada, b_cols)


def _ada_bwd(cond, d_out, w_ada, name):
    lyr, D, C = w_ada.shape
    tc = _pick(C, [512, 384, 256, 128])

    def body(c_ref, d_ref, w_ref, gw_ref, dc_ref):
        @pl.when((pl.program_id(0) == 0) & (pl.program_id(1) == 0))
        def _():
            dc_ref[...] = jnp.zeros_like(dc_ref)

        cv = c_ref[...]
        act = cv * _sigmoid(cv)
        dv = d_ref[...]
        gw_ref[...] = lax.dot_general(act, dv, TN_DIMS, preferred_element_type=F32, precision=lax.Precision.HIGHEST)
        dc_ref[...] += lax.dot_general(dv, w_ref[...], NT_DIMS, preferred_element_type=F32,
                                       precision=lax.Precision.HIGHEST)

    return pl.pallas_call(
        body, name=name, grid=(lyr, C // tc),
        in_specs=[pl.BlockSpec((16, D), lambda l, j: (0, 0)),
                  pl.BlockSpec((None, 16, tc), lambda l, j: (l, 0, j)),
                  pl.BlockSpec((None, D, tc), lambda l, j: (l, 0, j))],
        out_specs=[pl.BlockSpec((None, D, tc), lambda l, j: (l, 0, j)), pl.BlockSpec((16, D), lambda l, j: (0, 0))],
        out_shape=[jax.ShapeDtypeStruct((lyr, D, C), F32), jax.ShapeDtypeStruct((16, D), F32)],
        compiler_params=_cp(2),
    )(cond, d_out, w_ada)


def _sum_rows(d_rows, name):
    lyr, r, C = d_rows.shape

    def body(d_ref, o_ref):
        o_ref[...] = jnp.sum(d_ref[...], axis=0, keepdims=True)

    return pl.pallas_call(
        body, name=name, grid=(lyr,),
        in_specs=[pl.BlockSpec((None, r, C), lambda l: (l, 0, 0))],
        out_specs=pl.BlockSpec((None, 1, C), lambda l: (l, 0, 0)),
        out_shape=jax.ShapeDtypeStruct((lyr, 1, C), F32), compiler_params=_cp(1),
    )(d_rows)


def _cctx_grad(gathered, c_ctx_row, name):
    D = gathered.shape[1]

    def body(g_ref, c_ref, o_ref):
        acc = g_ref[0:16, :]
        for d in range(1, N_DEV):
            acc = acc + g_ref[16 * d:16 * (d + 1), :]
        cv = c_ref[...]
        sg = _sigmoid(cv)
        o_ref[...] = acc[8:16] * (sg * (1.0 + cv * (1.0 - sg)))

    return pl.pallas_call(
        body, name=name, out_shape=jax.ShapeDtypeStruct((8, D), F32),
        compiler_params=pltpu.CompilerParams(vmem_limit_bytes=VMEM_ELEMENTWISE),
    )(gathered, c_ctx_row)


def _pad_rows(a, rows):
    return jnp.concatenate([a, jnp.zeros((rows - a.shape[0],) + a.shape[1:], a.dtype)], axis=0)


def kernel(x, c, ctx, c_ctx, w_ada, b_ada, attn_w_qkv, attn_w_o, attn_q_gain, attn_k_gain, attn_sink, sc_w_in, sc_conv, sc_w_out, ffn_w_up, ffn_conv, ffn_conv_b, ffn_w_down, loss_target, m_c_ctx, m_w_ada, m_b_ada, m_attn_w_qkv, m_attn_w_o, m_attn_q_gain, m_attn_k_gain, m_attn_sink, m_sc_w_in, m_sc_conv, m_sc_w_out, m_ffn_w_up, m_ffn_conv, m_ffn_conv_b, m_ffn_w_down, v_c_ctx, v_w_ada, v_b_ada, v_attn_w_qkv, v_attn_w_o, v_attn_q_gain, v_attn_k_gain, v_attn_sink, v_sc_w_in, v_sc_conv, v_sc_w_out, v_ffn_w_up, v_ffn_conv, v_ffn_conv_b, v_ffn_w_down):
    T, D = x.shape[1], x.shape[2]
    L = ctx.shape[1]
    S = L + T
    depth = w_ada.shape[0]
    F = ffn_conv_b.shape[1]
    n_q = D // HEAD_DIM
    n_kv = n_q // GROUP
    ada_c = w_ada.shape[2]
    assert L % BLK == 0 and T % BLK == 0 and ada_c * N_DEV == 6 * D

    px, py, pc = _my_pos()
    me = 4 * px + 2 * py + pc
    me_idx = jnp.reshape(me, (1,)).astype(I32)

    tm_mm = _pick(S, [768, 704, 384, 256, 128])
    ts_tn = _pick(S, [2112, 1056, 768, 384, 256, 128])
    tm_half = _pick(S, [384, 256, 128])

    c_all = _gather_small(_pad_rows(c, 8), "gather_cond")
    cond = jnp.concatenate([c_all[0::8], c_ctx[None, :], jnp.zeros((7, D), F32)], axis=0)
    b_cols = lax.dynamic_slice_in_dim(b_ada, me * ada_c, ada_c, axis=1)[:, None, :]
    ada_mine = _ada_fwd(cond, w_ada, b_cols, "ada_fwd")
    ada_all = _gather_small(ada_mine.reshape(depth * 16, ada_c), "gather_ada")
    ada_all = ada_all.reshape(N_DEV, depth, 16, ada_c)
    ada_rows = jnp.transpose(ada_all, (1, 2, 0, 3)).reshape(depth, 16, 6, D)
    mod_lat = lax.dynamic_index_in_dim(ada_rows, me, axis=1, keepdims=False)
    mods = jnp.stack([ada_rows[:, 8], mod_lat], axis=1)

    gathered = [None] * depth
    tabs = _rope_tables(T, L)
    gains = jnp.stack([attn_q_gain, attn_k_gain], axis=1)
    conv_b3 = ffn_conv_b[:, None, :]
    sc_conv_all = _gather_small(_pad_rows(sc_conv.reshape(-1, sc_conv.shape[2]), 8), "gather_scconv")
    ffn_conv_all = _gather_small(_pad_rows(ffn_conv.reshape(-1, ffn_conv.shape[2]), 16), "gather_ffnconv")
    n_sc = sc_conv.shape[0]
    sc_conv_full = jnp.transpose(sc_conv_all.reshape(N_DEV, 8, -1)[:, :n_sc * 3], (1, 0, 2)).reshape(n_sc, 3, D)
    ffn_conv_full = jnp.transpose(ffn_conv_all.reshape(N_DEV, 16, -1)[:, :depth * 3], (1, 0, 2)).reshape(depth, 3, F)

    def start_weights(l, tag, after):
        if tag == "ffn":
            ws = [(ffn_w_up, l), (ffn_w_down, l)]
        else:
            ws = [(attn_w_qkv, l // 2), (attn_w_o, l // 2)] if l % 2 == 0 else [(sc_w_in, l // 2), (sc_w_out, l // 2)]
        lands = [_cast_layer(w, j, me_idx, f"cast_{tag}{k}_{l}") for k, (w, j) in enumerate(ws)]
        return _gather_start(lands, after, f"gather_start_{tag}{l}")

    def wait_weights(flight, after, name):
        send_sems, recv_sems, lands, _ = flight
        return _gather_wait(lands, send_sems, recv_sems, after, name)

    flight_mix = start_weights(0, "mix", [mods, sc_conv_full, ffn_conv_full])
    mods = mods + flight_mix[3][0, 0]

    xs = jnp.concatenate([ctx[0], x[0]], axis=0)
    saved = []
    for l in range(depth):
        j = l // 2
        mod = mods[l]
        if l == 0:
            w_a, w_b = wait_weights(flight_mix, mods, "gather_wait_mix0")
            flight_ffn = start_weights(0, "ffn", [w_a])
            mod = mod + flight_ffn[3][0, 0]
            h = _norm_mod(xs, mod, 0, L, "norm_m0")
        else:
            h = _norm_mod(xs, mod, 0, L, f"norm_m{l}")
            w_a, w_b = wait_weights(flight_mix, h, f"gather_wait_mix{l}")
        if l % 2 == 0:
            qkv = _mm_nn(h, w_a, tm=tm_mm, tn=w_a.shape[2], out_dtype=BF16, name=f"qkv{l}")
            qr, kr = _qk_prep(qkv, tabs, gains, j, n_q, n_kv, f"qk_prep{l}")
            z, lse = _attn_fwd(qr, kr, qkv, attn_sink, j, L, f"attn{l}")
            mix = (qkv, qr, kr, lse)
        else:
            u = _mm_nn(h, w_a, tm=tm_mm, tn=w_a.shape[2], out_dtype=BF16, name=f"scin{l}")
            z = _sc_act(u, sc_conv_full, j, L, f"sc_act{l}")
            mix = (u,)
        y_m, x1 = _mm_nn_resid(z, w_b.reshape(D, D), xs, mod, 2, L, tm=tm_mm, tn=_pick(D, [1024, 512]), name=f"mixout{l}")
        w_up, w_down = wait_weights(flight_ffn, x1, f"gather_wait_ffn{l}")
        gathered[l] = (w_a, w_b, w_up, w_down)
        if l + 1 < depth:
            flight_mix = start_weights(l + 1, "mix", [w_up])
            flight_ffn = start_weights(l + 1, "ffn", [w_up])
            mod = mod + (flight_mix[3][0, 0] + flight_ffn[3][0, 0])
        h2 = _norm_mod(x1, mod, 3, L, f"norm_f{l}")
        u_f = _mm_nn(h2, w_up, tm=tm_mm, tn=w_up.shape[2], out_dtype=BF16, name=f"up{l}")
        a_f = _ffn_act(u_f, ffn_conv_full, conv_b3, l, L, f"ffn_act{l}")
        y_f, x2 = _mm_nn_resid(a_f, w_down.reshape(F, D), x1, mod, 5, L, tm=tm_mm, tn=_pick(D, [512]), name=f"down{l}")
        saved.append((xs, h, mix, z, y_m, x1, h2, u_f, a_f, y_f))
        xs = x2

    dx, sq = _loss_grad(xs, loss_target[0], L, "loss")
    loss = lax.psum(sq[0, 0], ("x", "y", "c"))

    dmods = [None] * depth
    g_conv_b, g_ffn_conv, g_sc_conv = [None] * depth, [None] * depth, [None] * n_sc
    g_gain, g_sink = [None] * (depth - n_sc), [None] * (depth - n_sc)
    rs_flight = [None] * depth
    sent = jnp.zeros((), F32)
    for l in reversed(range(depth)):
        j = l // 2
        w_a, w_b, w_up, w_down = gathered[l]
        mod = mods[l] + sent
        x0, h, mix, z, y_m, x1, h2, u_f, a_f, y_f = saved[l]
        dy, s_gf = _gate_bwd(dx, y_f, mod, 5, L, f"gate_f_bwd{l}")
        da = _mm_nt(dy, w_down.reshape(F, D), tm=tm_mm, tn=_pick(F, [1408, 512]), out_dtype=BF16, name=f"down_dgrad{l}")
        gw_down = _mm_tn(a_f, [dy], nb=1, tka=_pick(F, [1408, 512]), tn=_pick(D, [1024, 512]), ts=ts_tn, name=f"down_wgrad{l}")
        dgate, dval, s_conv = _ffn_act_bwd(u_f, da, ffn_conv_full, conv_b3, l, L, f"ffn_act_bwd{l}")
        dh2 = _mm_nt_acc([dgate, dval], w_up, tm=tm_half, out_dtype=F32, name=f"up_dgrad{l}")
        gw_up = _mm_tn(h2, [dgate, dval], nb=N_DEV, tka=_pick(D, [1024, 512]), tn=w_up.shape[2], ts=ts_tn, name=f"up_wgrad{l}")
        dx1, s_nf = _norm_mod_bwd(dh2, x1, mod, dx, 3, L, f"norm_f_bwd{l}")
        g_ffn_conv[l], g_conv_b[l] = s_conv[0:3], s_conv[3]
        rs_ffn = _rs_start([gw_up, gw_down.reshape(N_DEV, -1, D)], f"rs_start_ffn{l}")
        mod = mods[l] + rs_ffn[4][0, 0]
        dy, s_gm = _gate_bwd(dx1, y_m, mod, 2, L, f"gate_m_bwd{l}")
        dz = _mm_nt(dy, w_b.reshape(D, D), tm=tm_mm, tn=_pick(D, [1024, 512]), out_dtype=BF16, name=f"mixout_dgrad{l}")
        gw_b = _mm_tn(z, [dy], nb=1, tka=_pick(D, [1024, 512]), tn=_pick(D, [1024, 512]), ts=ts_tn, name=f"mixout_wgrad{l}")
        if l % 2 == 0:
            qkv, qr, kr, lse = mix
            dq, delta, dkx, dvx, s_sink = _attn_bwd_q(qr, kr, qkv, z, dz, lse, attn_sink, j, L, f"attn_bwd_q{l}")
            dk, dv = _attn_bwd_kv(qr, kr, qkv, dz, lse, delta, dkx, dvx, L, f"attn_bwd_kv{l}")
            du_m, s_gain = _qk_prep_bwd(dq, dk, dv, qkv, tabs, gains, j, n_q, n_kv, f"qk_prep_bwd{l}")
            g_gain[j], g_sink[j] = s_gain[0:2], s_sink[0]
        else:
            (u,) = mix
            d_gb, d_gc, d_val, s_scconv = _sc_act_bwd(u, dz, sc_conv_full, j, L, f"sc_act_bwd{l}")
            du_m = jnp.concatenate([d_gb, d_gc, d_val], axis=1)
            g_sc_conv[j] = s_scconv[0:3]
        dh = _mm_nt_acc([du_m], w_a, tm=tm_mm, out_dtype=F32, name=f"mixin_dgrad{l}")
        gw_a = _mm_tn(h, [du_m], nb=N_DEV, tka=_pick(D, [1024, 512]), tn=w_a.shape[2], ts=ts_tn, name=f"mixin_wgrad{l}")
        dx, s_nm = _norm_mod_bwd(dh, x0, mod, dx1, 0, L, f"norm_m_bwd{l}")
        dmods[l] = jnp.stack([jnp.stack([s_nm[2 * k], s_nm[2 * k + 1], s_gm[k], s_nf[2 * k], s_nf[2 * k + 1], s_gf[k]])
                              for k in range(2)])
        rs_mix = _rs_start([gw_a, gw_b.reshape(N_DEV, -1, D)], f"rs_start_mix{l}")
        sent = rs_mix[4][0, 0]
        rs_flight[l] = (rs_mix, rs_ffn)

    grad_x = dx[L:][None]

    big_w = {"qkv": (attn_w_qkv, m_attn_w_qkv, v_attn_w_qkv), "wo": (attn_w_o, m_attn_w_o, v_attn_w_o),
             "scin": (sc_w_in, m_sc_w_in, v_sc_w_in), "scout": (sc_w_out, m_sc_w_out, v_sc_w_out),
             "up": (ffn_w_up, m_ffn_w_up, v_ffn_w_up), "down": (ffn_w_down, m_ffn_w_down, v_ffn_w_down)}
    big_out = {k: [] for k in big_w}
    for l in reversed(range(depth)):
        j = l // 2
        groups = [(["qkv", "wo"] if l % 2 == 0 else ["scin", "scout"], [j, j]), (["up", "down"], [l, l])]
        for (names, idxs), flight, tag in reversed(list(zip(groups, rs_flight[l], ("mix", "ffn")))):
            send_sems, recv_sems, own, zones, _ = flight
            own, zones = _rs_wait(own, zones, send_sems, recv_sems, dx, f"rs_wait_{tag}{l}")
            for n, li, p, z in zip(names, idxs, own, zones):
                w, m, v = big_w[n]
                big_out[n].insert(0, _adamw_reduced(p, z, me_idx, w, m, v, li, f"adamw_{n}{l}"))
    big_res = {k: [jnp.stack([o[t] for o in outs]) for t in range(4)] for k, outs in big_out.items()}

    n_attn = depth - n_sc
    pack = [jnp.stack(dmods)[:, 0].reshape(-1, 128), jnp.stack(dmods)[:, 1].reshape(-1, 128),
            jnp.stack(g_gain).reshape(-1, 128), jnp.stack(g_sink),
            jnp.stack(g_conv_b).reshape(-1, 128), jnp.stack(g_ffn_conv).reshape(-1, 128),
            jnp.stack(g_sc_conv).reshape(-1, 128)]
    used = [p.shape[0] for p in pack]
    pack = [_pad_rows(p, -(-p.shape[0] // 8) * 8) for p in pack]
    sizes = [p.shape[0] for p in pack]
    flat = jnp.concatenate(pack, axis=0)
    rows = flat.shape[0]
    small_all = _gather_small(flat, "gather_small_grads")
    small_sum = _sum8(small_all, rows, "sum_small_grads")
    offs = [sum(sizes[:k]) for k in range(len(sizes))]
    seg = lambda a, k: a[offs[k]:offs[k] + used[k]]
    dmod_ctx = seg(small_sum, 0).reshape(depth, 6 * D)
    dmod_lat = small_all.reshape(N_DEV, rows, 128)[:, offs[1]:offs[1] + used[1]].reshape(N_DEV, depth, 6 * D)
    g_gain_sum = seg(small_sum, 2).reshape(n_attn, 2, 128)
    g_sink_sum = seg(small_sum, 3)[:n_attn, :n_q]
    g_conv_b_sum = seg(small_sum, 4).reshape(depth, F)
    g_ffn_conv_sum = seg(small_sum, 5).reshape(depth, 3, F)
    g_sc_conv_sum = seg(small_sum, 6).reshape(n_sc, 3, D)

    d_rows = jnp.concatenate([jnp.transpose(dmod_lat, (1, 0, 2)), dmod_ctx[:, None, :],
                              jnp.zeros((depth, 7, 6 * D), F32)], axis=1)
    d_cols = lax.dynamic_slice_in_dim(d_rows, me * ada_c, ada_c, axis=2)
    g_w_ada, dcond_part = _ada_bwd(cond, d_cols, w_ada, "ada_bwd")
    dcond_all = _gather_small(dcond_part, "gather_dcond")
    g_c_ctx = _cctx_grad(dcond_all, jnp.broadcast_to(c_ctx[None, :], (8, D)), "cctx_grad")[0]
    g_b_ada = _sum_rows(d_rows, "b_ada_grad")[:, 0]

    def small_adam(w, g, m, v, name):
        w2 = w.reshape(-1, w.shape[-1])
        d, m2, v2 = _adamw_plain(w2, g.reshape(w2.shape), m.reshape(w2.shape), v.reshape(w2.shape), name)
        return g.reshape(w.shape), d.reshape(w.shape), m2.reshape(w.shape), v2.reshape(w.shape)

    g_sc_conv_mine = lax.dynamic_slice_in_dim(g_sc_conv_sum, me * sc_conv.shape[2], sc_conv.shape[2], axis=2)
    g_ffn_conv_mine = lax.dynamic_slice_in_dim(g_ffn_conv_sum, me * ffn_conv.shape[2], ffn_conv.shape[2], axis=2)
    res = {
        "c_ctx": small_adam(c_ctx[None, :], g_c_ctx[None, :], m_c_ctx[None, :], v_c_ctx[None, :], "adamw_c_ctx"),
        "b_ada": small_adam(b_ada, g_b_ada, m_b_ada, v_b_ada, "adamw_b_ada"),
        "attn_q_gain": small_adam(attn_q_gain, g_gain_sum[:, 0], m_attn_q_gain, v_attn_q_gain, "adamw_q_gain"),
        "attn_k_gain": small_adam(attn_k_gain, g_gain_sum[:, 1], m_attn_k_gain, v_attn_k_gain, "adamw_k_gain"),
        "attn_sink": small_adam(attn_sink, g_sink_sum, m_attn_sink, v_attn_sink, "adamw_sink"),
        "sc_conv": small_adam(sc_conv, g_sc_conv_mine, m_sc_conv, v_sc_conv, "adamw_sc_conv"),
        "ffn_conv": small_adam(ffn_conv, g_ffn_conv_mine, m_ffn_conv, v_ffn_conv, "adamw_ffn_conv"),
        "ffn_conv_b": small_adam(ffn_conv_b, g_conv_b_sum, m_ffn_conv_b, v_ffn_conv_b, "adamw_conv_b"),
    }
    res["c_ctx"] = tuple(t[0] for t in res["c_ctx"])
    res["w_ada"] = (g_w_ada,) + tuple(_adamw_tiled(w_ada, g_w_ada, m_w_ada, v_w_ada, "adamw_w_ada"))
    res["attn_w_qkv"], res["attn_w_o"] = big_res["qkv"], big_res["wo"]
    res["sc_w_in"], res["sc_w_out"] = big_res["scin"], big_res["scout"]
    res["ffn_w_up"], res["ffn_w_down"] = big_res["up"], big_res["down"]

    order = ["c_ctx", "w_ada", "b_ada", "attn_w_qkv", "attn_w_o", "attn_q_gain", "attn_k_gain", "attn_sink",
             "sc_w_in", "sc_conv", "sc_w_out", "ffn_w_up", "ffn_conv", "ffn_conv_b", "ffn_w_down"]
    outs = [loss, grad_x]
    for t in range(4):
        outs += [res[n][t] for n in order]
    return tuple(outs)
```

```python
import functools

import jax
import jax.numpy as jnp
from jax import lax
from jax.experimental import pallas as pl
from jax.experimental.pallas import tpu as pltpu

F32 = jnp.float32
BF16 = jnp.bfloat16
I32 = jnp.int32

N_DEV = 8
HEAD_DIM = 128
GROUP = 4
WINDOW = 128
BLK = 128
GRID_W = 64
ROPE_BASE = 10000.0
EPS = 1e-6
NEG = -1e30
HALO = 16

ADAM_LR = 0.001
ADAM_B1 = 0.9
ADAM_B2 = 0.999
ADAM_EPS = 1e-08
ADAM_WD = 0.01
ADAM_STEP = 10

V7X_VMEM_BYTES = 64 << 20
VMEM_MATMUL = 52 << 20
VMEM_ELEMENTWISE = 44 << 20

MESH = pl.DeviceIdType.MESH
ANY = pl.BlockSpec(memory_space=pl.ANY)
HBM = pl.BlockSpec(memory_space=pltpu.HBM)
SEM = pl.BlockSpec(memory_space=pltpu.SEMAPHORE)
EFFECT = pltpu.SideEffectType.DATAFLOW_SIDE_EFFECTING

NT_DIMS = (((1,), (1,)), ((), ()))
TN_DIMS = (((0,), (0,)), ((), ()))


def _pick(n, cands):
    for t in cands:
        if n % t == 0:
            return t
    raise ValueError(f"no tile for {n} in {cands}")


def _cp(n_axes, vmem=VMEM_ELEMENTWISE):
    return pltpu.CompilerParams(dimension_semantics=("arbitrary",) * n_axes, vmem_limit_bytes=vmem)


def _rows(i, tm, off=0):
    return i * tm + off + lax.broadcasted_iota(I32, (tm, 1), 0)


def _my_pos():
    return lax.axis_index("x"), lax.axis_index("y"), lax.axis_index("c")


def _gather_small(x_shard, name):
    m_per, n = x_shard.shape

    def body(x_ref, out_ref, send_sems, recv_sems, local_sem):
        x, y, c = _my_pos()
        me, sibling = (x, y, c), (x, y, 1 - c)
        chips = [(1 - x, y), (x, 1 - y), (1 - x, 1 - y)]

        def rows(px, py, pc):
            return out_ref.at[pl.ds((4 * px + 2 * py + pc) * m_per, m_per), :]

        def copy(k, block, to, src=None):
            return pltpu.make_async_remote_copy(
                src_ref=rows(*block) if src is None else src, dst_ref=rows(*block),
                send_sem=send_sems.at[k], recv_sem=recv_sems.at[k], device_id=to, device_id_type=MESH)

        mine = pltpu.make_async_copy(x_ref, rows(*me), local_sem)
        mine.start()
        first = [copy(0, me, sibling, src=x_ref)]
        first += [copy(1 + j, me, (*chip, c), src=x_ref) for j, chip in enumerate(chips)]
        for cp in first:
            cp.start()
        passed = [copy(4 + j, (*chip, c), sibling) for j, chip in enumerate(chips)]
        for j, chip in enumerate(chips):
            copy(1 + j, (*chip, c), me).wait_recv()
            passed[j].start()
        copy(0, sibling, me).wait_recv()
        for j, chip in enumerate(chips):
            copy(4 + j, (*chip, 1 - c), me).wait_recv()
        for cp in first + passed:
            cp.wait_send()
        mine.wait()

    return pl.pallas_call(
        body, name=name,
        out_shape=jax.ShapeDtypeStruct((N_DEV * m_per, n), x_shard.dtype),
        in_specs=[pl.BlockSpec(memory_space=pltpu.VMEM)],
        out_specs=pl.BlockSpec(memory_space=pltpu.VMEM),
        scratch_shapes=[pltpu.SemaphoreType.DMA((7,)), pltpu.SemaphoreType.DMA((7,)), pltpu.SemaphoreType.DMA],
        compiler_params=pltpu.CompilerParams(vmem_limit_bytes=VMEM_ELEMENTWISE),
    )(x_shard)


def _peer(k):
    x, y, c = _my_pos()
    b = k + 1
    return ((1 - x) if b & 4 else x, (1 - y) if b & 2 else y, (1 - c) if b & 1 else c)


def _slot(p):
    return 4 * p[0] + 2 * p[1] + p[2]


def _in_hbm(a):
    return pltpu.with_memory_space_constraint(a, pltpu.HBM)


def _gather_start(lands, after, name):
    n = len(lands)

    def body(*refs):
        l_refs, send_sems, recv_sems = refs[:n], refs[n + len(after)], refs[n + len(after) + 1]
        token = refs[2 * n + len(after) + 2]
        me = _slot(_my_pos())
        for a in range(n):
            for k in range(7):
                pltpu.make_async_remote_copy(
                    src_ref=l_refs[a].at[me], dst_ref=l_refs[a].at[me],
                    send_sem=send_sems.at[7 * a + k], recv_sem=recv_sems.at[7 * a + k],
                    device_id=_peer(k), device_id_type=MESH).start()
        token[...] = jnp.zeros_like(token)

    out = pl.pallas_call(
        body, name=name,
        out_shape=(pltpu.SemaphoreType.DMA((7 * n,)), pltpu.SemaphoreType.DMA((7 * n,)),
                   *[pltpu.HBM(a.shape, a.dtype) for a in lands], jax.ShapeDtypeStruct((8, 128), F32)),
        in_specs=[HBM] * n + [ANY] * len(after),
        out_specs=(SEM, SEM, *[HBM] * n, pl.BlockSpec(memory_space=pltpu.VMEM)),
        input_output_aliases={a: 2 + a for a in range(n)},
        compiler_params=pltpu.CompilerParams(has_side_effects=EFFECT),
    )(*[_in_hbm(a) for a in lands], *after)
    return out[0], out[1], list(out[2:2 + n]), out[2 + n]


def _gather_wait(lands, send_sems, recv_sems, after, name):
    n = len(lands)

    def body(*refs):
        l_refs, ss, rs = refs[:n], refs[n], refs[n + 1]
        me = _slot(_my_pos())
        for a in range(n):
            for k in range(7):
                cp = pltpu.make_async_remote_copy(
                    src_ref=l_refs[a].at[me], dst_ref=l_refs[a].at[_slot(_peer(k))],
                    send_sem=ss.at[7 * a + k], recv_sem=rs.at[7 * a + k], device_id=_peer(k), device_id_type=MESH)
                cp.wait_send()
                cp.wait_recv()

    out = pl.pallas_call(
        body, name=name,
        out_shape=tuple(pltpu.HBM(a.shape, a.dtype) for a in lands),
        in_specs=[HBM] * n + [SEM, SEM, ANY], out_specs=tuple([HBM] * n),
        input_output_aliases={a: a for a in range(n)},
        compiler_params=pltpu.CompilerParams(has_side_effects=EFFECT),
    )(*lands, send_sems, recv_sems, after)
    return list(out)


def _rs_start(grads, name):
    n = len(grads)

    def body(*refs):
        g_refs, z_refs, send_sems, recv_sems = refs[:n], refs[n:2 * n], refs[2 * n], refs[2 * n + 1]
        token = refs[4 * n + 2]
        for a in range(n):
            for k in range(7):
                pltpu.make_async_remote_copy(
                    src_ref=g_refs[a].at[_slot(_peer(k))], dst_ref=z_refs[a].at[k],
                    send_sem=send_sems.at[7 * a + k], recv_sem=recv_sems.at[7 * a + k],
                    device_id=_peer(k), device_id_type=MESH).start()
        token[...] = jnp.zeros_like(token)

    zones = [lax.empty((7,) + g.shape[1:], g.dtype) for g in grads]
    out = pl.pallas_call(
        body, name=name,
        out_shape=(pltpu.SemaphoreType.DMA((7 * n,)), pltpu.SemaphoreType.DMA((7 * n,)),
                   *[pltpu.HBM(a.shape, a.dtype) for a in grads], *[pltpu.HBM(z.shape, z.dtype) for z in zones],
                   jax.ShapeDtypeStruct((8, 128), F32)),
        in_specs=[HBM] * (2 * n),
        out_specs=(SEM, SEM, *[HBM] * (2 * n), pl.BlockSpec(memory_space=pltpu.VMEM)),
        input_output_aliases={a: 2 + a for a in range(2 * n)},
        compiler_params=pltpu.CompilerParams(has_side_effects=EFFECT),
    )(*[_in_hbm(a) for a in grads], *[_in_hbm(z) for z in zones])
    return out[0], out[1], list(out[2:2 + n]), list(out[2 + n:2 + 2 * n]), out[2 + 2 * n]


def _rs_wait(grads, zones, send_sems, recv_sems, after, name):
    n = len(grads)

    def body(*refs):
        g_refs, z_refs, ss, rs = refs[:n], refs[n:2 * n], refs[2 * n], refs[2 * n + 1]
        for a in range(n):
            for k in range(7):
                cp = pltpu.make_async_remote_copy(
                    src_ref=g_refs[a].at[_slot(_peer(k))], dst_ref=z_refs[a].at[k],
                    send_sem=ss.at[7 * a + k], recv_sem=rs.at[7 * a + k], device_id=_peer(k), device_id_type=MESH)
                cp.wait_send()
                cp.wait_recv()

    out = pl.pallas_call(
        body, name=name,
        out_shape=tuple(pltpu.HBM(a.shape, a.dtype) for a in list(grads) + list(zones)),
        in_specs=[HBM] * (2 * n) + [SEM, SEM, ANY], out_specs=tuple([HBM] * (2 * n)),
        input_output_aliases={a: a for a in range(2 * n)},
        compiler_params=pltpu.CompilerParams(has_side_effects=EFFECT),
    )(*grads, *zones, send_sems, recv_sems, after)
    return list(out[:n]), list(out[n:])


def _cast_layer(w, l, me_idx, name):
    _, r, c = w.shape
    tr = _pick(r, [512, 256, 128, 64, 32, 16])

    def body(me_ref, w_ref, o_ref):
        o_ref[...] = w_ref[...].astype(BF16)

    return pl.pallas_call(
        body, name=name,
        grid_spec=pltpu.PrefetchScalarGridSpec(
            num_scalar_prefetch=1, grid=(r // tr,),
            in_specs=[pl.BlockSpec((None, tr, c), lambda i, me_ref: (l, i, 0))],
            out_specs=pl.BlockSpec((None, tr, c), lambda i, me_ref: (me_ref[0], i, 0))),
        out_shape=jax.ShapeDtypeStruct((N_DEV, r, c), BF16), compiler_params=_cp(1),
    )(me_idx, w)


def _adam_math(w, g, m, v):
    m2 = ADAM_B1 * m + (1.0 - ADAM_B1) * g
    v2 = ADAM_B2 * v + (1.0 - ADAM_B2) * (g * g)
    m_hat = m2 / (1.0 - ADAM_B1 ** ADAM_STEP)
    v_hat = v2 / (1.0 - ADAM_B2 ** ADAM_STEP)
    delta = -ADAM_LR * (m_hat / (jnp.sqrt(v_hat) + ADAM_EPS) + ADAM_WD * w)
    return delta, m2, v2


def _adamw_reduced(own, zone, me_idx, w, m, v, l, name):
    _, r, c = own.shape
    tr = _pick(r, [256, 128, 64, 32, 16])

    def body(me_ref, p_ref, z_ref, w_ref, m_ref, v_ref, g_out, d_out, m_out, v_out):
        g = p_ref[...].astype(F32)
        for k in range(7):
            g = g + z_ref[k].astype(F32)
        d, m2, v2 = _adam_math(w_ref[...], g, m_ref[...], v_ref[...])
        g_out[...] = g
        d_out[...] = d
        m_out[...] = m2
        v_out[...] = v2

    wspec = pl.BlockSpec((None, tr, c), lambda i, me_ref: (l, i, 0))
    ospec = pl.BlockSpec((tr, c), lambda i, me_ref: (i, 0))
    return pl.pallas_call(
        body, name=name,
        grid_spec=pltpu.PrefetchScalarGridSpec(
            num_scalar_prefetch=1, grid=(r // tr,),
            in_specs=[pl.BlockSpec((None, tr, c), lambda i, me_ref: (me_ref[0], i, 0)),
                      pl.BlockSpec((7, tr, c), lambda i, me_ref: (0, i, 0)), wspec, wspec, wspec],
            out_specs=[ospec] * 4),
        out_shape=[jax.ShapeDtypeStruct((r, c), F32)] * 4, compiler_params=_cp(1),
    )(me_idx, own, zone, w, m, v)


def _adamw_plain(w, g, m, v, name):
    def body(w_ref, g_ref, m_ref, v_ref, d_out, m_out, v_out):
        d, m2, v2 = _adam_math(w_ref[...], g_ref[...], m_ref[...], v_ref[...])
        d_out[...] = d
        m_out[...] = m2
        v_out[...] = v2

    return pl.pallas_call(
        body, name=name, out_shape=[jax.ShapeDtypeStruct(w.shape, F32)] * 3,
        compiler_params=pltpu.CompilerParams(vmem_limit_bytes=VMEM_ELEMENTWISE),
    )(w, g, m, v)


def _adamw_tiled(w, g, m, v, name):
    lyr, r, c = w.shape
    tr = _pick(r, [256, 128, 64, 32, 16, 8])

    def body(w_ref, g_ref, m_ref, v_ref, d_out, m_out, v_out):
        d, m2, v2 = _adam_math(w_ref[...], g_ref[...], m_ref[...], v_ref[...])
        d_out[...] = d
        m_out[...] = m2
        v_out[...] = v2

    spec = pl.BlockSpec((None, tr, c), lambda l, i: (l, i, 0))
    return pl.pallas_call(
        body, name=name, grid=(lyr, r // tr), in_specs=[spec] * 4, out_specs=[spec] * 3,
        out_shape=[jax.ShapeDtypeStruct(w.shape, F32)] * 3, compiler_params=_cp(2),
    )(w, g, m, v)


def _sum8(gathered, rows, name):
    def body(g_ref, o_ref):
        acc = g_ref[0:rows, :]
        for d in range(1, N_DEV):
            acc = acc + g_ref[d * rows:(d + 1) * rows, :]
        o_ref[...] = acc

    return pl.pallas_call(
        body, name=name, out_shape=jax.ShapeDtypeStruct((rows, 128), F32),
        compiler_params=pltpu.CompilerParams(vmem_limit_bytes=VMEM_ELEMENTWISE),
    )(gathered)


MXU_COLS = 256


def _pairable(nb, nc):
    return nb % 2 == 0 and nc % MXU_COLS == MXU_COLS // 2 and nc > MXU_COLS // 2


def _mm_nn(a, b3, *, tm, tn, out_dtype, name):
    M, K = a.shape
    nb, _, nc = b3.shape
    q = nc // tn

    if _pairable(nb, nc) and tn == nc:
        cut = nc - 128

        def pair_body(a_ref, b_ref, o_ref):
            av = a_ref[...]
            mid = jnp.concatenate([b_ref[0, :, cut:nc], b_ref[1, :, 0:128]], axis=1)
            o_ref[:, 0:cut] = jnp.dot(av, b_ref[0, :, 0:cut], preferred_element_type=F32).astype(o_ref.dtype)
            o_ref[:, cut:nc + 128] = jnp.dot(av, mid, preferred_element_type=F32).astype(o_ref.dtype)
            o_ref[:, nc + 128:2 * nc] = jnp.dot(av, b_ref[1, :, 128:nc], preferred_element_type=F32).astype(o_ref.dtype)

        return pl.pallas_call(
            pair_body, name=name, grid=(nb // 2, M // tm),
            in_specs=[pl.BlockSpec((tm, K), lambda j, i: (i, 0)),
                      pl.BlockSpec((2, K, nc), lambda j, i: (j, 0, 0))],
            out_specs=pl.BlockSpec((tm, 2 * nc), lambda j, i: (i, j)),
            out_shape=jax.ShapeDtypeStruct((M, nb * nc), out_dtype), compiler_params=_cp(2, VMEM_MATMUL),
        )(a, b3)

    def body(a_ref, b_ref, o_ref):
        o_ref[...] = jnp.dot(a_ref[...], b_ref[...], preferred_element_type=F32).astype(o_ref.dtype)

    return pl.pallas_call(
        body, name=name, grid=(nb * q, M // tm),
        in_specs=[pl.BlockSpec((tm, K), lambda j, i: (i, 0)),
                  pl.BlockSpec((None, K, tn), lambda j, i: (j // q, 0, j % q))],
        out_specs=pl.BlockSpec((tm, tn), lambda j, i: (i, j)),
        out_shape=jax.ShapeDtypeStruct((M, nb * nc), out_dtype), compiler_params=_cp(2, VMEM_MATMUL),
    )(a, b3)


def _mm_nn_resid(a, b2, x_old, mod, gate_row, n_ctx, *, tm, tn, name):
    M, K = a.shape
    N = b2.shape[1]

    def body(a_ref, b_ref, x_ref, mod_ref, y_ref, xn_ref):
        y = jnp.dot(a_ref[...], b_ref[...], preferred_element_type=F32)
        is_ctx = _rows(pl.program_id(1), tm) < n_ctx
        g = jnp.where(is_ctx, mod_ref[0, gate_row:gate_row + 1, :], mod_ref[1, gate_row:gate_row + 1, :])
        y_ref[...] = y.astype(BF16)
        xn_ref[...] = x_ref[...] + g * y

    return pl.pallas_call(
        body, name=name, grid=(N // tn, M // tm),
        in_specs=[pl.BlockSpec((tm, K), lambda j, i: (i, 0)),
                  pl.BlockSpec((K, tn), lambda j, i: (0, j)),
                  pl.BlockSpec((tm, tn), lambda j, i: (i, j)),
                  pl.BlockSpec((2, 6, tn), lambda j, i: (0, 0, j))],
        out_specs=[pl.BlockSpec((tm, tn), lambda j, i: (i, j))] * 2,
        out_shape=[jax.ShapeDtypeStruct((M, N), BF16), jax.ShapeDtypeStruct((M, N), F32)],
        compiler_params=_cp(2, VMEM_MATMUL),
    )(a, b2, x_old, mod)


def _mm_nn_resid_norm(a, b2, x_old, mod, gate_row, norm_row0, n_ctx, *, tm, name):
    M, K = a.shape
    N = b2.shape[1]

    def body(a_ref, b_ref, x_ref, mod_ref, y_ref, xn_ref, h_ref):
        y = jnp.dot(a_ref[...], b_ref[...], preferred_element_type=F32)
        is_ctx = _rows(pl.program_id(0), tm) < n_ctx

        def row(k):
            return jnp.where(is_ctx, mod_ref[0, k:k + 1, :], mod_ref[1, k:k + 1, :])

        y_ref[...] = y.astype(BF16)
        xn = x_ref[...] + row(gate_row) * y
        xn_ref[...] = xn
        r = lax.rsqrt(jnp.mean(xn * xn, axis=-1, keepdims=True) + EPS)
        h_ref[...] = (xn * r * (1.0 + row(norm_row0 + 1)) + row(norm_row0)).astype(BF16)

    rows = pl.BlockSpec((tm, N), lambda i: (i, 0))
    return pl.pallas_call(
        body, name=name, grid=(M // tm,),
        in_specs=[pl.BlockSpec((tm, K), lambda i: (i, 0)), pl.BlockSpec((K, N), lambda i: (0, 0)), rows,
                  pl.BlockSpec((2, 6, N), lambda i: (0, 0, 0))],
        out_specs=[rows, rows, rows],
        out_shape=[jax.ShapeDtypeStruct((M, N), BF16), jax.ShapeDtypeStruct((M, N), F32),
                   jax.ShapeDtypeStruct((M, N), BF16)],
        compiler_params=_cp(1, VMEM_MATMUL),
    )(a, b2, x_old, mod)


def _norm_bwd_math(dh_v, xv, mod_ref, res, is_ctx, row0):
    D = xv.shape[1]
    r = lax.rsqrt(jnp.mean(xv * xv, axis=-1, keepdims=True) + EPS)
    xhat = xv * r
    sc = jnp.where(is_ctx, mod_ref[0, row0 + 1:row0 + 2, :], mod_ref[1, row0 + 1:row0 + 2, :])
    dxhat = dh_v * (1.0 + sc)
    dx = res + r * (dxhat - xhat * jnp.mean(dxhat * xhat, axis=-1, keepdims=True))
    dsc = dh_v * xhat
    zero = jnp.zeros_like(dh_v)
    sums = [jnp.sum(jnp.where(is_ctx, dh_v, zero), axis=0, keepdims=True),
            jnp.sum(jnp.where(is_ctx, dsc, zero), axis=0, keepdims=True),
            jnp.sum(jnp.where(is_ctx, zero, dh_v), axis=0, keepdims=True),
            jnp.sum(jnp.where(is_ctx, zero, dsc), axis=0, keepdims=True)]
    rid = lax.broadcasted_iota(I32, (8, D), 0)
    upd = jnp.zeros((8, D), F32)
    for k, s in enumerate(sums):
        upd = upd + jnp.where(rid == k, s, 0.0)
    return dx, upd


def _dgrad_norm_bwd(parts, w3, x, mod, dx_res, row0, n_ctx, *, tm, allow_pair, name):
    n_parts = len(parts)
    M = parts[0].shape[0]
    nb, K, nc = w3.shape
    pair = allow_pair and _pairable(nb // n_parts, nc)
    g = 2 if pair else 1
    steps, per = nb // g, nb // n_parts // g
    cut = nc - 128

    def nt(d, w):
        return lax.dot_general(d, w, NT_DIMS, preferred_element_type=F32)

    def contribution(d_ref, w_ref):
        if not pair:
            return nt(d_ref[...], w_ref[0])
        mid = jnp.concatenate([w_ref[0, :, cut:nc], w_ref[1, :, 0:128]], axis=1)
        return (nt(d_ref[:, 0:cut], w_ref[0, :, 0:cut]) + nt(d_ref[:, cut:nc + 128], mid)
                + nt(d_ref[:, nc + 128:2 * nc], w_ref[1, :, 128:nc]))

    def body(*refs):
        dy_refs, w_ref, x_ref, mod_ref, res_ref = refs[:n_parts], *refs[n_parts:n_parts + 4]
        dx_ref, sums_ref, acc_ref = refs[n_parts + 4:]
        i = pl.program_id(0)
        s = pl.program_id(1)

        @pl.when(s == 0)
        def _():
            acc_ref[...] = jnp.zeros_like(acc_ref)

        @pl.when((s == 0) & (i == 0))
        def _():
            sums_ref[...] = jnp.zeros_like(sums_ref)

        for p in range(n_parts):
            @pl.when(s // per == p)
            def _(p=p):
                acc_ref[...] += contribution(dy_refs[p], w_ref)

        @pl.when(s == steps - 1)
        def _():
            dx, upd = _norm_bwd_math(acc_ref[...], x_ref[...], mod_ref, res_ref[...], _rows(i, tm) < n_ctx, row0)
            dx_ref[...] = dx
            sums_ref[...] += upd

    def part_spec(p):
        return pl.BlockSpec((tm, g * nc), lambda i, s: (i, jnp.clip(s - p * per, 0, per - 1)))

    rows = pl.BlockSpec((tm, K), lambda i, s: (i, 0))
    return pl.pallas_call(
        body, name=name, grid=(M // tm, steps),
        in_specs=[part_spec(p) for p in range(n_parts)] + [pl.BlockSpec((g, K, nc), lambda i, s: (s, 0, 0)),
                                                           rows, pl.BlockSpec((2, 6, K), lambda i, s: (0, 0, 0)), rows],
        out_specs=[rows, pl.BlockSpec((8, K), lambda i, s: (0, 0))],
        out_shape=[jax.ShapeDtypeStruct((M, K), F32), jax.ShapeDtypeStruct((8, K), F32)],
        scratch_shapes=[pltpu.VMEM((tm, K), F32)], compiler_params=_cp(2, VMEM_MATMUL),
    )(*parts, w3, x, mod, dx_res)


def _gate_dgrad(dx, y, mod, gate_row, n_ctx, w2, *, tm, tn, name):
    M, N = dx.shape
    K = w2.shape[0]

    def body(dx_ref, y_ref, mod_ref, w_ref, da_ref, dy_ref, acc_ref):
        i = pl.program_id(0)
        j = pl.program_id(1)

        @pl.when((i == 0) & (j == 0))
        def _():
            acc_ref[...] = jnp.zeros_like(acc_ref)

        @pl.when(j == 0)
        def _():
            dxv = dx_ref[...]
            is_ctx = _rows(i, tm) < n_ctx
            g = jnp.where(is_ctx, mod_ref[0, gate_row:gate_row + 1, :], mod_ref[1, gate_row:gate_row + 1, :])
            dy_ref[...] = (g * dxv).astype(BF16)
            prod = dxv * y_ref[...].astype(F32)
            zero = jnp.zeros_like(prod)
            s_ctx = jnp.sum(jnp.where(is_ctx, prod, zero), axis=0, keepdims=True)
            s_lat = jnp.sum(jnp.where(is_ctx, zero, prod), axis=0, keepdims=True)
            rid = lax.broadcasted_iota(I32, (8, N), 0)
            acc_ref[...] += jnp.where(rid == 0, s_ctx, 0.0) + jnp.where(rid == 1, s_lat, 0.0)

        da_ref[...] = lax.dot_general(dy_ref[...], w_ref[...], NT_DIMS, preferred_element_type=F32).astype(BF16)

    rows = pl.BlockSpec((tm, N), lambda i, j: (i, 0))
    return pl.pallas_call(
        body, name=name, grid=(M // tm, K // tn),
        in_specs=[rows, rows, pl.BlockSpec((2, 6, N), lambda i, j: (0, 0, 0)),
                  pl.BlockSpec((tn, N), lambda i, j: (j, 0))],
        out_specs=[pl.BlockSpec((tm, tn), lambda i, j: (i, j)), rows, pl.BlockSpec((8, N), lambda i, j: (0, 0))],
        out_shape=[jax.ShapeDtypeStruct((M, K), BF16), jax.ShapeDtypeStruct((M, N), BF16),
                   jax.ShapeDtypeStruct((8, N), F32)],
        compiler_params=_cp(2, VMEM_MATMUL),
    )(dx, y, mod, w2)


def _mm_tn(a, parts, *, nb, tka, tn, ts, name):
    n_parts = len(parts)
    S, Ka = a.shape
    N = sum(p.shape[1] for p in parts)
    nc = N // nb
    q = nc // tn
    nk = S // ts
    g = 2 if (tn == nc and n_parts == 1 and _pairable(nb, nc)) else 1
    per = nb * q // n_parts // g

    def body(a_ref, *rest):
        dy_refs, o_ref, acc_ref = rest[:n_parts], rest[n_parts], rest[n_parts + 1]
        j = pl.program_id(0)
        k = pl.program_id(2)

        @pl.when(k == 0)
        def _():
            acc_ref[...] = jnp.zeros_like(acc_ref)

        for p in range(n_parts):
            @pl.when(j // per == p)
            def _(p=p):
                acc_ref[...] += lax.dot_general(a_ref[...], dy_refs[p][...], TN_DIMS, preferred_element_type=F32)

        @pl.when(k == nk - 1)
        def _():
            for t in range(g):
                o_ref[t] = acc_ref[:, t * tn:(t + 1) * tn].astype(o_ref.dtype)

    def part_spec(p):
        return pl.BlockSpec((ts, g * tn), lambda j, ia, k: (jnp.where(j // per == p, k, 0),
                                                            jnp.clip(j - p * per, 0, per - 1)))

    return pl.pallas_call(
        body, name=name, grid=(nb * q // g, Ka // tka, nk),
        in_specs=[pl.BlockSpec((ts, tka), lambda j, ia, k: (k, ia))] + [part_spec(p) for p in range(n_parts)],
        out_specs=pl.BlockSpec((g, tka, tn), lambda j, ia, k: (j // q, ia, j % q)),
        out_shape=jax.ShapeDtypeStruct((nb, Ka, nc), BF16),
        scratch_shapes=[pltpu.VMEM((tka, g * tn), F32)], compiler_params=_cp(3, VMEM_MATMUL),
    )(a, *parts)


def _norm_mod(x, mod, row0, n_ctx, name):
    S, D = x.shape
    tm = _pick(S, [256, 128])

    def body(x_ref, mod_ref, h_ref):
        xv = x_ref[...]
        r = lax.rsqrt(jnp.mean(xv * xv, axis=-1, keepdims=True) + EPS)
        is_ctx = _rows(pl.program_id(0), tm) < n_ctx
        sh = jnp.where(is_ctx, mod_ref[0, row0:row0 + 1, :], mod_ref[1, row0:row0 + 1, :])
        sc = jnp.where(is_ctx, mod_ref[0, row0 + 1:row0 + 2, :], mod_ref[1, row0 + 1:row0 + 2, :])
        h_ref[...] = (xv * r * (1.0 + sc) + sh).astype(BF16)

    return pl.pallas_call(
        body, name=name, grid=(S // tm,),
        in_specs=[pl.BlockSpec((tm, D), lambda i: (i, 0)), pl.BlockSpec((2, 6, D), lambda i: (0, 0, 0))],
        out_specs=pl.BlockSpec((tm, D), lambda i: (i, 0)),
        out_shape=jax.ShapeDtypeStruct((S, D), BF16), compiler_params=_cp(1),
    )(x, mod)


def _loss_grad(x, target, n_ctx, name):
    S, D = x.shape
    tm = _pick(n_ctx, [256, 128])
    nct = n_ctx // tm

    def body(x_ref, t_ref, dx_ref, tot_ref, acc_ref):
        i = pl.program_id(0)

        @pl.when(i == 0)
        def _():
            acc_ref[...] = jnp.zeros_like(acc_ref)

        @pl.when(i < nct)
        def _():
            dx_ref[...] = jnp.zeros_like(dx_ref)

        @pl.when(i >= nct)
        def _():
            err = x_ref[...] - t_ref[...]
            dx_ref[...] = err * (1.0 / D)
            acc_ref[...] += jnp.sum(err * err, axis=0, keepdims=True)

        @pl.when(i == S // tm - 1)
        def _():
            tot = jnp.sum(acc_ref[...], axis=1, keepdims=True) * (0.5 / D)
            tot_ref[...] = jnp.broadcast_to(tot, tot_ref.shape)

    return pl.pallas_call(
        body, name=name, grid=(S // tm,),
        in_specs=[pl.BlockSpec((tm, D), lambda i: (i, 0)),
                  pl.BlockSpec((tm, D), lambda i: (jnp.maximum(i - nct, 0), 0))],
        out_specs=[pl.BlockSpec((tm, D), lambda i: (i, 0)), pl.BlockSpec((1, 128), lambda i: (0, 0))],
        out_shape=[jax.ShapeDtypeStruct((S, D), F32), jax.ShapeDtypeStruct((1, 128), F32)],
        scratch_shapes=[pltpu.VMEM((1, D), F32)], compiler_params=_cp(1),
    )(x, target)


def _halo_specs(tm, tc, S, col_off):
    per = tm // HALO
    last = S // HALO - 1
    return [pl.BlockSpec((HALO, tc), lambda j, i: (jnp.maximum(i * per - 1, 0), j + col_off)),
            pl.BlockSpec((tm, tc), lambda j, i: (i, j + col_off)),
            pl.BlockSpec((HALO, tc), lambda j, i: (jnp.minimum((i + 1) * per, last), j + col_off))]


def _ext(p_ref, m_ref, n_ref):
    return jnp.concatenate([p_ref[...], m_ref[...], n_ref[...]], axis=0).astype(F32)


def _links(i, tm, S, n_ctx):
    n = tm + 2 * HALO
    rid = i * tm - HALO + lax.broadcasted_iota(I32, (n, 1), 0)
    has_prev = (rid != 0) & (rid != n_ctx)
    has_next = (rid != n_ctx - 1) & (rid != S - 1)
    return has_prev, has_next


def _up(x):
    return pltpu.roll(x, 1, 0)


def _dn(x):
    return pltpu.roll(x, x.shape[0] - 1, 0)


def _conv3(x, w_ref, has_prev, has_next):
    return (w_ref[0:1, :] * jnp.where(has_prev, _up(x), 0.0) + w_ref[1:2, :] * x
            + w_ref[2:3, :] * jnp.where(has_next, _dn(x), 0.0))


def _conv3_t(d, w_ref, has_prev, has_next):
    return (w_ref[0:1, :] * jnp.where(has_next, _dn(d), 0.0) + w_ref[1:2, :] * d
            + w_ref[2:3, :] * jnp.where(has_prev, _up(d), 0.0))


def _conv3_wgrad(d, x, has_prev, has_next, extra=None):
    c = slice(HALO, d.shape[0] - HALO)
    taps = [jnp.where(has_prev, _up(x), 0.0), x, jnp.where(has_next, _dn(x), 0.0)]
    sums = [jnp.sum((d * t)[c], axis=0, keepdims=True) for t in taps]
    if extra is not None:
        sums.append(jnp.sum(extra[c], axis=0, keepdims=True))
    rid = lax.broadcasted_iota(I32, (8, d.shape[1]), 0)
    upd = jnp.zeros((8, d.shape[1]), F32)
    for k, s in enumerate(sums):
        upd = upd + jnp.where(rid == k, s, 0.0)
    return upd


def _sigmoid(x):
    return 1.0 / (1.0 + jnp.exp(-x))


def _ffn_act(u, conv_w, conv_b, l, n_ctx, name):
    S, F2 = u.shape
    F = F2 // 2
    tm = _pick(S, [384, 256, 128])
    tc = _pick(F, [1408, 512, 256, 128])
    nj = F // tc

    def body(gp, gm, gn, v_ref, w_ref, b_ref, a_ref):
        has_prev, has_next = _links(pl.program_id(1), tm, S, n_ctx)
        gc = _conv3(_ext(gp, gm, gn), w_ref, has_prev, has_next)[HALO:HALO + tm] + b_ref[...]
        a_ref[...] = (gc * _sigmoid(gc) * v_ref[...].astype(F32)).astype(BF16)

    return pl.pallas_call(
        body, name=name, grid=(nj, S // tm),
        in_specs=_halo_specs(tm, tc, S, 0) + [
            pl.BlockSpec((tm, tc), lambda j, i: (i, j + nj)),
            pl.BlockSpec((None, 3, tc), lambda j, i: (l, 0, j)),
            pl.BlockSpec((None, 1, tc), lambda j, i: (l, 0, j))],
        out_specs=pl.BlockSpec((tm, tc), lambda j, i: (i, j)),
        out_shape=jax.ShapeDtypeStruct((S, F), BF16), compiler_params=_cp(2),
    )(u, u, u, u, conv_w, conv_b)


def _ffn_act_bwd(u, da, conv_w, conv_b, l, n_ctx, name):
    S, F2 = u.shape
    F = F2 // 2
    tm = _pick(S, [384, 256, 128])
    tc = _pick(F, [1408, 512, 256, 128])
    nj = F // tc

    def body(gp, gm, gn, vp, vm, vn, dp, dm, dn_, w_ref, b_ref, dg_ref, dv_ref, acc_ref):
        i = pl.program_id(1)

        @pl.when(i == 0)
        def _():
            acc_ref[...] = jnp.zeros_like(acc_ref)

        has_prev, has_next = _links(i, tm, S, n_ctx)
        g = _ext(gp, gm, gn)
        val = _ext(vp, vm, vn)
        d_a = _ext(dp, dm, dn_)
        gc = _conv3(g, w_ref, has_prev, has_next) + b_ref[...]
        sg = _sigmoid(gc)
        dgc = d_a * val * (sg * (1.0 + gc * (1.0 - sg)))
        c = slice(HALO, HALO + tm)
        dv_ref[...] = (d_a * gc * sg)[c].astype(BF16)
        dg_ref[...] = _conv3_t(dgc, w_ref, has_prev, has_next)[c].astype(BF16)
        acc_ref[...] += _conv3_wgrad(dgc, g, has_prev, has_next, extra=dgc)

    return pl.pallas_call(
        body, name=name, grid=(nj, S // tm),
        in_specs=_halo_specs(tm, tc, S, 0) + _halo_specs(tm, tc, S, nj) + _halo_specs(tm, tc, S, 0) + [
            pl.BlockSpec((None, 3, tc), lambda j, i: (l, 0, j)),
            pl.BlockSpec((None, 1, tc), lambda j, i: (l, 0, j))],
        out_specs=[pl.BlockSpec((tm, tc), lambda j, i: (i, j))] * 2 + [pl.BlockSpec((8, tc), lambda j, i: (0, j))],
        out_shape=[jax.ShapeDtypeStruct((S, F), BF16)] * 2 + [jax.ShapeDtypeStruct((8, F), F32)],
        compiler_params=_cp(2),
    )(u, u, u, u, u, u, da, da, da, conv_w, conv_b)


def _sc_act(u, conv_w, l, n_ctx, name):
    S, D3 = u.shape
    D = D3 // 3
    tm = _pick(S, [384, 256, 128])
    tc = _pick(D, [1024, 512, 256, 128])
    nj = D // tc

    def body(b_ref, cp, cm, cn, vp, vm, vn, w_ref, z_ref):
        has_prev, has_next = _links(pl.program_id(1), tm, S, n_ctx)
        t = _ext(cp, cm, cn) * _ext(vp, vm, vn)
        cv = _conv3(t, w_ref, has_prev, has_next)[HALO:HALO + tm]
        z_ref[...] = (b_ref[...].astype(F32) * cv).astype(BF16)

    return pl.pallas_call(
        body, name=name, grid=(nj, S // tm),
        in_specs=[pl.BlockSpec((tm, tc), lambda j, i: (i, j))] + _halo_specs(tm, tc, S, nj)
        + _halo_specs(tm, tc, S, 2 * nj) + [pl.BlockSpec((None, 3, tc), lambda j, i: (l, 0, j))],
        out_specs=pl.BlockSpec((tm, tc), lambda j, i: (i, j)),
        out_shape=jax.ShapeDtypeStruct((S, D), BF16), compiler_params=_cp(2),
    )(u, u, u, u, u, u, u, conv_w)


def _sc_act_bwd(u, dz, conv_w, l, n_ctx, name):
    S, D3 = u.shape
    D = D3 // 3
    tm = _pick(S, [128])
    tc = D
    nj = 1

    def body(bp, bm, bn, cp, cm, cn, vp, vm, vn, zp, zm, zn, w_ref, du_ref, acc_ref):
        db_ref, dc_ref, dv_ref = du_ref.at[:, 0:D], du_ref.at[:, D:2 * D], du_ref.at[:, 2 * D:3 * D]
        i = pl.program_id(1)

        @pl.when(i == 0)
        def _():
            acc_ref[...] = jnp.zeros_like(acc_ref)

        has_prev, has_next = _links(i, tm, S, n_ctx)
        gb = _ext(bp, bm, bn)
        gcv = _ext(cp, cm, cn)
        val = _ext(vp, vm, vn)
        d_z = _ext(zp, zm, zn)
        t = gcv * val
        c = slice(HALO, HALO + tm)
        db_ref[...] = (d_z * _conv3(t, w_ref, has_prev, has_next))[c].astype(BF16)
        dcv = d_z * gb
        dt = _conv3_t(dcv, w_ref, has_prev, has_next)
        dc_ref[...] = (dt * val)[c].astype(BF16)
        dv_ref[...] = (dt * gcv)[c].astype(BF16)
        acc_ref[...] += _conv3_wgrad(dcv, t, has_prev, has_next)

    return pl.pallas_call(
        body, name=name, grid=(nj, S // tm),
        in_specs=_halo_specs(tm, tc, S, 0) + _halo_specs(tm, tc, S, nj) + _halo_specs(tm, tc, S, 2 * nj)
        + _halo_specs(tm, tc, S, 0) + [pl.BlockSpec((None, 3, tc), lambda j, i: (l, 0, j))],
        out_specs=[pl.BlockSpec((tm, D3), lambda j, i: (i, 0)), pl.BlockSpec((8, tc), lambda j, i: (0, j))],
        out_shape=[jax.ShapeDtypeStruct((S, D3), BF16), jax.ShapeDtypeStruct((8, D), F32)],
        compiler_params=_cp(2),
    )(u, u, u, u, u, u, u, u, u, dz, dz, dz, conv_w)


def _rope_tables(T, n_ctx):
    rows = T // GRID_W
    pairs = HEAD_DIM // 4
    row = jnp.repeat(jnp.arange(rows), GRID_W).astype(F32)
    col = jnp.tile(jnp.arange(GRID_W), rows).astype(F32)
    inv = ROPE_BASE ** (-jnp.arange(pairs, dtype=F32) / pairs)
    ang = jnp.concatenate([row[:, None] * inv, row[:, None] * inv, col[:, None] * inv, col[:, None] * inv], axis=1)
    cos, sin = jnp.cos(ang), jnp.sin(ang)
    first = (jnp.arange(HEAD_DIM) % (2 * pairs)) < pairs
    sin_a = jnp.where(first, -sin, 0.0)
    sin_b = jnp.where(first, 0.0, sin)
    pad = jnp.zeros((n_ctx, HEAD_DIM), F32)
    return (jnp.concatenate([pad + 1.0, cos], axis=0), jnp.concatenate([pad, sin_a], axis=0),
            jnp.concatenate([pad, sin_b], axis=0))


def _qk_prep(qkv, tabs, gains, l, n_q, n_kv, name):
    S = qkv.shape[0]
    W = qkv.shape[1]
    tm = _pick(S, [256, 128])
    cos, sin_a, sin_b = tabs

    def body(x_ref, cos_ref, sa_ref, sb_ref, g_ref, q_ref, k_ref):
        cs, sa, sb = cos_ref[...], sa_ref[...], sb_ref[...]
        for h in range(n_q + n_kv):
            xv = x_ref[:, h * 128:(h + 1) * 128].astype(F32)
            r = lax.rsqrt(jnp.mean(xv * xv, axis=-1, keepdims=True) + EPS)
            gain = g_ref[0:1, :] if h < n_q else g_ref[1:2, :]
            y = xv * r * gain
            out = (y * cs + pltpu.roll(y, 96, 1) * sa + pltpu.roll(y, 32, 1) * sb).astype(BF16)
            if h < n_q:
                q_ref[:, h * 128:(h + 1) * 128] = out
            else:
                k_ref[:, (h - n_q) * 128:(h - n_q + 1) * 128] = out

    tspec = pl.BlockSpec((tm, 128), lambda i: (i, 0))
    return pl.pallas_call(
        body, name=name, grid=(S // tm,),
        in_specs=[pl.BlockSpec((tm, W), lambda i: (i, 0)), tspec, tspec, tspec,
                  pl.BlockSpec((None, 2, 128), lambda i: (l, 0, 0))],
        out_specs=[pl.BlockSpec((tm, n_q * 128), lambda i: (i, 0)), pl.BlockSpec((tm, n_kv * 128), lambda i: (i, 0))],
        out_shape=[jax.ShapeDtypeStruct((S, n_q * 128), BF16), jax.ShapeDtypeStruct((S, n_kv * 128), BF16)],
        compiler_params=_cp(1),
    )(qkv, cos, sin_a, sin_b, gains)


def _qk_prep_bwd(dq, dk, dv, qkv, tabs, gains, l, n_q, n_kv, name):
    S, W = qkv.shape
    tm = _pick(S, [256, 128])
    cos, sin_a, sin_b = tabs

    def body(dq_ref, dk_ref, dv_ref, x_ref, cos_ref, sa_ref, sb_ref, g_ref, o_ref, acc_ref):
        @pl.when(pl.program_id(0) == 0)
        def _():
            acc_ref[...] = jnp.zeros_like(acc_ref)

        cs, sa, sb = cos_ref[...], sa_ref[...], sb_ref[...]
        dgq = jnp.zeros((1, 128), F32)
        dgk = jnp.zeros((1, 128), F32)
        for h in range(n_q + n_kv):
            if h < n_q:
                d_out = dq_ref[:, h * 128:(h + 1) * 128]
                gain = g_ref[0:1, :]
            else:
                d_out = dk_ref[:, (h - n_q) * 128:(h - n_q + 1) * 128]
                gain = g_ref[1:2, :]
            dy = d_out * cs + pltpu.roll(d_out * sa, 32, 1) + pltpu.roll(d_out * sb, 96, 1)
            xv = x_ref[:, h * 128:(h + 1) * 128].astype(F32)
            r = lax.rsqrt(jnp.mean(xv * xv, axis=-1, keepdims=True) + EPS)
            xhat = xv * r
            dg = jnp.sum(dy * xhat, axis=0, keepdims=True)
            if h < n_q:
                dgq = dgq + dg
            else:
                dgk = dgk + dg
            dxhat = dy * gain
            dx = r * (dxhat - xhat * jnp.mean(dxhat * xhat, axis=-1, keepdims=True))
            o_ref[:, h * 128:(h + 1) * 128] = dx.astype(BF16)
        v0 = (n_q + n_kv) * 128
        o_ref[:, v0:] = dv_ref[...].astype(BF16)
        rid = lax.broadcasted_iota(I32, (8, 128), 0)
        acc_ref[...] += jnp.where(rid == 0, dgq, 0.0) + jnp.where(rid == 1, dgk, 0.0)

    tspec = pl.BlockSpec((tm, 128), lambda i: (i, 0))
    return pl.pallas_call(
        body, name=name, grid=(S // tm,),
        in_specs=[pl.BlockSpec((tm, n_q * 128), lambda i: (i, 0)), pl.BlockSpec((tm, n_kv * 128), lambda i: (i, 0)),
                  pl.BlockSpec((tm, n_kv * 128), lambda i: (i, 0)), pl.BlockSpec((tm, W), lambda i: (i, 0)),
                  tspec, tspec, tspec, pl.BlockSpec((None, 2, 128), lambda i: (l, 0, 0))],
        out_specs=[pl.BlockSpec((tm, W), lambda i: (i, 0)), pl.BlockSpec((8, 128), lambda i: (0, 0))],
        out_shape=[jax.ShapeDtypeStruct((S, W), BF16), jax.ShapeDtypeStruct((8, 128), F32)],
        compiler_params=_cp(1),
    )(dq, dk, dv, qkv, cos, sin_a, sin_b, gains)


def _band_specs(width, col, nb, n_ctx):
    return [pl.BlockSpec((BLK, width), lambda i: (jnp.maximum(i - 1, 0), col)),
            pl.BlockSpec((BLK, width), lambda i: (i, col)),
            pl.BlockSpec((BLK, width), lambda i: (jnp.minimum(i + 1, nb - 1), col)),
            pl.BlockSpec((n_ctx, width), lambda i: (0, col))]


def _q_side_mask(i, S, n_ctx):
    shape = (GROUP * BLK, 3 * BLK + n_ctx)
    a = lax.broadcasted_iota(I32, shape, 0) & (BLK - 1)
    kk = lax.broadcasted_iota(I32, shape, 1)
    rq = i * BLK + a
    rk = (i - 1) * BLK + kk
    band = (rq >= n_ctx) & (rk >= n_ctx) & (rk < S) & (jnp.abs(rq - rk) <= WINDOW)
    return (kk >= 3 * BLK) | band


def _stack_heads(ref, g):
    return jnp.concatenate([ref[:, (GROUP * g + hh) * 128:(GROUP * g + hh + 1) * 128] for hh in range(GROUP)], axis=0)


def _stack_cols(ref, g):
    return jnp.concatenate([ref[:, GROUP * g + hh:GROUP * g + hh + 1] for hh in range(GROUP)], axis=0)


def _sink_col(sink_ref, l, g):
    return jnp.concatenate([jnp.full((BLK, 1), sink_ref[l, GROUP * g + hh], F32) for hh in range(GROUP)], axis=0)


def _attn_fwd(q, k, qkv, sink, l, n_ctx, name):
    S, DQ = q.shape
    DK = k.shape[1]
    n_kv = DK // 128
    nb = S // BLK
    vcol = (DQ + DK) // DK
    scale = HEAD_DIM ** -0.5

    def body(sink_ref, q_ref, kp, kc, kn, kx, vp, vc, vn, vx, o_ref, lse_ref):
        i = pl.program_id(0)
        mask = _q_side_mask(i, S, n_ctx)
        lane = lax.broadcasted_iota(I32, (BLK, 128), 1)
        lse_tile = jnp.zeros((BLK, 128), F32)
        outs = []
        for g in range(n_kv):
            sl = slice(g * 128, (g + 1) * 128)
            kcat = jnp.concatenate([kp[:, sl], kc[:, sl], kn[:, sl], kx[:, sl]], axis=0)
            vcat = jnp.concatenate([vp[:, sl], vc[:, sl], vn[:, sl], vx[:, sl]], axis=0)
            s = lax.dot_general(_stack_heads(q_ref, g), kcat, NT_DIMS, preferred_element_type=F32) * scale
            s = jnp.where(mask, s, NEG)
            sk = _sink_col(sink_ref, l, g)
            m = jnp.maximum(jnp.max(s, axis=1, keepdims=True), sk)
            e = jnp.exp(s - m)
            den = jnp.sum(e, axis=1, keepdims=True) + jnp.exp(sk - m)
            p = (e / den).astype(BF16)
            o = jnp.dot(p, vcat, preferred_element_type=F32)
            lse = m + jnp.log(den)
            for hh in range(GROUP):
                h = GROUP * g + hh
                outs.append(o[hh * BLK:(hh + 1) * BLK].astype(BF16))
                lse_tile = jnp.where(lane == h, lse[hh * BLK:(hh + 1) * BLK], lse_tile)
        o_ref[...] = jnp.concatenate(outs, axis=1)
        lse_ref[...] = lse_tile

    return pl.pallas_call(
        body, name=name, grid=(nb,),
        in_specs=[pl.BlockSpec(memory_space=pltpu.SMEM), pl.BlockSpec((BLK, DQ), lambda i: (i, 0))]
        + _band_specs(DK, 0, nb, n_ctx) + _band_specs(DK, vcol, nb, n_ctx),
        out_specs=[pl.BlockSpec((BLK, DQ), lambda i: (i, 0)), pl.BlockSpec((BLK, 128), lambda i: (i, 0))],
        out_shape=[jax.ShapeDtypeStruct((S, DQ), BF16), jax.ShapeDtypeStruct((S, 128), F32)],
        compiler_params=_cp(1),
    )(sink, q, k, k, k, k, qkv, qkv, qkv, qkv)


def _attn_bwd_q(q, k, qkv, o, do, lse, sink, l, n_ctx, name):
    S, DQ = q.shape
    DK = k.shape[1]
    n_kv = DK // 128
    nb = S // BLK
    vcol = (DQ + DK) // DK
    scale = HEAD_DIM ** -0.5

    def body(sink_ref, q_ref, kp, kc, kn, kx, vp, vc, vn, vx, o_ref, do_ref, lse_ref,
             dq_ref, delta_ref, dkx_ref, dvx_ref, dsink_ref):
        i = pl.program_id(0)

        @pl.when(i == 0)
        def _():
            dkx_ref[...] = jnp.zeros_like(dkx_ref)
            dvx_ref[...] = jnp.zeros_like(dvx_ref)
            dsink_ref[...] = jnp.zeros_like(dsink_ref)

        mask = _q_side_mask(i, S, n_ctx)
        lane = lax.broadcasted_iota(I32, (BLK, 128), 1)
        lane8 = lax.broadcasted_iota(I32, (8, 128), 1)
        row8 = lax.broadcasted_iota(I32, (8, 128), 0)
        delta_tile = jnp.zeros((BLK, 128), F32)
        dsink_upd = jnp.zeros((8, 128), F32)
        dqs, dkx_upd, dvx_upd = [], [], []
        for g in range(n_kv):
            sl = slice(g * 128, (g + 1) * 128)
            kcat = jnp.concatenate([kp[:, sl], kc[:, sl], kn[:, sl], kx[:, sl]], axis=0)
            vcat = jnp.concatenate([vp[:, sl], vc[:, sl], vn[:, sl], vx[:, sl]], axis=0)
            qg = _stack_heads(q_ref, g)
            dog = _stack_heads(do_ref, g)
            delta = jnp.sum(dog.astype(F32) * _stack_heads(o_ref, g).astype(F32), axis=1, keepdims=True)
            lse_g = _stack_cols(lse_ref, g)
            s = lax.dot_general(qg, kcat, NT_DIMS, preferred_element_type=F32) * scale
            p = jnp.exp(jnp.where(mask, s - lse_g, NEG))
            dp = lax.dot_general(dog, vcat, NT_DIMS, preferred_element_type=F32)
            ds = (p * (dp - delta) * scale).astype(BF16)
            dqg = jnp.dot(ds, kcat, preferred_element_type=F32)
            dkx_upd.append(lax.dot_general(ds[:, 3 * BLK:], qg, TN_DIMS, preferred_element_type=F32))
            dvx_upd.append(lax.dot_general(p.astype(BF16)[:, 3 * BLK:], dog, TN_DIMS, preferred_element_type=F32))
            dsk = -jnp.exp(_sink_col(sink_ref, l, g) - lse_g) * delta
            for hh in range(GROUP):
                h = GROUP * g + hh
                rs = slice(hh * BLK, (hh + 1) * BLK)
                dqs.append(dqg[rs])
                delta_tile = jnp.where(lane == h, delta[rs], delta_tile)
                tot = jnp.sum(dsk[rs], axis=0, keepdims=True)
                dsink_upd = dsink_upd + jnp.where((lane8 == h) & (row8 == 0), tot, 0.0)
        dq_ref[...] = jnp.concatenate(dqs, axis=1)
        dkx_ref[...] += jnp.concatenate(dkx_upd, axis=1)
        dvx_ref[...] += jnp.concatenate(dvx_upd, axis=1)
        delta_ref[...] = delta_tile
        dsink_ref[...] += dsink_upd

    blk = pl.BlockSpec((BLK, DQ), lambda i: (i, 0))
    stat = pl.BlockSpec((BLK, 128), lambda i: (i, 0))
    return pl.pallas_call(
        body, name=name, grid=(nb,),
        in_specs=[pl.BlockSpec(memory_space=pltpu.SMEM), blk] + _band_specs(DK, 0, nb, n_ctx)
        + _band_specs(DK, vcol, nb, n_ctx) + [blk, blk, stat],
        out_specs=[blk, stat, pl.BlockSpec((n_ctx, DK), lambda i: (0, 0)), pl.BlockSpec((n_ctx, DK), lambda i: (0, 0)),
                   pl.BlockSpec((8, 128), lambda i: (0, 0))],
        out_shape=[jax.ShapeDtypeStruct((S, DQ), F32), jax.ShapeDtypeStruct((S, 128), F32),
                   jax.ShapeDtypeStruct((n_ctx, DK), F32), jax.ShapeDtypeStruct((n_ctx, DK), F32),
                   jax.ShapeDtypeStruct((8, 128), F32)],
        compiler_params=_cp(1),
    )(sink, q, k, k, k, k, qkv, qkv, qkv, qkv, o, do, lse)


def _attn_bwd_kv(q, k, qkv, do, lse, delta, dkx, dvx, n_ctx, name):
    S, DQ = q.shape
    DK = k.shape[1]
    n_kv = DK // 128
    nb = S // BLK
    nctx_b = n_ctx // BLK
    vcol = (DQ + DK) // DK
    scale = HEAD_DIM ** -0.5

    def three(width):
        return [pl.BlockSpec((BLK, width), lambda j: (jnp.maximum(j - 1, 0), 0)),
                pl.BlockSpec((BLK, width), lambda j: (j, 0)),
                pl.BlockSpec((BLK, width), lambda j: (jnp.minimum(j + 1, nb - 1), 0))]

    def body(k_ref, v_ref, qp, qc, qn, dop, doc, don, lp, lc, ln, dlp, dlc, dln, dkx_ref, dvx_ref, dk_ref, dv_ref):
        j = pl.program_id(0)

        @pl.when(j < nctx_b)
        def _():
            dk_ref[...] = dkx_ref[...]
            dv_ref[...] = dvx_ref[...]

        @pl.when(j >= nctx_b)
        def _():
            shape = (3 * GROUP * BLK, BLK)
            t = lax.broadcasted_iota(I32, shape, 0)
            rq = (j - 1 + t // (GROUP * BLK)) * BLK + (t & (BLK - 1))
            rk = j * BLK + lax.broadcasted_iota(I32, shape, 1)
            valid = (rq >= n_ctx) & (rq < S) & (jnp.abs(rq - rk) <= WINDOW)
            dks, dvs = [], []
            for g in range(n_kv):
                sl = slice(g * 128, (g + 1) * 128)
                qcat = jnp.concatenate([_stack_heads(r, g) for r in (qp, qc, qn)], axis=0)
                docat = jnp.concatenate([_stack_heads(r, g) for r in (dop, doc, don)], axis=0)
                lse_c = jnp.concatenate([_stack_cols(r, g) for r in (lp, lc, ln)], axis=0)
                delta_c = jnp.concatenate([_stack_cols(r, g) for r in (dlp, dlc, dln)], axis=0)
                s = lax.dot_general(qcat, k_ref[:, sl], NT_DIMS, preferred_element_type=F32) * scale
                p = jnp.exp(jnp.where(valid, s - lse_c, NEG))
                dp = lax.dot_general(docat, v_ref[:, sl], NT_DIMS, preferred_element_type=F32)
                ds = (p * (dp - delta_c) * scale).astype(BF16)
                dks.append(lax.dot_general(ds, qcat, TN_DIMS, preferred_element_type=F32))
                dvs.append(lax.dot_general(p.astype(BF16), docat, TN_DIMS, preferred_element_type=F32))
            dk_ref[...] = jnp.concatenate(dks, axis=1)
            dv_ref[...] = jnp.concatenate(dvs, axis=1)

    cspec = pl.BlockSpec((BLK, DK), lambda j: (jnp.minimum(j, nctx_b - 1), 0))
    return pl.pallas_call(
        body, name=name, grid=(nb,),
        in_specs=[pl.BlockSpec((BLK, DK), lambda j: (j, 0)), pl.BlockSpec((BLK, DK), lambda j: (j, vcol))]
        + three(DQ) + three(DQ) + three(128) + three(128) + [cspec, cspec],
        out_specs=[pl.BlockSpec((BLK, DK), lambda j: (j, 0))] * 2,
        out_shape=[jax.ShapeDtypeStruct((S, DK), F32)] * 2, compiler_params=_cp(1),
    )(k, qkv, q, q, q, do, do, do, lse, lse, lse, delta, delta, delta, dkx, dvx)


def _ada_fwd(cond, w_ada, b_cols, name):
    lyr, D, C = w_ada.shape
    tc = _pick(C, [512, 384, 256, 128])

    def body(c_ref, w_ref, b_ref, o_ref):
        cv = c_ref[...]
        act = cv * _sigmoid(cv)
        o_ref[...] = jnp.dot(act, w_ref[...], preferred_element_type=F32,
                             precision=lax.Precision.HIGHEST) + b_ref[...]

    return pl.pallas_call(
        body, name=name, grid=(lyr, C // tc),
        in_specs=[pl.BlockSpec((16, D), lambda l, j: (0, 0)),
                  pl.BlockSpec((None, D, tc), lambda l, j: (l, 0, j)),
                  pl.BlockSpec((None, 1, tc), lambda l, j: (l, 0, j))],
        out_specs=pl.BlockSpec((None, 16, tc), lambda l, j: (l, 0, j)),
        out_shape=jax.ShapeDtypeStruct((lyr, 16, C), F32), compiler_params=_cp(2),
    )(cond, w_ada, b_cols)


def _ada_bwd(cond, d_out, w_ada, name):
    lyr, D, C = w_ada.shape
    tc = _pick(C, [512, 384, 256, 128])

    def body(c_ref, d_ref, w_ref, gw_ref, dc_ref):
        @pl.when((pl.program_id(0) == 0) & (pl.program_id(1) == 0))
        def _():
            dc_ref[...] = jnp.zeros_like(dc_ref)

        cv = c_ref[...]
        act = cv * _sigmoid(cv)
        dv = d_ref[...]
        gw_ref[...] = lax.dot_general(act, dv, TN_DIMS, preferred_element_type=F32, precision=lax.Precision.HIGHEST)
        dc_ref[...] += lax.dot_general(dv, w_ref[...], NT_DIMS, preferred_element_type=F32,
                                       precision=lax.Precision.HIGHEST)

    return pl.pallas_call(
        body, name=name, grid=(lyr, C // tc),
        in_specs=[pl.BlockSpec((16, D), lambda l, j: (0, 0)),
                  pl.BlockSpec((None, 16, tc), lambda l, j: (l, 0, j)),
                  pl.BlockSpec((None, D, tc), lambda l, j: (l, 0, j))],
        out_specs=[pl.BlockSpec((None, D, tc), lambda l, j: (l, 0, j)), pl.BlockSpec((16, D), lambda l, j: (0, 0))],
        out_shape=[jax.ShapeDtypeStruct((lyr, D, C), F32), jax.ShapeDtypeStruct((16, D), F32)],
        compiler_params=_cp(2),
    )(cond, d_out, w_ada)


def _sum_rows(d_rows, name):
    lyr, r, C = d_rows.shape

    def body(d_ref, o_ref):
        o_ref[...] = jnp.sum(d_ref[...], axis=0, keepdims=True)

    return pl.pallas_call(
        body, name=name, grid=(lyr,),
        in_specs=[pl.BlockSpec((None, r, C), lambda l: (l, 0, 0))],
        out_specs=pl.BlockSpec((None, 1, C), lambda l: (l, 0, 0)),
        out_shape=jax.ShapeDtypeStruct((lyr, 1, C), F32), compiler_params=_cp(1),
    )(d_rows)


def _cctx_grad(gathered, c_ctx_row, name):
    D = gathered.shape[1]

    def body(g_ref, c_ref, o_ref):
        acc = g_ref[0:16, :]
        for d in range(1, N_DEV):
            acc = acc + g_ref[16 * d:16 * (d + 1), :]
        cv = c_ref[...]
        sg = _sigmoid(cv)
        o_ref[...] = acc[8:16] * (sg * (1.0 + cv * (1.0 - sg)))

    return pl.pallas_call(
        body, name=name, out_shape=jax.ShapeDtypeStruct((8, D), F32),
        compiler_params=pltpu.CompilerParams(vmem_limit_bytes=VMEM_ELEMENTWISE),
    )(gathered, c_ctx_row)


def _pad_rows(a, rows):
    return jnp.concatenate([a, jnp.zeros((rows - a.shape[0],) + a.shape[1:], a.dtype)], axis=0)


def kernel(x, c, ctx, c_ctx, w_ada, b_ada, attn_w_qkv, attn_w_o, attn_q_gain, attn_k_gain, attn_sink, sc_w_in, sc_conv, sc_w_out, ffn_w_up, ffn_conv, ffn_conv_b, ffn_w_down, loss_target, m_c_ctx, m_w_ada, m_b_ada, m_attn_w_qkv, m_attn_w_o, m_attn_q_gain, m_attn_k_gain, m_attn_sink, m_sc_w_in, m_sc_conv, m_sc_w_out, m_ffn_w_up, m_ffn_conv, m_ffn_conv_b, m_ffn_w_down, v_c_ctx, v_w_ada, v_b_ada, v_attn_w_qkv, v_attn_w_o, v_attn_q_gain, v_attn_k_gain, v_attn_sink, v_sc_w_in, v_sc_conv, v_sc_w_out, v_ffn_w_up, v_ffn_conv, v_ffn_conv_b, v_ffn_w_down):
    T, D = x.shape[1], x.shape[2]
    L = ctx.shape[1]
    S = L + T
    depth = w_ada.shape[0]
    F = ffn_conv_b.shape[1]
    n_q = D // HEAD_DIM
    n_kv = n_q // GROUP
    ada_c = w_ada.shape[2]
    assert L % BLK == 0 and T % BLK == 0 and ada_c * N_DEV == 6 * D

    px, py, pc = _my_pos()
    me = 4 * px + 2 * py + pc
    me_idx = jnp.reshape(me, (1,)).astype(I32)

    tm_mm = _pick(S, [768, 704, 384, 256, 128])
    ts_tn = _pick(S, [2112, 1056, 768, 384, 256, 128])
    tm_half = _pick(S, [384, 256, 128])

    c_all = _gather_small(_pad_rows(c, 8), "gather_cond")
    cond = jnp.concatenate([c_all[0::8], c_ctx[None, :], jnp.zeros((7, D), F32)], axis=0)
    b_cols = lax.dynamic_slice_in_dim(b_ada, me * ada_c, ada_c, axis=1)[:, None, :]
    ada_mine = _ada_fwd(cond, w_ada, b_cols, "ada_fwd")
    ada_all = _gather_small(ada_mine.reshape(depth * 16, ada_c), "gather_ada")
    ada_all = ada_all.reshape(N_DEV, depth, 16, ada_c)
    ada_rows = jnp.transpose(ada_all, (1, 2, 0, 3)).reshape(depth, 16, 6, D)
    mod_lat = lax.dynamic_index_in_dim(ada_rows, me, axis=1, keepdims=False)
    mods = jnp.stack([ada_rows[:, 8], mod_lat], axis=1)

    gathered = [None] * depth
    tabs = _rope_tables(T, L)
    gains = jnp.stack([attn_q_gain, attn_k_gain], axis=1)
    conv_b3 = ffn_conv_b[:, None, :]
    sc_conv_all = _gather_small(_pad_rows(sc_conv.reshape(-1, sc_conv.shape[2]), 8), "gather_scconv")
    ffn_conv_all = _gather_small(_pad_rows(ffn_conv.reshape(-1, ffn_conv.shape[2]), 16), "gather_ffnconv")
    n_sc = sc_conv.shape[0]
    sc_conv_full = jnp.transpose(sc_conv_all.reshape(N_DEV, 8, -1)[:, :n_sc * 3], (1, 0, 2)).reshape(n_sc, 3, D)
    ffn_conv_full = jnp.transpose(ffn_conv_all.reshape(N_DEV, 16, -1)[:, :depth * 3], (1, 0, 2)).reshape(depth, 3, F)

    def start_weights(l, tag, after):
        if tag == "ffn":
            ws = [(ffn_w_up, l), (ffn_w_down, l)]
        else:
            ws = [(attn_w_qkv, l // 2), (attn_w_o, l // 2)] if l % 2 == 0 else [(sc_w_in, l // 2), (sc_w_out, l // 2)]
        lands = [_cast_layer(w, j, me_idx, f"cast_{tag}{k}_{l}") for k, (w, j) in enumerate(ws)]
        return _gather_start(lands, after, f"gather_start_{tag}{l}")

    def wait_weights(flight, after, name):
        send_sems, recv_sems, lands, _ = flight
        return _gather_wait(lands, send_sems, recv_sems, after, name)

    flight_mix = start_weights(0, "mix", [mods, sc_conv_full, ffn_conv_full])
    mods = mods + flight_mix[3][0, 0]

    xs = jnp.concatenate([ctx[0], x[0]], axis=0)
    saved = []
    for l in range(depth):
        j = l // 2
        mod = mods[l]
        if l == 0:
            w_a, w_b = wait_weights(flight_mix, mods, "gather_wait_mix0")
            flight_ffn = start_weights(0, "ffn", [w_a])
            mod = mod + flight_ffn[3][0, 0]
            h = _norm_mod(xs, mod, 0, L, "norm_m0")
        else:
            h = _norm_mod(xs, mod, 0, L, f"norm_m{l}")
            w_a, w_b = wait_weights(flight_mix, h, f"gather_wait_mix{l}")
        if l % 2 == 0:
            qkv = _mm_nn(h, w_a, tm=tm_mm, tn=w_a.shape[2], out_dtype=BF16, name=f"qkv{l}")
            qr, kr = _qk_prep(qkv, tabs, gains, j, n_q, n_kv, f"qk_prep{l}")
            z, lse = _attn_fwd(qr, kr, qkv, attn_sink, j, L, f"attn{l}")
            mix = (qkv, qr, kr, lse)
        else:
            u = _mm_nn(h, w_a, tm=tm_mm, tn=w_a.shape[2], out_dtype=BF16, name=f"scin{l}")
            z = _sc_act(u, sc_conv_full, j, L, f"sc_act{l}")
            mix = (u,)
        if l + 1 < depth:
            flight_mix = start_weights(l + 1, "mix", [z])
            mod = mod + flight_mix[3][0, 0]
        y_m, x1, h2 = _mm_nn_resid_norm(z, w_b.reshape(D, D), xs, mod, 2, 3, L, tm=tm_half, name=f"mixout{l}")
        w_up, w_down = wait_weights(flight_ffn, x1, f"gather_wait_ffn{l}")
        gathered[l] = (w_a, w_b, w_up, w_down)
        u_f = _mm_nn(h2, w_up, tm=tm_mm, tn=w_up.shape[2], out_dtype=BF16, name=f"up{l}")
        a_f = _ffn_act(u_f, ffn_conv_full, conv_b3, l, L, f"ffn_act{l}")
        if l + 1 < depth:
            flight_ffn = start_weights(l + 1, "ffn", [a_f])
            mod = mod + flight_ffn[3][0, 0]
        y_f, x2 = _mm_nn_resid(a_f, w_down.reshape(F, D), x1, mod, 5, L, tm=tm_mm, tn=_pick(D, [512]), name=f"down{l}")
        saved.append((xs, h, mix, z, y_m, x1, h2, u_f, a_f, y_f))
        xs = x2

    dx, sq = _loss_grad(xs, loss_target[0], L, "loss")
    loss = lax.psum(sq[0, 0], ("x", "y", "c"))

    dmods = [None] * depth
    g_conv_b, g_ffn_conv, g_sc_conv = [None] * depth, [None] * depth, [None] * n_sc
    g_gain, g_sink = [None] * (depth - n_sc), [None] * (depth - n_sc)
    rs_flight = [None] * depth
    sent = jnp.zeros((), F32)
    for l in reversed(range(depth)):
        j = l // 2
        w_a, w_b, w_up, w_down = gathered[l]
        mod = mods[l] + sent
        x0, h, mix, z, y_m, x1, h2, u_f, a_f, y_f = saved[l]
        da, dy, s_gf = _gate_dgrad(dx, y_f, mod, 5, L, w_down.reshape(F, D), tm=tm_mm, tn=_pick(F, [1408, 512]),
                                   name=f"down_dgrad{l}")
        gw_down = _mm_tn(a_f, [dy], nb=1, tka=_pick(F, [1408, 512]), tn=_pick(D, [1024, 512]), ts=ts_tn, name=f"down_wgrad{l}")
        dgate, dval, s_conv = _ffn_act_bwd(u_f, da, ffn_conv_full, conv_b3, l, L, f"ffn_act_bwd{l}")
        dx1, s_nf = _dgrad_norm_bwd([dgate, dval], w_up, x1, mod, dx, 3, L, tm=tm_half, allow_pair=False,
                                    name=f"up_dgrad{l}")
        gw_up = _mm_tn(h2, [dgate, dval], nb=N_DEV, tka=_pick(D, [1024, 512]), tn=w_up.shape[2], ts=ts_tn, name=f"up_wgrad{l}")
        g_ffn_conv[l], g_conv_b[l] = s_conv[0:3], s_conv[3]
        rs_ffn = _rs_start([gw_up, gw_down.reshape(N_DEV, -1, D)], f"rs_start_ffn{l}")
        mod = mods[l] + rs_ffn[4][0, 0]
        dz, dy, s_gm = _gate_dgrad(dx1, y_m, mod, 2, L, w_b.reshape(D, D), tm=tm_mm, tn=_pick(D, [1024, 512]),
                                   name=f"mixout_dgrad{l}")
        gw_b = _mm_tn(z, [dy], nb=1, tka=_pick(D, [1024, 512]), tn=_pick(D, [1024, 512]), ts=ts_tn, name=f"mixout_wgrad{l}")
        if l % 2 == 0:
            qkv, qr, kr, lse = mix
            dq, delta, dkx, dvx, s_sink = _attn_bwd_q(qr, kr, qkv, z, dz, lse, attn_sink, j, L, f"attn_bwd_q{l}")
            dk, dv = _attn_bwd_kv(qr, kr, qkv, dz, lse, delta, dkx, dvx, L, f"attn_bwd_kv{l}")
            du_m, s_gain = _qk_prep_bwd(dq, dk, dv, qkv, tabs, gains, j, n_q, n_kv, f"qk_prep_bwd{l}")
            g_gain[j], g_sink[j] = s_gain[0:2], s_sink[0]
        else:
            (u,) = mix
            du_m, s_scconv = _sc_act_bwd(u, dz, sc_conv_full, j, L, f"sc_act_bwd{l}")
            g_sc_conv[j] = s_scconv[0:3]
        dx, s_nm = _dgrad_norm_bwd([du_m], w_a, x0, mod, dx1, 0, L, tm=tm_half, allow_pair=True, name=f"mixin_dgrad{l}")
        gw_a = _mm_tn(h, [du_m], nb=N_DEV, tka=_pick(D, [1024, 512]), tn=w_a.shape[2], ts=ts_tn, name=f"mixin_wgrad{l}")
        dmods[l] = jnp.stack([jnp.stack([s_nm[2 * k], s_nm[2 * k + 1], s_gm[k], s_nf[2 * k], s_nf[2 * k + 1], s_gf[k]])
                              for k in range(2)])
        rs_mix = _rs_start([gw_a, gw_b.reshape(N_DEV, -1, D)], f"rs_start_mix{l}")
        sent = rs_mix[4][0, 0]
        rs_flight[l] = (rs_mix, rs_ffn)

    grad_x = dx[L:][None]

    big_w = {"qkv": (attn_w_qkv, m_attn_w_qkv, v_attn_w_qkv), "wo": (attn_w_o, m_attn_w_o, v_attn_w_o),
             "scin": (sc_w_in, m_sc_w_in, v_sc_w_in), "scout": (sc_w_out, m_sc_w_out, v_sc_w_out),
             "up": (ffn_w_up, m_ffn_w_up, v_ffn_w_up), "down": (ffn_w_down, m_ffn_w_down, v_ffn_w_down)}
    big_out = {k: [] for k in big_w}
    for l in reversed(range(depth)):
        j = l // 2
        groups = [(["qkv", "wo"] if l % 2 == 0 else ["scin", "scout"], [j, j]), (["up", "down"], [l, l])]
        for (names, idxs), flight, tag in reversed(list(zip(groups, rs_flight[l], ("mix", "ffn")))):
            send_sems, recv_sems, own, zones, _ = flight
            own, zones = _rs_wait(own, zones, send_sems, recv_sems, dx, f"rs_wait_{tag}{l}")
            for n, li, p, z in zip(names, idxs, own, zones):
                w, m, v = big_w[n]
                big_out[n].insert(0, _adamw_reduced(p, z, me_idx, w, m, v, li, f"adamw_{n}{l}"))
    big_res = {k: [jnp.stack([o[t] for o in outs]) for t in range(4)] for k, outs in big_out.items()}

    n_attn = depth - n_sc
    pack = [jnp.stack(dmods)[:, 0].reshape(-1, 128), jnp.stack(dmods)[:, 1].reshape(-1, 128),
            jnp.stack(g_gain).reshape(-1, 128), jnp.stack(g_sink),
            jnp.stack(g_conv_b).reshape(-1, 128), jnp.stack(g_ffn_conv).reshape(-1, 128),
            jnp.stack(g_sc_conv).reshape(-1, 128)]
    used = [p.shape[0] for p in pack]
    pack = [_pad_rows(p, -(-p.shape[0] // 8) * 8) for p in pack]
    sizes = [p.shape[0] for p in pack]
    flat = jnp.concatenate(pack, axis=0)
    rows = flat.shape[0]
    small_all = _gather_small(flat, "gather_small_grads")
    small_sum = _sum8(small_all, rows, "sum_small_grads")
    offs = [sum(sizes[:k]) for k in range(len(sizes))]
    seg = lambda a, k: a[offs[k]:offs[k] + used[k]]
    dmod_ctx = seg(small_sum, 0).reshape(depth, 6 * D)
    dmod_lat = small_all.reshape(N_DEV, rows, 128)[:, offs[1]:offs[1] + used[1]].reshape(N_DEV, depth, 6 * D)
    g_gain_sum = seg(small_sum, 2).reshape(n_attn, 2, 128)
    g_sink_sum = seg(small_sum, 3)[:n_attn, :n_q]
    g_conv_b_sum = seg(small_sum, 4).reshape(depth, F)
    g_ffn_conv_sum = seg(small_sum, 5).reshape(depth, 3, F)
    g_sc_conv_sum = seg(small_sum, 6).reshape(n_sc, 3, D)

    d_rows = jnp.concatenate([jnp.transpose(dmod_lat, (1, 0, 2)), dmod_ctx[:, None, :],
                              jnp.zeros((depth, 7, 6 * D), F32)], axis=1)
    d_cols = lax.dynamic_slice_in_dim(d_rows, me * ada_c, ada_c, axis=2)
    g_w_ada, dcond_part = _ada_bwd(cond, d_cols, w_ada, "ada_bwd")
    dcond_all = _gather_small(dcond_part, "gather_dcond")
    g_c_ctx = _cctx_grad(dcond_all, jnp.broadcast_to(c_ctx[None, :], (8, D)), "cctx_grad")[0]
    g_b_ada = _sum_rows(d_rows, "b_ada_grad")[:, 0]

    def small_adam(w, g, m, v, name):
        w2 = w.reshape(-1, w.shape[-1])
        d, m2, v2 = _adamw_plain(w2, g.reshape(w2.shape), m.reshape(w2.shape), v.reshape(w2.shape), name)
        return g.reshape(w.shape), d.reshape(w.shape), m2.reshape(w.shape), v2.reshape(w.shape)

    g_sc_conv_mine = lax.dynamic_slice_in_dim(g_sc_conv_sum, me * sc_conv.shape[2], sc_conv.shape[2], axis=2)
    g_ffn_conv_mine = lax.dynamic_slice_in_dim(g_ffn_conv_sum, me * ffn_conv.shape[2], ffn_conv.shape[2], axis=2)
    res = {
        "c_ctx": small_adam(c_ctx[None, :], g_c_ctx[None, :], m_c_ctx[None, :], v_c_ctx[None, :], "adamw_c_ctx"),
        "b_ada": small_adam(b_ada, g_b_ada, m_b_ada, v_b_ada, "adamw_b_ada"),
        "attn_q_gain": small_adam(attn_q_gain, g_gain_sum[:, 0], m_attn_q_gain, v_attn_q_gain, "adamw_q_gain"),
        "attn_k_gain": small_adam(attn_k_gain, g_gain_sum[:, 1], m_attn_k_gain, v_attn_k_gain, "adamw_k_gain"),
        "attn_sink": small_adam(attn_sink, g_sink_sum, m_attn_sink, v_attn_sink, "adamw_sink"),
        "sc_conv": small_adam(sc_conv, g_sc_conv_mine, m_sc_conv, v_sc_conv, "adamw_sc_conv"),
        "ffn_conv": small_adam(ffn_conv, g_ffn_conv_mine, m_ffn_conv, v_ffn_conv, "adamw_ffn_conv"),
        "ffn_conv_b": small_adam(ffn_conv_b, g_conv_b_sum, m_ffn_conv_b, v_ffn_conv_b, "adamw_conv_b"),
    }
    res["c_ctx"] = tuple(t[0] for t in res["c_ctx"])
    res["w_ada"] = (g_w_ada,) + tuple(_adamw_tiled(w_ada, g_w_ada, m_w_ada, v_w_ada, "adamw_w_ada"))
    res["attn_w_qkv"], res["attn_w_o"] = big_res["qkv"], big_res["wo"]
    res["sc_w_in"], res["sc_w_out"] = big_res["scin"], big_res["scout"]
    res["ffn_w_up"], res["ffn_w_down"] = big_res["up"], big_res["down"]

    order = ["c_ctx", "w_ada", "b_ada", "attn_w_qkv", "attn_w_o", "attn_q_gain", "attn_k_gain", "attn_sink",
             "sc_w_in", "sc_conv", "sc_w_out", "ffn_w_up", "ffn_conv", "ffn_conv_b", "ffn_w_down"]
    outs = [loss, grad_x]
    for t in range(4):
        outs += [res[n][t] for n in order]
    return tuple(outs)
```

```python
import functools

import jax
import jax.numpy as jnp
from jax import lax
from jax.experimental import pallas as pl
from jax.experimental.pallas import tpu as pltpu

F32 = jnp.float32
BF16 = jnp.bfloat16
I32 = jnp.int32

N_DEV = 8
HEAD_DIM = 128
GROUP = 4
WINDOW = 128
BLK = 128
GRID_W = 64
ROPE_BASE = 10000.0
EPS = 1e-6
NEG = -1e30
HALO = 16

ADAM_LR = 0.001
ADAM_B1 = 0.9
ADAM_B2 = 0.999
ADAM_EPS = 1e-08
ADAM_WD = 0.01
ADAM_STEP = 10

V7X_VMEM_BYTES = 64 << 20
VMEM_MATMUL = 52 << 20
VMEM_ELEMENTWISE = 44 << 20

MESH = pl.DeviceIdType.MESH
ANY = pl.BlockSpec(memory_space=pl.ANY)
HBM = pl.BlockSpec(memory_space=pltpu.HBM)
SEM = pl.BlockSpec(memory_space=pltpu.SEMAPHORE)
EFFECT = pltpu.SideEffectType.DATAFLOW_SIDE_EFFECTING

NT_DIMS = (((1,), (1,)), ((), ()))
TN_DIMS = (((0,), (0,)), ((), ()))


def _pick(n, cands):
    for t in cands:
        if n % t == 0:
            return t
    raise ValueError(f"no tile for {n} in {cands}")


def _cp(n_axes, vmem=VMEM_ELEMENTWISE):
    return pltpu.CompilerParams(dimension_semantics=("arbitrary",) * n_axes, vmem_limit_bytes=vmem)


def _rows(i, tm, off=0):
    return i * tm + off + lax.broadcasted_iota(I32, (tm, 1), 0)


def _my_pos():
    return lax.axis_index("x"), lax.axis_index("y"), lax.axis_index("c")


def _gather_small(x_shard, name):
    m_per, n = x_shard.shape

    def body(x_ref, out_ref, send_sems, recv_sems, local_sem):
        x, y, c = _my_pos()
        me, sibling = (x, y, c), (x, y, 1 - c)
        chips = [(1 - x, y), (x, 1 - y), (1 - x, 1 - y)]

        def rows(px, py, pc):
            return out_ref.at[pl.ds((4 * px + 2 * py + pc) * m_per, m_per), :]

        def copy(k, block, to, src=None):
            return pltpu.make_async_remote_copy(
                src_ref=rows(*block) if src is None else src, dst_ref=rows(*block),
                send_sem=send_sems.at[k], recv_sem=recv_sems.at[k], device_id=to, device_id_type=MESH)

        mine = pltpu.make_async_copy(x_ref, rows(*me), local_sem)
        mine.start()
        first = [copy(0, me, sibling, src=x_ref)]
        first += [copy(1 + j, me, (*chip, c), src=x_ref) for j, chip in enumerate(chips)]
        for cp in first:
            cp.start()
        passed = [copy(4 + j, (*chip, c), sibling) for j, chip in enumerate(chips)]
        for j, chip in enumerate(chips):
            copy(1 + j, (*chip, c), me).wait_recv()
            passed[j].start()
        copy(0, sibling, me).wait_recv()
        for j, chip in enumerate(chips):
            copy(4 + j, (*chip, 1 - c), me).wait_recv()
        for cp in first + passed:
            cp.wait_send()
        mine.wait()

    return pl.pallas_call(
        body, name=name,
        out_shape=jax.ShapeDtypeStruct((N_DEV * m_per, n), x_shard.dtype),
        in_specs=[pl.BlockSpec(memory_space=pltpu.VMEM)],
        out_specs=pl.BlockSpec(memory_space=pltpu.VMEM),
        scratch_shapes=[pltpu.SemaphoreType.DMA((7,)), pltpu.SemaphoreType.DMA((7,)), pltpu.SemaphoreType.DMA],
        compiler_params=pltpu.CompilerParams(vmem_limit_bytes=VMEM_ELEMENTWISE),
    )(x_shard)


def _peer(k):
    x, y, c = _my_pos()
    b = k + 1
    return ((1 - x) if b & 4 else x, (1 - y) if b & 2 else y, (1 - c) if b & 1 else c)


def _slot(p):
    return 4 * p[0] + 2 * p[1] + p[2]


def _in_hbm(a):
    return pltpu.with_memory_space_constraint(a, pltpu.HBM)


def _gather_start(lands, after, name):
    n = len(lands)

    def body(*refs):
        l_refs, send_sems, recv_sems = refs[:n], refs[n + len(after)], refs[n + len(after) + 1]
        token = refs[2 * n + len(after) + 2]
        me = _slot(_my_pos())
        for a in range(n):
            for k in range(7):
                pltpu.make_async_remote_copy(
                    src_ref=l_refs[a].at[me], dst_ref=l_refs[a].at[me],
                    send_sem=send_sems.at[7 * a + k], recv_sem=recv_sems.at[7 * a + k],
                    device_id=_peer(k), device_id_type=MESH).start()
        token[...] = jnp.zeros_like(token)

    out = pl.pallas_call(
        body, name=name,
        out_shape=(pltpu.SemaphoreType.DMA((7 * n,)), pltpu.SemaphoreType.DMA((7 * n,)),
                   *[pltpu.HBM(a.shape, a.dtype) for a in lands], jax.ShapeDtypeStruct((8, 128), F32)),
        in_specs=[HBM] * n + [ANY] * len(after),
        out_specs=(SEM, SEM, *[HBM] * n, pl.BlockSpec(memory_space=pltpu.VMEM)),
        input_output_aliases={a: 2 + a for a in range(n)},
        compiler_params=pltpu.CompilerParams(has_side_effects=EFFECT),
    )(*[_in_hbm(a) for a in lands], *after)
    return out[0], out[1], list(out[2:2 + n]), out[2 + n]


def _gather_wait(lands, send_sems, recv_sems, after, name):
    n = len(lands)

    def body(*refs):
        l_refs, ss, rs = refs[:n], refs[n], refs[n + 1]
        me = _slot(_my_pos())
        for a in range(n):
            for k in range(7):
                cp = pltpu.make_async_remote_copy(
                    src_ref=l_refs[a].at[me], dst_ref=l_refs[a].at[_slot(_peer(k))],
                    send_sem=ss.at[7 * a + k], recv_sem=rs.at[7 * a + k], device_id=_peer(k), device_id_type=MESH)
                cp.wait_send()
                cp.wait_recv()

    out = pl.pallas_call(
        body, name=name,
        out_shape=tuple(pltpu.HBM(a.shape, a.dtype) for a in lands),
        in_specs=[HBM] * n + [SEM, SEM, ANY], out_specs=tuple([HBM] * n),
        input_output_aliases={a: a for a in range(n)},
        compiler_params=pltpu.CompilerParams(has_side_effects=EFFECT),
    )(*lands, send_sems, recv_sems, after)
    return list(out)


def _rs_start(grads, name):
    n = len(grads)

    def body(*refs):
        g_refs, z_refs, send_sems, recv_sems = refs[:n], refs[n:2 * n], refs[2 * n], refs[2 * n + 1]
        token = refs[4 * n + 2]
        for a in range(n):
            for k in range(7):
                pltpu.make_async_remote_copy(
                    src_ref=g_refs[a].at[_slot(_peer(k))], dst_ref=z_refs[a].at[k],
                    send_sem=send_sems.at[7 * a + k], recv_sem=recv_sems.at[7 * a + k],
                    device_id=_peer(k), device_id_type=MESH).start()
        token[...] = jnp.zeros_like(token)

    zones = [lax.empty((7,) + g.shape[1:], g.dtype) for g in grads]
    out = pl.pallas_call(
        body, name=name,
        out_shape=(pltpu.SemaphoreType.DMA((7 * n,)), pltpu.SemaphoreType.DMA((7 * n,)),
                   *[pltpu.HBM(a.shape, a.dtype) for a in grads], *[pltpu.HBM(z.shape, z.dtype) for z in zones],
                   jax.ShapeDtypeStruct((8, 128), F32)),
        in_specs=[HBM] * (2 * n),
        out_specs=(SEM, SEM, *[HBM] * (2 * n), pl.BlockSpec(memory_space=pltpu.VMEM)),
        input_output_aliases={a: 2 + a for a in range(2 * n)},
        compiler_params=pltpu.CompilerParams(has_side_effects=EFFECT),
    )(*[_in_hbm(a) for a in grads], *[_in_hbm(z) for z in zones])
    return out[0], out[1], list(out[2:2 + n]), list(out[2 + n:2 + 2 * n]), out[2 + 2 * n]


def _rs_wait(grads, zones, send_sems, recv_sems, after, name):
    n = len(grads)

    def body(*refs):
        g_refs, z_refs, ss, rs = refs[:n], refs[n:2 * n], refs[2 * n], refs[2 * n + 1]
        for a in range(n):
            for k in range(7):
                cp = pltpu.make_async_remote_copy(
                    src_ref=g_refs[a].at[_slot(_peer(k))], dst_ref=z_refs[a].at[k],
                    send_sem=ss.at[7 * a + k], recv_sem=rs.at[7 * a + k], device_id=_peer(k), device_id_type=MESH)
                cp.wait_send()
                cp.wait_recv()

    out = pl.pallas_call(
        body, name=name,
        out_shape=tuple(pltpu.HBM(a.shape, a.dtype) for a in list(grads) + list(zones)),
        in_specs=[HBM] * (2 * n) + [SEM, SEM, ANY], out_specs=tuple([HBM] * (2 * n)),
        input_output_aliases={a: a for a in range(2 * n)},
        compiler_params=pltpu.CompilerParams(has_side_effects=EFFECT),
    )(*grads, *zones, send_sems, recv_sems, after)
    return list(out[:n]), list(out[n:])


def _cast_layer(w, l, me_idx, name):
    _, r, c = w.shape
    tr = _pick(r, [512, 256, 128, 64, 32, 16])

    def body(me_ref, w_ref, o_ref):
        o_ref[...] = w_ref[...].astype(BF16)

    return pl.pallas_call(
        body, name=name,
        grid_spec=pltpu.PrefetchScalarGridSpec(
            num_scalar_prefetch=1, grid=(r // tr,),
            in_specs=[pl.BlockSpec((None, tr, c), lambda i, me_ref: (l, i, 0))],
            out_specs=pl.BlockSpec((None, tr, c), lambda i, me_ref: (me_ref[0], i, 0))),
        out_shape=jax.ShapeDtypeStruct((N_DEV, r, c), BF16), compiler_params=_cp(1),
    )(me_idx, w)


def _adam_math(w, g, m, v):
    m2 = ADAM_B1 * m + (1.0 - ADAM_B1) * g
    v2 = ADAM_B2 * v + (1.0 - ADAM_B2) * (g * g)
    m_hat = m2 / (1.0 - ADAM_B1 ** ADAM_STEP)
    v_hat = v2 / (1.0 - ADAM_B2 ** ADAM_STEP)
    delta = -ADAM_LR * (m_hat / (jnp.sqrt(v_hat) + ADAM_EPS) + ADAM_WD * w)
    return delta, m2, v2


def _adamw_reduced(own, zone, me_idx, w, m, v, l, name):
    _, r, c = own.shape
    tr = _pick(r, [256, 128, 64, 32, 16])

    def body(me_ref, p_ref, z_ref, w_ref, m_ref, v_ref, g_out, d_out, m_out, v_out):
        g = p_ref[...].astype(F32)
        for k in range(7):
            g = g + z_ref[k].astype(F32)
        d, m2, v2 = _adam_math(w_ref[...], g, m_ref[...], v_ref[...])
        g_out[...] = g
        d_out[...] = d
        m_out[...] = m2
        v_out[...] = v2

    wspec = pl.BlockSpec((None, tr, c), lambda i, me_ref: (l, i, 0))
    ospec = pl.BlockSpec((tr, c), lambda i, me_ref: (i, 0))
    return pl.pallas_call(
        body, name=name,
        grid_spec=pltpu.PrefetchScalarGridSpec(
            num_scalar_prefetch=1, grid=(r // tr,),
            in_specs=[pl.BlockSpec((None, tr, c), lambda i, me_ref: (me_ref[0], i, 0)),
                      pl.BlockSpec((7, tr, c), lambda i, me_ref: (0, i, 0)), wspec, wspec, wspec],
            out_specs=[ospec] * 4),
        out_shape=[jax.ShapeDtypeStruct((r, c), F32)] * 4, compiler_params=_cp(1),
    )(me_idx, own, zone, w, m, v)


def _adamw_plain(w, g, m, v, name):
    def body(w_ref, g_ref, m_ref, v_ref, d_out, m_out, v_out):
        d, m2, v2 = _adam_math(w_ref[...], g_ref[...], m_ref[...], v_ref[...])
        d_out[...] = d
        m_out[...] = m2
        v_out[...] = v2

    return pl.pallas_call(
        body, name=name, out_shape=[jax.ShapeDtypeStruct(w.shape, F32)] * 3,
        compiler_params=pltpu.CompilerParams(vmem_limit_bytes=VMEM_ELEMENTWISE),
    )(w, g, m, v)


def _adamw_tiled(w, g, m, v, name):
    lyr, r, c = w.shape
    tr = _pick(r, [256, 128, 64, 32, 16, 8])

    def body(w_ref, g_ref, m_ref, v_ref, d_out, m_out, v_out):
        d, m2, v2 = _adam_math(w_ref[...], g_ref[...], m_ref[...], v_ref[...])
        d_out[...] = d
        m_out[...] = m2
        v_out[...] = v2

    spec = pl.BlockSpec((None, tr, c), lambda l, i: (l, i, 0))
    return pl.pallas_call(
        body, name=name, grid=(lyr, r // tr), in_specs=[spec] * 4, out_specs=[spec] * 3,
        out_shape=[jax.ShapeDtypeStruct(w.shape, F32)] * 3, compiler_params=_cp(2),
    )(w, g, m, v)


def _sum8(gathered, rows, name):
    def body(g_ref, o_ref):
        acc = g_ref[0:rows, :]
        for d in range(1, N_DEV):
            acc = acc + g_ref[d * rows:(d + 1) * rows, :]
        o_ref[...] = acc

    return pl.pallas_call(
        body, name=name, out_shape=jax.ShapeDtypeStruct((rows, 128), F32),
        compiler_params=pltpu.CompilerParams(vmem_limit_bytes=VMEM_ELEMENTWISE),
    )(gathered)


MXU_COLS = 256


def _pairable(nb, nc):
    return nb % 2 == 0 and nc % MXU_COLS == MXU_COLS // 2 and nc > MXU_COLS // 2


def _mm_nn(a, b3, *, tm, tn, out_dtype, name):
    M, K = a.shape
    nb, _, nc = b3.shape
    q = nc // tn

    if _pairable(nb, nc) and tn == nc:
        cut = nc - 128

        def pair_body(a_ref, b_ref, o_ref):
            av = a_ref[...]
            mid = jnp.concatenate([b_ref[0, :, cut:nc], b_ref[1, :, 0:128]], axis=1)
            o_ref[:, 0:cut] = jnp.dot(av, b_ref[0, :, 0:cut], preferred_element_type=F32).astype(o_ref.dtype)
            o_ref[:, cut:nc + 128] = jnp.dot(av, mid, preferred_element_type=F32).astype(o_ref.dtype)
            o_ref[:, nc + 128:2 * nc] = jnp.dot(av, b_ref[1, :, 128:nc], preferred_element_type=F32).astype(o_ref.dtype)

        return pl.pallas_call(
            pair_body, name=name, grid=(nb // 2, M // tm),
            in_specs=[pl.BlockSpec((tm, K), lambda j, i: (i, 0)),
                      pl.BlockSpec((2, K, nc), lambda j, i: (j, 0, 0))],
            out_specs=pl.BlockSpec((tm, 2 * nc), lambda j, i: (i, j)),
            out_shape=jax.ShapeDtypeStruct((M, nb * nc), out_dtype), compiler_params=_cp(2, VMEM_MATMUL),
        )(a, b3)

    def body(a_ref, b_ref, o_ref):
        o_ref[...] = jnp.dot(a_ref[...], b_ref[...], preferred_element_type=F32).astype(o_ref.dtype)

    return pl.pallas_call(
        body, name=name, grid=(nb * q, M // tm),
        in_specs=[pl.BlockSpec((tm, K), lambda j, i: (i, 0)),
                  pl.BlockSpec((None, K, tn), lambda j, i: (j // q, 0, j % q))],
        out_specs=pl.BlockSpec((tm, tn), lambda j, i: (i, j)),
        out_shape=jax.ShapeDtypeStruct((M, nb * nc), out_dtype), compiler_params=_cp(2, VMEM_MATMUL),
    )(a, b3)


def _mm_nn_resid(a, b2, x_old, mod, gate_row, n_ctx, *, tm, tn, name):
    M, K = a.shape
    N = b2.shape[1]

    def body(a_ref, b_ref, x_ref, mod_ref, y_ref, xn_ref):
        y = jnp.dot(a_ref[...], b_ref[...], preferred_element_type=F32)
        is_ctx = _rows(pl.program_id(1), tm) < n_ctx
        g = jnp.where(is_ctx, mod_ref[0, gate_row:gate_row + 1, :], mod_ref[1, gate_row:gate_row + 1, :])
        y_ref[...] = y.astype(BF16)
        xn_ref[...] = x_ref[...] + g * y

    return pl.pallas_call(
        body, name=name, grid=(N // tn, M // tm),
        in_specs=[pl.BlockSpec((tm, K), lambda j, i: (i, 0)),
                  pl.BlockSpec((K, tn), lambda j, i: (0, j)),
                  pl.BlockSpec((tm, tn), lambda j, i: (i, j)),
                  pl.BlockSpec((2, 6, tn), lambda j, i: (0, 0, j))],
        out_specs=[pl.BlockSpec((tm, tn), lambda j, i: (i, j))] * 2,
        out_shape=[jax.ShapeDtypeStruct((M, N), BF16), jax.ShapeDtypeStruct((M, N), F32)],
        compiler_params=_cp(2, VMEM_MATMUL),
    )(a, b2, x_old, mod)


def _mm_nn_resid_norm(a, b2, x_old, mod, gate_row, norm_row0, n_ctx, *, tm, name):
    M, K = a.shape
    N = b2.shape[1]

    def body(a_ref, b_ref, x_ref, mod_ref, y_ref, xn_ref, h_ref, ht_ref):
        y = jnp.dot(a_ref[...], b_ref[...], preferred_element_type=F32)
        is_ctx = _rows(pl.program_id(0), tm) < n_ctx

        def row(k):
            return jnp.where(is_ctx, mod_ref[0, k:k + 1, :], mod_ref[1, k:k + 1, :])

        y_ref[...] = y.astype(BF16)
        xn = x_ref[...] + row(gate_row) * y
        xn_ref[...] = xn
        r = lax.rsqrt(jnp.mean(xn * xn, axis=-1, keepdims=True) + EPS)
        hv = xn * r * (1.0 + row(norm_row0 + 1)) + row(norm_row0)
        h_ref[...] = hv.astype(BF16)
        ht_ref[...] = hv.T.astype(BF16)

    rows = pl.BlockSpec((tm, N), lambda i: (i, 0))
    return pl.pallas_call(
        body, name=name, grid=(M // tm,),
        in_specs=[pl.BlockSpec((tm, K), lambda i: (i, 0)), pl.BlockSpec((K, N), lambda i: (0, 0)), rows,
                  pl.BlockSpec((2, 6, N), lambda i: (0, 0, 0))],
        out_specs=[rows, rows, rows, pl.BlockSpec((N, tm), lambda i: (0, i))],
        out_shape=[jax.ShapeDtypeStruct((M, N), BF16), jax.ShapeDtypeStruct((M, N), F32),
                   jax.ShapeDtypeStruct((M, N), BF16), jax.ShapeDtypeStruct((N, M), BF16)],
        compiler_params=_cp(1, VMEM_MATMUL),
    )(a, b2, x_old, mod)


def _norm_bwd_math(dh_v, xv, mod_ref, res, is_ctx, row0):
    D = xv.shape[1]
    r = lax.rsqrt(jnp.mean(xv * xv, axis=-1, keepdims=True) + EPS)
    xhat = xv * r
    sc = jnp.where(is_ctx, mod_ref[0, row0 + 1:row0 + 2, :], mod_ref[1, row0 + 1:row0 + 2, :])
    dxhat = dh_v * (1.0 + sc)
    dx = res + r * (dxhat - xhat * jnp.mean(dxhat * xhat, axis=-1, keepdims=True))
    dsc = dh_v * xhat
    zero = jnp.zeros_like(dh_v)
    sums = [jnp.sum(jnp.where(is_ctx, dh_v, zero), axis=0, keepdims=True),
            jnp.sum(jnp.where(is_ctx, dsc, zero), axis=0, keepdims=True),
            jnp.sum(jnp.where(is_ctx, zero, dh_v), axis=0, keepdims=True),
            jnp.sum(jnp.where(is_ctx, zero, dsc), axis=0, keepdims=True)]
    rid = lax.broadcasted_iota(I32, (8, D), 0)
    upd = jnp.zeros((8, D), F32)
    for k, s in enumerate(sums):
        upd = upd + jnp.where(rid == k, s, 0.0)
    return dx, upd


def _mm_nt_acc(parts, w3, *, tm, name):
    n_parts = len(parts)
    M = parts[0].shape[0]
    nb, K, nc = w3.shape
    pair = _pairable(nb // n_parts, nc)
    g = 2 if pair else 1
    steps, per = nb // g, nb // n_parts // g
    cut = nc - 128

    def nt(d, w):
        return lax.dot_general(d, w, NT_DIMS, preferred_element_type=F32)

    def contribution(d_ref, w_ref):
        if not pair:
            return nt(d_ref[...], w_ref[0])
        mid = jnp.concatenate([w_ref[0, :, cut:nc], w_ref[1, :, 0:128]], axis=1)
        return (nt(d_ref[:, 0:cut], w_ref[0, :, 0:cut]) + nt(d_ref[:, cut:nc + 128], mid)
                + nt(d_ref[:, nc + 128:2 * nc], w_ref[1, :, 128:nc]))

    def body(*refs):
        dy_refs, w_ref = refs[:n_parts], refs[n_parts]
        o_ref, acc_ref = refs[n_parts + 1], refs[n_parts + 2]
        s = pl.program_id(1)

        @pl.when(s == 0)
        def _():
            acc_ref[...] = jnp.zeros_like(acc_ref)

        for p in range(n_parts):
            @pl.when(s // per == p)
            def _(p=p):
                acc_ref[...] += contribution(dy_refs[p], w_ref)

        @pl.when(s == steps - 1)
        def _():
            o_ref[...] = acc_ref[...].astype(o_ref.dtype)

    def part_spec(p):
        return pl.BlockSpec((tm, g * nc), lambda i, s: (i, jnp.clip(s - p * per, 0, per - 1)))

    return pl.pallas_call(
        body, name=name, grid=(M // tm, steps),
        in_specs=[part_spec(p) for p in range(n_parts)] + [pl.BlockSpec((g, K, nc), lambda i, s: (s, 0, 0))],
        out_specs=pl.BlockSpec((tm, K), lambda i, s: (i, 0)),
        out_shape=jax.ShapeDtypeStruct((M, K), BF16),
        scratch_shapes=[pltpu.VMEM((tm, K), F32)], compiler_params=_cp(2, VMEM_MATMUL),
    )(*parts, w3)


def _norm_mod_bwd(dh, x, mod, dx_res, row0, n_ctx, name):
    S, D = x.shape
    tm = _pick(S, [256, 128])

    def body(dh_ref, x_ref, mod_ref, res_ref, dx_ref, acc_ref):
        i = pl.program_id(0)

        @pl.when(i == 0)
        def _():
            acc_ref[...] = jnp.zeros_like(acc_ref)

        dx, upd = _norm_bwd_math(dh_ref[...].astype(F32), x_ref[...], mod_ref, res_ref[...], _rows(i, tm) < n_ctx, row0)
        dx_ref[...] = dx
        acc_ref[...] += upd

    rows = pl.BlockSpec((tm, D), lambda i: (i, 0))
    return pl.pallas_call(
        body, name=name, grid=(S // tm,),
        in_specs=[rows, rows, pl.BlockSpec((2, 6, D), lambda i: (0, 0, 0)), rows],
        out_specs=[rows, pl.BlockSpec((8, D), lambda i: (0, 0))],
        out_shape=[jax.ShapeDtypeStruct((S, D), F32), jax.ShapeDtypeStruct((8, D), F32)],
        compiler_params=_cp(1),
    )(dh, x, mod, dx_res)


def _gate_dgrad(dx, y, mod, gate_row, n_ctx, w2, *, tm, tn, name):
    M, N = dx.shape
    K = w2.shape[0]

    def body(dx_ref, y_ref, mod_ref, w_ref, da_ref, dy_ref, acc_ref):
        i = pl.program_id(0)
        j = pl.program_id(1)

        @pl.when((i == 0) & (j == 0))
        def _():
            acc_ref[...] = jnp.zeros_like(acc_ref)

        @pl.when(j == 0)
        def _():
            dxv = dx_ref[...]
            is_ctx = _rows(i, tm) < n_ctx
            g = jnp.where(is_ctx, mod_ref[0, gate_row:gate_row + 1, :], mod_ref[1, gate_row:gate_row + 1, :])
            dy_ref[...] = (g * dxv).astype(BF16)
            prod = dxv * y_ref[...].astype(F32)
            zero = jnp.zeros_like(prod)
            s_ctx = jnp.sum(jnp.where(is_ctx, prod, zero), axis=0, keepdims=True)
            s_lat = jnp.sum(jnp.where(is_ctx, zero, prod), axis=0, keepdims=True)
            rid = lax.broadcasted_iota(I32, (8, N), 0)
            acc_ref[...] += jnp.where(rid == 0, s_ctx, 0.0) + jnp.where(rid == 1, s_lat, 0.0)

        da_ref[...] = lax.dot_general(dy_ref[...], w_ref[...], NT_DIMS, preferred_element_type=F32).astype(BF16)

    rows = pl.BlockSpec((tm, N), lambda i, j: (i, 0))
    return pl.pallas_call(
        body, name=name, grid=(M // tm, K // tn),
        in_specs=[rows, rows, pl.BlockSpec((2, 6, N), lambda i, j: (0, 0, 0)),
                  pl.BlockSpec((tn, N), lambda i, j: (j, 0))],
        out_specs=[pl.BlockSpec((tm, tn), lambda i, j: (i, j)), rows, pl.BlockSpec((8, N), lambda i, j: (0, 0))],
        out_shape=[jax.ShapeDtypeStruct((M, K), BF16), jax.ShapeDtypeStruct((M, N), BF16),
                   jax.ShapeDtypeStruct((8, N), F32)],
        compiler_params=_cp(2, VMEM_MATMUL),
    )(dx, y, mod, w2)


def _mm_tn(a, parts, *, nb, tka, tn, ts, name, a_transposed=False):
    n_parts = len(parts)
    S, Ka = a.shape[::-1] if a_transposed else a.shape
    N = sum(p.shape[1] for p in parts)
    nc = N // nb
    q = nc // tn
    nk = S // ts
    g = 2 if (tn == nc and n_parts == 1 and _pairable(nb, nc)) else 1
    per = nb * q // n_parts // g

    def body(a_ref, *rest):
        dy_refs, o_ref, acc_ref = rest[:n_parts], rest[n_parts], rest[n_parts + 1]
        j = pl.program_id(0)
        k = pl.program_id(2)

        @pl.when(k == 0)
        def _():
            acc_ref[...] = jnp.zeros_like(acc_ref)

        for p in range(n_parts):
            @pl.when(j // per == p)
            def _(p=p):
                if a_transposed:
                    acc_ref[...] += jnp.dot(a_ref[...], dy_refs[p][...], preferred_element_type=F32)
                else:
                    acc_ref[...] += lax.dot_general(a_ref[...], dy_refs[p][...], TN_DIMS, preferred_element_type=F32)

        @pl.when(k == nk - 1)
        def _():
            for t in range(g):
                o_ref[t] = acc_ref[:, t * tn:(t + 1) * tn].astype(o_ref.dtype)

    def part_spec(p):
        return pl.BlockSpec((ts, g * tn), lambda j, ia, k: (jnp.where(j // per == p, k, 0),
                                                            jnp.clip(j - p * per, 0, per - 1)))

    return pl.pallas_call(
        body, name=name, grid=(nb * q // g, Ka // tka, nk),
        in_specs=[pl.BlockSpec((tka, ts), lambda j, ia, k: (ia, k)) if a_transposed
                  else pl.BlockSpec((ts, tka), lambda j, ia, k: (k, ia))] + [part_spec(p) for p in range(n_parts)],
        out_specs=pl.BlockSpec((g, tka, tn), lambda j, ia, k: (j // q, ia, j % q)),
        out_shape=jax.ShapeDtypeStruct((nb, Ka, nc), BF16),
        scratch_shapes=[pltpu.VMEM((tka, g * tn), F32)], compiler_params=_cp(3, VMEM_MATMUL),
    )(a, *parts)


def _norm_mod(x, mod, row0, n_ctx, name):
    S, D = x.shape
    tm = _pick(S, [256, 128])

    def body(x_ref, mod_ref, h_ref, ht_ref):
        xv = x_ref[...]
        r = lax.rsqrt(jnp.mean(xv * xv, axis=-1, keepdims=True) + EPS)
        is_ctx = _rows(pl.program_id(0), tm) < n_ctx
        sh = jnp.where(is_ctx, mod_ref[0, row0:row0 + 1, :], mod_ref[1, row0:row0 + 1, :])
        sc = jnp.where(is_ctx, mod_ref[0, row0 + 1:row0 + 2, :], mod_ref[1, row0 + 1:row0 + 2, :])
        hv = xv * r * (1.0 + sc) + sh
        h_ref[...] = hv.astype(BF16)
        ht_ref[...] = hv.T.astype(BF16)

    return pl.pallas_call(
        body, name=name, grid=(S // tm,),
        in_specs=[pl.BlockSpec((tm, D), lambda i: (i, 0)), pl.BlockSpec((2, 6, D), lambda i: (0, 0, 0))],
        out_specs=[pl.BlockSpec((tm, D), lambda i: (i, 0)), pl.BlockSpec((D, tm), lambda i: (0, i))],
        out_shape=[jax.ShapeDtypeStruct((S, D), BF16), jax.ShapeDtypeStruct((D, S), BF16)], compiler_params=_cp(1),
    )(x, mod)


def _loss_grad(x, target, n_ctx, name):
    S, D = x.shape
    tm = _pick(n_ctx, [256, 128])
    nct = n_ctx // tm

    def body(x_ref, t_ref, dx_ref, tot_ref, acc_ref):
        i = pl.program_id(0)

        @pl.when(i == 0)
        def _():
            acc_ref[...] = jnp.zeros_like(acc_ref)

        @pl.when(i < nct)
        def _():
            dx_ref[...] = jnp.zeros_like(dx_ref)

        @pl.when(i >= nct)
        def _():
            err = x_ref[...] - t_ref[...]
            dx_ref[...] = err * (1.0 / D)
            acc_ref[...] += jnp.sum(err * err, axis=0, keepdims=True)

        @pl.when(i == S // tm - 1)
        def _():
            tot = jnp.sum(acc_ref[...], axis=1, keepdims=True) * (0.5 / D)
            tot_ref[...] = jnp.broadcast_to(tot, tot_ref.shape)

    return pl.pallas_call(
        body, name=name, grid=(S // tm,),
        in_specs=[pl.BlockSpec((tm, D), lambda i: (i, 0)),
                  pl.BlockSpec((tm, D), lambda i: (jnp.maximum(i - nct, 0), 0))],
        out_specs=[pl.BlockSpec((tm, D), lambda i: (i, 0)), pl.BlockSpec((1, 128), lambda i: (0, 0))],
        out_shape=[jax.ShapeDtypeStruct((S, D), F32), jax.ShapeDtypeStruct((1, 128), F32)],
        scratch_shapes=[pltpu.VMEM((1, D), F32)], compiler_params=_cp(1),
    )(x, target)


def _halo_specs(tm, tc, S, col_off):
    per = tm // HALO
    last = S // HALO - 1
    return [pl.BlockSpec((HALO, tc), lambda j, i: (jnp.maximum(i * per - 1, 0), j + col_off)),
            pl.BlockSpec((tm, tc), lambda j, i: (i, j + col_off)),
            pl.BlockSpec((HALO, tc), lambda j, i: (jnp.minimum((i + 1) * per, last), j + col_off))]


def _ext(p_ref, m_ref, n_ref):
    return jnp.concatenate([p_ref[...], m_ref[...], n_ref[...]], axis=0).astype(F32)


def _links(i, tm, S, n_ctx):
    n = tm + 2 * HALO
    rid = i * tm - HALO + lax.broadcasted_iota(I32, (n, 1), 0)
    has_prev = (rid != 0) & (rid != n_ctx)
    has_next = (rid != n_ctx - 1) & (rid != S - 1)
    return has_prev, has_next


def _up(x):
    return pltpu.roll(x, 1, 0)


def _dn(x):
    return pltpu.roll(x, x.shape[0] - 1, 0)


def _conv3(x, w_ref, has_prev, has_next):
    return (w_ref[0:1, :] * jnp.where(has_prev, _up(x), 0.0) + w_ref[1:2, :] * x
            + w_ref[2:3, :] * jnp.where(has_next, _dn(x), 0.0))


def _conv3_t(d, w_ref, has_prev, has_next):
    return (w_ref[0:1, :] * jnp.where(has_next, _dn(d), 0.0) + w_ref[1:2, :] * d
            + w_ref[2:3, :] * jnp.where(has_prev, _up(d), 0.0))


def _conv3_wgrad(d, x, has_prev, has_next, extra=None):
    c = slice(HALO, d.shape[0] - HALO)
    taps = [jnp.where(has_prev, _up(x), 0.0), x, jnp.where(has_next, _dn(x), 0.0)]
    sums = [jnp.sum((d * t)[c], axis=0, keepdims=True) for t in taps]
    if extra is not None:
        sums.append(jnp.sum(extra[c], axis=0, keepdims=True))
    rid = lax.broadcasted_iota(I32, (8, d.shape[1]), 0)
    upd = jnp.zeros((8, d.shape[1]), F32)
    for k, s in enumerate(sums):
        upd = upd + jnp.where(rid == k, s, 0.0)
    return upd


def _sigmoid(x):
    return 1.0 / (1.0 + jnp.exp(-x))


def _ffn_act(u, conv_w, conv_b, l, n_ctx, name):
    S, F2 = u.shape
    F = F2 // 2
    tm = _pick(S, [384, 256, 128])
    tc = _pick(F, [1408, 512, 256, 128])
    nj = F // tc

    def body(gp, gm, gn, v_ref, w_ref, b_ref, a_ref):
        has_prev, has_next = _links(pl.program_id(1), tm, S, n_ctx)
        gc = _conv3(_ext(gp, gm, gn), w_ref, has_prev, has_next)[HALO:HALO + tm] + b_ref[...]
        a_ref[...] = (gc * _sigmoid(gc) * v_ref[...].astype(F32)).astype(BF16)

    return pl.pallas_call(
        body, name=name, grid=(nj, S // tm),
        in_specs=_halo_specs(tm, tc, S, 0) + [
            pl.BlockSpec((tm, tc), lambda j, i: (i, j + nj)),
            pl.BlockSpec((None, 3, tc), lambda j, i: (l, 0, j)),
            pl.BlockSpec((None, 1, tc), lambda j, i: (l, 0, j))],
        out_specs=pl.BlockSpec((tm, tc), lambda j, i: (i, j)),
        out_shape=jax.ShapeDtypeStruct((S, F), BF16), compiler_params=_cp(2),
    )(u, u, u, u, conv_w, conv_b)


def _ffn_act_bwd(u, da, conv_w, conv_b, l, n_ctx, name):
    S, F2 = u.shape
    F = F2 // 2
    tm = _pick(S, [384, 256, 128])
    tc = _pick(F, [1408, 512, 256, 128])
    nj = F // tc

    def body(gp, gm, gn, vp, vm, vn, dp, dm, dn_, w_ref, b_ref, dg_ref, dv_ref, acc_ref):
        i = pl.program_id(1)

        @pl.when(i == 0)
        def _():
            acc_ref[...] = jnp.zeros_like(acc_ref)

        has_prev, has_next = _links(i, tm, S, n_ctx)
        g = _ext(gp, gm, gn)
        val = _ext(vp, vm, vn)
        d_a = _ext(dp, dm, dn_)
        gc = _conv3(g, w_ref, has_prev, has_next) + b_ref[...]
        sg = _sigmoid(gc)
        dgc = d_a * val * (sg * (1.0 + gc * (1.0 - sg)))
        c = slice(HALO, HALO + tm)
        dv_ref[...] = (d_a * gc * sg)[c].astype(BF16)
        dg_ref[...] = _conv3_t(dgc, w_ref, has_prev, has_next)[c].astype(BF16)
        acc_ref[...] += _conv3_wgrad(dgc, g, has_prev, has_next, extra=dgc)

    return pl.pallas_call(
        body, name=name, grid=(nj, S // tm),
        in_specs=_halo_specs(tm, tc, S, 0) + _halo_specs(tm, tc, S, nj) + _halo_specs(tm, tc, S, 0) + [
            pl.BlockSpec((None, 3, tc), lambda j, i: (l, 0, j)),
            pl.BlockSpec((None, 1, tc), lambda j, i: (l, 0, j))],
        out_specs=[pl.BlockSpec((tm, tc), lambda j, i: (i, j))] * 2 + [pl.BlockSpec((8, tc), lambda j, i: (0, j))],
        out_shape=[jax.ShapeDtypeStruct((S, F), BF16)] * 2 + [jax.ShapeDtypeStruct((8, F), F32)],
        compiler_params=_cp(2),
    )(u, u, u, u, u, u, da, da, da, conv_w, conv_b)


def _sc_act(u, conv_w, l, n_ctx, name):
    S, D3 = u.shape
    D = D3 // 3
    tm = _pick(S, [384, 256, 128])
    tc = _pick(D, [1024, 512, 256, 128])
    nj = D // tc

    def body(b_ref, cp, cm, cn, vp, vm, vn, w_ref, z_ref):
        has_prev, has_next = _links(pl.program_id(1), tm, S, n_ctx)
        t = _ext(cp, cm, cn) * _ext(vp, vm, vn)
        cv = _conv3(t, w_ref, has_prev, has_next)[HALO:HALO + tm]
        z_ref[...] = (b_ref[...].astype(F32) * cv).astype(BF16)

    return pl.pallas_call(
        body, name=name, grid=(nj, S // tm),
        in_specs=[pl.BlockSpec((tm, tc), lambda j, i: (i, j))] + _halo_specs(tm, tc, S, nj)
        + _halo_specs(tm, tc, S, 2 * nj) + [pl.BlockSpec((None, 3, tc), lambda j, i: (l, 0, j))],
        out_specs=pl.BlockSpec((tm, tc), lambda j, i: (i, j)),
        out_shape=jax.ShapeDtypeStruct((S, D), BF16), compiler_params=_cp(2),
    )(u, u, u, u, u, u, u, conv_w)


def _sc_act_bwd(u, dz, conv_w, l, n_ctx, name):
    S, D3 = u.shape
    D = D3 // 3
    tm = _pick(S, [128])
    tc = D
    nj = 1

    def body(bp, bm, bn, cp, cm, cn, vp, vm, vn, zp, zm, zn, w_ref, du_ref, acc_ref):
        db_ref, dc_ref, dv_ref = du_ref.at[:, 0:D], du_ref.at[:, D:2 * D], du_ref.at[:, 2 * D:3 * D]
        i = pl.program_id(1)

        @pl.when(i == 0)
        def _():
            acc_ref[...] = jnp.zeros_like(acc_ref)

        has_prev, has_next = _links(i, tm, S, n_ctx)
        gb = _ext(bp, bm, bn)
        gcv = _ext(cp, cm, cn)
        val = _ext(vp, vm, vn)
        d_z = _ext(zp, zm, zn)
        t = gcv * val
        c = slice(HALO, HALO + tm)
        db_ref[...] = (d_z * _conv3(t, w_ref, has_prev, has_next))[c].astype(BF16)
        dcv = d_z * gb
        dt = _conv3_t(dcv, w_ref, has_prev, has_next)
        dc_ref[...] = (dt * val)[c].astype(BF16)
        dv_ref[...] = (dt * gcv)[c].astype(BF16)
        acc_ref[...] += _conv3_wgrad(dcv, t, has_prev, has_next)

    return pl.pallas_call(
        body, name=name, grid=(nj, S // tm),
        in_specs=_halo_specs(tm, tc, S, 0) + _halo_specs(tm, tc, S, nj) + _halo_specs(tm, tc, S, 2 * nj)
        + _halo_specs(tm, tc, S, 0) + [pl.BlockSpec((None, 3, tc), lambda j, i: (l, 0, j))],
        out_specs=[pl.BlockSpec((tm, D3), lambda j, i: (i, 0)), pl.BlockSpec((8, tc), lambda j, i: (0, j))],
        out_shape=[jax.ShapeDtypeStruct((S, D3), BF16), jax.ShapeDtypeStruct((8, D), F32)],
        compiler_params=_cp(2),
    )(u, u, u, u, u, u, u, u, u, dz, dz, dz, conv_w)


def _rope_tables(T, n_ctx):
    rows = T // GRID_W
    pairs = HEAD_DIM // 4
    row = jnp.repeat(jnp.arange(rows), GRID_W).astype(F32)
    col = jnp.tile(jnp.arange(GRID_W), rows).astype(F32)
    inv = ROPE_BASE ** (-jnp.arange(pairs, dtype=F32) / pairs)
    ang = jnp.concatenate([row[:, None] * inv, row[:, None] * inv, col[:, None] * inv, col[:, None] * inv], axis=1)
    cos, sin = jnp.cos(ang), jnp.sin(ang)
    first = (jnp.arange(HEAD_DIM) % (2 * pairs)) < pairs
    sin_a = jnp.where(first, -sin, 0.0)
    sin_b = jnp.where(first, 0.0, sin)
    pad = jnp.zeros((n_ctx, HEAD_DIM), F32)
    return (jnp.concatenate([pad + 1.0, cos], axis=0), jnp.concatenate([pad, sin_a], axis=0),
            jnp.concatenate([pad, sin_b], axis=0))


def _qk_prep(qkv, tabs, gains, l, n_q, n_kv, name):
    S = qkv.shape[0]
    W = qkv.shape[1]
    tm = _pick(S, [256, 128])
    cos, sin_a, sin_b = tabs

    def body(x_ref, cos_ref, sa_ref, sb_ref, g_ref, q_ref, k_ref):
        cs, sa, sb = cos_ref[...], sa_ref[...], sb_ref[...]
        for h in range(n_q + n_kv):
            xv = x_ref[:, h * 128:(h + 1) * 128].astype(F32)
            r = lax.rsqrt(jnp.mean(xv * xv, axis=-1, keepdims=True) + EPS)
            gain = g_ref[0:1, :] if h < n_q else g_ref[1:2, :]
            y = xv * r * gain
            out = (y * cs + pltpu.roll(y, 96, 1) * sa + pltpu.roll(y, 32, 1) * sb).astype(BF16)
            if h < n_q:
                q_ref[:, h * 128:(h + 1) * 128] = out
            else:
                k_ref[:, (h - n_q) * 128:(h - n_q + 1) * 128] = out

    tspec = pl.BlockSpec((tm, 128), lambda i: (i, 0))
    return pl.pallas_call(
        body, name=name, grid=(S // tm,),
        in_specs=[pl.BlockSpec((tm, W), lambda i: (i, 0)), tspec, tspec, tspec,
                  pl.BlockSpec((None, 2, 128), lambda i: (l, 0, 0))],
        out_specs=[pl.BlockSpec((tm, n_q * 128), lambda i: (i, 0)), pl.BlockSpec((tm, n_kv * 128), lambda i: (i, 0))],
        out_shape=[jax.ShapeDtypeStruct((S, n_q * 128), BF16), jax.ShapeDtypeStruct((S, n_kv * 128), BF16)],
        compiler_params=_cp(1),
    )(qkv, cos, sin_a, sin_b, gains)


def _qk_prep_bwd(dq, dk, dv, qkv, tabs, gains, l, n_q, n_kv, name):
    S, W = qkv.shape
    tm = _pick(S, [256, 128])
    cos, sin_a, sin_b = tabs

    def body(dq_ref, dk_ref, dv_ref, x_ref, cos_ref, sa_ref, sb_ref, g_ref, o_ref, acc_ref):
        @pl.when(pl.program_id(0) == 0)
        def _():
            acc_ref[...] = jnp.zeros_like(acc_ref)

        cs, sa, sb = cos_ref[...], sa_ref[...], sb_ref[...]
        dgq = jnp.zeros((1, 128), F32)
        dgk = jnp.zeros((1, 128), F32)
        for h in range(n_q + n_kv):
            if h < n_q:
                d_out = dq_ref[:, h * 128:(h + 1) * 128]
                gain = g_ref[0:1, :]
            else:
                d_out = dk_ref[:, (h - n_q) * 128:(h - n_q + 1) * 128]
                gain = g_ref[1:2, :]
            dy = d_out * cs + pltpu.roll(d_out * sa, 32, 1) + pltpu.roll(d_out * sb, 96, 1)
            xv = x_ref[:, h * 128:(h + 1) * 128].astype(F32)
            r = lax.rsqrt(jnp.mean(xv * xv, axis=-1, keepdims=True) + EPS)
            xhat = xv * r
            dg = jnp.sum(dy * xhat, axis=0, keepdims=True)
            if h < n_q:
                dgq = dgq + dg
            else:
                dgk = dgk + dg
            dxhat = dy * gain
            dx = r * (dxhat - xhat * jnp.mean(dxhat * xhat, axis=-1, keepdims=True))
            o_ref[:, h * 128:(h + 1) * 128] = dx.astype(BF16)
        v0 = (n_q + n_kv) * 128
        o_ref[:, v0:] = dv_ref[...].astype(BF16)
        rid = lax.broadcasted_iota(I32, (8, 128), 0)
        acc_ref[...] += jnp.where(rid == 0, dgq, 0.0) + jnp.where(rid == 1, dgk, 0.0)

    tspec = pl.BlockSpec((tm, 128), lambda i: (i, 0))
    return pl.pallas_call(
        body, name=name, grid=(S // tm,),
        in_specs=[pl.BlockSpec((tm, n_q * 128), lambda i: (i, 0)), pl.BlockSpec((tm, n_kv * 128), lambda i: (i, 0)),
                  pl.BlockSpec((tm, n_kv * 128), lambda i: (i, 0)), pl.BlockSpec((tm, W), lambda i: (i, 0)),
                  tspec, tspec, tspec, pl.BlockSpec((None, 2, 128), lambda i: (l, 0, 0))],
        out_specs=[pl.BlockSpec((tm, W), lambda i: (i, 0)), pl.BlockSpec((8, 128), lambda i: (0, 0))],
        out_shape=[jax.ShapeDtypeStruct((S, W), BF16), jax.ShapeDtypeStruct((8, 128), F32)],
        compiler_params=_cp(1),
    )(dq, dk, dv, qkv, cos, sin_a, sin_b, gains)


def _band_specs(width, col, nb, n_ctx):
    return [pl.BlockSpec((BLK, width), lambda i: (jnp.maximum(i - 1, 0), col)),
            pl.BlockSpec((BLK, width), lambda i: (i, col)),
            pl.BlockSpec((BLK, width), lambda i: (jnp.minimum(i + 1, nb - 1), col)),
            pl.BlockSpec((n_ctx, width), lambda i: (0, col))]


def _q_side_mask(i, S, n_ctx):
    shape = (GROUP * BLK, 3 * BLK + n_ctx)
    a = lax.broadcasted_iota(I32, shape, 0) & (BLK - 1)
    kk = lax.broadcasted_iota(I32, shape, 1)
    rq = i * BLK + a
    rk = (i - 1) * BLK + kk
    band = (rq >= n_ctx) & (rk >= n_ctx) & (rk < S) & (jnp.abs(rq - rk) <= WINDOW)
    return (kk >= 3 * BLK) | band


def _stack_heads(ref, g):
    return jnp.concatenate([ref[:, (GROUP * g + hh) * 128:(GROUP * g + hh + 1) * 128] for hh in range(GROUP)], axis=0)


def _stack_cols(ref, g):
    return jnp.concatenate([ref[:, GROUP * g + hh:GROUP * g + hh + 1] for hh in range(GROUP)], axis=0)


def _sink_col(sink_ref, l, g):
    return jnp.concatenate([jnp.full((BLK, 1), sink_ref[l, GROUP * g + hh], F32) for hh in range(GROUP)], axis=0)


def _attn_fwd(q, k, qkv, sink, l, n_ctx, name):
    S, DQ = q.shape
    DK = k.shape[1]
    n_kv = DK // 128
    nb = S // BLK
    vcol = (DQ + DK) // DK
    scale = HEAD_DIM ** -0.5

    def body(sink_ref, q_ref, kp, kc, kn, kx, vp, vc, vn, vx, o_ref, lse_ref):
        i = pl.program_id(0)
        mask = _q_side_mask(i, S, n_ctx)
        lane = lax.broadcasted_iota(I32, (BLK, 128), 1)
        lse_tile = jnp.zeros((BLK, 128), F32)
        outs = []
        for g in range(n_kv):
            sl = slice(g * 128, (g + 1) * 128)
            kcat = jnp.concatenate([kp[:, sl], kc[:, sl], kn[:, sl], kx[:, sl]], axis=0)
            vcat = jnp.concatenate([vp[:, sl], vc[:, sl], vn[:, sl], vx[:, sl]], axis=0)
            s = lax.dot_general(_stack_heads(q_ref, g), kcat, NT_DIMS, preferred_element_type=F32) * scale
            s = jnp.where(mask, s, NEG)
            sk = _sink_col(sink_ref, l, g)
            m = jnp.maximum(jnp.max(s, axis=1, keepdims=True), sk)
            e = jnp.exp(s - m)
            den = jnp.sum(e, axis=1, keepdims=True) + jnp.exp(sk - m)
            p = (e / den).astype(BF16)
            o = jnp.dot(p, vcat, preferred_element_type=F32)
            lse = m + jnp.log(den)
            for hh in range(GROUP):
                h = GROUP * g + hh
                outs.append(o[hh * BLK:(hh + 1) * BLK].astype(BF16))
                lse_tile = jnp.where(lane == h, lse[hh * BLK:(hh + 1) * BLK], lse_tile)
        o_ref[...] = jnp.concatenate(outs, axis=1)
        lse_ref[...] = lse_tile

    return pl.pallas_call(
        body, name=name, grid=(nb,),
        in_specs=[pl.BlockSpec(memory_space=pltpu.SMEM), pl.BlockSpec((BLK, DQ), lambda i: (i, 0))]
        + _band_specs(DK, 0, nb, n_ctx) + _band_specs(DK, vcol, nb, n_ctx),
        out_specs=[pl.BlockSpec((BLK, DQ), lambda i: (i, 0)), pl.BlockSpec((BLK, 128), lambda i: (i, 0))],
        out_shape=[jax.ShapeDtypeStruct((S, DQ), BF16), jax.ShapeDtypeStruct((S, 128), F32)],
        compiler_params=_cp(1),
    )(sink, q, k, k, k, k, qkv, qkv, qkv, qkv)


def _attn_bwd_q(q, k, qkv, o, do, lse, sink, l, n_ctx, name):
    S, DQ = q.shape
    DK = k.shape[1]
    n_kv = DK // 128
    nb = S // BLK
    vcol = (DQ + DK) // DK
    scale = HEAD_DIM ** -0.5

    def body(sink_ref, q_ref, kp, kc, kn, kx, vp, vc, vn, vx, o_ref, do_ref, lse_ref,
             dq_ref, delta_ref, dkx_ref, dvx_ref, dsink_ref):
        i = pl.program_id(0)

        @pl.when(i == 0)
        def _():
            dkx_ref[...] = jnp.zeros_like(dkx_ref)
            dvx_ref[...] = jnp.zeros_like(dvx_ref)
            dsink_ref[...] = jnp.zeros_like(dsink_ref)

        mask = _q_side_mask(i, S, n_ctx)
        lane = lax.broadcasted_iota(I32, (BLK, 128), 1)
        lane8 = lax.broadcasted_iota(I32, (8, 128), 1)
        row8 = lax.broadcasted_iota(I32, (8, 128), 0)
        delta_tile = jnp.zeros((BLK, 128), F32)
        dsink_upd = jnp.zeros((8, 128), F32)
        dqs, dkx_upd, dvx_upd = [], [], []
        for g in range(n_kv):
            sl = slice(g * 128, (g + 1) * 128)
            kcat = jnp.concatenate([kp[:, sl], kc[:, sl], kn[:, sl], kx[:, sl]], axis=0)
            vcat = jnp.concatenate([vp[:, sl], vc[:, sl], vn[:, sl], vx[:, sl]], axis=0)
            qg = _stack_heads(q_ref, g)
            dog = _stack_heads(do_ref, g)
            delta = jnp.sum(dog.astype(F32) * _stack_heads(o_ref, g).astype(F32), axis=1, keepdims=True)
            lse_g = _stack_cols(lse_ref, g)
            s = lax.dot_general(qg, kcat, NT_DIMS, preferred_element_type=F32) * scale
            p = jnp.exp(jnp.where(mask, s - lse_g, NEG))
            dp = lax.dot_general(dog, vcat, NT_DIMS, preferred_element_type=F32)
            ds = (p * (dp - delta) * scale).astype(BF16)
            dqg = jnp.dot(ds, kcat, preferred_element_type=F32)
            dkx_upd.append(lax.dot_general(ds[:, 3 * BLK:], qg, TN_DIMS, preferred_element_type=F32))
            dvx_upd.append(lax.dot_general(p.astype(BF16)[:, 3 * BLK:], dog, TN_DIMS, preferred_element_type=F32))
            dsk = -jnp.exp(_sink_col(sink_ref, l, g) - lse_g) * delta
            for hh in range(GROUP):
                h = GROUP * g + hh
                rs = slice(hh * BLK, (hh + 1) * BLK)
                dqs.append(dqg[rs])
                delta_tile = jnp.where(lane == h, delta[rs], delta_tile)
                tot = jnp.sum(dsk[rs], axis=0, keepdims=True)
                dsink_upd = dsink_upd + jnp.where((lane8 == h) & (row8 == 0), tot, 0.0)
        dq_ref[...] = jnp.concatenate(dqs, axis=1)
        dkx_ref[...] += jnp.concatenate(dkx_upd, axis=1)
        dvx_ref[...] += jnp.concatenate(dvx_upd, axis=1)
        delta_ref[...] = delta_tile
        dsink_ref[...] += dsink_upd

    blk = pl.BlockSpec((BLK, DQ), lambda i: (i, 0))
    stat = pl.BlockSpec((BLK, 128), lambda i: (i, 0))
    return pl.pallas_call(
        body, name=name, grid=(nb,),
        in_specs=[pl.BlockSpec(memory_space=pltpu.SMEM), blk] + _band_specs(DK, 0, nb, n_ctx)
        + _band_specs(DK, vcol, nb, n_ctx) + [blk, blk, stat],
        out_specs=[blk, stat, pl.BlockSpec((n_ctx, DK), lambda i: (0, 0)), pl.BlockSpec((n_ctx, DK), lambda i: (0, 0)),
                   pl.BlockSpec((8, 128), lambda i: (0, 0))],
        out_shape=[jax.ShapeDtypeStruct((S, DQ), F32), jax.ShapeDtypeStruct((S, 128), F32),
                   jax.ShapeDtypeStruct((n_ctx, DK), F32), jax.ShapeDtypeStruct((n_ctx, DK), F32),
                   jax.ShapeDtypeStruct((8, 128), F32)],
        compiler_params=_cp(1),
    )(sink, q, k, k, k, k, qkv, qkv, qkv, qkv, o, do, lse)


def _attn_bwd_kv(q, k, qkv, do, lse, delta, dkx, dvx, n_ctx, name):
    S, DQ = q.shape
    DK = k.shape[1]
    n_kv = DK // 128
    nb = S // BLK
    nctx_b = n_ctx // BLK
    vcol = (DQ + DK) // DK
    scale = HEAD_DIM ** -0.5

    def three(width):
        return [pl.BlockSpec((BLK, width), lambda j: (jnp.maximum(j - 1, 0), 0)),
                pl.BlockSpec((BLK, width), lambda j: (j, 0)),
                pl.BlockSpec((BLK, width), lambda j: (jnp.minimum(j + 1, nb - 1), 0))]

    def body(k_ref, v_ref, qp, qc, qn, dop, doc, don, lp, lc, ln, dlp, dlc, dln, dkx_ref, dvx_ref, dk_ref, dv_ref):
        j = pl.program_id(0)

        @pl.when(j < nctx_b)
        def _():
            dk_ref[...] = dkx_ref[...]
            dv_ref[...] = dvx_ref[...]

        @pl.when(j >= nctx_b)
        def _():
            shape = (3 * GROUP * BLK, BLK)
            t = lax.broadcasted_iota(I32, shape, 0)
            rq = (j - 1 + t // (GROUP * BLK)) * BLK + (t & (BLK - 1))
            rk = j * BLK + lax.broadcasted_iota(I32, shape, 1)
            valid = (rq >= n_ctx) & (rq < S) & (jnp.abs(rq - rk) <= WINDOW)
            dks, dvs = [], []
            for g in range(n_kv):
                sl = slice(g * 128, (g + 1) * 128)
                qcat = jnp.concatenate([_stack_heads(r, g) for r in (qp, qc, qn)], axis=0)
                docat = jnp.concatenate([_stack_heads(r, g) for r in (dop, doc, don)], axis=0)
                lse_c = jnp.concatenate([_stack_cols(r, g) for r in (lp, lc, ln)], axis=0)
                delta_c = jnp.concatenate([_stack_cols(r, g) for r in (dlp, dlc, dln)], axis=0)
                s = lax.dot_general(qcat, k_ref[:, sl], NT_DIMS, preferred_element_type=F32) * scale
                p = jnp.exp(jnp.where(valid, s - lse_c, NEG))
                dp = lax.dot_general(docat, v_ref[:, sl], NT_DIMS, preferred_element_type=F32)
                ds = (p * (dp - delta_c) * scale).astype(BF16)
                dks.append(lax.dot_general(ds, qcat, TN_DIMS, preferred_element_type=F32))
                dvs.append(lax.dot_general(p.astype(BF16), docat, TN_DIMS, preferred_element_type=F32))
            dk_ref[...] = jnp.concatenate(dks, axis=1)
            dv_ref[...] = jnp.concatenate(dvs, axis=1)

    cspec = pl.BlockSpec((BLK, DK), lambda j: (jnp.minimum(j, nctx_b - 1), 0))
    return pl.pallas_call(
        body, name=name, grid=(nb,),
        in_specs=[pl.BlockSpec((BLK, DK), lambda j: (j, 0)), pl.BlockSpec((BLK, DK), lambda j: (j, vcol))]
        + three(DQ) + three(DQ) + three(128) + three(128) + [cspec, cspec],
        out_specs=[pl.BlockSpec((BLK, DK), lambda j: (j, 0))] * 2,
        out_shape=[jax.ShapeDtypeStruct((S, DK), F32)] * 2, compiler_params=_cp(1),
    )(k, qkv, q, q, q, do, do, do, lse, lse, lse, delta, delta, delta, dkx, dvx)


def _ada_fwd(cond, w_ada, b_cols, name):
    lyr, D, C = w_ada.shape
    tc = _pick(C, [512, 384, 256, 128])

    def body(c_ref, w_ref, b_ref, o_ref):
        cv = c_ref[...]
        act = cv * _sigmoid(cv)
        o_ref[...] = jnp.dot(act, w_ref[...], preferred_element_type=F32,
                             precision=lax.Precision.HIGHEST) + b_ref[...]

    return pl.pallas_call(
        body, name=name, grid=(lyr, C // tc),
        in_specs=[pl.BlockSpec((16, D), lambda l, j: (0, 0)),
                  pl.BlockSpec((None, D, tc), lambda l, j: (l, 0, j)),
                  pl.BlockSpec((None, 1, tc), lambda l, j: (l, 0, j))],
        out_specs=pl.BlockSpec((None, 16, tc), lambda l, j: (l, 0, j)),
        out_shape=jax.ShapeDtypeStruct((lyr, 16, C), F32), compiler_params=_cp(2),
    )(cond, w_ada, b_cols)


def _ada_bwd(cond, d_out, w_ada, name):
    lyr, D, C = w_ada.shape
    tc = _pick(C, [512, 384, 256, 128])

    def body(c_ref, d_ref, w_ref, gw_ref, dc_ref):
        @pl.when((pl.program_id(0) == 0) & (pl.program_id(1) == 0))
        def _():
            dc_ref[...] = jnp.zeros_like(dc_ref)

        cv = c_ref[...]
        act = cv * _sigmoid(cv)
        dv = d_ref[...]
        gw_ref[...] = lax.dot_general(act, dv, TN_DIMS, preferred_element_type=F32, precision=lax.Precision.HIGHEST)
        dc_ref[...] += lax.dot_general(dv, w_ref[...], NT_DIMS, preferred_element_type=F32,
                                       precision=lax.Precision.HIGHEST)

    return pl.pallas_call(
        body, name=name, grid=(lyr, C // tc),
        in_specs=[pl.BlockSpec((16, D), lambda l, j: (0, 0)),
                  pl.BlockSpec((None, 16, tc), lambda l, j: (l, 0, j)),
                  pl.BlockSpec((None, D, tc), lambda l, j: (l, 0, j))],
        out_specs=[pl.BlockSpec((None, D, tc), lambda l, j: (l, 0, j)), pl.BlockSpec((16, D), lambda l, j: (0, 0))],
        out_shape=[jax.ShapeDtypeStruct((lyr, D, C), F32), jax.ShapeDtypeStruct((16, D), F32)],
        compiler_params=_cp(2),
    )(cond, d_out, w_ada)


def _sum_rows(d_rows, name):
    lyr, r, C = d_rows.shape

    def body(d_ref, o_ref):
        o_ref[...] = jnp.sum(d_ref[...], axis=0, keepdims=True)

    return pl.pallas_call(
        body, name=name, grid=(lyr,),
        in_specs=[pl.BlockSpec((None, r, C), lambda l: (l, 0, 0))],
        out_specs=pl.BlockSpec((None, 1, C), lambda l: (l, 0, 0)),
        out_shape=jax.ShapeDtypeStruct((lyr, 1, C), F32), compiler_params=_cp(1),
    )(d_rows)


def _cctx_grad(gathered, c_ctx_row, name):
    D = gathered.shape[1]

    def body(g_ref, c_ref, o_ref):
        acc = g_ref[0:16, :]
        for d in range(1, N_DEV):
            acc = acc + g_ref[16 * d:16 * (d + 1), :]
        cv = c_ref[...]
        sg = _sigmoid(cv)
        o_ref[...] = acc[8:16] * (sg * (1.0 + cv * (1.0 - sg)))

    return pl.pallas_call(
        body, name=name, out_shape=jax.ShapeDtypeStruct((8, D), F32),
        compiler_params=pltpu.CompilerParams(vmem_limit_bytes=VMEM_ELEMENTWISE),
    )(gathered, c_ctx_row)


def _pad_rows(a, rows):
    return jnp.concatenate([a, jnp.zeros((rows - a.shape[0],) + a.shape[1:], a.dtype)], axis=0)


def kernel(x, c, ctx, c_ctx, w_ada, b_ada, attn_w_qkv, attn_w_o, attn_q_gain, attn_k_gain, attn_sink, sc_w_in, sc_conv, sc_w_out, ffn_w_up, ffn_conv, ffn_conv_b, ffn_w_down, loss_target, m_c_ctx, m_w_ada, m_b_ada, m_attn_w_qkv, m_attn_w_o, m_attn_q_gain, m_attn_k_gain, m_attn_sink, m_sc_w_in, m_sc_conv, m_sc_w_out, m_ffn_w_up, m_ffn_conv, m_ffn_conv_b, m_ffn_w_down, v_c_ctx, v_w_ada, v_b_ada, v_attn_w_qkv, v_attn_w_o, v_attn_q_gain, v_attn_k_gain, v_attn_sink, v_sc_w_in, v_sc_conv, v_sc_w_out, v_ffn_w_up, v_ffn_conv, v_ffn_conv_b, v_ffn_w_down):
    T, D = x.shape[1], x.shape[2]
    L = ctx.shape[1]
    S = L + T
    depth = w_ada.shape[0]
    F = ffn_conv_b.shape[1]
    n_q = D // HEAD_DIM
    n_kv = n_q // GROUP
    ada_c = w_ada.shape[2]
    assert L % BLK == 0 and T % BLK == 0 and ada_c * N_DEV == 6 * D

    px, py, pc = _my_pos()
    me = 4 * px + 2 * py + pc
    me_idx = jnp.reshape(me, (1,)).astype(I32)

    tm_mm = _pick(S, [768, 704, 384, 256, 128])
    ts_tn = _pick(S, [2112, 1056, 768, 384, 256, 128])
    tm_half = _pick(S, [384, 256, 128])
    ts_lane = _pick(S, [2816, 768, 384, 256, 128])

    c_all = _gather_small(_pad_rows(c, 8), "gather_cond")
    cond = jnp.concatenate([c_all[0::8], c_ctx[None, :], jnp.zeros((7, D), F32)], axis=0)
    b_cols = lax.dynamic_slice_in_dim(b_ada, me * ada_c, ada_c, axis=1)[:, None, :]
    ada_mine = _ada_fwd(cond, w_ada, b_cols, "ada_fwd")
    ada_all = _gather_small(ada_mine.reshape(depth * 16, ada_c), "gather_ada")
    ada_all = ada_all.reshape(N_DEV, depth, 16, ada_c)
    ada_rows = jnp.transpose(ada_all, (1, 2, 0, 3)).reshape(depth, 16, 6, D)
    mod_lat = lax.dynamic_index_in_dim(ada_rows, me, axis=1, keepdims=False)
    mods = jnp.stack([ada_rows[:, 8], mod_lat], axis=1)

    gathered = [None] * depth
    tabs = _rope_tables(T, L)
    gains = jnp.stack([attn_q_gain, attn_k_gain], axis=1)
    conv_b3 = ffn_conv_b[:, None, :]
    sc_conv_all = _gather_small(_pad_rows(sc_conv.reshape(-1, sc_conv.shape[2]), 8), "gather_scconv")
    ffn_conv_all = _gather_small(_pad_rows(ffn_conv.reshape(-1, ffn_conv.shape[2]), 16), "gather_ffnconv")
    n_sc = sc_conv.shape[0]
    sc_conv_full = jnp.transpose(sc_conv_all.reshape(N_DEV, 8, -1)[:, :n_sc * 3], (1, 0, 2)).reshape(n_sc, 3, D)
    ffn_conv_full = jnp.transpose(ffn_conv_all.reshape(N_DEV, 16, -1)[:, :depth * 3], (1, 0, 2)).reshape(depth, 3, F)

    def start_weights(l, tag, after):
        if tag == "ffn":
            ws = [(ffn_w_up, l), (ffn_w_down, l)]
        else:
            ws = [(attn_w_qkv, l // 2), (attn_w_o, l // 2)] if l % 2 == 0 else [(sc_w_in, l // 2), (sc_w_out, l // 2)]
        lands = [_cast_layer(w, j, me_idx, f"cast_{tag}{k}_{l}") for k, (w, j) in enumerate(ws)]
        return _gather_start(lands, after, f"gather_start_{tag}{l}")

    def wait_weights(flight, after, name):
        send_sems, recv_sems, lands, _ = flight
        return _gather_wait(lands, send_sems, recv_sems, after, name)

    flight_mix = start_weights(0, "mix", [mods, sc_conv_full, ffn_conv_full])
    mods = mods + flight_mix[3][0, 0]

    xs = jnp.concatenate([ctx[0], x[0]], axis=0)
    saved = []
    for l in range(depth):
        j = l // 2
        mod = mods[l]
        if l == 0:
            w_a, w_b = wait_weights(flight_mix, mods, "gather_wait_mix0")
            flight_ffn = start_weights(0, "ffn", [w_a])
            mod = mod + flight_ffn[3][0, 0]
            h, h_t = _norm_mod(xs, mod, 0, L, "norm_m0")
        else:
            h, h_t = _norm_mod(xs, mod, 0, L, f"norm_m{l}")
            w_a, w_b = wait_weights(flight_mix, h, f"gather_wait_mix{l}")
        if l % 2 == 0:
            qkv = _mm_nn(h, w_a, tm=tm_mm, tn=w_a.shape[2], out_dtype=BF16, name=f"qkv{l}")
            qr, kr = _qk_prep(qkv, tabs, gains, j, n_q, n_kv, f"qk_prep{l}")
            z, lse = _attn_fwd(qr, kr, qkv, attn_sink, j, L, f"attn{l}")
            mix = (qkv, qr, kr, lse)
        else:
            u = _mm_nn(h, w_a, tm=tm_mm, tn=w_a.shape[2], out_dtype=BF16, name=f"scin{l}")
            z = _sc_act(u, sc_conv_full, j, L, f"sc_act{l}")
            mix = (u,)
        if l + 1 < depth:
            flight_mix = start_weights(l + 1, "mix", [z])
            mod = mod + flight_mix[3][0, 0]
        y_m, x1, h2, h2_t = _mm_nn_resid_norm(z, w_b.reshape(D, D), xs, mod, 2, 3, L, tm=tm_half, name=f"mixout{l}")
        w_up, w_down = wait_weights(flight_ffn, x1, f"gather_wait_ffn{l}")
        gathered[l] = (w_a, w_b, w_up, w_down)
        u_f = _mm_nn(h2, w_up, tm=tm_mm, tn=w_up.shape[2], out_dtype=BF16, name=f"up{l}")
        a_f = _ffn_act(u_f, ffn_conv_full, conv_b3, l, L, f"ffn_act{l}")
        if l + 1 < depth:
            flight_ffn = start_weights(l + 1, "ffn", [a_f])
            mod = mod + flight_ffn[3][0, 0]
        y_f, x2 = _mm_nn_resid(a_f, w_down.reshape(F, D), x1, mod, 5, L, tm=tm_mm, tn=_pick(D, [512]), name=f"down{l}")
        saved.append((xs, h_t, mix, z, y_m, x1, h2_t, u_f, a_f, y_f))
        xs = x2

    dx, sq = _loss_grad(xs, loss_target[0], L, "loss")
    loss = lax.psum(sq[0, 0], ("x", "y", "c"))

    dmods = [None] * depth
    g_conv_b, g_ffn_conv, g_sc_conv = [None] * depth, [None] * depth, [None] * n_sc
    g_gain, g_sink = [None] * (depth - n_sc), [None] * (depth - n_sc)
    rs_flight = [None] * depth
    sent = jnp.zeros((), F32)
    for l in reversed(range(depth)):
        j = l // 2
        w_a, w_b, w_up, w_down = gathered[l]
        mod = mods[l] + sent
        x0, h_t, mix, z, y_m, x1, h2_t, u_f, a_f, y_f = saved[l]
        da, dy, s_gf = _gate_dgrad(dx, y_f, mod, 5, L, w_down.reshape(F, D), tm=tm_mm, tn=_pick(F, [1408, 512]),
                                   name=f"down_dgrad{l}")
        gw_down = _mm_tn(a_f, [dy], nb=1, tka=_pick(F, [1408, 512]), tn=_pick(D, [1024, 512]), ts=ts_tn, name=f"down_wgrad{l}")
        dgate, dval, s_conv = _ffn_act_bwd(u_f, da, ffn_conv_full, conv_b3, l, L, f"ffn_act_bwd{l}")
        dh2 = _mm_nt_acc([dgate, dval], w_up, tm=tm_half, name=f"up_dgrad{l}")
        dx1, s_nf = _norm_mod_bwd(dh2, x1, mod, dx, 3, L, f"norm_f_bwd{l}")
        gw_up = _mm_tn(h2_t, [dgate, dval], nb=N_DEV, tka=_pick(D, [512]), tn=w_up.shape[2], ts=ts_lane,
                       name=f"up_wgrad{l}", a_transposed=True)
        g_ffn_conv[l], g_conv_b[l] = s_conv[0:3], s_conv[3]
        rs_ffn = _rs_start([gw_up, gw_down.reshape(N_DEV, -1, D)], f"rs_start_ffn{l}")
        mod = mods[l] + rs_ffn[4][0, 0]
        dz, dy, s_gm = _gate_dgrad(dx1, y_m, mod, 2, L, w_b.reshape(D, D), tm=tm_mm, tn=_pick(D, [1024, 512]),
                                   name=f"mixout_dgrad{l}")
        gw_b = _mm_tn(z, [dy], nb=1, tka=_pick(D, [1024, 512]), tn=_pick(D, [1024, 512]), ts=ts_tn, name=f"mixout_wgrad{l}")
        if l % 2 == 0:
            qkv, qr, kr, lse = mix
            dq, delta, dkx, dvx, s_sink = _attn_bwd_q(qr, kr, qkv, z, dz, lse, attn_sink, j, L, f"attn_bwd_q{l}")
            dk, dv = _attn_bwd_kv(qr, kr, qkv, dz, lse, delta, dkx, dvx, L, f"attn_bwd_kv{l}")
            du_m, s_gain = _qk_prep_bwd(dq, dk, dv, qkv, tabs, gains, j, n_q, n_kv, f"qk_prep_bwd{l}")
            g_gain[j], g_sink[j] = s_gain[0:2], s_sink[0]
        else:
            (u,) = mix
            du_m, s_scconv = _sc_act_bwd(u, dz, sc_conv_full, j, L, f"sc_act_bwd{l}")
            g_sc_conv[j] = s_scconv[0:3]
        dh = _mm_nt_acc([du_m], w_a, tm=tm_mm, name=f"mixin_dgrad{l}")
        gw_a = _mm_tn(h_t, [du_m], nb=N_DEV, tka=_pick(D, [1024, 512]), tn=w_a.shape[2], ts=ts_lane,
                      name=f"mixin_wgrad{l}", a_transposed=True)
        dx, s_nm = _norm_mod_bwd(dh, x0, mod, dx1, 0, L, f"norm_m_bwd{l}")
        dmods[l] = jnp.stack([jnp.stack([s_nm[2 * k], s_nm[2 * k + 1], s_gm[k], s_nf[2 * k], s_nf[2 * k + 1], s_gf[k]])
                              for k in range(2)])
        rs_mix = _rs_start([gw_a, gw_b.reshape(N_DEV, -1, D)], f"rs_start_mix{l}")
        sent = rs_mix[4][0, 0]
        rs_flight[l] = (rs_mix, rs_ffn)

    grad_x = dx[L:][None]

    big_w = {"qkv": (attn_w_qkv, m_attn_w_qkv, v_attn_w_qkv), "wo": (attn_w_o, m_attn_w_o, v_attn_w_o),
             "scin": (sc_w_in, m_sc_w_in, v_sc_w_in), "scout": (sc_w_out, m_sc_w_out, v_sc_w_out),
             "up": (ffn_w_up, m_ffn_w_up, v_ffn_w_up), "down": (ffn_w_down, m_ffn_w_down, v_ffn_w_down)}
    big_out = {k: [] for k in big_w}
    for l in reversed(range(depth)):
        j = l // 2
        groups = [(["qkv", "wo"] if l % 2 == 0 else ["scin", "scout"], [j, j]), (["up", "down"], [l, l])]
        for (names, idxs), flight, tag in reversed(list(zip(groups, rs_flight[l], ("mix", "ffn")))):
            send_sems, recv_sems, own, zones, _ = flight
            own, zones = _rs_wait(own, zones, send_sems, recv_sems, dx, f"rs_wait_{tag}{l}")
            for n, li, p, z in zip(names, idxs, own, zones):
                w, m, v = big_w[n]
                big_out[n].insert(0, _adamw_reduced(p, z, me_idx, w, m, v, li, f"adamw_{n}{l}"))
    big_res = {k: [jnp.stack([o[t] for o in outs]) for t in range(4)] for k, outs in big_out.items()}

    n_attn = depth - n_sc
    pack = [jnp.stack(dmods)[:, 0].reshape(-1, 128), jnp.stack(dmods)[:, 1].reshape(-1, 128),
            jnp.stack(g_gain).reshape(-1, 128), jnp.stack(g_sink),
            jnp.stack(g_conv_b).reshape(-1, 128), jnp.stack(g_ffn_conv).reshape(-1, 128),
            jnp.stack(g_sc_conv).reshape(-1, 128)]
    used = [p.shape[0] for p in pack]
    pack = [_pad_rows(p, -(-p.shape[0] // 8) * 8) for p in pack]
    sizes = [p.shape[0] for p in pack]
    flat = jnp.concatenate(pack, axis=0)
    rows = flat.shape[0]
    small_all = _gather_small(flat, "gather_small_grads")
    small_sum = _sum8(small_all, rows, "sum_small_grads")
    offs = [sum(sizes[:k]) for k in range(len(sizes))]
    seg = lambda a, k: a[offs[k]:offs[k] + used[k]]
    dmod_ctx = seg(small_sum, 0).reshape(depth, 6 * D)
    dmod_lat = small_all.reshape(N_DEV, rows, 128)[:, offs[1]:offs[1] + used[1]].reshape(N_DEV, depth, 6 * D)
    g_gain_sum = seg(small_sum, 2).reshape(n_attn, 2, 128)
    g_sink_sum = seg(small_sum, 3)[:n_attn, :n_q]
    g_conv_b_sum = seg(small_sum, 4).reshape(depth, F)
    g_ffn_conv_sum = seg(small_sum, 5).reshape(depth, 3, F)
    g_sc_conv_sum = seg(small_sum, 6).reshape(n_sc, 3, D)

    d_rows = jnp.concatenate([jnp.transpose(dmod_lat, (1, 0, 2)), dmod_ctx[:, None, :],
                              jnp.zeros((depth, 7, 6 * D), F32)], axis=1)
    d_cols = lax.dynamic_slice_in_dim(d_rows, me * ada_c, ada_c, axis=2)
    g_w_ada, dcond_part = _ada_bwd(cond, d_cols, w_ada, "ada_bwd")
    dcond_all = _gather_small(dcond_part, "gather_dcond")
    g_c_ctx = _cctx_grad(dcond_all, jnp.broadcast_to(c_ctx[None, :], (8, D)), "cctx_grad")[0]
    g_b_ada = _sum_rows(d_rows, "b_ada_grad")[:, 0]

    def small_adam(w, g, m, v, name):
        w2 = w.reshape(-1, w.shape[-1])
        d, m2, v2 = _adamw_plain(w2, g.reshape(w2.shape), m.reshape(w2.shape), v.reshape(w2.shape), name)
        return g.reshape(w.shape), d.reshape(w.shape), m2.reshape(w.shape), v2.reshape(w.shape)

    g_sc_conv_mine = lax.dynamic_slice_in_dim(g_sc_conv_sum, me * sc_conv.shape[2], sc_conv.shape[2], axis=2)
    g_ffn_conv_mine = lax.dynamic_slice_in_dim(g_ffn_conv_sum, me * ffn_conv.shape[2], ffn_conv.shape[2], axis=2)
    res = {
        "c_ctx": small_adam(c_ctx[None, :], g_c_ctx[None, :], m_c_ctx[None, :], v_c_ctx[None, :], "adamw_c_ctx"),
        "b_ada": small_adam(b_ada, g_b_ada, m_b_ada, v_b_ada, "adamw_b_ada"),
        "attn_q_gain": small_adam(attn_q_gain, g_gain_sum[:, 0], m_attn_q_gain, v_attn_q_gain, "adamw_q_gain"),
        "attn_k_gain": small_adam(attn_k_gain, g_gain_sum[:, 1], m_attn_k_gain, v_attn_k_gain, "adamw_k_gain"),
        "attn_sink": small_adam(attn_sink, g_sink_sum, m_attn_sink, v_attn_sink, "adamw_sink"),
        "sc_conv": small_adam(sc_conv, g_sc_conv_mine, m_sc_conv, v_sc_conv, "adamw_sc_conv"),
        "ffn_conv": small_adam(ffn_conv, g_ffn_conv_mine, m_ffn_conv, v_ffn_conv, "adamw_ffn_conv"),
        "ffn_conv_b": small_adam(ffn_conv_b, g_conv_b_sum, m_ffn_conv_b, v_ffn_conv_b, "adamw_conv_b"),
    }
    res["c_ctx"] = tuple(t[0] for t in res["c_ctx"])
    res["w_ada"] = (g_w_ada,) + tuple(_adamw_tiled(w_ada, g_w_ada, m_w_ada, v_w_ada, "adamw_w_ada"))
    res["attn_w_qkv"], res["attn_w_o"] = big_res["qkv"], big_res["wo"]
    res["sc_w_in"], res["sc_w_out"] = big_res["scin"], big_res["scout"]
    res["ffn_w_up"], res["ffn_w_down"] = big_res["up"], big_res["down"]

    order = ["c_ctx", "w_ada", "b_ada", "attn_w_qkv", "attn_w_o", "attn_q_gain", "attn_k_gain", "attn_sink",
             "sc_w_in", "sc_conv", "sc_w_out", "ffn_w_up", "ffn_conv", "ffn_conv_b", "ffn_w_down"]
    outs = [loss, grad_x]
    for t in range(4):
        outs += [res[n][t] for n in order]
    return tuple(outs)
```

```python
import functools

import jax
import jax.numpy as jnp
from jax import lax
from jax.experimental import pallas as pl
from jax.experimental.pallas import tpu as pltpu

F32 = jnp.float32
BF16 = jnp.bfloat16
I32 = jnp.int32

N_DEV = 8
HEAD_DIM = 128
GROUP = 4
WINDOW = 128
BLK = 128
GRID_W = 64
ROPE_BASE = 10000.0
EPS = 1e-6
NEG = -1e30
HALO = 16

ADAM_LR = 0.001
ADAM_B1 = 0.9
ADAM_B2 = 0.999
ADAM_EPS = 1e-08
ADAM_WD = 0.01
ADAM_STEP = 10

V7X_VMEM_BYTES = 64 << 20
VMEM_MATMUL = 52 << 20
VMEM_ELEMENTWISE = 44 << 20
VMEM_NEAR_FULL = 62 << 20

MESH = pl.DeviceIdType.MESH
ANY = pl.BlockSpec(memory_space=pl.ANY)
HBM = pl.BlockSpec(memory_space=pltpu.HBM)
SEM = pl.BlockSpec(memory_space=pltpu.SEMAPHORE)
EFFECT = pltpu.SideEffectType.DATAFLOW_SIDE_EFFECTING

NT_DIMS = (((1,), (1,)), ((), ()))
TN_DIMS = (((0,), (0,)), ((), ()))


def _pick(n, cands):
    for t in cands:
        if n % t == 0:
            return t
    raise ValueError(f"no tile for {n} in {cands}")


def _cp(n_axes, vmem=VMEM_ELEMENTWISE):
    return pltpu.CompilerParams(dimension_semantics=("arbitrary",) * n_axes, vmem_limit_bytes=vmem)


def _rows(i, tm, off=0):
    return i * tm + off + lax.broadcasted_iota(I32, (tm, 1), 0)


def _my_pos():
    return lax.axis_index("x"), lax.axis_index("y"), lax.axis_index("c")


def _gather_small(x_shard, name):
    m_per, n = x_shard.shape

    def body(x_ref, out_ref, send_sems, recv_sems, local_sem):
        x, y, c = _my_pos()
        me, sibling = (x, y, c), (x, y, 1 - c)
        chips = [(1 - x, y), (x, 1 - y), (1 - x, 1 - y)]

        def rows(px, py, pc):
            return out_ref.at[pl.ds((4 * px + 2 * py + pc) * m_per, m_per), :]

        def copy(k, block, to, src=None):
            return pltpu.make_async_remote_copy(
                src_ref=rows(*block) if src is None else src, dst_ref=rows(*block),
                send_sem=send_sems.at[k], recv_sem=recv_sems.at[k], device_id=to, device_id_type=MESH)

        mine = pltpu.make_async_copy(x_ref, rows(*me), local_sem)
        mine.start()
        first = [copy(0, me, sibling, src=x_ref)]
        first += [copy(1 + j, me, (*chip, c), src=x_ref) for j, chip in enumerate(chips)]
        for cp in first:
            cp.start()
        passed = [copy(4 + j, (*chip, c), sibling) for j, chip in enumerate(chips)]
        for j, chip in enumerate(chips):
            copy(1 + j, (*chip, c), me).wait_recv()
            passed[j].start()
        copy(0, sibling, me).wait_recv()
        for j, chip in enumerate(chips):
            copy(4 + j, (*chip, 1 - c), me).wait_recv()
        for cp in first + passed:
            cp.wait_send()
        mine.wait()

    return pl.pallas_call(
        body, name=name,
        out_shape=jax.ShapeDtypeStruct((N_DEV * m_per, n), x_shard.dtype),
        in_specs=[pl.BlockSpec(memory_space=pltpu.VMEM)],
        out_specs=pl.BlockSpec(memory_space=pltpu.VMEM),
        scratch_shapes=[pltpu.SemaphoreType.DMA((7,)), pltpu.SemaphoreType.DMA((7,)), pltpu.SemaphoreType.DMA],
        compiler_params=pltpu.CompilerParams(vmem_limit_bytes=VMEM_ELEMENTWISE),
    )(x_shard)


def _peer(k):
    x, y, c = _my_pos()
    b = k + 1
    return ((1 - x) if b & 4 else x, (1 - y) if b & 2 else y, (1 - c) if b & 1 else c)


def _slot(p):
    return 4 * p[0] + 2 * p[1] + p[2]


def _in_hbm(a):
    return pltpu.with_memory_space_constraint(a, pltpu.HBM)


def _gather_start(lands, after, name):
    n = len(lands)

    def body(*refs):
        l_refs, send_sems, recv_sems = refs[:n], refs[n + len(after)], refs[n + len(after) + 1]
        token = refs[2 * n + len(after) + 2]
        me = _slot(_my_pos())
        for a in range(n):
            for k in range(7):
                pltpu.make_async_remote_copy(
                    src_ref=l_refs[a].at[me], dst_ref=l_refs[a].at[me],
                    send_sem=send_sems.at[7 * a + k], recv_sem=recv_sems.at[7 * a + k],
                    device_id=_peer(k), device_id_type=MESH).start()
        token[...] = jnp.zeros_like(token)

    out = pl.pallas_call(
        body, name=name,
        out_shape=(pltpu.SemaphoreType.DMA((7 * n,)), pltpu.SemaphoreType.DMA((7 * n,)),
                   *[pltpu.HBM(a.shape, a.dtype) for a in lands], jax.ShapeDtypeStruct((8, 128), F32)),
        in_specs=[HBM] * n + [ANY] * len(after),
        out_specs=(SEM, SEM, *[HBM] * n, pl.BlockSpec(memory_space=pltpu.VMEM)),
        input_output_aliases={a: 2 + a for a in range(n)},
        compiler_params=pltpu.CompilerParams(has_side_effects=EFFECT),
    )(*[_in_hbm(a) for a in lands], *after)
    return out[0], out[1], list(out[2:2 + n]), out[2 + n]


def _gather_wait(lands, send_sems, recv_sems, after, name):
    n = len(lands)

    def body(*refs):
        l_refs, ss, rs = refs[:n], refs[n], refs[n + 1]
        me = _slot(_my_pos())
        for a in range(n):
            for k in range(7):
                cp = pltpu.make_async_remote_copy(
                    src_ref=l_refs[a].at[me], dst_ref=l_refs[a].at[_slot(_peer(k))],
                    send_sem=ss.at[7 * a + k], recv_sem=rs.at[7 * a + k], device_id=_peer(k), device_id_type=MESH)
                cp.wait_send()
                cp.wait_recv()

    out = pl.pallas_call(
        body, name=name,
        out_shape=tuple(pltpu.HBM(a.shape, a.dtype) for a in lands),
        in_specs=[HBM] * n + [SEM, SEM, ANY], out_specs=tuple([HBM] * n),
        input_output_aliases={a: a for a in range(n)},
        compiler_params=pltpu.CompilerParams(has_side_effects=EFFECT),
    )(*lands, send_sems, recv_sems, after)
    return list(out)


def _rs_start(grads, name):
    n = len(grads)

    def body(*refs):
        g_refs, z_refs, send_sems, recv_sems = refs[:n], refs[n:2 * n], refs[2 * n], refs[2 * n + 1]
        token = refs[4 * n + 2]
        for a in range(n):
            for k in range(7):
                pltpu.make_async_remote_copy(
                    src_ref=g_refs[a].at[_slot(_peer(k))], dst_ref=z_refs[a].at[k],
                    send_sem=send_sems.at[7 * a + k], recv_sem=recv_sems.at[7 * a + k],
                    device_id=_peer(k), device_id_type=MESH).start()
        token[...] = jnp.zeros_like(token)

    zones = [lax.empty((7,) + g.shape[1:], g.dtype) for g in grads]
    out = pl.pallas_call(
        body, name=name,
        out_shape=(pltpu.SemaphoreType.DMA((7 * n,)), pltpu.SemaphoreType.DMA((7 * n,)),
                   *[pltpu.HBM(a.shape, a.dtype) for a in grads], *[pltpu.HBM(z.shape, z.dtype) for z in zones],
                   jax.ShapeDtypeStruct((8, 128), F32)),
        in_specs=[HBM] * (2 * n),
        out_specs=(SEM, SEM, *[HBM] * (2 * n), pl.BlockSpec(memory_space=pltpu.VMEM)),
        input_output_aliases={a: 2 + a for a in range(2 * n)},
        compiler_params=pltpu.CompilerParams(has_side_effects=EFFECT),
    )(*[_in_hbm(a) for a in grads], *[_in_hbm(z) for z in zones])
    return out[0], out[1], list(out[2:2 + n]), list(out[2 + n:2 + 2 * n]), out[2 + 2 * n]


def _rs_wait(grads, zones, send_sems, recv_sems, after, name):
    n = len(grads)

    def body(*refs):
        g_refs, z_refs, ss, rs = refs[:n], refs[n:2 * n], refs[2 * n], refs[2 * n + 1]
        for a in range(n):
            for k in range(7):
                cp = pltpu.make_async_remote_copy(
                    src_ref=g_refs[a].at[_slot(_peer(k))], dst_ref=z_refs[a].at[k],
                    send_sem=ss.at[7 * a + k], recv_sem=rs.at[7 * a + k], device_id=_peer(k), device_id_type=MESH)
                cp.wait_send()
                cp.wait_recv()

    out = pl.pallas_call(
        body, name=name,
        out_shape=tuple(pltpu.HBM(a.shape, a.dtype) for a in list(grads) + list(zones)),
        in_specs=[HBM] * (2 * n) + [SEM, SEM, ANY], out_specs=tuple([HBM] * (2 * n)),
        input_output_aliases={a: a for a in range(2 * n)},
        compiler_params=pltpu.CompilerParams(has_side_effects=EFFECT),
    )(*grads, *zones, send_sems, recv_sems, after)
    return list(out[:n]), list(out[n:])


def _cast_layer(w, l, me_idx, name):
    _, r, c = w.shape
    tr = _pick(r, [512, 256, 128, 64, 32, 16])

    def body(me_ref, w_ref, o_ref):
        o_ref[...] = w_ref[...].astype(BF16)

    return pl.pallas_call(
        body, name=name,
        grid_spec=pltpu.PrefetchScalarGridSpec(
            num_scalar_prefetch=1, grid=(r // tr,),
            in_specs=[pl.BlockSpec((None, tr, c), lambda i, me_ref: (l, i, 0))],
            out_specs=pl.BlockSpec((None, tr, c), lambda i, me_ref: (me_ref[0], i, 0))),
        out_shape=jax.ShapeDtypeStruct((N_DEV, r, c), BF16), compiler_params=_cp(1),
    )(me_idx, w)


def _adam_math(w, g, m, v):
    m2 = ADAM_B1 * m + (1.0 - ADAM_B1) * g
    v2 = ADAM_B2 * v + (1.0 - ADAM_B2) * (g * g)
    m_hat = m2 / (1.0 - ADAM_B1 ** ADAM_STEP)
    v_hat = v2 / (1.0 - ADAM_B2 ** ADAM_STEP)
    delta = -ADAM_LR * (m_hat / (jnp.sqrt(v_hat) + ADAM_EPS) + ADAM_WD * w)
    return delta, m2, v2


def _adamw_reduced(own, zone, me_idx, w, m, v, l, name):
    _, r, c = own.shape
    tr = _pick(r, [256, 128, 64, 32, 16])

    def body(me_ref, p_ref, z_ref, w_ref, m_ref, v_ref, g_out, d_out, m_out, v_out):
        g = p_ref[...].astype(F32)
        for k in range(7):
            g = g + z_ref[k].astype(F32)
        d, m2, v2 = _adam_math(w_ref[...], g, m_ref[...], v_ref[...])
        g_out[...] = g
        d_out[...] = d
        m_out[...] = m2
        v_out[...] = v2

    wspec = pl.BlockSpec((None, tr, c), lambda i, me_ref: (l, i, 0))
    ospec = pl.BlockSpec((tr, c), lambda i, me_ref: (i, 0))
    return pl.pallas_call(
        body, name=name,
        grid_spec=pltpu.PrefetchScalarGridSpec(
            num_scalar_prefetch=1, grid=(r // tr,),
            in_specs=[pl.BlockSpec((None, tr, c), lambda i, me_ref: (me_ref[0], i, 0)),
                      pl.BlockSpec((7, tr, c), lambda i, me_ref: (0, i, 0)), wspec, wspec, wspec],
            out_specs=[ospec] * 4),
        out_shape=[jax.ShapeDtypeStruct((r, c), F32)] * 4, compiler_params=_cp(1),
    )(me_idx, own, zone, w, m, v)


def _adamw_plain(w, g, m, v, name):
    def body(w_ref, g_ref, m_ref, v_ref, d_out, m_out, v_out):
        d, m2, v2 = _adam_math(w_ref[...], g_ref[...], m_ref[...], v_ref[...])
        d_out[...] = d
        m_out[...] = m2
        v_out[...] = v2

    return pl.pallas_call(
        body, name=name, out_shape=[jax.ShapeDtypeStruct(w.shape, F32)] * 3,
        compiler_params=pltpu.CompilerParams(vmem_limit_bytes=VMEM_ELEMENTWISE),
    )(w, g, m, v)


def _adamw_tiled(w, g, m, v, name):
    lyr, r, c = w.shape
    tr = _pick(r, [256, 128, 64, 32, 16, 8])

    def body(w_ref, g_ref, m_ref, v_ref, d_out, m_out, v_out):
        d, m2, v2 = _adam_math(w_ref[...], g_ref[...], m_ref[...], v_ref[...])
        d_out[...] = d
        m_out[...] = m2
        v_out[...] = v2

    spec = pl.BlockSpec((None, tr, c), lambda l, i: (l, i, 0))
    return pl.pallas_call(
        body, name=name, grid=(lyr, r // tr), in_specs=[spec] * 4, out_specs=[spec] * 3,
        out_shape=[jax.ShapeDtypeStruct(w.shape, F32)] * 3, compiler_params=_cp(2),
    )(w, g, m, v)


def _sum8(gathered, rows, name):
    def body(g_ref, o_ref):
        acc = g_ref[0:rows, :]
        for d in range(1, N_DEV):
            acc = acc + g_ref[d * rows:(d + 1) * rows, :]
        o_ref[...] = acc

    return pl.pallas_call(
        body, name=name, out_shape=jax.ShapeDtypeStruct((rows, 128), F32),
        compiler_params=pltpu.CompilerParams(vmem_limit_bytes=VMEM_ELEMENTWISE),
    )(gathered)


MXU_COLS = 256


def _pairable(nb, nc):
    return nb % 2 == 0 and nc % MXU_COLS == MXU_COLS // 2 and nc > MXU_COLS // 2


def _mm_nn(a, b3, *, tm, tn, out_dtype, name):
    M, K = a.shape
    nb, _, nc = b3.shape
    q = nc // tn

    if _pairable(nb, nc) and tn == nc:
        cut = nc - 128

        def pair_body(a_ref, b_ref, o_ref):
            av = a_ref[...]
            mid = jnp.concatenate([b_ref[0, :, cut:nc], b_ref[1, :, 0:128]], axis=1)
            o_ref[:, 0:cut] = jnp.dot(av, b_ref[0, :, 0:cut], preferred_element_type=F32).astype(o_ref.dtype)
            o_ref[:, cut:nc + 128] = jnp.dot(av, mid, preferred_element_type=F32).astype(o_ref.dtype)
            o_ref[:, nc + 128:2 * nc] = jnp.dot(av, b_ref[1, :, 128:nc], preferred_element_type=F32).astype(o_ref.dtype)

        return pl.pallas_call(
            pair_body, name=name, grid=(nb // 2, M // tm),
            in_specs=[pl.BlockSpec((tm, K), lambda j, i: (i, 0)),
                      pl.BlockSpec((2, K, nc), lambda j, i: (j, 0, 0))],
            out_specs=pl.BlockSpec((tm, 2 * nc), lambda j, i: (i, j)),
            out_shape=jax.ShapeDtypeStruct((M, nb * nc), out_dtype), compiler_params=_cp(2, VMEM_MATMUL),
        )(a, b3)

    def body(a_ref, b_ref, o_ref):
        o_ref[...] = jnp.dot(a_ref[...], b_ref[...], preferred_element_type=F32).astype(o_ref.dtype)

    return pl.pallas_call(
        body, name=name, grid=(nb * q, M // tm),
        in_specs=[pl.BlockSpec((tm, K), lambda j, i: (i, 0)),
                  pl.BlockSpec((None, K, tn), lambda j, i: (j // q, 0, j % q))],
        out_specs=pl.BlockSpec((tm, tn), lambda j, i: (i, j)),
        out_shape=jax.ShapeDtypeStruct((M, nb * nc), out_dtype), compiler_params=_cp(2, VMEM_MATMUL),
    )(a, b3)


def _mm_nn_resid(a, b2, x_old, mod, gate_row, n_ctx, *, tm, tn, name):
    M, K = a.shape
    N = b2.shape[1]

    def body(a_ref, b_ref, x_ref, mod_ref, y_ref, xn_ref):
        y = jnp.dot(a_ref[...], b_ref[...], preferred_element_type=F32)
        is_ctx = _rows(pl.program_id(1), tm) < n_ctx
        g = jnp.where(is_ctx, mod_ref[0, gate_row:gate_row + 1, :], mod_ref[1, gate_row:gate_row + 1, :])
        y_ref[...] = y.astype(BF16)
        xn_ref[...] = x_ref[...] + g * y

    return pl.pallas_call(
        body, name=name, grid=(N // tn, M // tm),
        in_specs=[pl.BlockSpec((tm, K), lambda j, i: (i, 0)),
                  pl.BlockSpec((K, tn), lambda j, i: (0, j)),
                  pl.BlockSpec((tm, tn), lambda j, i: (i, j)),
                  pl.BlockSpec((2, 6, tn), lambda j, i: (0, 0, j))],
        out_specs=[pl.BlockSpec((tm, tn), lambda j, i: (i, j))] * 2,
        out_shape=[jax.ShapeDtypeStruct((M, N), BF16), jax.ShapeDtypeStruct((M, N), F32)],
        compiler_params=_cp(2, VMEM_MATMUL),
    )(a, b2, x_old, mod)


def _mm_nn_resid_norm(a, b2, x_old, mod, gate_row, norm_row0, n_ctx, *, tm, name):
    M, K = a.shape
    N = b2.shape[1]

    def body(a_ref, b_ref, x_ref, mod_ref, y_ref, xn_ref, h_ref, ht_ref):
        y = jnp.dot(a_ref[...], b_ref[...], preferred_element_type=F32)
        is_ctx = _rows(pl.program_id(0), tm) < n_ctx

        def row(k):
            return jnp.where(is_ctx, mod_ref[0, k:k + 1, :], mod_ref[1, k:k + 1, :])

        y_ref[...] = y.astype(BF16)
        xn = x_ref[...] + row(gate_row) * y
        xn_ref[...] = xn
        r = lax.rsqrt(jnp.mean(xn * xn, axis=-1, keepdims=True) + EPS)
        hv = xn * r * (1.0 + row(norm_row0 + 1)) + row(norm_row0)
        h_ref[...] = hv.astype(BF16)
        ht_ref[...] = hv.T.astype(BF16)

    rows = pl.BlockSpec((tm, N), lambda i: (i, 0))
    return pl.pallas_call(
        body, name=name, grid=(M // tm,),
        in_specs=[pl.BlockSpec((tm, K), lambda i: (i, 0)), pl.BlockSpec((K, N), lambda i: (0, 0)), rows,
                  pl.BlockSpec((2, 6, N), lambda i: (0, 0, 0))],
        out_specs=[rows, rows, rows, pl.BlockSpec((N, tm), lambda i: (0, i))],
        out_shape=[jax.ShapeDtypeStruct((M, N), BF16), jax.ShapeDtypeStruct((M, N), F32),
                   jax.ShapeDtypeStruct((M, N), BF16), jax.ShapeDtypeStruct((N, M), BF16)],
        compiler_params=_cp(1, VMEM_MATMUL),
    )(a, b2, x_old, mod)


def _norm_bwd_math(dh_v, xv, mod_ref, res, is_ctx, row0):
    D = xv.shape[1]
    r = lax.rsqrt(jnp.mean(xv * xv, axis=-1, keepdims=True) + EPS)
    xhat = xv * r
    sc = jnp.where(is_ctx, mod_ref[0, row0 + 1:row0 + 2, :], mod_ref[1, row0 + 1:row0 + 2, :])
    dxhat = dh_v * (1.0 + sc)
    dx = res + r * (dxhat - xhat * jnp.mean(dxhat * xhat, axis=-1, keepdims=True))
    dsc = dh_v * xhat
    zero = jnp.zeros_like(dh_v)
    sums = [jnp.sum(jnp.where(is_ctx, dh_v, zero), axis=0, keepdims=True),
            jnp.sum(jnp.where(is_ctx, dsc, zero), axis=0, keepdims=True),
            jnp.sum(jnp.where(is_ctx, zero, dh_v), axis=0, keepdims=True),
            jnp.sum(jnp.where(is_ctx, zero, dsc), axis=0, keepdims=True)]
    rid = lax.broadcasted_iota(I32, (8, D), 0)
    upd = jnp.zeros((8, D), F32)
    for k, s in enumerate(sums):
        upd = upd + jnp.where(rid == k, s, 0.0)
    return dx, upd


def _mm_nt_acc(parts, w3, *, tm, name, vmem=VMEM_MATMUL):
    n_parts = len(parts)
    M = parts[0].shape[0]
    nb, K, nc = w3.shape
    pair = _pairable(nb // n_parts, nc)
    g = 2 if pair else 1
    steps, per = nb // g, nb // n_parts // g
    cut = nc - 128

    def nt(d, w):
        return lax.dot_general(d, w, NT_DIMS, preferred_element_type=F32)

    def contribution(d_ref, w_ref):
        if not pair:
            return nt(d_ref[...], w_ref[0])
        mid = jnp.concatenate([w_ref[0, :, cut:nc], w_ref[1, :, 0:128]], axis=1)
        return (nt(d_ref[:, 0:cut], w_ref[0, :, 0:cut]) + nt(d_ref[:, cut:nc + 128], mid)
                + nt(d_ref[:, nc + 128:2 * nc], w_ref[1, :, 128:nc]))

    def body(*refs):
        dy_refs, w_ref = refs[:n_parts], refs[n_parts]
        o_ref, acc_ref = refs[n_parts + 1], refs[n_parts + 2]
        s = pl.program_id(1)

        @pl.when(s == 0)
        def _():
            acc_ref[...] = jnp.zeros_like(acc_ref)

        for p in range(n_parts):
            @pl.when(s // per == p)
            def _(p=p):
                acc_ref[...] += contribution(dy_refs[p], w_ref)

        @pl.when(s == steps - 1)
        def _():
            o_ref[...] = acc_ref[...].astype(o_ref.dtype)

    def part_spec(p):
        return pl.BlockSpec((tm, g * nc), lambda i, s: (i, jnp.clip(s - p * per, 0, per - 1)))

    return pl.pallas_call(
        body, name=name, grid=(M // tm, steps),
        in_specs=[part_spec(p) for p in range(n_parts)] + [pl.BlockSpec((g, K, nc), lambda i, s: (s, 0, 0))],
        out_specs=pl.BlockSpec((tm, K), lambda i, s: (i, 0)),
        out_shape=jax.ShapeDtypeStruct((M, K), BF16),
        scratch_shapes=[pltpu.VMEM((tm, K), F32)], compiler_params=_cp(2, vmem),
    )(*parts, w3)


def _norm_mod_bwd(dh, x, mod, dx_res, row0, n_ctx, name):
    S, D = x.shape
    tm = _pick(S, [256, 128])

    def body(dh_ref, x_ref, mod_ref, res_ref, dx_ref, acc_ref):
        i = pl.program_id(0)

        @pl.when(i == 0)
        def _():
            acc_ref[...] = jnp.zeros_like(acc_ref)

        dx, upd = _norm_bwd_math(dh_ref[...].astype(F32), x_ref[...], mod_ref, res_ref[...], _rows(i, tm) < n_ctx, row0)
        dx_ref[...] = dx
        acc_ref[...] += upd

    rows = pl.BlockSpec((tm, D), lambda i: (i, 0))
    return pl.pallas_call(
        body, name=name, grid=(S // tm,),
        in_specs=[rows, rows, pl.BlockSpec((2, 6, D), lambda i: (0, 0, 0)), rows],
        out_specs=[rows, pl.BlockSpec((8, D), lambda i: (0, 0))],
        out_shape=[jax.ShapeDtypeStruct((S, D), F32), jax.ShapeDtypeStruct((8, D), F32)],
        compiler_params=_cp(1),
    )(dh, x, mod, dx_res)


def _gate_dgrad(dx, y, mod, gate_row, n_ctx, w2, *, tm, tn, name):
    M, N = dx.shape
    K = w2.shape[0]

    def body(dx_ref, y_ref, mod_ref, w_ref, da_ref, dy_ref, acc_ref):
        i = pl.program_id(0)
        j = pl.program_id(1)

        @pl.when((i == 0) & (j == 0))
        def _():
            acc_ref[...] = jnp.zeros_like(acc_ref)

        @pl.when(j == 0)
        def _():
            dxv = dx_ref[...]
            is_ctx = _rows(i, tm) < n_ctx
            g = jnp.where(is_ctx, mod_ref[0, gate_row:gate_row + 1, :], mod_ref[1, gate_row:gate_row + 1, :])
            dy_ref[...] = (g * dxv).astype(BF16)
            prod = dxv * y_ref[...].astype(F32)
            zero = jnp.zeros_like(prod)
            s_ctx = jnp.sum(jnp.where(is_ctx, prod, zero), axis=0, keepdims=True)
            s_lat = jnp.sum(jnp.where(is_ctx, zero, prod), axis=0, keepdims=True)
            rid = lax.broadcasted_iota(I32, (8, N), 0)
            acc_ref[...] += jnp.where(rid == 0, s_ctx, 0.0) + jnp.where(rid == 1, s_lat, 0.0)

        da_ref[...] = lax.dot_general(dy_ref[...], w_ref[...], NT_DIMS, preferred_element_type=F32).astype(BF16)

    rows = pl.BlockSpec((tm, N), lambda i, j: (i, 0))
    return pl.pallas_call(
        body, name=name, grid=(M // tm, K // tn),
        in_specs=[rows, rows, pl.BlockSpec((2, 6, N), lambda i, j: (0, 0, 0)),
                  pl.BlockSpec((tn, N), lambda i, j: (j, 0))],
        out_specs=[pl.BlockSpec((tm, tn), lambda i, j: (i, j)), rows, pl.BlockSpec((8, N), lambda i, j: (0, 0))],
        out_shape=[jax.ShapeDtypeStruct((M, K), BF16), jax.ShapeDtypeStruct((M, N), BF16),
                   jax.ShapeDtypeStruct((8, N), F32)],
        compiler_params=_cp(2, VMEM_MATMUL),
    )(dx, y, mod, w2)


def _mm_tn(a, parts, *, nb, tka, tn, ts, name, a_transposed=False):
    n_parts = len(parts)
    S, Ka = a.shape[::-1] if a_transposed else a.shape
    N = sum(p.shape[1] for p in parts)
    nc = N // nb
    q = nc // tn
    nk = S // ts
    g = 2 if (tn == nc and n_parts == 1 and _pairable(nb, nc)) else 1
    per = nb * q // n_parts // g

    def body(a_ref, *rest):
        dy_refs, o_ref, acc_ref = rest[:n_parts], rest[n_parts], rest[n_parts + 1]
        j = pl.program_id(0)
        k = pl.program_id(2)

        @pl.when(k == 0)
        def _():
            acc_ref[...] = jnp.zeros_like(acc_ref)

        for p in range(n_parts):
            @pl.when(j // per == p)
            def _(p=p):
                if a_transposed:
                    acc_ref[...] += jnp.dot(a_ref[...], dy_refs[p][...], preferred_element_type=F32)
                else:
                    acc_ref[...] += lax.dot_general(a_ref[...], dy_refs[p][...], TN_DIMS, preferred_element_type=F32)

        @pl.when(k == nk - 1)
        def _():
            for t in range(g):
                o_ref[t] = acc_ref[:, t * tn:(t + 1) * tn].astype(o_ref.dtype)

    def part_spec(p):
        return pl.BlockSpec((ts, g * tn), lambda j, ia, k: (jnp.where(j // per == p, k, 0),
                                                            jnp.clip(j - p * per, 0, per - 1)))

    return pl.pallas_call(
        body, name=name, grid=(nb * q // g, Ka // tka, nk),
        in_specs=[pl.BlockSpec((tka, ts), lambda j, ia, k: (ia, k)) if a_transposed
                  else pl.BlockSpec((ts, tka), lambda j, ia, k: (k, ia))] + [part_spec(p) for p in range(n_parts)],
        out_specs=pl.BlockSpec((g, tka, tn), lambda j, ia, k: (j // q, ia, j % q)),
        out_shape=jax.ShapeDtypeStruct((nb, Ka, nc), BF16),
        scratch_shapes=[pltpu.VMEM((tka, g * tn), F32)], compiler_params=_cp(3, VMEM_MATMUL),
    )(a, *parts)


def _norm_mod(x, mod, row0, n_ctx, name):
    S, D = x.shape
    tm = _pick(S, [256, 128])

    def body(x_ref, mod_ref, h_ref, ht_ref):
        xv = x_ref[...]
        r = lax.rsqrt(jnp.mean(xv * xv, axis=-1, keepdims=True) + EPS)
        is_ctx = _rows(pl.program_id(0), tm) < n_ctx
        sh = jnp.where(is_ctx, mod_ref[0, row0:row0 + 1, :], mod_ref[1, row0:row0 + 1, :])
        sc = jnp.where(is_ctx, mod_ref[0, row0 + 1:row0 + 2, :], mod_ref[1, row0 + 1:row0 + 2, :])
        hv = xv * r * (1.0 + sc) + sh
        h_ref[...] = hv.astype(BF16)
        ht_ref[...] = hv.T.astype(BF16)

    return pl.pallas_call(
        body, name=name, grid=(S // tm,),
        in_specs=[pl.BlockSpec((tm, D), lambda i: (i, 0)), pl.BlockSpec((2, 6, D), lambda i: (0, 0, 0))],
        out_specs=[pl.BlockSpec((tm, D), lambda i: (i, 0)), pl.BlockSpec((D, tm), lambda i: (0, i))],
        out_shape=[jax.ShapeDtypeStruct((S, D), BF16), jax.ShapeDtypeStruct((D, S), BF16)], compiler_params=_cp(1),
    )(x, mod)


def _loss_grad(x, target, n_ctx, name):
    S, D = x.shape
    tm = _pick(n_ctx, [256, 128])
    nct = n_ctx // tm

    def body(x_ref, t_ref, dx_ref, tot_ref, acc_ref):
        i = pl.program_id(0)

        @pl.when(i == 0)
        def _():
            acc_ref[...] = jnp.zeros_like(acc_ref)

        @pl.when(i < nct)
        def _():
            dx_ref[...] = jnp.zeros_like(dx_ref)

        @pl.when(i >= nct)
        def _():
            err = x_ref[...] - t_ref[...]
            dx_ref[...] = err * (1.0 / D)
            acc_ref[...] += jnp.sum(err * err, axis=0, keepdims=True)

        @pl.when(i == S // tm - 1)
        def _():
            tot = jnp.sum(acc_ref[...], axis=1, keepdims=True) * (0.5 / D)
            tot_ref[...] = jnp.broadcast_to(tot, tot_ref.shape)

    return pl.pallas_call(
        body, name=name, grid=(S // tm,),
        in_specs=[pl.BlockSpec((tm, D), lambda i: (i, 0)),
                  pl.BlockSpec((tm, D), lambda i: (jnp.maximum(i - nct, 0), 0))],
        out_specs=[pl.BlockSpec((tm, D), lambda i: (i, 0)), pl.BlockSpec((1, 128), lambda i: (0, 0))],
        out_shape=[jax.ShapeDtypeStruct((S, D), F32), jax.ShapeDtypeStruct((1, 128), F32)],
        scratch_shapes=[pltpu.VMEM((1, D), F32)], compiler_params=_cp(1),
    )(x, target)


def _halo_specs(tm, tc, S, col_off):
    per = tm // HALO
    last = S // HALO - 1
    return [pl.BlockSpec((HALO, tc), lambda j, i: (jnp.maximum(i * per - 1, 0), j + col_off)),
            pl.BlockSpec((tm, tc), lambda j, i: (i, j + col_off)),
            pl.BlockSpec((HALO, tc), lambda j, i: (jnp.minimum((i + 1) * per, last), j + col_off))]


def _ext(p_ref, m_ref, n_ref):
    return jnp.concatenate([p_ref[...], m_ref[...], n_ref[...]], axis=0).astype(F32)


def _links(i, tm, S, n_ctx):
    n = tm + 2 * HALO
    rid = i * tm - HALO + lax.broadcasted_iota(I32, (n, 1), 0)
    has_prev = (rid != 0) & (rid != n_ctx)
    has_next = (rid != n_ctx - 1) & (rid != S - 1)
    return has_prev, has_next


def _edge_tile(i, tm, S, n_ctx):
    tiles = sorted({t for b in (0, n_ctx - 1, n_ctx, S - 1) for t in range(S // tm)
                    if t * tm - HALO <= b < (t + 1) * tm + HALO})
    edge = i == tiles[0]
    for t in tiles[1:]:
        edge = edge | (i == t)
    return edge


def _up(x, keep=None):
    y = pltpu.roll(x, 1, 0)
    return y if keep is None else jnp.where(keep, y, 0.0)


def _dn(x, keep=None):
    y = pltpu.roll(x, x.shape[0] - 1, 0)
    return y if keep is None else jnp.where(keep, y, 0.0)


def _conv3(x, w_ref, has_prev, has_next):
    return w_ref[0:1, :] * _up(x, has_prev) + w_ref[1:2, :] * x + w_ref[2:3, :] * _dn(x, has_next)


def _conv3_t(d, w_ref, has_prev, has_next):
    return w_ref[0:1, :] * _dn(d, has_next) + w_ref[1:2, :] * d + w_ref[2:3, :] * _up(d, has_prev)


def _conv3_wgrad(d, x, has_prev, has_next, extra=None):
    c = slice(HALO, d.shape[0] - HALO)
    taps = [_up(x, has_prev), x, _dn(x, has_next)]
    sums = [jnp.sum((d * t)[c], axis=0, keepdims=True) for t in taps]
    if extra is not None:
        sums.append(jnp.sum(extra[c], axis=0, keepdims=True))
    rid = lax.broadcasted_iota(I32, (8, d.shape[1]), 0)
    upd = jnp.zeros((8, d.shape[1]), F32)
    for k, s in enumerate(sums):
        upd = upd + jnp.where(rid == k, s, 0.0)
    return upd


def _sigmoid(x):
    return 1.0 / (1.0 + jnp.exp(-x))


def _ffn_act(u, conv_w, conv_b, l, n_ctx, name):
    S, F2 = u.shape
    F = F2 // 2
    tm = _pick(S, [384, 256, 128])
    tc = _pick(F, [1408, 512, 256, 128])
    nj = F // tc

    def body(gp, gm, gn, v_ref, w_ref, b_ref, a_ref):
        i = pl.program_id(1)

        def compute(has_prev, has_next):
            gc = _conv3(_ext(gp, gm, gn), w_ref, has_prev, has_next)[HALO:HALO + tm] + b_ref[...]
            a_ref[...] = (gc * _sigmoid(gc) * v_ref[...].astype(F32)).astype(BF16)

        edge = _edge_tile(i, tm, S, n_ctx)
        pl.when(edge)(lambda: compute(*_links(i, tm, S, n_ctx)))
        pl.when(jnp.logical_not(edge))(lambda: compute(None, None))

    return pl.pallas_call(
        body, name=name, grid=(nj, S // tm),
        in_specs=_halo_specs(tm, tc, S, 0) + [
            pl.BlockSpec((tm, tc), lambda j, i: (i, j + nj)),
            pl.BlockSpec((None, 3, tc), lambda j, i: (l, 0, j)),
            pl.BlockSpec((None, 1, tc), lambda j, i: (l, 0, j))],
        out_specs=pl.BlockSpec((tm, tc), lambda j, i: (i, j)),
        out_shape=jax.ShapeDtypeStruct((S, F), BF16), compiler_params=_cp(2),
    )(u, u, u, u, conv_w, conv_b)


def _ffn_act_bwd(u, da, conv_w, conv_b, l, n_ctx, name):
    S, F2 = u.shape
    F = F2 // 2
    tm = _pick(S, [384, 256, 128])
    tc = _pick(F, [1408, 512, 256, 128])
    nj = F // tc

    def body(gp, gm, gn, vp, vm, vn, dp, dm, dn_, w_ref, b_ref, dg_ref, dv_ref, acc_ref):
        i = pl.program_id(1)

        @pl.when(i == 0)
        def _():
            acc_ref[...] = jnp.zeros_like(acc_ref)

        def compute(has_prev, has_next):
            g = _ext(gp, gm, gn)
            val = _ext(vp, vm, vn)
            d_a = _ext(dp, dm, dn_)
            gc = _conv3(g, w_ref, has_prev, has_next) + b_ref[...]
            sg = _sigmoid(gc)
            dgc = d_a * val * (sg * (1.0 + gc * (1.0 - sg)))
            c = slice(HALO, HALO + tm)
            dv_ref[...] = (d_a * gc * sg)[c].astype(BF16)
            dg_ref[...] = _conv3_t(dgc, w_ref, has_prev, has_next)[c].astype(BF16)
            acc_ref[...] += _conv3_wgrad(dgc, g, has_prev, has_next, extra=dgc)

        edge = _edge_tile(i, tm, S, n_ctx)
        pl.when(edge)(lambda: compute(*_links(i, tm, S, n_ctx)))
        pl.when(jnp.logical_not(edge))(lambda: compute(None, None))

    return pl.pallas_call(
        body, name=name, grid=(nj, S // tm),
        in_specs=_halo_specs(tm, tc, S, 0) + _halo_specs(tm, tc, S, nj) + _halo_specs(tm, tc, S, 0) + [
            pl.BlockSpec((None, 3, tc), lambda j, i: (l, 0, j)),
            pl.BlockSpec((None, 1, tc), lambda j, i: (l, 0, j))],
        out_specs=[pl.BlockSpec((tm, tc), lambda j, i: (i, j))] * 2 + [pl.BlockSpec((8, tc), lambda j, i: (0, j))],
        out_shape=[jax.ShapeDtypeStruct((S, F), BF16)] * 2 + [jax.ShapeDtypeStruct((8, F), F32)],
        compiler_params=_cp(2),
    )(u, u, u, u, u, u, da, da, da, conv_w, conv_b)


def _sc_act(u, conv_w, l, n_ctx, name):
    S, D3 = u.shape
    D = D3 // 3
    tm = _pick(S, [384, 256, 128])
    tc = _pick(D, [1024, 512, 256, 128])
    nj = D // tc

    def body(b_ref, cp, cm, cn, vp, vm, vn, w_ref, z_ref):
        has_prev, has_next = _links(pl.program_id(1), tm, S, n_ctx)
        t = _ext(cp, cm, cn) * _ext(vp, vm, vn)
        cv = _conv3(t, w_ref, has_prev, has_next)[HALO:HALO + tm]
        z_ref[...] = (b_ref[...].astype(F32) * cv).astype(BF16)

    return pl.pallas_call(
        body, name=name, grid=(nj, S // tm),
        in_specs=[pl.BlockSpec((tm, tc), lambda j, i: (i, j))] + _halo_specs(tm, tc, S, nj)
        + _halo_specs(tm, tc, S, 2 * nj) + [pl.BlockSpec((None, 3, tc), lambda j, i: (l, 0, j))],
        out_specs=pl.BlockSpec((tm, tc), lambda j, i: (i, j)),
        out_shape=jax.ShapeDtypeStruct((S, D), BF16), compiler_params=_cp(2),
    )(u, u, u, u, u, u, u, conv_w)


def _sc_act_bwd(u, dz, conv_w, l, n_ctx, name):
    S, D3 = u.shape
    D = D3 // 3
    tm = _pick(S, [128])
    tc = D
    nj = 1

    def body(bp, bm, bn, cp, cm, cn, vp, vm, vn, zp, zm, zn, w_ref, du_ref, acc_ref):
        db_ref, dc_ref, dv_ref = du_ref.at[:, 0:D], du_ref.at[:, D:2 * D], du_ref.at[:, 2 * D:3 * D]
        i = pl.program_id(1)

        @pl.when(i == 0)
        def _():
            acc_ref[...] = jnp.zeros_like(acc_ref)

        has_prev, has_next = _links(i, tm, S, n_ctx)
        gb = _ext(bp, bm, bn)
        gcv = _ext(cp, cm, cn)
        val = _ext(vp, vm, vn)
        d_z = _ext(zp, zm, zn)
        t = gcv * val
        c = slice(HALO, HALO + tm)
        db_ref[...] = (d_z * _conv3(t, w_ref, has_prev, has_next))[c].astype(BF16)
        dcv = d_z * gb
        dt = _conv3_t(dcv, w_ref, has_prev, has_next)
        dc_ref[...] = (dt * val)[c].astype(BF16)
        dv_ref[...] = (dt * gcv)[c].astype(BF16)
        acc_ref[...] += _conv3_wgrad(dcv, t, has_prev, has_next)

    return pl.pallas_call(
        body, name=name, grid=(nj, S // tm),
        in_specs=_halo_specs(tm, tc, S, 0) + _halo_specs(tm, tc, S, nj) + _halo_specs(tm, tc, S, 2 * nj)
        + _halo_specs(tm, tc, S, 0) + [pl.BlockSpec((None, 3, tc), lambda j, i: (l, 0, j))],
        out_specs=[pl.BlockSpec((tm, D3), lambda j, i: (i, 0)), pl.BlockSpec((8, tc), lambda j, i: (0, j))],
        out_shape=[jax.ShapeDtypeStruct((S, D3), BF16), jax.ShapeDtypeStruct((8, D), F32)],
        compiler_params=_cp(2),
    )(u, u, u, u, u, u, u, u, u, dz, dz, dz, conv_w)


def _rope_tables(T, n_ctx):
    rows = T // GRID_W
    pairs = HEAD_DIM // 4
    row = jnp.repeat(jnp.arange(rows), GRID_W).astype(F32)
    col = jnp.tile(jnp.arange(GRID_W), rows).astype(F32)
    inv = ROPE_BASE ** (-jnp.arange(pairs, dtype=F32) / pairs)
    ang = jnp.concatenate([row[:, None] * inv, row[:, None] * inv, col[:, None] * inv, col[:, None] * inv], axis=1)
    cos, sin = jnp.cos(ang), jnp.sin(ang)
    first = (jnp.arange(HEAD_DIM) % (2 * pairs)) < pairs
    sin_a = jnp.where(first, -sin, 0.0)
    sin_b = jnp.where(first, 0.0, sin)
    pad = jnp.zeros((n_ctx, HEAD_DIM), F32)
    return (jnp.concatenate([pad + 1.0, cos], axis=0), jnp.concatenate([pad, sin_a], axis=0),
            jnp.concatenate([pad, sin_b], axis=0))


def _qk_prep(qkv, tabs, gains, l, n_q, n_kv, name):
    S = qkv.shape[0]
    W = qkv.shape[1]
    tm = _pick(S, [256, 128])
    cos, sin_a, sin_b = tabs

    def body(x_ref, cos_ref, sa_ref, sb_ref, g_ref, q_ref, k_ref):
        cs, sa, sb = cos_ref[...], sa_ref[...], sb_ref[...]
        for h in range(n_q + n_kv):
            xv = x_ref[:, h * 128:(h + 1) * 128].astype(F32)
            r = lax.rsqrt(jnp.mean(xv * xv, axis=-1, keepdims=True) + EPS)
            gain = g_ref[0:1, :] if h < n_q else g_ref[1:2, :]
            y = xv * r * gain
            out = (y * cs + pltpu.roll(y, 96, 1) * sa + pltpu.roll(y, 32, 1) * sb).astype(BF16)
            if h < n_q:
                q_ref[:, h * 128:(h + 1) * 128] = out
            else:
                k_ref[:, (h - n_q) * 128:(h - n_q + 1) * 128] = out

    tspec = pl.BlockSpec((tm, 128), lambda i: (i, 0))
    return pl.pallas_call(
        body, name=name, grid=(S // tm,),
        in_specs=[pl.BlockSpec((tm, W), lambda i: (i, 0)), tspec, tspec, tspec,
                  pl.BlockSpec((None, 2, 128), lambda i: (l, 0, 0))],
        out_specs=[pl.BlockSpec((tm, n_q * 128), lambda i: (i, 0)), pl.BlockSpec((tm, n_kv * 128), lambda i: (i, 0))],
        out_shape=[jax.ShapeDtypeStruct((S, n_q * 128), BF16), jax.ShapeDtypeStruct((S, n_kv * 128), BF16)],
        compiler_params=_cp(1),
    )(qkv, cos, sin_a, sin_b, gains)


def _qk_prep_bwd(dq, dk, dv, qkv, tabs, gains, l, n_q, n_kv, name):
    S, W = qkv.shape
    tm = _pick(S, [256, 128])
    cos, sin_a, sin_b = tabs

    def body(dq_ref, dk_ref, dv_ref, x_ref, cos_ref, sa_ref, sb_ref, g_ref, o_ref, acc_ref):
        @pl.when(pl.program_id(0) == 0)
        def _():
            acc_ref[...] = jnp.zeros_like(acc_ref)

        cs, sa, sb = cos_ref[...], sa_ref[...], sb_ref[...]
        dgq = jnp.zeros((1, 128), F32)
        dgk = jnp.zeros((1, 128), F32)
        for h in range(n_q + n_kv):
            if h < n_q:
                d_out = dq_ref[:, h * 128:(h + 1) * 128]
                gain = g_ref[0:1, :]
            else:
                d_out = dk_ref[:, (h - n_q) * 128:(h - n_q + 1) * 128]
                gain = g_ref[1:2, :]
            dy = d_out * cs + pltpu.roll(d_out * sa, 32, 1) + pltpu.roll(d_out * sb, 96, 1)
            xv = x_ref[:, h * 128:(h + 1) * 128].astype(F32)
            r = lax.rsqrt(jnp.mean(xv * xv, axis=-1, keepdims=True) + EPS)
            xhat = xv * r
            dg = jnp.sum(dy * xhat, axis=0, keepdims=True)
            if h < n_q:
                dgq = dgq + dg
            else:
                dgk = dgk + dg
            dxhat = dy * gain
            dx = r * (dxhat - xhat * jnp.mean(dxhat * xhat, axis=-1, keepdims=True))
            o_ref[:, h * 128:(h + 1) * 128] = dx.astype(BF16)
        v0 = (n_q + n_kv) * 128
        o_ref[:, v0:] = dv_ref[...].astype(BF16)
        rid = lax.broadcasted_iota(I32, (8, 128), 0)
        acc_ref[...] += jnp.where(rid == 0, dgq, 0.0) + jnp.where(rid == 1, dgk, 0.0)

    tspec = pl.BlockSpec((tm, 128), lambda i: (i, 0))
    return pl.pallas_call(
        body, name=name, grid=(S // tm,),
        in_specs=[pl.BlockSpec((tm, n_q * 128), lambda i: (i, 0)), pl.BlockSpec((tm, n_kv * 128), lambda i: (i, 0)),
                  pl.BlockSpec((tm, n_kv * 128), lambda i: (i, 0)), pl.BlockSpec((tm, W), lambda i: (i, 0)),
                  tspec, tspec, tspec, pl.BlockSpec((None, 2, 128), lambda i: (l, 0, 0))],
        out_specs=[pl.BlockSpec((tm, W), lambda i: (i, 0)), pl.BlockSpec((8, 128), lambda i: (0, 0))],
        out_shape=[jax.ShapeDtypeStruct((S, W), BF16), jax.ShapeDtypeStruct((8, 128), F32)],
        compiler_params=_cp(1),
    )(dq, dk, dv, qkv, cos, sin_a, sin_b, gains)


def _band_specs(width, col, nb, n_ctx):
    return [pl.BlockSpec((BLK, width), lambda i: (jnp.maximum(i - 1, 0), col)),
            pl.BlockSpec((BLK, width), lambda i: (i, col)),
            pl.BlockSpec((BLK, width), lambda i: (jnp.minimum(i + 1, nb - 1), col)),
            pl.BlockSpec((n_ctx, width), lambda i: (0, col))]


def _q_side_mask(i, S, n_ctx):
    shape = (GROUP * BLK, 3 * BLK + n_ctx)
    a = lax.broadcasted_iota(I32, shape, 0) & (BLK - 1)
    kk = lax.broadcasted_iota(I32, shape, 1)
    rq = i * BLK + a
    rk = (i - 1) * BLK + kk
    band = (rq >= n_ctx) & (rk >= n_ctx) & (rk < S) & (jnp.abs(rq - rk) <= WINDOW)
    return (kk >= 3 * BLK) | band


def _stack_heads(ref, g):
    return jnp.concatenate([ref[:, (GROUP * g + hh) * 128:(GROUP * g + hh + 1) * 128] for hh in range(GROUP)], axis=0)


def _stack_cols(ref, g):
    return jnp.concatenate([ref[:, GROUP * g + hh:GROUP * g + hh + 1] for hh in range(GROUP)], axis=0)


def _sink_col(sink_ref, l, g):
    return jnp.concatenate([jnp.full((BLK, 1), sink_ref[l, GROUP * g + hh], F32) for hh in range(GROUP)], axis=0)


def _attn_fwd(q, k, qkv, sink, l, n_ctx, name):
    S, DQ = q.shape
    DK = k.shape[1]
    n_kv = DK // 128
    nb = S // BLK
    vcol = (DQ + DK) // DK
    scale = HEAD_DIM ** -0.5

    def body(sink_ref, q_ref, kp, kc, kn, kx, vp, vc, vn, vx, o_ref, lse_ref):
        i = pl.program_id(0)
        mask = _q_side_mask(i, S, n_ctx)
        lane = lax.broadcasted_iota(I32, (BLK, 128), 1)
        lse_tile = jnp.zeros((BLK, 128), F32)
        outs = []
        for g in range(n_kv):
            sl = slice(g * 128, (g + 1) * 128)
            kcat = jnp.concatenate([kp[:, sl], kc[:, sl], kn[:, sl], kx[:, sl]], axis=0)
            vcat = jnp.concatenate([vp[:, sl], vc[:, sl], vn[:, sl], vx[:, sl]], axis=0)
            s = lax.dot_general(_stack_heads(q_ref, g), kcat, NT_DIMS, preferred_element_type=F32) * scale
            s = jnp.where(mask, s, NEG)
            sk = _sink_col(sink_ref, l, g)
            m = jnp.maximum(jnp.max(s, axis=1, keepdims=True), sk)
            e = jnp.exp(s - m)
            den = jnp.sum(e, axis=1, keepdims=True) + jnp.exp(sk - m)
            p = (e / den).astype(BF16)
            o = jnp.dot(p, vcat, preferred_element_type=F32)
            lse = m + jnp.log(den)
            for hh in range(GROUP):
                h = GROUP * g + hh
                outs.append(o[hh * BLK:(hh + 1) * BLK].astype(BF16))
                lse_tile = jnp.where(lane == h, lse[hh * BLK:(hh + 1) * BLK], lse_tile)
        o_ref[...] = jnp.concatenate(outs, axis=1)
        lse_ref[...] = lse_tile

    return pl.pallas_call(
        body, name=name, grid=(nb,),
        in_specs=[pl.BlockSpec(memory_space=pltpu.SMEM), pl.BlockSpec((BLK, DQ), lambda i: (i, 0))]
        + _band_specs(DK, 0, nb, n_ctx) + _band_specs(DK, vcol, nb, n_ctx),
        out_specs=[pl.BlockSpec((BLK, DQ), lambda i: (i, 0)), pl.BlockSpec((BLK, 128), lambda i: (i, 0))],
        out_shape=[jax.ShapeDtypeStruct((S, DQ), BF16), jax.ShapeDtypeStruct((S, 128), F32)],
        compiler_params=_cp(1),
    )(sink, q, k, k, k, k, qkv, qkv, qkv, qkv)


def _attn_bwd_q(q, k, qkv, o, do, lse, sink, l, n_ctx, name):
    S, DQ = q.shape
    DK = k.shape[1]
    n_kv = DK // 128
    nb = S // BLK
    vcol = (DQ + DK) // DK
    scale = HEAD_DIM ** -0.5

    def body(sink_ref, q_ref, kp, kc, kn, kx, vp, vc, vn, vx, o_ref, do_ref, lse_ref,
             dq_ref, delta_ref, dkx_ref, dvx_ref, dsink_ref):
        i = pl.program_id(0)

        @pl.when(i == 0)
        def _():
            dkx_ref[...] = jnp.zeros_like(dkx_ref)
            dvx_ref[...] = jnp.zeros_like(dvx_ref)
            dsink_ref[...] = jnp.zeros_like(dsink_ref)

        mask = _q_side_mask(i, S, n_ctx)
        lane = lax.broadcasted_iota(I32, (BLK, 128), 1)
        lane8 = lax.broadcasted_iota(I32, (8, 128), 1)
        row8 = lax.broadcasted_iota(I32, (8, 128), 0)
        delta_tile = jnp.zeros((BLK, 128), F32)
        dsink_upd = jnp.zeros((8, 128), F32)
        dqs, dkx_upd, dvx_upd = [], [], []
        for g in range(n_kv):
            sl = slice(g * 128, (g + 1) * 128)
            kcat = jnp.concatenate([kp[:, sl], kc[:, sl], kn[:, sl], kx[:, sl]], axis=0)
            vcat = jnp.concatenate([vp[:, sl], vc[:, sl], vn[:, sl], vx[:, sl]], axis=0)
            qg = _stack_heads(q_ref, g)
            dog = _stack_heads(do_ref, g)
            delta = jnp.sum(dog.astype(F32) * _stack_heads(o_ref, g).astype(F32), axis=1, keepdims=True)
            lse_g = _stack_cols(lse_ref, g)
            s = lax.dot_general(qg, kcat, NT_DIMS, preferred_element_type=F32) * scale
            p = jnp.exp(jnp.where(mask, s - lse_g, NEG))
            dp = lax.dot_general(dog, vcat, NT_DIMS, preferred_element_type=F32)
            ds = (p * (dp - delta) * scale).astype(BF16)
            dqg = jnp.dot(ds, kcat, preferred_element_type=F32)
            dkx_upd.append(lax.dot_general(ds[:, 3 * BLK:], qg, TN_DIMS, preferred_element_type=F32))
            dvx_upd.append(lax.dot_general(p.astype(BF16)[:, 3 * BLK:], dog, TN_DIMS, preferred_element_type=F32))
            dsk = -jnp.exp(_sink_col(sink_ref, l, g) - lse_g) * delta
            for hh in range(GROUP):
                h = GROUP * g + hh
                rs = slice(hh * BLK, (hh + 1) * BLK)
                dqs.append(dqg[rs])
                delta_tile = jnp.where(lane == h, delta[rs], delta_tile)
                tot = jnp.sum(dsk[rs], axis=0, keepdims=True)
                dsink_upd = dsink_upd + jnp.where((lane8 == h) & (row8 == 0), tot, 0.0)
        dq_ref[...] = jnp.concatenate(dqs, axis=1)
        dkx_ref[...] += jnp.concatenate(dkx_upd, axis=1)
        dvx_ref[...] += jnp.concatenate(dvx_upd, axis=1)
        delta_ref[...] = delta_tile
        dsink_ref[...] += dsink_upd

    blk = pl.BlockSpec((BLK, DQ), lambda i: (i, 0))
    stat = pl.BlockSpec((BLK, 128), lambda i: (i, 0))
    return pl.pallas_call(
        body, name=name, grid=(nb,),
        in_specs=[pl.BlockSpec(memory_space=pltpu.SMEM), blk] + _band_specs(DK, 0, nb, n_ctx)
        + _band_specs(DK, vcol, nb, n_ctx) + [blk, blk, stat],
        out_specs=[blk, stat, pl.BlockSpec((n_ctx, DK), lambda i: (0, 0)), pl.BlockSpec((n_ctx, DK), lambda i: (0, 0)),
                   pl.BlockSpec((8, 128), lambda i: (0, 0))],
        out_shape=[jax.ShapeDtypeStruct((S, DQ), F32), jax.ShapeDtypeStruct((S, 128), F32),
                   jax.ShapeDtypeStruct((n_ctx, DK), F32), jax.ShapeDtypeStruct((n_ctx, DK), F32),
                   jax.ShapeDtypeStruct((8, 128), F32)],
        compiler_params=_cp(1),
    )(sink, q, k, k, k, k, qkv, qkv, qkv, qkv, o, do, lse)


def _attn_bwd_kv(q, k, qkv, do, lse, delta, dkx, dvx, n_ctx, name):
    S, DQ = q.shape
    DK = k.shape[1]
    n_kv = DK // 128
    nb = S // BLK
    nctx_b = n_ctx // BLK
    vcol = (DQ + DK) // DK
    scale = HEAD_DIM ** -0.5

    def three(width):
        return [pl.BlockSpec((BLK, width), lambda j: (jnp.maximum(j - 1, 0), 0)),
                pl.BlockSpec((BLK, width), lambda j: (j, 0)),
                pl.BlockSpec((BLK, width), lambda j: (jnp.minimum(j + 1, nb - 1), 0))]

    def body(k_ref, v_ref, qp, qc, qn, dop, doc, don, lp, lc, ln, dlp, dlc, dln, dkx_ref, dvx_ref, dk_ref, dv_ref):
        j = pl.program_id(0)

        @pl.when(j < nctx_b)
        def _():
            dk_ref[...] = dkx_ref[...]
            dv_ref[...] = dvx_ref[...]

        @pl.when(j >= nctx_b)
        def _():
            shape = (3 * GROUP * BLK, BLK)
            t = lax.broadcasted_iota(I32, shape, 0)
            rq = (j - 1 + t // (GROUP * BLK)) * BLK + (t & (BLK - 1))
            rk = j * BLK + lax.broadcasted_iota(I32, shape, 1)
            valid = (rq >= n_ctx) & (rq < S) & (jnp.abs(rq - rk) <= WINDOW)
            dks, dvs = [], []
            for g in range(n_kv):
                sl = slice(g * 128, (g + 1) * 128)
                qcat = jnp.concatenate([_stack_heads(r, g) for r in (qp, qc, qn)], axis=0)
                docat = jnp.concatenate([_stack_heads(r, g) for r in (dop, doc, don)], axis=0)
                lse_c = jnp.concatenate([_stack_cols(r, g) for r in (lp, lc, ln)], axis=0)
                delta_c = jnp.concatenate([_stack_cols(r, g) for r in (dlp, dlc, dln)], axis=0)
                s = lax.dot_general(qcat, k_ref[:, sl], NT_DIMS, preferred_element_type=F32) * scale
                p = jnp.exp(jnp.where(valid, s - lse_c, NEG))
                dp = lax.dot_general(docat, v_ref[:, sl], NT_DIMS, preferred_element_type=F32)
                ds = (p * (dp - delta_c) * scale).astype(BF16)
                dks.append(lax.dot_general(ds, qcat, TN_DIMS, preferred_element_type=F32))
                dvs.append(lax.dot_general(p.astype(BF16), docat, TN_DIMS, preferred_element_type=F32))
            dk_ref[...] = jnp.concatenate(dks, axis=1)
            dv_ref[...] = jnp.concatenate(dvs, axis=1)

    cspec = pl.BlockSpec((BLK, DK), lambda j: (jnp.minimum(j, nctx_b - 1), 0))
    return pl.pallas_call(
        body, name=name, grid=(nb,),
        in_specs=[pl.BlockSpec((BLK, DK), lambda j: (j, 0)), pl.BlockSpec((BLK, DK), lambda j: (j, vcol))]
        + three(DQ) + three(DQ) + three(128) + three(128) + [cspec, cspec],
        out_specs=[pl.BlockSpec((BLK, DK), lambda j: (j, 0))] * 2,
        out_shape=[jax.ShapeDtypeStruct((S, DK), F32)] * 2, compiler_params=_cp(1),
    )(k, qkv, q, q, q, do, do, do, lse, lse, lse, delta, delta, delta, dkx, dvx)


def _ada_fwd(cond, w_ada, b_cols, name):
    lyr, D, C = w_ada.shape
    tc = _pick(C, [512, 384, 256, 128])

    def body(c_ref, w_ref, b_ref, o_ref):
        cv = c_ref[...]
        act = cv * _sigmoid(cv)
        o_ref[...] = jnp.dot(act, w_ref[...], preferred_element_type=F32,
                             precision=lax.Precision.HIGHEST) + b_ref[...]

    return pl.pallas_call(
        body, name=name, grid=(lyr, C // tc),
        in_specs=[pl.BlockSpec((16, D), lambda l, j: (0, 0)),
                  pl.BlockSpec((None, D, tc), lambda l, j: (l, 0, j)),
                  pl.BlockSpec((None, 1, tc), lambda l, j: (l, 0, j))],
        out_specs=pl.BlockSpec((None, 16, tc), lambda l, j: (l, 0, j)),
        out_shape=jax.ShapeDtypeStruct((lyr, 16, C), F32), compiler_params=_cp(2),
    )(cond, w_ada, b_cols)


def _ada_bwd(cond, d_out, w_ada, name):
    lyr, D, C = w_ada.shape
    tc = _pick(C, [512, 384, 256, 128])

    def body(c_ref, d_ref, w_ref, gw_ref, dc_ref):
        @pl.when((pl.program_id(0) == 0) & (pl.program_id(1) == 0))
        def _():
            dc_ref[...] = jnp.zeros_like(dc_ref)

        cv = c_ref[...]
        act = cv * _sigmoid(cv)
        dv = d_ref[...]
        gw_ref[...] = lax.dot_general(act, dv, TN_DIMS, preferred_element_type=F32, precision=lax.Precision.HIGHEST)
        dc_ref[...] += lax.dot_general(dv, w_ref[...], NT_DIMS, preferred_element_type=F32,
                                       precision=lax.Precision.HIGHEST)

    return pl.pallas_call(
        body, name=name, grid=(lyr, C // tc),
        in_specs=[pl.BlockSpec((16, D), lambda l, j: (0, 0)),
                  pl.BlockSpec((None, 16, tc), lambda l, j: (l, 0, j)),
                  pl.BlockSpec((None, D, tc), lambda l, j: (l, 0, j))],
        out_specs=[pl.BlockSpec((None, D, tc), lambda l, j: (l, 0, j)), pl.BlockSpec((16, D), lambda l, j: (0, 0))],
        out_shape=[jax.ShapeDtypeStruct((lyr, D, C), F32), jax.ShapeDtypeStruct((16, D), F32)],
        compiler_params=_cp(2),
    )(cond, d_out, w_ada)


def _sum_rows(d_rows, name):
    lyr, r, C = d_rows.shape

    def body(d_ref, o_ref):
        o_ref[...] = jnp.sum(d_ref[...], axis=0, keepdims=True)

    return pl.pallas_call(
        body, name=name, grid=(lyr,),
        in_specs=[pl.BlockSpec((None, r, C), lambda l: (l, 0, 0))],
        out_specs=pl.BlockSpec((None, 1, C), lambda l: (l, 0, 0)),
        out_shape=jax.ShapeDtypeStruct((lyr, 1, C), F32), compiler_params=_cp(1),
    )(d_rows)


def _cctx_grad(gathered, c_ctx_row, name):
    D = gathered.shape[1]

    def body(g_ref, c_ref, o_ref):
        acc = g_ref[0:16, :]
        for d in range(1, N_DEV):
            acc = acc + g_ref[16 * d:16 * (d + 1), :]
        cv = c_ref[...]
        sg = _sigmoid(cv)
        o_ref[...] = acc[8:16] * (sg * (1.0 + cv * (1.0 - sg)))

    return pl.pallas_call(
        body, name=name, out_shape=jax.ShapeDtypeStruct((8, D), F32),
        compiler_params=pltpu.CompilerParams(vmem_limit_bytes=VMEM_ELEMENTWISE),
    )(gathered, c_ctx_row)


def _pad_rows(a, rows):
    return jnp.concatenate([a, jnp.zeros((rows - a.shape[0],) + a.shape[1:], a.dtype)], axis=0)


def kernel(x, c, ctx, c_ctx, w_ada, b_ada, attn_w_qkv, attn_w_o, attn_q_gain, attn_k_gain, attn_sink, sc_w_in, sc_conv, sc_w_out, ffn_w_up, ffn_conv, ffn_conv_b, ffn_w_down, loss_target, m_c_ctx, m_w_ada, m_b_ada, m_attn_w_qkv, m_attn_w_o, m_attn_q_gain, m_attn_k_gain, m_attn_sink, m_sc_w_in, m_sc_conv, m_sc_w_out, m_ffn_w_up, m_ffn_conv, m_ffn_conv_b, m_ffn_w_down, v_c_ctx, v_w_ada, v_b_ada, v_attn_w_qkv, v_attn_w_o, v_attn_q_gain, v_attn_k_gain, v_attn_sink, v_sc_w_in, v_sc_conv, v_sc_w_out, v_ffn_w_up, v_ffn_conv, v_ffn_conv_b, v_ffn_w_down):
    T, D = x.shape[1], x.shape[2]
    L = ctx.shape[1]
    S = L + T
    depth = w_ada.shape[0]
    F = ffn_conv_b.shape[1]
    n_q = D // HEAD_DIM
    n_kv = n_q // GROUP
    ada_c = w_ada.shape[2]
    assert L % BLK == 0 and T % BLK == 0 and ada_c * N_DEV == 6 * D

    px, py, pc = _my_pos()
    me = 4 * px + 2 * py + pc
    me_idx = jnp.reshape(me, (1,)).astype(I32)

    tm_mm = _pick(S, [768, 704, 384, 256, 128])
    ts_tn = _pick(S, [2112, 1056, 768, 384, 256, 128])
    tm_half = _pick(S, [384, 256, 128])
    ts_lane = _pick(S, [2816, 768, 384, 256, 128])

    c_all = _gather_small(_pad_rows(c, 8), "gather_cond")
    cond = jnp.concatenate([c_all[0::8], c_ctx[None, :], jnp.zeros((7, D), F32)], axis=0)
    b_cols = lax.dynamic_slice_in_dim(b_ada, me * ada_c, ada_c, axis=1)[:, None, :]
    ada_mine = _ada_fwd(cond, w_ada, b_cols, "ada_fwd")
    ada_all = _gather_small(ada_mine.reshape(depth * 16, ada_c), "gather_ada")
    ada_all = ada_all.reshape(N_DEV, depth, 16, ada_c)
    ada_rows = jnp.transpose(ada_all, (1, 2, 0, 3)).reshape(depth, 16, 6, D)
    mod_lat = lax.dynamic_index_in_dim(ada_rows, me, axis=1, keepdims=False)
    mods = jnp.stack([ada_rows[:, 8], mod_lat], axis=1)

    gathered = [None] * depth
    tabs = _rope_tables(T, L)
    gains = jnp.stack([attn_q_gain, attn_k_gain], axis=1)
    conv_b3 = ffn_conv_b[:, None, :]
    sc_conv_all = _gather_small(_pad_rows(sc_conv.reshape(-1, sc_conv.shape[2]), 8), "gather_scconv")
    ffn_conv_all = _gather_small(_pad_rows(ffn_conv.reshape(-1, ffn_conv.shape[2]), 16), "gather_ffnconv")
    n_sc = sc_conv.shape[0]
    sc_conv_full = jnp.transpose(sc_conv_all.reshape(N_DEV, 8, -1)[:, :n_sc * 3], (1, 0, 2)).reshape(n_sc, 3, D)
    ffn_conv_full = jnp.transpose(ffn_conv_all.reshape(N_DEV, 16, -1)[:, :depth * 3], (1, 0, 2)).reshape(depth, 3, F)

    def start_weights(l, tag, after):
        if tag == "ffn":
            ws = [(ffn_w_up, l), (ffn_w_down, l)]
        else:
            ws = [(attn_w_qkv, l // 2), (attn_w_o, l // 2)] if l % 2 == 0 else [(sc_w_in, l // 2), (sc_w_out, l // 2)]
        lands = [_cast_layer(w, j, me_idx, f"cast_{tag}{k}_{l}") for k, (w, j) in enumerate(ws)]
        return _gather_start(lands, after, f"gather_start_{tag}{l}")

    def wait_weights(flight, after, name):
        send_sems, recv_sems, lands, _ = flight
        return _gather_wait(lands, send_sems, recv_sems, after, name)

    flight_mix = start_weights(0, "mix", [mods, sc_conv_full, ffn_conv_full])
    mods = mods + flight_mix[3][0, 0]

    xs = jnp.concatenate([ctx[0], x[0]], axis=0)
    saved = []
    for l in range(depth):
        j = l // 2
        mod = mods[l]
        if l == 0:
            w_a, w_b = wait_weights(flight_mix, mods, "gather_wait_mix0")
            flight_ffn = start_weights(0, "ffn", [w_a])
            mod = mod + flight_ffn[3][0, 0]
            h, h_t = _norm_mod(xs, mod, 0, L, "norm_m0")
        else:
            h, h_t = _norm_mod(xs, mod, 0, L, f"norm_m{l}")
            w_a, w_b = wait_weights(flight_mix, h, f"gather_wait_mix{l}")
        if l % 2 == 0:
            qkv = _mm_nn(h, w_a, tm=tm_mm, tn=w_a.shape[2], out_dtype=BF16, name=f"qkv{l}")
            qr, kr = _qk_prep(qkv, tabs, gains, j, n_q, n_kv, f"qk_prep{l}")
            z, lse = _attn_fwd(qr, kr, qkv, attn_sink, j, L, f"attn{l}")
            mix = (qkv, qr, kr, lse)
        else:
            u = _mm_nn(h, w_a, tm=tm_mm, tn=w_a.shape[2], out_dtype=BF16, name=f"scin{l}")
            z = _sc_act(u, sc_conv_full, j, L, f"sc_act{l}")
            mix = (u,)
        if l + 1 < depth:
            flight_mix = start_weights(l + 1, "mix", [z])
            mod = mod + flight_mix[3][0, 0]
        y_m, x1, h2, h2_t = _mm_nn_resid_norm(z, w_b.reshape(D, D), xs, mod, 2, 3, L, tm=tm_half, name=f"mixout{l}")
        w_up, w_down = wait_weights(flight_ffn, x1, f"gather_wait_ffn{l}")
        gathered[l] = (w_a, w_b, w_up, w_down)
        u_f = _mm_nn(h2, w_up, tm=tm_mm, tn=w_up.shape[2], out_dtype=BF16, name=f"up{l}")
        a_f = _ffn_act(u_f, ffn_conv_full, conv_b3, l, L, f"ffn_act{l}")
        if l + 1 < depth:
            flight_ffn = start_weights(l + 1, "ffn", [a_f])
            mod = mod + flight_ffn[3][0, 0]
        y_f, x2 = _mm_nn_resid(a_f, w_down.reshape(F, D), x1, mod, 5, L, tm=tm_mm, tn=_pick(D, [512]), name=f"down{l}")
        saved.append((xs, h_t, mix, z, y_m, x1, h2_t, u_f, a_f, y_f))
        xs = x2

    dx, sq = _loss_grad(xs, loss_target[0], L, "loss")
    loss = lax.psum(sq[0, 0], ("x", "y", "c"))

    dmods = [None] * depth
    g_conv_b, g_ffn_conv, g_sc_conv = [None] * depth, [None] * depth, [None] * n_sc
    g_gain, g_sink = [None] * (depth - n_sc), [None] * (depth - n_sc)
    rs_flight = [None] * depth
    sent = jnp.zeros((), F32)
    for l in reversed(range(depth)):
        j = l // 2
        w_a, w_b, w_up, w_down = gathered[l]
        mod = mods[l] + sent
        x0, h_t, mix, z, y_m, x1, h2_t, u_f, a_f, y_f = saved[l]
        da, dy, s_gf = _gate_dgrad(dx, y_f, mod, 5, L, w_down.reshape(F, D), tm=tm_mm, tn=_pick(F, [1408, 512]),
                                   name=f"down_dgrad{l}")
        gw_down = _mm_tn(a_f, [dy], nb=1, tka=_pick(F, [1408, 512]), tn=_pick(D, [1024, 512]), ts=ts_tn, name=f"down_wgrad{l}")
        dgate, dval, s_conv = _ffn_act_bwd(u_f, da, ffn_conv_full, conv_b3, l, L, f"ffn_act_bwd{l}")
        dh2 = _mm_nt_acc([dgate, dval], w_up, tm=tm_mm, name=f"up_dgrad{l}", vmem=VMEM_NEAR_FULL)
        dx1, s_nf = _norm_mod_bwd(dh2, x1, mod, dx, 3, L, f"norm_f_bwd{l}")
        gw_up = _mm_tn(h2_t, [dgate, dval], nb=N_DEV, tka=_pick(D, [512]), tn=w_up.shape[2], ts=ts_lane,
                       name=f"up_wgrad{l}", a_transposed=True)
        g_ffn_conv[l], g_conv_b[l] = s_conv[0:3], s_conv[3]
        rs_ffn = _rs_start([gw_up, gw_down.reshape(N_DEV, -1, D)], f"rs_start_ffn{l}")
        mod = mods[l] + rs_ffn[4][0, 0]
        dz, dy, s_gm = _gate_dgrad(dx1, y_m, mod, 2, L, w_b.reshape(D, D), tm=tm_mm, tn=_pick(D, [1024, 512]),
                                   name=f"mixout_dgrad{l}")
        gw_b = _mm_tn(z, [dy], nb=1, tka=_pick(D, [1024, 512]), tn=_pick(D, [1024, 512]), ts=ts_tn, name=f"mixout_wgrad{l}")
        if l % 2 == 0:
            qkv, qr, kr, lse = mix
            dq, delta, dkx, dvx, s_sink = _attn_bwd_q(qr, kr, qkv, z, dz, lse, attn_sink, j, L, f"attn_bwd_q{l}")
            dk, dv = _attn_bwd_kv(qr, kr, qkv, dz, lse, delta, dkx, dvx, L, f"attn_bwd_kv{l}")
            du_m, s_gain = _qk_prep_bwd(dq, dk, dv, qkv, tabs, gains, j, n_q, n_kv, f"qk_prep_bwd{l}")
            g_gain[j], g_sink[j] = s_gain[0:2], s_sink[0]
        else:
            (u,) = mix
            du_m, s_scconv = _sc_act_bwd(u, dz, sc_conv_full, j, L, f"sc_act_bwd{l}")
            g_sc_conv[j] = s_scconv[0:3]
        dh = _mm_nt_acc([du_m], w_a, tm=tm_mm, name=f"mixin_dgrad{l}")
        gw_a = _mm_tn(h_t, [du_m], nb=N_DEV, tka=_pick(D, [1024, 512]), tn=w_a.shape[2], ts=ts_lane,
                      name=f"mixin_wgrad{l}", a_transposed=True)
        dx, s_nm = _norm_mod_bwd(dh, x0, mod, dx1, 0, L, f"norm_m_bwd{l}")
        dmods[l] = jnp.stack([jnp.stack([s_nm[2 * k], s_nm[2 * k + 1], s_gm[k], s_nf[2 * k], s_nf[2 * k + 1], s_gf[k]])
                              for k in range(2)])
        rs_mix = _rs_start([gw_a, gw_b.reshape(N_DEV, -1, D)], f"rs_start_mix{l}")
        sent = rs_mix[4][0, 0]
        rs_flight[l] = (rs_mix, rs_ffn)

    grad_x = dx[L:][None]

    big_w = {"qkv": (attn_w_qkv, m_attn_w_qkv, v_attn_w_qkv), "wo": (attn_w_o, m_attn_w_o, v_attn_w_o),
             "scin": (sc_w_in, m_sc_w_in, v_sc_w_in), "scout": (sc_w_out, m_sc_w_out, v_sc_w_out),
             "up": (ffn_w_up, m_ffn_w_up, v_ffn_w_up), "down": (ffn_w_down, m_ffn_w_down, v_ffn_w_down)}
    big_out = {k: [] for k in big_w}
    for l in reversed(range(depth)):
        j = l // 2
        groups = [(["qkv", "wo"] if l % 2 == 0 else ["scin", "scout"], [j, j]), (["up", "down"], [l, l])]
        for (names, idxs), flight, tag in reversed(list(zip(groups, rs_flight[l], ("mix", "ffn")))):
            send_sems, recv_sems, own, zones, _ = flight
            own, zones = _rs_wait(own, zones, send_sems, recv_sems, dx, f"rs_wait_{tag}{l}")
            for n, li, p, z in zip(names, idxs, own, zones):
                w, m, v = big_w[n]
                big_out[n].insert(0, _adamw_reduced(p, z, me_idx, w, m, v, li, f"adamw_{n}{l}"))
    big_res = {k: [jnp.stack([o[t] for o in outs]) for t in range(4)] for k, outs in big_out.items()}

    n_attn = depth - n_sc
    pack = [jnp.stack(dmods)[:, 0].reshape(-1, 128), jnp.stack(dmods)[:, 1].reshape(-1, 128),
            jnp.stack(g_gain).reshape(-1, 128), jnp.stack(g_sink),
            jnp.stack(g_conv_b).reshape(-1, 128), jnp.stack(g_ffn_conv).reshape(-1, 128),
            jnp.stack(g_sc_conv).reshape(-1, 128)]
    used = [p.shape[0] for p in pack]
    pack = [_pad_rows(p, -(-p.shape[0] // 8) * 8) for p in pack]
    sizes = [p.shape[0] for p in pack]
    flat = jnp.concatenate(pack, axis=0)
    rows = flat.shape[0]
    small_all = _gather_small(flat, "gather_small_grads")
    small_sum = _sum8(small_all, rows, "sum_small_grads")
    offs = [sum(sizes[:k]) for k in range(len(sizes))]
    seg = lambda a, k: a[offs[k]:offs[k] + used[k]]
    dmod_ctx = seg(small_sum, 0).reshape(depth, 6 * D)
    dmod_lat = small_all.reshape(N_DEV, rows, 128)[:, offs[1]:offs[1] + used[1]].reshape(N_DEV, depth, 6 * D)
    g_gain_sum = seg(small_sum, 2).reshape(n_attn, 2, 128)
    g_sink_sum = seg(small_sum, 3)[:n_attn, :n_q]
    g_conv_b_sum = seg(small_sum, 4).reshape(depth, F)
    g_ffn_conv_sum = seg(small_sum, 5).reshape(depth, 3, F)
    g_sc_conv_sum = seg(small_sum, 6).reshape(n_sc, 3, D)

    d_rows = jnp.concatenate([jnp.transpose(dmod_lat, (1, 0, 2)), dmod_ctx[:, None, :],
                              jnp.zeros((depth, 7, 6 * D), F32)], axis=1)
    d_cols = lax.dynamic_slice_in_dim(d_rows, me * ada_c, ada_c, axis=2)
    g_w_ada, dcond_part = _ada_bwd(cond, d_cols, w_ada, "ada_bwd")
    dcond_all = _gather_small(dcond_part, "gather_dcond")
    g_c_ctx = _cctx_grad(dcond_all, jnp.broadcast_to(c_ctx[None, :], (8, D)), "cctx_grad")[0]
    g_b_ada = _sum_rows(d_rows, "b_ada_grad")[:, 0]

    def small_adam(w, g, m, v, name):
        w2 = w.reshape(-1, w.shape[-1])
        d, m2, v2 = _adamw_plain(w2, g.reshape(w2.shape), m.reshape(w2.shape), v.reshape(w2.shape), name)
        return g.reshape(w.shape), d.reshape(w.shape), m2.reshape(w.shape), v2.reshape(w.shape)

    g_sc_conv_mine = lax.dynamic_slice_in_dim(g_sc_conv_sum, me * sc_conv.shape[2], sc_conv.shape[2], axis=2)
    g_ffn_conv_mine = lax.dynamic_slice_in_dim(g_ffn_conv_sum, me * ffn_conv.shape[2], ffn_conv.shape[2], axis=2)
    res = {
        "c_ctx": small_adam(c_ctx[None, :], g_c_ctx[None, :], m_c_ctx[None, :], v_c_ctx[None, :], "adamw_c_ctx"),
        "b_ada": small_adam(b_ada, g_b_ada, m_b_ada, v_b_ada, "adamw_b_ada"),
        "attn_q_gain": small_adam(attn_q_gain, g_gain_sum[:, 0], m_attn_q_gain, v_attn_q_gain, "adamw_q_gain"),
        "attn_k_gain": small_adam(attn_k_gain, g_gain_sum[:, 1], m_attn_k_gain, v_attn_k_gain, "adamw_k_gain"),
        "attn_sink": small_adam(attn_sink, g_sink_sum, m_attn_sink, v_attn_sink, "adamw_sink"),
        "sc_conv": small_adam(sc_conv, g_sc_conv_mine, m_sc_conv, v_sc_conv, "adamw_sc_conv"),
        "ffn_conv": small_adam(ffn_conv, g_ffn_conv_mine, m_ffn_conv, v_ffn_conv, "adamw_ffn_conv"),
        "ffn_conv_b": small_adam(ffn_conv_b, g_conv_b_sum, m_ffn_conv_b, v_ffn_conv_b, "adamw_conv_b"),
    }
    res["c_ctx"] = tuple(t[0] for t in res["c_ctx"])
    res["w_ada"] = (g_w_ada,) + tuple(_adamw_tiled(w_ada, g_w_ada, m_w_ada, v_w_ada, "adamw_w_ada"))
    res["attn_w_qkv"], res["attn_w_o"] = big_res["qkv"], big_res["wo"]
    res["sc_w_in"], res["sc_w_out"] = big_res["scin"], big_res["scout"]
    res["ffn_w_up"], res["ffn_w_down"] = big_res["up"], big_res["down"]

    order = ["c_ctx", "w_ada", "b_ada", "attn_w_qkv", "attn_w_o", "attn_q_gain", "attn_k_gain", "attn_sink",
             "sc_w_in", "sc_conv", "sc_w_out", "ffn_w_up", "ffn_conv", "ffn_conv_b", "ffn_w_down"]
    outs = [loss, grad_x]
    for t in range(4):
        outs += [res[n][t] for n in order]
    return tuple(outs)
```

```python
import functools

import jax
import jax.numpy as jnp
from jax import lax
from jax.experimental import pallas as pl
from jax.experimental.pallas import tpu as pltpu

F32 = jnp.float32
BF16 = jnp.bfloat16
I32 = jnp.int32

N_DEV = 8
HEAD_DIM = 128
GROUP = 4
WINDOW = 128
BLK = 128
GRID_W = 64
ROPE_BASE = 10000.0
EPS = 1e-6
NEG = -1e30
HALO = 16

ADAM_LR = 0.001
ADAM_B1 = 0.9
ADAM_B2 = 0.999
ADAM_EPS = 1e-08
ADAM_WD = 0.01
ADAM_STEP = 10

V7X_VMEM_BYTES = 64 << 20
VMEM_MATMUL = 52 << 20
VMEM_ELEMENTWISE = 44 << 20
VMEM_NEAR_FULL = 62 << 20

MESH = pl.DeviceIdType.MESH
ANY = pl.BlockSpec(memory_space=pl.ANY)
HBM = pl.BlockSpec(memory_space=pltpu.HBM)
SEM = pl.BlockSpec(memory_space=pltpu.SEMAPHORE)
EFFECT = pltpu.SideEffectType.DATAFLOW_SIDE_EFFECTING

NT_DIMS = (((1,), (1,)), ((), ()))
TN_DIMS = (((0,), (0,)), ((), ()))


def _pick(n, cands):
    for t in cands:
        if n % t == 0:
            return t
    raise ValueError(f"no tile for {n} in {cands}")


def _cp(n_axes, vmem=VMEM_ELEMENTWISE):
    return pltpu.CompilerParams(dimension_semantics=("arbitrary",) * n_axes, vmem_limit_bytes=vmem)


def _rows(i, tm, off=0):
    return i * tm + off + lax.broadcasted_iota(I32, (tm, 1), 0)


def _my_pos():
    return lax.axis_index("x"), lax.axis_index("y"), lax.axis_index("c")


def _gather_small(x_shard, name):
    m_per, n = x_shard.shape

    def body(x_ref, out_ref, send_sems, recv_sems, local_sem):
        x, y, c = _my_pos()
        me, sibling = (x, y, c), (x, y, 1 - c)
        chips = [(1 - x, y), (x, 1 - y), (1 - x, 1 - y)]

        def rows(px, py, pc):
            return out_ref.at[pl.ds((4 * px + 2 * py + pc) * m_per, m_per), :]

        def copy(k, block, to, src=None):
            return pltpu.make_async_remote_copy(
                src_ref=rows(*block) if src is None else src, dst_ref=rows(*block),
                send_sem=send_sems.at[k], recv_sem=recv_sems.at[k], device_id=to, device_id_type=MESH)

        mine = pltpu.make_async_copy(x_ref, rows(*me), local_sem)
        mine.start()
        first = [copy(0, me, sibling, src=x_ref)]
        first += [copy(1 + j, me, (*chip, c), src=x_ref) for j, chip in enumerate(chips)]
        for cp in first:
            cp.start()
        passed = [copy(4 + j, (*chip, c), sibling) for j, chip in enumerate(chips)]
        for j, chip in enumerate(chips):
            copy(1 + j, (*chip, c), me).wait_recv()
            passed[j].start()
        copy(0, sibling, me).wait_recv()
        for j, chip in enumerate(chips):
            copy(4 + j, (*chip, 1 - c), me).wait_recv()
        for cp in first + passed:
            cp.wait_send()
        mine.wait()

    return pl.pallas_call(
        body, name=name,
        out_shape=jax.ShapeDtypeStruct((N_DEV * m_per, n), x_shard.dtype),
        in_specs=[pl.BlockSpec(memory_space=pltpu.VMEM)],
        out_specs=pl.BlockSpec(memory_space=pltpu.VMEM),
        scratch_shapes=[pltpu.SemaphoreType.DMA((7,)), pltpu.SemaphoreType.DMA((7,)), pltpu.SemaphoreType.DMA],
        compiler_params=pltpu.CompilerParams(vmem_limit_bytes=VMEM_ELEMENTWISE),
    )(x_shard)


def _peer(k):
    x, y, c = _my_pos()
    b = k + 1
    return ((1 - x) if b & 4 else x, (1 - y) if b & 2 else y, (1 - c) if b & 1 else c)


def _slot(p):
    return 4 * p[0] + 2 * p[1] + p[2]


def _in_hbm(a):
    return pltpu.with_memory_space_constraint(a, pltpu.HBM)


def _gather_start(lands, after, name):
    n = len(lands)

    def body(*refs):
        l_refs, send_sems, recv_sems = refs[:n], refs[n + len(after)], refs[n + len(after) + 1]
        token = refs[2 * n + len(after) + 2]
        me = _slot(_my_pos())
        for a in range(n):
            for k in range(7):
                pltpu.make_async_remote_copy(
                    src_ref=l_refs[a].at[me], dst_ref=l_refs[a].at[me],
                    send_sem=send_sems.at[7 * a + k], recv_sem=recv_sems.at[7 * a + k],
                    device_id=_peer(k), device_id_type=MESH).start()
        token[...] = jnp.zeros_like(token)

    out = pl.pallas_call(
        body, name=name,
        out_shape=(pltpu.SemaphoreType.DMA((7 * n,)), pltpu.SemaphoreType.DMA((7 * n,)),
                   *[pltpu.HBM(a.shape, a.dtype) for a in lands], jax.ShapeDtypeStruct((8, 128), F32)),
        in_specs=[HBM] * n + [ANY] * len(after),
        out_specs=(SEM, SEM, *[HBM] * n, pl.BlockSpec(memory_space=pltpu.VMEM)),
        input_output_aliases={a: 2 + a for a in range(n)},
        compiler_params=pltpu.CompilerParams(has_side_effects=EFFECT),
    )(*[_in_hbm(a) for a in lands], *after)
    return out[0], out[1], list(out[2:2 + n]), out[2 + n]


def _gather_wait(lands, send_sems, recv_sems, after, name):
    n = len(lands)

    def body(*refs):
        l_refs, ss, rs = refs[:n], refs[n], refs[n + 1]
        me = _slot(_my_pos())
        for a in range(n):
            for k in range(7):
                cp = pltpu.make_async_remote_copy(
                    src_ref=l_refs[a].at[me], dst_ref=l_refs[a].at[_slot(_peer(k))],
                    send_sem=ss.at[7 * a + k], recv_sem=rs.at[7 * a + k], device_id=_peer(k), device_id_type=MESH)
                cp.wait_send()
                cp.wait_recv()

    out = pl.pallas_call(
        body, name=name,
        out_shape=tuple(pltpu.HBM(a.shape, a.dtype) for a in lands),
        in_specs=[HBM] * n + [SEM, SEM, ANY], out_specs=tuple([HBM] * n),
        input_output_aliases={a: a for a in range(n)},
        compiler_params=pltpu.CompilerParams(has_side_effects=EFFECT),
    )(*lands, send_sems, recv_sems, after)
    return list(out)


def _rs_start(grads, name):
    n = len(grads)

    def body(*refs):
        g_refs, z_refs, send_sems, recv_sems = refs[:n], refs[n:2 * n], refs[2 * n], refs[2 * n + 1]
        token = refs[4 * n + 2]
        for a in range(n):
            for k in range(7):
                pltpu.make_async_remote_copy(
                    src_ref=g_refs[a].at[_slot(_peer(k))], dst_ref=z_refs[a].at[k],
                    send_sem=send_sems.at[7 * a + k], recv_sem=recv_sems.at[7 * a + k],
                    device_id=_peer(k), device_id_type=MESH).start()
        token[...] = jnp.zeros_like(token)

    zones = [lax.empty((7,) + g.shape[1:], g.dtype) for g in grads]
    out = pl.pallas_call(
        body, name=name,
        out_shape=(pltpu.SemaphoreType.DMA((7 * n,)), pltpu.SemaphoreType.DMA((7 * n,)),
                   *[pltpu.HBM(a.shape, a.dtype) for a in grads], *[pltpu.HBM(z.shape, z.dtype) for z in zones],
                   jax.ShapeDtypeStruct((8, 128), F32)),
        in_specs=[HBM] * (2 * n),
        out_specs=(SEM, SEM, *[HBM] * (2 * n), pl.BlockSpec(memory_space=pltpu.VMEM)),
        input_output_aliases={a: 2 + a for a in range(2 * n)},
        compiler_params=pltpu.CompilerParams(has_side_effects=EFFECT),
    )(*[_in_hbm(a) for a in grads], *[_in_hbm(z) for z in zones])
    return out[0], out[1], list(out[2:2 + n]), list(out[2 + n:2 + 2 * n]), out[2 + 2 * n]


def _rs_wait(grads, zones, send_sems, recv_sems, after, name):
    n = len(grads)

    def body(*refs):
        g_refs, z_refs, ss, rs = refs[:n], refs[n:2 * n], refs[2 * n], refs[2 * n + 1]
        for a in range(n):
            for k in range(7):
                cp = pltpu.make_async_remote_copy(
                    src_ref=g_refs[a].at[_slot(_peer(k))], dst_ref=z_refs[a].at[k],
                    send_sem=ss.at[7 * a + k], recv_sem=rs.at[7 * a + k], device_id=_peer(k), device_id_type=MESH)
                cp.wait_send()
                cp.wait_recv()

    out = pl.pallas_call(
        body, name=name,
        out_shape=tuple(pltpu.HBM(a.shape, a.dtype) for a in list(grads) + list(zones)),
        in_specs=[HBM] * (2 * n) + [SEM, SEM, ANY], out_specs=tuple([HBM] * (2 * n)),
        input_output_aliases={a: a for a in range(2 * n)},
        compiler_params=pltpu.CompilerParams(has_side_effects=EFFECT),
    )(*grads, *zones, send_sems, recv_sems, after)
    return list(out[:n]), list(out[n:])


def _cast_layer(w, l, me_idx, name):
    _, r, c = w.shape
    tr = _pick(r, [512, 256, 128, 64, 32, 16])

    def body(me_ref, w_ref, o_ref):
        o_ref[...] = w_ref[...].astype(BF16)

    return pl.pallas_call(
        body, name=name,
        grid_spec=pltpu.PrefetchScalarGridSpec(
            num_scalar_prefetch=1, grid=(r // tr,),
            in_specs=[pl.BlockSpec((None, tr, c), lambda i, me_ref: (l, i, 0))],
            out_specs=pl.BlockSpec((None, tr, c), lambda i, me_ref: (me_ref[0], i, 0))),
        out_shape=jax.ShapeDtypeStruct((N_DEV, r, c), BF16), compiler_params=_cp(1),
    )(me_idx, w)


def _adam_math(w, g, m, v):
    m2 = ADAM_B1 * m + (1.0 - ADAM_B1) * g
    v2 = ADAM_B2 * v + (1.0 - ADAM_B2) * (g * g)
    m_hat = m2 / (1.0 - ADAM_B1 ** ADAM_STEP)
    v_hat = v2 / (1.0 - ADAM_B2 ** ADAM_STEP)
    delta = -ADAM_LR * (m_hat / (jnp.sqrt(v_hat) + ADAM_EPS) + ADAM_WD * w)
    return delta, m2, v2


def _adamw_reduced(own, zone, me_idx, w, m, v, l, outs, name):
    _, r, c = own.shape
    tr = _pick(r, [256, 128, 64, 32, 16])
    if outs is None:
        outs = [lax.empty(w.shape, F32) for _ in range(4)]

    def body(me_ref, p_ref, z_ref, w_ref, m_ref, v_ref, *rest):
        g_out, d_out, m_out, v_out = rest[4:]
        g = p_ref[...].astype(F32)
        for k in range(7):
            g = g + z_ref[k].astype(F32)
        d, m2, v2 = _adam_math(w_ref[...], g, m_ref[...], v_ref[...])
        g_out[...] = g
        d_out[...] = d
        m_out[...] = m2
        v_out[...] = v2

    wspec = pl.BlockSpec((None, tr, c), lambda i, me_ref: (l, i, 0))
    return pl.pallas_call(
        body, name=name,
        grid_spec=pltpu.PrefetchScalarGridSpec(
            num_scalar_prefetch=1, grid=(r // tr,),
            in_specs=[pl.BlockSpec((None, tr, c), lambda i, me_ref: (me_ref[0], i, 0)),
                      pl.BlockSpec((7, tr, c), lambda i, me_ref: (0, i, 0)), wspec, wspec, wspec] + [ANY] * 4,
            out_specs=[wspec] * 4),
        out_shape=[jax.ShapeDtypeStruct(w.shape, F32)] * 4,
        input_output_aliases={6 + t: t for t in range(4)}, compiler_params=_cp(1),
    )(me_idx, own, zone, w, m, v, *outs)


def _adamw_plain(w, g, m, v, name):
    def body(w_ref, g_ref, m_ref, v_ref, d_out, m_out, v_out):
        d, m2, v2 = _adam_math(w_ref[...], g_ref[...], m_ref[...], v_ref[...])
        d_out[...] = d
        m_out[...] = m2
        v_out[...] = v2

    return pl.pallas_call(
        body, name=name, out_shape=[jax.ShapeDtypeStruct(w.shape, F32)] * 3,
        compiler_params=pltpu.CompilerParams(vmem_limit_bytes=VMEM_ELEMENTWISE),
    )(w, g, m, v)


def _adamw_tiled(w, g, m, v, name):
    lyr, r, c = w.shape
    tr = _pick(r, [256, 128, 64, 32, 16, 8])

    def body(w_ref, g_ref, m_ref, v_ref, d_out, m_out, v_out):
        d, m2, v2 = _adam_math(w_ref[...], g_ref[...], m_ref[...], v_ref[...])
        d_out[...] = d
        m_out[...] = m2
        v_out[...] = v2

    spec = pl.BlockSpec((None, tr, c), lambda l, i: (l, i, 0))
    return pl.pallas_call(
        body, name=name, grid=(lyr, r // tr), in_specs=[spec] * 4, out_specs=[spec] * 3,
        out_shape=[jax.ShapeDtypeStruct(w.shape, F32)] * 3, compiler_params=_cp(2),
    )(w, g, m, v)


def _sum8(gathered, rows, name):
    def body(g_ref, o_ref):
        acc = g_ref[0:rows, :]
        for d in range(1, N_DEV):
            acc = acc + g_ref[d * rows:(d + 1) * rows, :]
        o_ref[...] = acc

    return pl.pallas_call(
        body, name=name, out_shape=jax.ShapeDtypeStruct((rows, 128), F32),
        compiler_params=pltpu.CompilerParams(vmem_limit_bytes=VMEM_ELEMENTWISE),
    )(gathered)


MXU_COLS = 256


def _pairable(nb, nc):
    return nb % 2 == 0 and nc % MXU_COLS == MXU_COLS // 2 and nc > MXU_COLS // 2


def _mm_nn(a, b3, *, tm, tn, out_dtype, name):
    M, K = a.shape
    nb, _, nc = b3.shape
    q = nc // tn

    if _pairable(nb, nc) and tn == nc:
        cut = nc - 128

        def pair_body(a_ref, b_ref, o_ref):
            av = a_ref[...]
            mid = jnp.concatenate([b_ref[0, :, cut:nc], b_ref[1, :, 0:128]], axis=1)
            o_ref[:, 0:cut] = jnp.dot(av, b_ref[0, :, 0:cut], preferred_element_type=F32).astype(o_ref.dtype)
            o_ref[:, cut:nc + 128] = jnp.dot(av, mid, preferred_element_type=F32).astype(o_ref.dtype)
            o_ref[:, nc + 128:2 * nc] = jnp.dot(av, b_ref[1, :, 128:nc], preferred_element_type=F32).astype(o_ref.dtype)

        return pl.pallas_call(
            pair_body, name=name, grid=(nb // 2, M // tm),
            in_specs=[pl.BlockSpec((tm, K), lambda j, i: (i, 0)),
                      pl.BlockSpec((2, K, nc), lambda j, i: (j, 0, 0))],
            out_specs=pl.BlockSpec((tm, 2 * nc), lambda j, i: (i, j)),
            out_shape=jax.ShapeDtypeStruct((M, nb * nc), out_dtype), compiler_params=_cp(2, VMEM_MATMUL),
        )(a, b3)

    def body(a_ref, b_ref, o_ref):
        o_ref[...] = jnp.dot(a_ref[...], b_ref[...], preferred_element_type=F32).astype(o_ref.dtype)

    return pl.pallas_call(
        body, name=name, grid=(nb * q, M // tm),
        in_specs=[pl.BlockSpec((tm, K), lambda j, i: (i, 0)),
                  pl.BlockSpec((None, K, tn), lambda j, i: (j // q, 0, j % q))],
        out_specs=pl.BlockSpec((tm, tn), lambda j, i: (i, j)),
        out_shape=jax.ShapeDtypeStruct((M, nb * nc), out_dtype), compiler_params=_cp(2, VMEM_MATMUL),
    )(a, b3)


def _mm_nn_resid(a, b2, x_old, mod, gate_row, n_ctx, *, tm, tn, name):
    M, K = a.shape
    N = b2.shape[1]

    def body(a_ref, b_ref, x_ref, mod_ref, y_ref, xn_ref):
        y = jnp.dot(a_ref[...], b_ref[...], preferred_element_type=F32)
        is_ctx = _rows(pl.program_id(1), tm) < n_ctx
        g = jnp.where(is_ctx, mod_ref[0, gate_row:gate_row + 1, :], mod_ref[1, gate_row:gate_row + 1, :])
        y_ref[...] = y.astype(BF16)
        xn_ref[...] = x_ref[...] + g * y

    return pl.pallas_call(
        body, name=name, grid=(N // tn, M // tm),
        in_specs=[pl.BlockSpec((tm, K), lambda j, i: (i, 0)),
                  pl.BlockSpec((K, tn), lambda j, i: (0, j)),
                  pl.BlockSpec((tm, tn), lambda j, i: (i, j)),
                  pl.BlockSpec((2, 6, tn), lambda j, i: (0, 0, j))],
        out_specs=[pl.BlockSpec((tm, tn), lambda j, i: (i, j))] * 2,
        out_shape=[jax.ShapeDtypeStruct((M, N), BF16), jax.ShapeDtypeStruct((M, N), F32)],
        compiler_params=_cp(2, VMEM_MATMUL),
    )(a, b2, x_old, mod)


def _mm_nn_resid_norm(a, b2, x_old, mod, gate_row, norm_row0, n_ctx, *, tm, name):
    M, K = a.shape
    N = b2.shape[1]

    def body(a_ref, b_ref, x_ref, mod_ref, y_ref, xn_ref, h_ref, ht_ref):
        y = jnp.dot(a_ref[...], b_ref[...], preferred_element_type=F32)
        is_ctx = _rows(pl.program_id(0), tm) < n_ctx

        def row(k):
            return jnp.where(is_ctx, mod_ref[0, k:k + 1, :], mod_ref[1, k:k + 1, :])

        y_ref[...] = y.astype(BF16)
        xn = x_ref[...] + row(gate_row) * y
        xn_ref[...] = xn
        r = lax.rsqrt(jnp.mean(xn * xn, axis=-1, keepdims=True) + EPS)
        hv = xn * r * (1.0 + row(norm_row0 + 1)) + row(norm_row0)
        h_ref[...] = hv.astype(BF16)
        ht_ref[...] = hv.T.astype(BF16)

    rows = pl.BlockSpec((tm, N), lambda i: (i, 0))
    return pl.pallas_call(
        body, name=name, grid=(M // tm,),
        in_specs=[pl.BlockSpec((tm, K), lambda i: (i, 0)), pl.BlockSpec((K, N), lambda i: (0, 0)), rows,
                  pl.BlockSpec((2, 6, N), lambda i: (0, 0, 0))],
        out_specs=[rows, rows, rows, pl.BlockSpec((N, tm), lambda i: (0, i))],
        out_shape=[jax.ShapeDtypeStruct((M, N), BF16), jax.ShapeDtypeStruct((M, N), F32),
                   jax.ShapeDtypeStruct((M, N), BF16), jax.ShapeDtypeStruct((N, M), BF16)],
        compiler_params=_cp(1, VMEM_MATMUL),
    )(a, b2, x_old, mod)


def _norm_bwd_math(dh_v, xv, mod_ref, res, is_ctx, row0):
    D = xv.shape[1]
    r = lax.rsqrt(jnp.mean(xv * xv, axis=-1, keepdims=True) + EPS)
    xhat = xv * r
    sc = jnp.where(is_ctx, mod_ref[0, row0 + 1:row0 + 2, :], mod_ref[1, row0 + 1:row0 + 2, :])
    dxhat = dh_v * (1.0 + sc)
    dx = res + r * (dxhat - xhat * jnp.mean(dxhat * xhat, axis=-1, keepdims=True))
    dsc = dh_v * xhat
    zero = jnp.zeros_like(dh_v)
    sums = [jnp.sum(jnp.where(is_ctx, dh_v, zero), axis=0, keepdims=True),
            jnp.sum(jnp.where(is_ctx, dsc, zero), axis=0, keepdims=True),
            jnp.sum(jnp.where(is_ctx, zero, dh_v), axis=0, keepdims=True),
            jnp.sum(jnp.where(is_ctx, zero, dsc), axis=0, keepdims=True)]
    rid = lax.broadcasted_iota(I32, (8, D), 0)
    upd = jnp.zeros((8, D), F32)
    for k, s in enumerate(sums):
        upd = upd + jnp.where(rid == k, s, 0.0)
    return dx, upd


def _mm_nt_acc(parts, w3, *, tm, name, vmem=VMEM_MATMUL):
    n_parts = len(parts)
    M = parts[0].shape[0]
    nb, K, nc = w3.shape
    pair = _pairable(nb // n_parts, nc)
    g = 2 if pair else 1
    steps, per = nb // g, nb // n_parts // g
    cut = nc - 128

    def nt(d, w):
        return lax.dot_general(d, w, NT_DIMS, preferred_element_type=F32)

    def contribution(d_ref, w_ref):
        if not pair:
            return nt(d_ref[...], w_ref[0])
        mid = jnp.concatenate([w_ref[0, :, cut:nc], w_ref[1, :, 0:128]], axis=1)
        return (nt(d_ref[:, 0:cut], w_ref[0, :, 0:cut]) + nt(d_ref[:, cut:nc + 128], mid)
                + nt(d_ref[:, nc + 128:2 * nc], w_ref[1, :, 128:nc]))

    def body(*refs):
        dy_refs, w_ref = refs[:n_parts], refs[n_parts]
        o_ref, acc_ref = refs[n_parts + 1], refs[n_parts + 2]
        s = pl.program_id(1)

        @pl.when(s == 0)
        def _():
            acc_ref[...] = jnp.zeros_like(acc_ref)

        for p in range(n_parts):
            @pl.when(s // per == p)
            def _(p=p):
                acc_ref[...] += contribution(dy_refs[p], w_ref)

        @pl.when(s == steps - 1)
        def _():
            o_ref[...] = acc_ref[...].astype(o_ref.dtype)

    def part_spec(p):
        return pl.BlockSpec((tm, g * nc), lambda i, s: (i, jnp.clip(s - p * per, 0, per - 1)))

    return pl.pallas_call(
        body, name=name, grid=(M // tm, steps),
        in_specs=[part_spec(p) for p in range(n_parts)] + [pl.BlockSpec((g, K, nc), lambda i, s: (s, 0, 0))],
        out_specs=pl.BlockSpec((tm, K), lambda i, s: (i, 0)),
        out_shape=jax.ShapeDtypeStruct((M, K), BF16),
        scratch_shapes=[pltpu.VMEM((tm, K), F32)], compiler_params=_cp(2, vmem),
    )(*parts, w3)


def _norm_mod_bwd(dh, x, mod, dx_res, row0, n_ctx, name):
    S, D = x.shape
    tm = _pick(S, [256, 128])

    def body(dh_ref, x_ref, mod_ref, res_ref, dx_ref, acc_ref):
        i = pl.program_id(0)

        @pl.when(i == 0)
        def _():
            acc_ref[...] = jnp.zeros_like(acc_ref)

        dx, upd = _norm_bwd_math(dh_ref[...].astype(F32), x_ref[...], mod_ref, res_ref[...], _rows(i, tm) < n_ctx, row0)
        dx_ref[...] = dx
        acc_ref[...] += upd

    rows = pl.BlockSpec((tm, D), lambda i: (i, 0))
    return pl.pallas_call(
        body, name=name, grid=(S // tm,),
        in_specs=[rows, rows, pl.BlockSpec((2, 6, D), lambda i: (0, 0, 0)), rows],
        out_specs=[rows, pl.BlockSpec((8, D), lambda i: (0, 0))],
        out_shape=[jax.ShapeDtypeStruct((S, D), F32), jax.ShapeDtypeStruct((8, D), F32)],
        compiler_params=_cp(1),
    )(dh, x, mod, dx_res)


def _gate_dgrad(dx, y, mod, gate_row, n_ctx, w2, *, tm, tn, name):
    M, N = dx.shape
    K = w2.shape[0]

    def body(dx_ref, y_ref, mod_ref, w_ref, da_ref, dy_ref, acc_ref):
        i = pl.program_id(0)
        j = pl.program_id(1)

        @pl.when((i == 0) & (j == 0))
        def _():
            acc_ref[...] = jnp.zeros_like(acc_ref)

        @pl.when(j == 0)
        def _():
            dxv = dx_ref[...]
            is_ctx = _rows(i, tm) < n_ctx
            g = jnp.where(is_ctx, mod_ref[0, gate_row:gate_row + 1, :], mod_ref[1, gate_row:gate_row + 1, :])
            dy_ref[...] = (g * dxv).astype(BF16)
            prod = dxv * y_ref[...].astype(F32)
            zero = jnp.zeros_like(prod)
            s_ctx = jnp.sum(jnp.where(is_ctx, prod, zero), axis=0, keepdims=True)
            s_lat = jnp.sum(jnp.where(is_ctx, zero, prod), axis=0, keepdims=True)
            rid = lax.broadcasted_iota(I32, (8, N), 0)
            acc_ref[...] += jnp.where(rid == 0, s_ctx, 0.0) + jnp.where(rid == 1, s_lat, 0.0)

        da_ref[...] = lax.dot_general(dy_ref[...], w_ref[...], NT_DIMS, preferred_element_type=F32).astype(BF16)

    rows = pl.BlockSpec((tm, N), lambda i, j: (i, 0))
    return pl.pallas_call(
        body, name=name, grid=(M // tm, K // tn),
        in_specs=[rows, rows, pl.BlockSpec((2, 6, N), lambda i, j: (0, 0, 0)),
                  pl.BlockSpec((tn, N), lambda i, j: (j, 0))],
        out_specs=[pl.BlockSpec((tm, tn), lambda i, j: (i, j)), rows, pl.BlockSpec((8, N), lambda i, j: (0, 0))],
        out_shape=[jax.ShapeDtypeStruct((M, K), BF16), jax.ShapeDtypeStruct((M, N), BF16),
                   jax.ShapeDtypeStruct((8, N), F32)],
        compiler_params=_cp(2, VMEM_MATMUL),
    )(dx, y, mod, w2)


def _mm_tn(a, parts, *, nb, tka, tn, ts, name, a_transposed=False):
    n_parts = len(parts)
    S, Ka = a.shape[::-1] if a_transposed else a.shape
    N = sum(p.shape[1] for p in parts)
    nc = N // nb
    q = nc // tn
    nk = S // ts
    g = 2 if (tn == nc and n_parts == 1 and _pairable(nb, nc)) else 1
    per = nb * q // n_parts // g

    def body(a_ref, *rest):
        dy_refs, o_ref, acc_ref = rest[:n_parts], rest[n_parts], rest[n_parts + 1]
        j = pl.program_id(0)
        k = pl.program_id(2)

        @pl.when(k == 0)
        def _():
            acc_ref[...] = jnp.zeros_like(acc_ref)

        for p in range(n_parts):
            @pl.when(j // per == p)
            def _(p=p):
                if a_transposed:
                    acc_ref[...] += jnp.dot(a_ref[...], dy_refs[p][...], preferred_element_type=F32)
                else:
                    acc_ref[...] += lax.dot_general(a_ref[...], dy_refs[p][...], TN_DIMS, preferred_element_type=F32)

        @pl.when(k == nk - 1)
        def _():
            for t in range(g):
                o_ref[t] = acc_ref[:, t * tn:(t + 1) * tn].astype(o_ref.dtype)

    def part_spec(p):
        return pl.BlockSpec((ts, g * tn), lambda j, ia, k: (jnp.where(j // per == p, k, 0),
                                                            jnp.clip(j - p * per, 0, per - 1)))

    return pl.pallas_call(
        body, name=name, grid=(nb * q // g, Ka // tka, nk),
        in_specs=[pl.BlockSpec((tka, ts), lambda j, ia, k: (ia, k)) if a_transposed
                  else pl.BlockSpec((ts, tka), lambda j, ia, k: (k, ia))] + [part_spec(p) for p in range(n_parts)],
        out_specs=pl.BlockSpec((g, tka, tn), lambda j, ia, k: (j // q, ia, j % q)),
        out_shape=jax.ShapeDtypeStruct((nb, Ka, nc), BF16),
        scratch_shapes=[pltpu.VMEM((tka, g * tn), F32)], compiler_params=_cp(3, VMEM_MATMUL),
    )(a, *parts)


def _norm_mod(x, mod, row0, n_ctx, name):
    S, D = x.shape
    tm = _pick(S, [256, 128])

    def body(x_ref, mod_ref, h_ref, ht_ref):
        xv = x_ref[...]
        r = lax.rsqrt(jnp.mean(xv * xv, axis=-1, keepdims=True) + EPS)
        is_ctx = _rows(pl.program_id(0), tm) < n_ctx
        sh = jnp.where(is_ctx, mod_ref[0, row0:row0 + 1, :], mod_ref[1, row0:row0 + 1, :])
        sc = jnp.where(is_ctx, mod_ref[0, row0 + 1:row0 + 2, :], mod_ref[1, row0 + 1:row0 + 2, :])
        hv = xv * r * (1.0 + sc) + sh
        h_ref[...] = hv.astype(BF16)
        ht_ref[...] = hv.T.astype(BF16)

    return pl.pallas_call(
        body, name=name, grid=(S // tm,),
        in_specs=[pl.BlockSpec((tm, D), lambda i: (i, 0)), pl.BlockSpec((2, 6, D), lambda i: (0, 0, 0))],
        out_specs=[pl.BlockSpec((tm, D), lambda i: (i, 0)), pl.BlockSpec((D, tm), lambda i: (0, i))],
        out_shape=[jax.ShapeDtypeStruct((S, D), BF16), jax.ShapeDtypeStruct((D, S), BF16)], compiler_params=_cp(1),
    )(x, mod)


def _loss_grad(x, target, n_ctx, name):
    S, D = x.shape
    tm = _pick(n_ctx, [256, 128])
    nct = n_ctx // tm

    def body(x_ref, t_ref, dx_ref, tot_ref, acc_ref):
        i = pl.program_id(0)

        @pl.when(i == 0)
        def _():
            acc_ref[...] = jnp.zeros_like(acc_ref)

        @pl.when(i < nct)
        def _():
            dx_ref[...] = jnp.zeros_like(dx_ref)

        @pl.when(i >= nct)
        def _():
            err = x_ref[...] - t_ref[...]
            dx_ref[...] = err * (1.0 / D)
            acc_ref[...] += jnp.sum(err * err, axis=0, keepdims=True)

        @pl.when(i == S // tm - 1)
        def _():
            tot = jnp.sum(acc_ref[...], axis=1, keepdims=True) * (0.5 / D)
            tot_ref[...] = jnp.broadcast_to(tot, tot_ref.shape)

    return pl.pallas_call(
        body, name=name, grid=(S // tm,),
        in_specs=[pl.BlockSpec((tm, D), lambda i: (i, 0)),
                  pl.BlockSpec((tm, D), lambda i: (jnp.maximum(i - nct, 0), 0))],
        out_specs=[pl.BlockSpec((tm, D), lambda i: (i, 0)), pl.BlockSpec((1, 128), lambda i: (0, 0))],
        out_shape=[jax.ShapeDtypeStruct((S, D), F32), jax.ShapeDtypeStruct((1, 128), F32)],
        scratch_shapes=[pltpu.VMEM((1, D), F32)], compiler_params=_cp(1),
    )(x, target)


def _halo_specs(tm, tc, S, col_off):
    per = tm // HALO
    last = S // HALO - 1
    return [pl.BlockSpec((HALO, tc), lambda j, i: (jnp.maximum(i * per - 1, 0), j + col_off)),
            pl.BlockSpec((tm, tc), lambda j, i: (i, j + col_off)),
            pl.BlockSpec((HALO, tc), lambda j, i: (jnp.minimum((i + 1) * per, last), j + col_off))]


def _ext(p_ref, m_ref, n_ref):
    return jnp.concatenate([p_ref[...], m_ref[...], n_ref[...]], axis=0).astype(F32)


def _links(i, tm, S, n_ctx):
    n = tm + 2 * HALO
    rid = i * tm - HALO + lax.broadcasted_iota(I32, (n, 1), 0)
    has_prev = (rid != 0) & (rid != n_ctx)
    has_next = (rid != n_ctx - 1) & (rid != S - 1)
    return has_prev, has_next


def _edge_tile(i, tm, S, n_ctx):
    tiles = sorted({t for b in (0, n_ctx - 1, n_ctx, S - 1) for t in range(S // tm)
                    if t * tm - HALO <= b < (t + 1) * tm + HALO})
    edge = i == tiles[0]
    for t in tiles[1:]:
        edge = edge | (i == t)
    return edge


def _up(x, keep=None):
    y = pltpu.roll(x, 1, 0)
    return y if keep is None else jnp.where(keep, y, 0.0)


def _dn(x, keep=None):
    y = pltpu.roll(x, x.shape[0] - 1, 0)
    return y if keep is None else jnp.where(keep, y, 0.0)


def _conv3(x, w_ref, has_prev, has_next):
    return w_ref[0:1, :] * _up(x, has_prev) + w_ref[1:2, :] * x + w_ref[2:3, :] * _dn(x, has_next)


def _conv3_t(d, w_ref, has_prev, has_next):
    return w_ref[0:1, :] * _dn(d, has_next) + w_ref[1:2, :] * d + w_ref[2:3, :] * _up(d, has_prev)


def _conv3_wgrad(d, x, has_prev, has_next, extra=None):
    c = slice(HALO, d.shape[0] - HALO)
    taps = [_up(x, has_prev), x, _dn(x, has_next)]
    sums = [jnp.sum((d * t)[c], axis=0, keepdims=True) for t in taps]
    if extra is not None:
        sums.append(jnp.sum(extra[c], axis=0, keepdims=True))
    rid = lax.broadcasted_iota(I32, (8, d.shape[1]), 0)
    upd = jnp.zeros((8, d.shape[1]), F32)
    for k, s in enumerate(sums):
        upd = upd + jnp.where(rid == k, s, 0.0)
    return upd


def _sigmoid(x):
    return 1.0 / (1.0 + jnp.exp(-x))


def _ffn_act(u, conv_w, conv_b, l, n_ctx, name):
    S, F2 = u.shape
    F = F2 // 2
    tm = _pick(S, [384, 256, 128])
    tc = _pick(F, [1408, 512, 256, 128])
    nj = F // tc

    def body(gp, gm, gn, v_ref, w_ref, b_ref, a_ref):
        i = pl.program_id(1)

        def compute(has_prev, has_next):
            gc = _conv3(_ext(gp, gm, gn), w_ref, has_prev, has_next)[HALO:HALO + tm] + b_ref[...]
            a_ref[...] = (gc * _sigmoid(gc) * v_ref[...].astype(F32)).astype(BF16)

        edge = _edge_tile(i, tm, S, n_ctx)
        pl.when(edge)(lambda: compute(*_links(i, tm, S, n_ctx)))
        pl.when(jnp.logical_not(edge))(lambda: compute(None, None))

    return pl.pallas_call(
        body, name=name, grid=(nj, S // tm),
        in_specs=_halo_specs(tm, tc, S, 0) + [
            pl.BlockSpec((tm, tc), lambda j, i: (i, j + nj)),
            pl.BlockSpec((None, 3, tc), lambda j, i: (l, 0, j)),
            pl.BlockSpec((None, 1, tc), lambda j, i: (l, 0, j))],
        out_specs=pl.BlockSpec((tm, tc), lambda j, i: (i, j)),
        out_shape=jax.ShapeDtypeStruct((S, F), BF16), compiler_params=_cp(2),
    )(u, u, u, u, conv_w, conv_b)


def _ffn_act_bwd(u, da, conv_w, conv_b, l, n_ctx, name):
    S, F2 = u.shape
    F = F2 // 2
    tm = _pick(S, [384, 256, 128])
    tc = _pick(F, [1408, 512, 256, 128])
    nj = F // tc

    def body(gp, gm, gn, vp, vm, vn, dp, dm, dn_, w_ref, b_ref, dg_ref, dv_ref, acc_ref):
        i = pl.program_id(1)

        @pl.when(i == 0)
        def _():
            acc_ref[...] = jnp.zeros_like(acc_ref)

        def compute(has_prev, has_next):
            g = _ext(gp, gm, gn)
            val = _ext(vp, vm, vn)
            d_a = _ext(dp, dm, dn_)
            gc = _conv3(g, w_ref, has_prev, has_next) + b_ref[...]
            sg = _sigmoid(gc)
            dgc = d_a * val * (sg * (1.0 + gc * (1.0 - sg)))
            c = slice(HALO, HALO + tm)
            dv_ref[...] = (d_a * gc * sg)[c].astype(BF16)
            dg_ref[...] = _conv3_t(dgc, w_ref, has_prev, has_next)[c].astype(BF16)
            acc_ref[...] += _conv3_wgrad(dgc, g, has_prev, has_next, extra=dgc)

        edge = _edge_tile(i, tm, S, n_ctx)
        pl.when(edge)(lambda: compute(*_links(i, tm, S, n_ctx)))
        pl.when(jnp.logical_not(edge))(lambda: compute(None, None))

    return pl.pallas_call(
        body, name=name, grid=(nj, S // tm),
        in_specs=_halo_specs(tm, tc, S, 0) + _halo_specs(tm, tc, S, nj) + _halo_specs(tm, tc, S, 0) + [
            pl.BlockSpec((None, 3, tc), lambda j, i: (l, 0, j)),
            pl.BlockSpec((None, 1, tc), lambda j, i: (l, 0, j))],
        out_specs=[pl.BlockSpec((tm, tc), lambda j, i: (i, j))] * 2 + [pl.BlockSpec((8, tc), lambda j, i: (0, j))],
        out_shape=[jax.ShapeDtypeStruct((S, F), BF16)] * 2 + [jax.ShapeDtypeStruct((8, F), F32)],
        compiler_params=_cp(2),
    )(u, u, u, u, u, u, da, da, da, conv_w, conv_b)


def _sc_act(u, conv_w, l, n_ctx, name):
    S, D3 = u.shape
    D = D3 // 3
    tm = _pick(S, [384, 256, 128])
    tc = _pick(D, [1024, 512, 256, 128])
    nj = D // tc

    def body(b_ref, cp, cm, cn, vp, vm, vn, w_ref, z_ref):
        i = pl.program_id(1)

        def compute(has_prev, has_next):
            t = _ext(cp, cm, cn) * _ext(vp, vm, vn)
            cv = _conv3(t, w_ref, has_prev, has_next)[HALO:HALO + tm]
            z_ref[...] = (b_ref[...].astype(F32) * cv).astype(BF16)

        edge = _edge_tile(i, tm, S, n_ctx)
        pl.when(edge)(lambda: compute(*_links(i, tm, S, n_ctx)))
        pl.when(jnp.logical_not(edge))(lambda: compute(None, None))

    return pl.pallas_call(
        body, name=name, grid=(nj, S // tm),
        in_specs=[pl.BlockSpec((tm, tc), lambda j, i: (i, j))] + _halo_specs(tm, tc, S, nj)
        + _halo_specs(tm, tc, S, 2 * nj) + [pl.BlockSpec((None, 3, tc), lambda j, i: (l, 0, j))],
        out_specs=pl.BlockSpec((tm, tc), lambda j, i: (i, j)),
        out_shape=jax.ShapeDtypeStruct((S, D), BF16), compiler_params=_cp(2),
    )(u, u, u, u, u, u, u, conv_w)


def _sc_act_bwd(u, dz, conv_w, l, n_ctx, name):
    S, D3 = u.shape
    D = D3 // 3
    tm = _pick(S, [128])
    tc = D
    nj = 1

    def body(bp, bm, bn, cp, cm, cn, vp, vm, vn, zp, zm, zn, w_ref, du_ref, acc_ref):
        db_ref, dc_ref, dv_ref = du_ref.at[:, 0:D], du_ref.at[:, D:2 * D], du_ref.at[:, 2 * D:3 * D]
        i = pl.program_id(1)

        @pl.when(i == 0)
        def _():
            acc_ref[...] = jnp.zeros_like(acc_ref)

        def compute(has_prev, has_next):
            gb = _ext(bp, bm, bn)
            gcv = _ext(cp, cm, cn)
            val = _ext(vp, vm, vn)
            d_z = _ext(zp, zm, zn)
            t = gcv * val
            c = slice(HALO, HALO + tm)
            db_ref[...] = (d_z * _conv3(t, w_ref, has_prev, has_next))[c].astype(BF16)
            dcv = d_z * gb
            dt = _conv3_t(dcv, w_ref, has_prev, has_next)
            dc_ref[...] = (dt * val)[c].astype(BF16)
            dv_ref[...] = (dt * gcv)[c].astype(BF16)
            acc_ref[...] += _conv3_wgrad(dcv, t, has_prev, has_next)

        edge = _edge_tile(i, tm, S, n_ctx)
        pl.when(edge)(lambda: compute(*_links(i, tm, S, n_ctx)))
        pl.when(jnp.logical_not(edge))(lambda: compute(None, None))

    return pl.pallas_call(
        body, name=name, grid=(nj, S // tm),
        in_specs=_halo_specs(tm, tc, S, 0) + _halo_specs(tm, tc, S, nj) + _halo_specs(tm, tc, S, 2 * nj)
        + _halo_specs(tm, tc, S, 0) + [pl.BlockSpec((None, 3, tc), lambda j, i: (l, 0, j))],
        out_specs=[pl.BlockSpec((tm, D3), lambda j, i: (i, 0)), pl.BlockSpec((8, tc), lambda j, i: (0, j))],
        out_shape=[jax.ShapeDtypeStruct((S, D3), BF16), jax.ShapeDtypeStruct((8, D), F32)],
        compiler_params=_cp(2),
    )(u, u, u, u, u, u, u, u, u, dz, dz, dz, conv_w)


def _rope_tables(T, n_ctx):
    rows = T // GRID_W
    pairs = HEAD_DIM // 4
    row = jnp.repeat(jnp.arange(rows), GRID_W).astype(F32)
    col = jnp.tile(jnp.arange(GRID_W), rows).astype(F32)
    inv = ROPE_BASE ** (-jnp.arange(pairs, dtype=F32) / pairs)
    ang = jnp.concatenate([row[:, None] * inv, row[:, None] * inv, col[:, None] * inv, col[:, None] * inv], axis=1)
    cos, sin = jnp.cos(ang), jnp.sin(ang)
    first = (jnp.arange(HEAD_DIM) % (2 * pairs)) < pairs
    sin_a = jnp.where(first, -sin, 0.0)
    sin_b = jnp.where(first, 0.0, sin)
    pad = jnp.zeros((n_ctx, HEAD_DIM), F32)
    return (jnp.concatenate([pad + 1.0, cos], axis=0), jnp.concatenate([pad, sin_a], axis=0),
            jnp.concatenate([pad, sin_b], axis=0))


def _qk_prep(qkv, tabs, gains, l, n_q, n_kv, name):
    S = qkv.shape[0]
    W = qkv.shape[1]
    tm = _pick(S, [256, 128])
    cos, sin_a, sin_b = tabs

    def body(x_ref, cos_ref, sa_ref, sb_ref, g_ref, q_ref, k_ref):
        cs, sa, sb = cos_ref[...], sa_ref[...], sb_ref[...]
        for h in range(n_q + n_kv):
            xv = x_ref[:, h * 128:(h + 1) * 128].astype(F32)
            r = lax.rsqrt(jnp.mean(xv * xv, axis=-1, keepdims=True) + EPS)
            gain = g_ref[0:1, :] if h < n_q else g_ref[1:2, :]
            y = xv * r * gain
            out = (y * cs + pltpu.roll(y, 96, 1) * sa + pltpu.roll(y, 32, 1) * sb).astype(BF16)
            if h < n_q:
                q_ref[:, h * 128:(h + 1) * 128] = out
            else:
                k_ref[:, (h - n_q) * 128:(h - n_q + 1) * 128] = out

    tspec = pl.BlockSpec((tm, 128), lambda i: (i, 0))
    return pl.pallas_call(
        body, name=name, grid=(S // tm,),
        in_specs=[pl.BlockSpec((tm, W), lambda i: (i, 0)), tspec, tspec, tspec,
                  pl.BlockSpec((None, 2, 128), lambda i: (l, 0, 0))],
        out_specs=[pl.BlockSpec((tm, n_q * 128), lambda i: (i, 0)), pl.BlockSpec((tm, n_kv * 128), lambda i: (i, 0))],
        out_shape=[jax.ShapeDtypeStruct((S, n_q * 128), BF16), jax.ShapeDtypeStruct((S, n_kv * 128), BF16)],
        compiler_params=_cp(1),
    )(qkv, cos, sin_a, sin_b, gains)


def _qk_prep_bwd(dq, dk, dv, qkv, tabs, gains, l, n_q, n_kv, name):
    S, W = qkv.shape
    tm = _pick(S, [256, 128])
    cos, sin_a, sin_b = tabs

    def body(dq_ref, dk_ref, dv_ref, x_ref, cos_ref, sa_ref, sb_ref, g_ref, o_ref, acc_ref):
        @pl.when(pl.program_id(0) == 0)
        def _():
            acc_ref[...] = jnp.zeros_like(acc_ref)

        cs, sa, sb = cos_ref[...], sa_ref[...], sb_ref[...]
        dgq = jnp.zeros((1, 128), F32)
        dgk = jnp.zeros((1, 128), F32)
        for h in range(n_q + n_kv):
            if h < n_q:
                d_out = dq_ref[:, h * 128:(h + 1) * 128]
                gain = g_ref[0:1, :]
            else:
                d_out = dk_ref[:, (h - n_q) * 128:(h - n_q + 1) * 128]
                gain = g_ref[1:2, :]
            dy = d_out * cs + pltpu.roll(d_out * sa, 32, 1) + pltpu.roll(d_out * sb, 96, 1)
            xv = x_ref[:, h * 128:(h + 1) * 128].astype(F32)
            r = lax.rsqrt(jnp.mean(xv * xv, axis=-1, keepdims=True) + EPS)
            xhat = xv * r
            dg = jnp.sum(dy * xhat, axis=0, keepdims=True)
            if h < n_q:
                dgq = dgq + dg
            else:
                dgk = dgk + dg
            dxhat = dy * gain
            dx = r * (dxhat - xhat * jnp.mean(dxhat * xhat, axis=-1, keepdims=True))
            o_ref[:, h * 128:(h + 1) * 128] = dx.astype(BF16)
        v0 = (n_q + n_kv) * 128
        o_ref[:, v0:] = dv_ref[...].astype(BF16)
        rid = lax.broadcasted_iota(I32, (8, 128), 0)
        acc_ref[...] += jnp.where(rid == 0, dgq, 0.0) + jnp.where(rid == 1, dgk, 0.0)

    tspec = pl.BlockSpec((tm, 128), lambda i: (i, 0))
    return pl.pallas_call(
        body, name=name, grid=(S // tm,),
        in_specs=[pl.BlockSpec((tm, n_q * 128), lambda i: (i, 0)), pl.BlockSpec((tm, n_kv * 128), lambda i: (i, 0)),
                  pl.BlockSpec((tm, n_kv * 128), lambda i: (i, 0)), pl.BlockSpec((tm, W), lambda i: (i, 0)),
                  tspec, tspec, tspec, pl.BlockSpec((None, 2, 128), lambda i: (l, 0, 0))],
        out_specs=[pl.BlockSpec((tm, W), lambda i: (i, 0)), pl.BlockSpec((8, 128), lambda i: (0, 0))],
        out_shape=[jax.ShapeDtypeStruct((S, W), BF16), jax.ShapeDtypeStruct((8, 128), F32)],
        compiler_params=_cp(1),
    )(dq, dk, dv, qkv, cos, sin_a, sin_b, gains)


def _band_specs(width, col, nb, n_ctx):
    return [pl.BlockSpec((BLK, width), lambda i: (jnp.maximum(i - 1, 0), col)),
            pl.BlockSpec((BLK, width), lambda i: (i, col)),
            pl.BlockSpec((BLK, width), lambda i: (jnp.minimum(i + 1, nb - 1), col)),
            pl.BlockSpec((n_ctx, width), lambda i: (0, col))]


def _q_side_mask(i, S, n_ctx):
    shape = (GROUP * BLK, 3 * BLK + n_ctx)
    a = lax.broadcasted_iota(I32, shape, 0) & (BLK - 1)
    kk = lax.broadcasted_iota(I32, shape, 1)
    rq = i * BLK + a
    rk = (i - 1) * BLK + kk
    band = (rq >= n_ctx) & (rk >= n_ctx) & (rk < S) & (jnp.abs(rq - rk) <= WINDOW)
    return (kk >= 3 * BLK) | band


def _stack_heads(ref, g):
    return jnp.concatenate([ref[:, (GROUP * g + hh) * 128:(GROUP * g + hh + 1) * 128] for hh in range(GROUP)], axis=0)


def _stack_cols(ref, g):
    return jnp.concatenate([ref[:, GROUP * g + hh:GROUP * g + hh + 1] for hh in range(GROUP)], axis=0)


def _sink_col(sink_ref, l, g):
    return jnp.concatenate([jnp.full((BLK, 1), sink_ref[l, GROUP * g + hh], F32) for hh in range(GROUP)], axis=0)


def _attn_fwd(q, k, qkv, sink, l, n_ctx, name):
    S, DQ = q.shape
    DK = k.shape[1]
    n_kv = DK // 128
    nb = S // BLK
    vcol = (DQ + DK) // DK
    scale = HEAD_DIM ** -0.5

    def body(sink_ref, q_ref, kp, kc, kn, kx, vp, vc, vn, vx, o_ref, lse_ref):
        i = pl.program_id(0)
        mask = _q_side_mask(i, S, n_ctx)
        lane = lax.broadcasted_iota(I32, (BLK, 128), 1)
        lse_tile = jnp.zeros((BLK, 128), F32)
        outs = []
        for g in range(n_kv):
            sl = slice(g * 128, (g + 1) * 128)
            kcat = jnp.concatenate([kp[:, sl], kc[:, sl], kn[:, sl], kx[:, sl]], axis=0)
            vcat = jnp.concatenate([vp[:, sl], vc[:, sl], vn[:, sl], vx[:, sl]], axis=0)
            s = lax.dot_general(_stack_heads(q_ref, g), kcat, NT_DIMS, preferred_element_type=F32) * scale
            s = jnp.where(mask, s, NEG)
            sk = _sink_col(sink_ref, l, g)
            m = jnp.maximum(jnp.max(s, axis=1, keepdims=True), sk)
            e = jnp.exp(s - m)
            den = jnp.sum(e, axis=1, keepdims=True) + jnp.exp(sk - m)
            p = (e / den).astype(BF16)
            o = jnp.dot(p, vcat, preferred_element_type=F32)
            lse = m + jnp.log(den)
            for hh in range(GROUP):
                h = GROUP * g + hh
                outs.append(o[hh * BLK:(hh + 1) * BLK].astype(BF16))
                lse_tile = jnp.where(lane == h, lse[hh * BLK:(hh + 1) * BLK], lse_tile)
        o_ref[...] = jnp.concatenate(outs, axis=1)
        lse_ref[...] = lse_tile

    return pl.pallas_call(
        body, name=name, grid=(nb,),
        in_specs=[pl.BlockSpec(memory_space=pltpu.SMEM), pl.BlockSpec((BLK, DQ), lambda i: (i, 0))]
        + _band_specs(DK, 0, nb, n_ctx) + _band_specs(DK, vcol, nb, n_ctx),
        out_specs=[pl.BlockSpec((BLK, DQ), lambda i: (i, 0)), pl.BlockSpec((BLK, 128), lambda i: (i, 0))],
        out_shape=[jax.ShapeDtypeStruct((S, DQ), BF16), jax.ShapeDtypeStruct((S, 128), F32)],
        compiler_params=_cp(1),
    )(sink, q, k, k, k, k, qkv, qkv, qkv, qkv)


def _attn_bwd_q(q, k, qkv, o, do, lse, sink, l, n_ctx, name):
    S, DQ = q.shape
    DK = k.shape[1]
    n_kv = DK // 128
    nb = S // BLK
    vcol = (DQ + DK) // DK
    scale = HEAD_DIM ** -0.5

    def body(sink_ref, q_ref, kp, kc, kn, kx, vp, vc, vn, vx, o_ref, do_ref, lse_ref,
             dq_ref, delta_ref, dkx_ref, dvx_ref, dsink_ref):
        i = pl.program_id(0)

        @pl.when(i == 0)
        def _():
            dkx_ref[...] = jnp.zeros_like(dkx_ref)
            dvx_ref[...] = jnp.zeros_like(dvx_ref)
            dsink_ref[...] = jnp.zeros_like(dsink_ref)

        mask = _q_side_mask(i, S, n_ctx)
        lane = lax.broadcasted_iota(I32, (BLK, 128), 1)
        lane8 = lax.broadcasted_iota(I32, (8, 128), 1)
        row8 = lax.broadcasted_iota(I32, (8, 128), 0)
        delta_tile = jnp.zeros((BLK, 128), F32)
        dsink_upd = jnp.zeros((8, 128), F32)
        dqs, dkx_upd, dvx_upd = [], [], []
        for g in range(n_kv):
            sl = slice(g * 128, (g + 1) * 128)
            kcat = jnp.concatenate([kp[:, sl], kc[:, sl], kn[:, sl], kx[:, sl]], axis=0)
            vcat = jnp.concatenate([vp[:, sl], vc[:, sl], vn[:, sl], vx[:, sl]], axis=0)
            qg = _stack_heads(q_ref, g)
            dog = _stack_heads(do_ref, g)
            delta = jnp.sum(dog.astype(F32) * _stack_heads(o_ref, g).astype(F32), axis=1, keepdims=True)
            lse_g = _stack_cols(lse_ref, g)
            s = lax.dot_general(qg, kcat, NT_DIMS, preferred_element_type=F32) * scale
            p = jnp.exp(jnp.where(mask, s - lse_g, NEG))
            dp = lax.dot_general(dog, vcat, NT_DIMS, preferred_element_type=F32)
            ds = (p * (dp - delta) * scale).astype(BF16)
            dqg = jnp.dot(ds, kcat, preferred_element_type=F32)
            dkx_upd.append(lax.dot_general(ds[:, 3 * BLK:], qg, TN_DIMS, preferred_element_type=F32))
            dvx_upd.append(lax.dot_general(p.astype(BF16)[:, 3 * BLK:], dog, TN_DIMS, preferred_element_type=F32))
            dsk = -jnp.exp(_sink_col(sink_ref, l, g) - lse_g) * delta
            for hh in range(GROUP):
                h = GROUP * g + hh
                rs = slice(hh * BLK, (hh + 1) * BLK)
                dqs.append(dqg[rs])
                delta_tile = jnp.where(lane == h, delta[rs], delta_tile)
                tot = jnp.sum(dsk[rs], axis=0, keepdims=True)
                dsink_upd = dsink_upd + jnp.where((lane8 == h) & (row8 == 0), tot, 0.0)
        dq_ref[...] = jnp.concatenate(dqs, axis=1)
        dkx_ref[...] += jnp.concatenate(dkx_upd, axis=1)
        dvx_ref[...] += jnp.concatenate(dvx_upd, axis=1)
        delta_ref[...] = delta_tile
        dsink_ref[...] += dsink_upd

    blk = pl.BlockSpec((BLK, DQ), lambda i: (i, 0))
    stat = pl.BlockSpec((BLK, 128), lambda i: (i, 0))
    return pl.pallas_call(
        body, name=name, grid=(nb,),
        in_specs=[pl.BlockSpec(memory_space=pltpu.SMEM), blk] + _band_specs(DK, 0, nb, n_ctx)
        + _band_specs(DK, vcol, nb, n_ctx) + [blk, blk, stat],
        out_specs=[blk, stat, pl.BlockSpec((n_ctx, DK), lambda i: (0, 0)), pl.BlockSpec((n_ctx, DK), lambda i: (0, 0)),
                   pl.BlockSpec((8, 128), lambda i: (0, 0))],
        out_shape=[jax.ShapeDtypeStruct((S, DQ), F32), jax.ShapeDtypeStruct((S, 128), F32),
                   jax.ShapeDtypeStruct((n_ctx, DK), F32), jax.ShapeDtypeStruct((n_ctx, DK), F32),
                   jax.ShapeDtypeStruct((8, 128), F32)],
        compiler_params=_cp(1),
    )(sink, q, k, k, k, k, qkv, qkv, qkv, qkv, o, do, lse)


def _attn_bwd_kv(q, k, qkv, do, lse, delta, dkx, dvx, n_ctx, name):
    S, DQ = q.shape
    DK = k.shape[1]
    n_kv = DK // 128
    nb = S // BLK
    nctx_b = n_ctx // BLK
    vcol = (DQ + DK) // DK
    scale = HEAD_DIM ** -0.5

    def three(width):
        return [pl.BlockSpec((BLK, width), lambda j: (jnp.maximum(j - 1, 0), 0)),
                pl.BlockSpec((BLK, width), lambda j: (j, 0)),
                pl.BlockSpec((BLK, width), lambda j: (jnp.minimum(j + 1, nb - 1), 0))]

    def body(k_ref, v_ref, qp, qc, qn, dop, doc, don, lp, lc, ln, dlp, dlc, dln, dkx_ref, dvx_ref, dk_ref, dv_ref):
        j = pl.program_id(0)

        @pl.when(j < nctx_b)
        def _():
            dk_ref[...] = dkx_ref[...]
            dv_ref[...] = dvx_ref[...]

        @pl.when(j >= nctx_b)
        def _():
            shape = (3 * GROUP * BLK, BLK)
            t = lax.broadcasted_iota(I32, shape, 0)
            rq = (j - 1 + t // (GROUP * BLK)) * BLK + (t & (BLK - 1))
            rk = j * BLK + lax.broadcasted_iota(I32, shape, 1)
            valid = (rq >= n_ctx) & (rq < S) & (jnp.abs(rq - rk) <= WINDOW)
            dks, dvs = [], []
            for g in range(n_kv):
                sl = slice(g * 128, (g + 1) * 128)
                qcat = jnp.concatenate([_stack_heads(r, g) for r in (qp, qc, qn)], axis=0)
                docat = jnp.concatenate([_stack_heads(r, g) for r in (dop, doc, don)], axis=0)
                lse_c = jnp.concatenate([_stack_cols(r, g) for r in (lp, lc, ln)], axis=0)
                delta_c = jnp.concatenate([_stack_cols(r, g) for r in (dlp, dlc, dln)], axis=0)
                s = lax.dot_general(qcat, k_ref[:, sl], NT_DIMS, preferred_element_type=F32) * scale
                p = jnp.exp(jnp.where(valid, s - lse_c, NEG))
                dp = lax.dot_general(docat, v_ref[:, sl], NT_DIMS, preferred_element_type=F32)
                ds = (p * (dp - delta_c) * scale).astype(BF16)
                dks.append(lax.dot_general(ds, qcat, TN_DIMS, preferred_element_type=F32))
                dvs.append(lax.dot_general(p.astype(BF16), docat, TN_DIMS, preferred_element_type=F32))
            dk_ref[...] = jnp.concatenate(dks, axis=1)
            dv_ref[...] = jnp.concatenate(dvs, axis=1)

    cspec = pl.BlockSpec((BLK, DK), lambda j: (jnp.minimum(j, nctx_b - 1), 0))
    return pl.pallas_call(
        body, name=name, grid=(nb,),
        in_specs=[pl.BlockSpec((BLK, DK), lambda j: (j, 0)), pl.BlockSpec((BLK, DK), lambda j: (j, vcol))]
        + three(DQ) + three(DQ) + three(128) + three(128) + [cspec, cspec],
        out_specs=[pl.BlockSpec((BLK, DK), lambda j: (j, 0))] * 2,
        out_shape=[jax.ShapeDtypeStruct((S, DK), F32)] * 2, compiler_params=_cp(1),
    )(k, qkv, q, q, q, do, do, do, lse, lse, lse, delta, delta, delta, dkx, dvx)


def _ada_fwd(cond, w_ada, b_cols, name):
    lyr, D, C = w_ada.shape
    tc = _pick(C, [512, 384, 256, 128])

    def body(c_ref, w_ref, b_ref, o_ref):
        cv = c_ref[...]
        act = cv * _sigmoid(cv)
        o_ref[...] = jnp.dot(act, w_ref[...], preferred_element_type=F32,
                             precision=lax.Precision.HIGHEST) + b_ref[...]

    return pl.pallas_call(
        body, name=name, grid=(lyr, C // tc),
        in_specs=[pl.BlockSpec((16, D), lambda l, j: (0, 0)),
                  pl.BlockSpec((None, D, tc), lambda l, j: (l, 0, j)),
                  pl.BlockSpec((None, 1, tc), lambda l, j: (l, 0, j))],
        out_specs=pl.BlockSpec((None, 16, tc), lambda l, j: (l, 0, j)),
        out_shape=jax.ShapeDtypeStruct((lyr, 16, C), F32), compiler_params=_cp(2),
    )(cond, w_ada, b_cols)


def _ada_bwd(cond, d_out, w_ada, name):
    lyr, D, C = w_ada.shape
    tc = _pick(C, [512, 384, 256, 128])

    def body(c_ref, d_ref, w_ref, gw_ref, dc_ref):
        @pl.when((pl.program_id(0) == 0) & (pl.program_id(1) == 0))
        def _():
            dc_ref[...] = jnp.zeros_like(dc_ref)

        cv = c_ref[...]
        act = cv * _sigmoid(cv)
        dv = d_ref[...]
        gw_ref[...] = lax.dot_general(act, dv, TN_DIMS, preferred_element_type=F32, precision=lax.Precision.HIGHEST)
        dc_ref[...] += lax.dot_general(dv, w_ref[...], NT_DIMS, preferred_element_type=F32,
                                       precision=lax.Precision.HIGHEST)

    return pl.pallas_call(
        body, name=name, grid=(lyr, C // tc),
        in_specs=[pl.BlockSpec((16, D), lambda l, j: (0, 0)),
                  pl.BlockSpec((None, 16, tc), lambda l, j: (l, 0, j)),
                  pl.BlockSpec((None, D, tc), lambda l, j: (l, 0, j))],
        out_specs=[pl.BlockSpec((None, D, tc), lambda l, j: (l, 0, j)), pl.BlockSpec((16, D), lambda l, j: (0, 0))],
        out_shape=[jax.ShapeDtypeStruct((lyr, D, C), F32), jax.ShapeDtypeStruct((16, D), F32)],
        compiler_params=_cp(2),
    )(cond, d_out, w_ada)


def _sum_rows(d_rows, name):
    lyr, r, C = d_rows.shape

    def body(d_ref, o_ref):
        o_ref[...] = jnp.sum(d_ref[...], axis=0, keepdims=True)

    return pl.pallas_call(
        body, name=name, grid=(lyr,),
        in_specs=[pl.BlockSpec((None, r, C), lambda l: (l, 0, 0))],
        out_specs=pl.BlockSpec((None, 1, C), lambda l: (l, 0, 0)),
        out_shape=jax.ShapeDtypeStruct((lyr, 1, C), F32), compiler_params=_cp(1),
    )(d_rows)


def _cctx_grad(gathered, c_ctx_row, name):
    D = gathered.shape[1]

    def body(g_ref, c_ref, o_ref):
        acc = g_ref[0:16, :]
        for d in range(1, N_DEV):
            acc = acc + g_ref[16 * d:16 * (d + 1), :]
        cv = c_ref[...]
        sg = _sigmoid(cv)
        o_ref[...] = acc[8:16] * (sg * (1.0 + cv * (1.0 - sg)))

    return pl.pallas_call(
        body, name=name, out_shape=jax.ShapeDtypeStruct((8, D), F32),
        compiler_params=pltpu.CompilerParams(vmem_limit_bytes=VMEM_ELEMENTWISE),
    )(gathered, c_ctx_row)


def _pad_rows(a, rows):
    return jnp.concatenate([a, jnp.zeros((rows - a.shape[0],) + a.shape[1:], a.dtype)], axis=0)


def kernel(x, c, ctx, c_ctx, w_ada, b_ada, attn_w_qkv, attn_w_o, attn_q_gain, attn_k_gain, attn_sink, sc_w_in, sc_conv, sc_w_out, ffn_w_up, ffn_conv, ffn_conv_b, ffn_w_down, loss_target, m_c_ctx, m_w_ada, m_b_ada, m_attn_w_qkv, m_attn_w_o, m_attn_q_gain, m_attn_k_gain, m_attn_sink, m_sc_w_in, m_sc_conv, m_sc_w_out, m_ffn_w_up, m_ffn_conv, m_ffn_conv_b, m_ffn_w_down, v_c_ctx, v_w_ada, v_b_ada, v_attn_w_qkv, v_attn_w_o, v_attn_q_gain, v_attn_k_gain, v_attn_sink, v_sc_w_in, v_sc_conv, v_sc_w_out, v_ffn_w_up, v_ffn_conv, v_ffn_conv_b, v_ffn_w_down):
    T, D = x.shape[1], x.shape[2]
    L = ctx.shape[1]
    S = L + T
    depth = w_ada.shape[0]
    F = ffn_conv_b.shape[1]
    n_q = D // HEAD_DIM
    n_kv = n_q // GROUP
    ada_c = w_ada.shape[2]
    assert L % BLK == 0 and T % BLK == 0 and ada_c * N_DEV == 6 * D

    px, py, pc = _my_pos()
    me = 4 * px + 2 * py + pc
    me_idx = jnp.reshape(me, (1,)).astype(I32)

    tm_mm = _pick(S, [768, 704, 384, 256, 128])
    ts_tn = _pick(S, [2112, 1056, 768, 384, 256, 128])
    tm_half = _pick(S, [384, 256, 128])
    ts_lane = _pick(S, [2816, 768, 384, 256, 128])

    c_all = _gather_small(_pad_rows(c, 8), "gather_cond")
    cond = jnp.concatenate([c_all[0::8], c_ctx[None, :], jnp.zeros((7, D), F32)], axis=0)
    b_cols = lax.dynamic_slice_in_dim(b_ada, me * ada_c, ada_c, axis=1)[:, None, :]
    ada_mine = _ada_fwd(cond, w_ada, b_cols, "ada_fwd")
    ada_all = _gather_small(ada_mine.reshape(depth * 16, ada_c), "gather_ada")
    ada_all = ada_all.reshape(N_DEV, depth, 16, ada_c)
    ada_rows = jnp.transpose(ada_all, (1, 2, 0, 3)).reshape(depth, 16, 6, D)
    mod_lat = lax.dynamic_index_in_dim(ada_rows, me, axis=1, keepdims=False)
    mods = jnp.stack([ada_rows[:, 8], mod_lat], axis=1)

    gathered = [None] * depth
    tabs = _rope_tables(T, L)
    gains = jnp.stack([attn_q_gain, attn_k_gain], axis=1)
    conv_b3 = ffn_conv_b[:, None, :]
    sc_conv_all = _gather_small(_pad_rows(sc_conv.reshape(-1, sc_conv.shape[2]), 8), "gather_scconv")
    ffn_conv_all = _gather_small(_pad_rows(ffn_conv.reshape(-1, ffn_conv.shape[2]), 16), "gather_ffnconv")
    n_sc = sc_conv.shape[0]
    sc_conv_full = jnp.transpose(sc_conv_all.reshape(N_DEV, 8, -1)[:, :n_sc * 3], (1, 0, 2)).reshape(n_sc, 3, D)
    ffn_conv_full = jnp.transpose(ffn_conv_all.reshape(N_DEV, 16, -1)[:, :depth * 3], (1, 0, 2)).reshape(depth, 3, F)

    def start_weights(l, tag, after):
        if tag == "ffn":
            ws = [(ffn_w_up, l), (ffn_w_down, l)]
        else:
            ws = [(attn_w_qkv, l // 2), (attn_w_o, l // 2)] if l % 2 == 0 else [(sc_w_in, l // 2), (sc_w_out, l // 2)]
        lands = [_cast_layer(w, j, me_idx, f"cast_{tag}{k}_{l}") for k, (w, j) in enumerate(ws)]
        return _gather_start(lands, after, f"gather_start_{tag}{l}")

    def wait_weights(flight, after, name):
        send_sems, recv_sems, lands, _ = flight
        return _gather_wait(lands, send_sems, recv_sems, after, name)

    flight_mix = start_weights(0, "mix", [mods, sc_conv_full, ffn_conv_full])
    mods = mods + flight_mix[3][0, 0]

    xs = jnp.concatenate([ctx[0], x[0]], axis=0)
    saved = []
    for l in range(depth):
        j = l // 2
        mod = mods[l]
        if l == 0:
            w_a, w_b = wait_weights(flight_mix, mods, "gather_wait_mix0")
            flight_ffn = start_weights(0, "ffn", [w_a])
            mod = mod + flight_ffn[3][0, 0]
            h, h_t = _norm_mod(xs, mod, 0, L, "norm_m0")
        else:
            h, h_t = _norm_mod(xs, mod, 0, L, f"norm_m{l}")
            w_a, w_b = wait_weights(flight_mix, h, f"gather_wait_mix{l}")
        if l % 2 == 0:
            qkv = _mm_nn(h, w_a, tm=tm_mm, tn=w_a.shape[2], out_dtype=BF16, name=f"qkv{l}")
            qr, kr = _qk_prep(qkv, tabs, gains, j, n_q, n_kv, f"qk_prep{l}")
            z, lse = _attn_fwd(qr, kr, qkv, attn_sink, j, L, f"attn{l}")
            mix = (qkv, qr, kr, lse)
        else:
            u = _mm_nn(h, w_a, tm=tm_mm, tn=w_a.shape[2], out_dtype=BF16, name=f"scin{l}")
            z = _sc_act(u, sc_conv_full, j, L, f"sc_act{l}")
            mix = (u,)
        if l + 1 < depth:
            flight_mix = start_weights(l + 1, "mix", [z])
            mod = mod + flight_mix[3][0, 0]
        y_m, x1, h2, h2_t = _mm_nn_resid_norm(z, w_b.reshape(D, D), xs, mod, 2, 3, L, tm=tm_half, name=f"mixout{l}")
        w_up, w_down = wait_weights(flight_ffn, x1, f"gather_wait_ffn{l}")
        gathered[l] = (w_a, w_b, w_up, w_down)
        u_f = _mm_nn(h2, w_up, tm=tm_mm, tn=w_up.shape[2], out_dtype=BF16, name=f"up{l}")
        a_f = _ffn_act(u_f, ffn_conv_full, conv_b3, l, L, f"ffn_act{l}")
        if l + 1 < depth:
            flight_ffn = start_weights(l + 1, "ffn", [a_f])
            mod = mod + flight_ffn[3][0, 0]
        y_f, x2 = _mm_nn_resid(a_f, w_down.reshape(F, D), x1, mod, 5, L, tm=tm_mm, tn=_pick(D, [512]), name=f"down{l}")
        saved.append((xs, h_t, mix, z, y_m, x1, h2_t, u_f, a_f, y_f))
        xs = x2

    dx, sq = _loss_grad(xs, loss_target[0], L, "loss")
    loss = lax.psum(sq[0, 0], ("x", "y", "c"))

    dmods = [None] * depth
    g_conv_b, g_ffn_conv, g_sc_conv = [None] * depth, [None] * depth, [None] * n_sc
    g_gain, g_sink = [None] * (depth - n_sc), [None] * (depth - n_sc)
    rs_flight = [None] * depth
    sent = jnp.zeros((), F32)
    for l in reversed(range(depth)):
        j = l // 2
        w_a, w_b, w_up, w_down = gathered[l]
        mod = mods[l] + sent
        x0, h_t, mix, z, y_m, x1, h2_t, u_f, a_f, y_f = saved[l]
        da, dy, s_gf = _gate_dgrad(dx, y_f, mod, 5, L, w_down.reshape(F, D), tm=tm_mm, tn=_pick(F, [1408, 512]),
                                   name=f"down_dgrad{l}")
        gw_down = _mm_tn(a_f, [dy], nb=1, tka=_pick(F, [1408, 512]), tn=_pick(D, [1024, 512]), ts=ts_tn, name=f"down_wgrad{l}")
        dgate, dval, s_conv = _ffn_act_bwd(u_f, da, ffn_conv_full, conv_b3, l, L, f"ffn_act_bwd{l}")
        dh2 = _mm_nt_acc([dgate, dval], w_up, tm=tm_mm, name=f"up_dgrad{l}", vmem=VMEM_NEAR_FULL)
        dx1, s_nf = _norm_mod_bwd(dh2, x1, mod, dx, 3, L, f"norm_f_bwd{l}")
        gw_up = _mm_tn(h2_t, [dgate, dval], nb=N_DEV, tka=_pick(D, [512]), tn=w_up.shape[2], ts=ts_lane,
                       name=f"up_wgrad{l}", a_transposed=True)
        g_ffn_conv[l], g_conv_b[l] = s_conv[0:3], s_conv[3]
        rs_ffn = _rs_start([gw_up, gw_down.reshape(N_DEV, -1, D)], f"rs_start_ffn{l}")
        mod = mods[l] + rs_ffn[4][0, 0]
        dz, dy, s_gm = _gate_dgrad(dx1, y_m, mod, 2, L, w_b.reshape(D, D), tm=tm_mm, tn=_pick(D, [1024, 512]),
                                   name=f"mixout_dgrad{l}")
        gw_b = _mm_tn(z, [dy], nb=1, tka=_pick(D, [1024, 512]), tn=_pick(D, [1024, 512]), ts=ts_tn, name=f"mixout_wgrad{l}")
        if l % 2 == 0:
            qkv, qr, kr, lse = mix
            dq, delta, dkx, dvx, s_sink = _attn_bwd_q(qr, kr, qkv, z, dz, lse, attn_sink, j, L, f"attn_bwd_q{l}")
            dk, dv = _attn_bwd_kv(qr, kr, qkv, dz, lse, delta, dkx, dvx, L, f"attn_bwd_kv{l}")
            du_m, s_gain = _qk_prep_bwd(dq, dk, dv, qkv, tabs, gains, j, n_q, n_kv, f"qk_prep_bwd{l}")
            g_gain[j], g_sink[j] = s_gain[0:2], s_sink[0]
        else:
            (u,) = mix
            du_m, s_scconv = _sc_act_bwd(u, dz, sc_conv_full, j, L, f"sc_act_bwd{l}")
            g_sc_conv[j] = s_scconv[0:3]
        dh = _mm_nt_acc([du_m], w_a, tm=tm_mm, name=f"mixin_dgrad{l}")
        gw_a = _mm_tn(h_t, [du_m], nb=N_DEV, tka=_pick(D, [1024, 512]), tn=w_a.shape[2], ts=ts_lane,
                      name=f"mixin_wgrad{l}", a_transposed=True)
        dx, s_nm = _norm_mod_bwd(dh, x0, mod, dx1, 0, L, f"norm_m_bwd{l}")
        dmods[l] = jnp.stack([jnp.stack([s_nm[2 * k], s_nm[2 * k + 1], s_gm[k], s_nf[2 * k], s_nf[2 * k + 1], s_gf[k]])
                              for k in range(2)])
        rs_mix = _rs_start([gw_a, gw_b.reshape(N_DEV, -1, D)], f"rs_start_mix{l}")
        sent = rs_mix[4][0, 0]
        rs_flight[l] = (rs_mix, rs_ffn)

    grad_x = dx[L:][None]

    big_w = {"qkv": (attn_w_qkv, m_attn_w_qkv, v_attn_w_qkv), "wo": (attn_w_o, m_attn_w_o, v_attn_w_o),
             "scin": (sc_w_in, m_sc_w_in, v_sc_w_in), "scout": (sc_w_out, m_sc_w_out, v_sc_w_out),
             "up": (ffn_w_up, m_ffn_w_up, v_ffn_w_up), "down": (ffn_w_down, m_ffn_w_down, v_ffn_w_down)}
    big_res = {k: None for k in big_w}
    for l in reversed(range(depth)):
        j = l // 2
        groups = [(["qkv", "wo"] if l % 2 == 0 else ["scin", "scout"], [j, j]), (["up", "down"], [l, l])]
        for (names, idxs), flight, tag in reversed(list(zip(groups, rs_flight[l], ("mix", "ffn")))):
            send_sems, recv_sems, own, zones, _ = flight
            own, zones = _rs_wait(own, zones, send_sems, recv_sems, dx, f"rs_wait_{tag}{l}")
            for n, li, p, z in zip(names, idxs, own, zones):
                w, m, v = big_w[n]
                big_res[n] = _adamw_reduced(p, z, me_idx, w, m, v, li, big_res[n], f"adamw_{n}{l}")

    n_attn = depth - n_sc
    pack = [jnp.stack(dmods)[:, 0].reshape(-1, 128), jnp.stack(dmods)[:, 1].reshape(-1, 128),
            jnp.stack(g_gain).reshape(-1, 128), jnp.stack(g_sink),
            jnp.stack(g_conv_b).reshape(-1, 128), jnp.stack(g_ffn_conv).reshape(-1, 128),
            jnp.stack(g_sc_conv).reshape(-1, 128)]
    used = [p.shape[0] for p in pack]
    pack = [_pad_rows(p, -(-p.shape[0] // 8) * 8) for p in pack]
    sizes = [p.shape[0] for p in pack]
    flat = jnp.concatenate(pack, axis=0)
    rows = flat.shape[0]
    small_all = _gather_small(flat, "gather_small_grads")
    small_sum = _sum8(small_all, rows, "sum_small_grads")
    offs = [sum(sizes[:k]) for k in range(len(sizes))]
    seg = lambda a, k: a[offs[k]:offs[k] + used[k]]
    dmod_ctx = seg(small_sum, 0).reshape(depth, 6 * D)
    dmod_lat = small_all.reshape(N_DEV, rows, 128)[:, offs[1]:offs[1] + used[1]].reshape(N_DEV, depth, 6 * D)
    g_gain_sum = seg(small_sum, 2).reshape(n_attn, 2, 128)
    g_sink_sum = seg(small_sum, 3)[:n_attn, :n_q]
    g_conv_b_sum = seg(small_sum, 4).reshape(depth, F)
    g_ffn_conv_sum = seg(small_sum, 5).reshape(depth, 3, F)
    g_sc_conv_sum = seg(small_sum, 6).reshape(n_sc, 3, D)

    d_rows = jnp.concatenate([jnp.transpose(dmod_lat, (1, 0, 2)), dmod_ctx[:, None, :],
                              jnp.zeros((depth, 7, 6 * D), F32)], axis=1)
    d_cols = lax.dynamic_slice_in_dim(d_rows, me * ada_c, ada_c, axis=2)
    g_w_ada, dcond_part = _ada_bwd(cond, d_cols, w_ada, "ada_bwd")
    dcond_all = _gather_small(dcond_part, "gather_dcond")
    g_c_ctx = _cctx_grad(dcond_all, jnp.broadcast_to(c_ctx[None, :], (8, D)), "cctx_grad")[0]
    g_b_ada = _sum_rows(d_rows, "b_ada_grad")[:, 0]

    def small_adam(w, g, m, v, name):
        w2 = w.reshape(-1, w.shape[-1])
        d, m2, v2 = _adamw_plain(w2, g.reshape(w2.shape), m.reshape(w2.shape), v.reshape(w2.shape), name)
        return g.reshape(w.shape), d.reshape(w.shape), m2.reshape(w.shape), v2.reshape(w.shape)

    g_sc_conv_mine = lax.dynamic_slice_in_dim(g_sc_conv_sum, me * sc_conv.shape[2], sc_conv.shape[2], axis=2)
    g_ffn_conv_mine = lax.dynamic_slice_in_dim(g_ffn_conv_sum, me * ffn_conv.shape[2], ffn_conv.shape[2], axis=2)
    res = {
        "c_ctx": small_adam(c_ctx[None, :], g_c_ctx[None, :], m_c_ctx[None, :], v_c_ctx[None, :], "adamw_c_ctx"),
        "b_ada": small_adam(b_ada, g_b_ada, m_b_ada, v_b_ada, "adamw_b_ada"),
        "attn_q_gain": small_adam(attn_q_gain, g_gain_sum[:, 0], m_attn_q_gain, v_attn_q_gain, "adamw_q_gain"),
        "attn_k_gain": small_adam(attn_k_gain, g_gain_sum[:, 1], m_attn_k_gain, v_attn_k_gain, "adamw_k_gain"),
        "attn_sink": small_adam(attn_sink, g_sink_sum, m_attn_sink, v_attn_sink, "adamw_sink"),
        "sc_conv": small_adam(sc_conv, g_sc_conv_mine, m_sc_conv, v_sc_conv, "adamw_sc_conv"),
        "ffn_conv": small_adam(ffn_conv, g_ffn_conv_mine, m_ffn_conv, v_ffn_conv, "adamw_ffn_conv"),
        "ffn_conv_b": small_adam(ffn_conv_b, g_conv_b_sum, m_ffn_conv_b, v_ffn_conv_b, "adamw_conv_b"),
    }
    res["c_ctx"] = tuple(t[0] for t in res["c_ctx"])
    res["w_ada"] = (g_w_ada,) + tuple(_adamw_tiled(w_ada, g_w_ada, m_w_ada, v_w_ada, "adamw_w_ada"))
    res["attn_w_qkv"], res["attn_w_o"] = big_res["qkv"], big_res["wo"]
    res["sc_w_in"], res["sc_w_out"] = big_res["scin"], big_res["scout"]
    res["ffn_w_up"], res["ffn_w_down"] = big_res["up"], big_res["down"]

    order = ["c_ctx", "w_ada", "b_ada", "attn_w_qkv", "attn_w_o", "attn_q_gain", "attn_k_gain", "attn_sink",
             "sc_w_in", "sc_conv", "sc_w_out", "ffn_w_up", "ffn_conv", "ffn_conv_b", "ffn_w_down"]
    outs = [loss, grad_x]
    for t in range(4):
        outs += [res[n][t] for n in order]
    return tuple(outs)
```

```python
import jax
import jax.numpy as jnp
from jax import lax
from jax.experimental import pallas as pl
from jax.experimental.pallas import tpu as pltpu

F32 = jnp.float32
BF16 = jnp.bfloat16
I32 = jnp.int32

N_DEV = 8
HEAD_DIM = 128
GROUP = 4
WINDOW = 128
BLK = 128
GRID_W = 64
ROPE_BASE = 10000.0
EPS = 1e-6
NEG = -1e30
HALO = 16

ADAM_LR = 0.001
ADAM_B1 = 0.9
ADAM_B2 = 0.999
ADAM_EPS = 1e-08
ADAM_WD = 0.01
ADAM_STEP = 10

V7X_VMEM_BYTES = 64 << 20
VMEM_MATMUL = 52 << 20
VMEM_ELEMENTWISE = 44 << 20
VMEM_NEAR_FULL = 62 << 20

MESH = pl.DeviceIdType.MESH
ANY = pl.BlockSpec(memory_space=pl.ANY)
HBM = pl.BlockSpec(memory_space=pltpu.HBM)
SEM = pl.BlockSpec(memory_space=pltpu.SEMAPHORE)
EFFECT = pltpu.SideEffectType.DATAFLOW_SIDE_EFFECTING

NT_DIMS = (((1,), (1,)), ((), ()))
TN_DIMS = (((0,), (0,)), ((), ()))


def _pick(n, cands):
    for t in cands:
        if n % t == 0:
            return t
    raise ValueError(f"no tile for {n} in {cands}")


def _cp(n_axes, vmem=VMEM_ELEMENTWISE):
    return pltpu.CompilerParams(dimension_semantics=("arbitrary",) * n_axes, vmem_limit_bytes=vmem)


def _rows(i, tm, off=0):
    return i * tm + off + lax.broadcasted_iota(I32, (tm, 1), 0)


def _my_pos():
    return lax.axis_index("x"), lax.axis_index("y"), lax.axis_index("c")


def _gather_small(x_shard, name):
    m_per, n = x_shard.shape

    def body(x_ref, out_ref, send_sems, recv_sems, local_sem):
        x, y, c = _my_pos()
        me, sibling = (x, y, c), (x, y, 1 - c)
        chips = [(1 - x, y), (x, 1 - y), (1 - x, 1 - y)]

        def rows(px, py, pc):
            return out_ref.at[pl.ds((4 * px + 2 * py + pc) * m_per, m_per), :]

        def copy(k, block, to, src=None):
            return pltpu.make_async_remote_copy(
                src_ref=rows(*block) if src is None else src, dst_ref=rows(*block),
                send_sem=send_sems.at[k], recv_sem=recv_sems.at[k], device_id=to, device_id_type=MESH)

        mine = pltpu.make_async_copy(x_ref, rows(*me), local_sem)
        mine.start()
        first = [copy(0, me, sibling, src=x_ref)]
        first += [copy(1 + j, me, (*chip, c), src=x_ref) for j, chip in enumerate(chips)]
        for cp in first:
            cp.start()
        passed = [copy(4 + j, (*chip, c), sibling) for j, chip in enumerate(chips)]
        for j, chip in enumerate(chips):
            copy(1 + j, (*chip, c), me).wait_recv()
            passed[j].start()
        copy(0, sibling, me).wait_recv()
        for j, chip in enumerate(chips):
            copy(4 + j, (*chip, 1 - c), me).wait_recv()
        for cp in first + passed:
            cp.wait_send()
        mine.wait()

    return pl.pallas_call(
        body, name=name,
        out_shape=jax.ShapeDtypeStruct((N_DEV * m_per, n), x_shard.dtype),
        in_specs=[pl.BlockSpec(memory_space=pltpu.VMEM)],
        out_specs=pl.BlockSpec(memory_space=pltpu.VMEM),
        scratch_shapes=[pltpu.SemaphoreType.DMA((7,)), pltpu.SemaphoreType.DMA((7,)), pltpu.SemaphoreType.DMA],
        compiler_params=pltpu.CompilerParams(vmem_limit_bytes=VMEM_ELEMENTWISE),
    )(x_shard)


def _peer(k):
    x, y, c = _my_pos()
    b = k + 1
    return ((1 - x) if b & 4 else x, (1 - y) if b & 2 else y, (1 - c) if b & 1 else c)


def _slot(p):
    return 4 * p[0] + 2 * p[1] + p[2]


def _in_hbm(a):
    return pltpu.with_memory_space_constraint(a, pltpu.HBM)


def _gather_start(lands, after, name):
    n = len(lands)

    def body(*refs):
        l_refs, send_sems, recv_sems = refs[:n], refs[n + len(after)], refs[n + len(after) + 1]
        token = refs[2 * n + len(after) + 2]
        me = _slot(_my_pos())
        for a in range(n):
            for k in range(7):
                pltpu.make_async_remote_copy(
                    src_ref=l_refs[a].at[me], dst_ref=l_refs[a].at[me],
                    send_sem=send_sems.at[7 * a + k], recv_sem=recv_sems.at[7 * a + k],
                    device_id=_peer(k), device_id_type=MESH).start()
        token[...] = jnp.zeros_like(token)

    out = pl.pallas_call(
        body, name=name,
        out_shape=(pltpu.SemaphoreType.DMA((7 * n,)), pltpu.SemaphoreType.DMA((7 * n,)),
                   *[pltpu.HBM(a.shape, a.dtype) for a in lands], jax.ShapeDtypeStruct((8, 128), F32)),
        in_specs=[HBM] * n + [ANY] * len(after),
        out_specs=(SEM, SEM, *[HBM] * n, pl.BlockSpec(memory_space=pltpu.VMEM)),
        input_output_aliases={a: 2 + a for a in range(n)},
        compiler_params=pltpu.CompilerParams(has_side_effects=EFFECT),
    )(*[_in_hbm(a) for a in lands], *after)
    return out[0], out[1], list(out[2:2 + n]), out[2 + n]


def _gather_wait(lands, send_sems, recv_sems, after, name):
    n = len(lands)

    def body(*refs):
        l_refs, ss, rs = refs[:n], refs[n], refs[n + 1]
        me = _slot(_my_pos())
        for a in range(n):
            for k in range(7):
                cp = pltpu.make_async_remote_copy(
                    src_ref=l_refs[a].at[me], dst_ref=l_refs[a].at[_slot(_peer(k))],
                    send_sem=ss.at[7 * a + k], recv_sem=rs.at[7 * a + k], device_id=_peer(k), device_id_type=MESH)
                cp.wait_send()
                cp.wait_recv()

    out = pl.pallas_call(
        body, name=name,
        out_shape=tuple(pltpu.HBM(a.shape, a.dtype) for a in lands),
        in_specs=[HBM] * n + [SEM, SEM, ANY], out_specs=tuple([HBM] * n),
        input_output_aliases={a: a for a in range(n)},
        compiler_params=pltpu.CompilerParams(has_side_effects=EFFECT),
    )(*lands, send_sems, recv_sems, after)
    return list(out)


def _rs_start(grads, name):
    n = len(grads)

    def body(*refs):
        g_refs, z_refs, send_sems, recv_sems = refs[:n], refs[n:2 * n], refs[2 * n], refs[2 * n + 1]
        token = refs[4 * n + 2]
        for a in range(n):
            for k in range(7):
                pltpu.make_async_remote_copy(
                    src_ref=g_refs[a].at[_slot(_peer(k))], dst_ref=z_refs[a].at[k],
                    send_sem=send_sems.at[7 * a + k], recv_sem=recv_sems.at[7 * a + k],
                    device_id=_peer(k), device_id_type=MESH).start()
        token[...] = jnp.zeros_like(token)

    zones = [lax.empty((7,) + g.shape[1:], g.dtype) for g in grads]
    out = pl.pallas_call(
        body, name=name,
        out_shape=(pltpu.SemaphoreType.DMA((7 * n,)), pltpu.SemaphoreType.DMA((7 * n,)),
                   *[pltpu.HBM(a.shape, a.dtype) for a in grads], *[pltpu.HBM(z.shape, z.dtype) for z in zones],
                   jax.ShapeDtypeStruct((8, 128), F32)),
        in_specs=[HBM] * (2 * n),
        out_specs=(SEM, SEM, *[HBM] * (2 * n), pl.BlockSpec(memory_space=pltpu.VMEM)),
        input_output_aliases={a: 2 + a for a in range(2 * n)},
        compiler_params=pltpu.CompilerParams(has_side_effects=EFFECT),
    )(*[_in_hbm(a) for a in grads], *[_in_hbm(z) for z in zones])
    return out[0], out[1], list(out[2:2 + n]), list(out[2 + n:2 + 2 * n]), out[2 + 2 * n]


def _rs_wait(grads, zones, send_sems, recv_sems, after, name):
    n = len(grads)

    def body(*refs):
        g_refs, z_refs, ss, rs = refs[:n], refs[n:2 * n], refs[2 * n], refs[2 * n + 1]
        for a in range(n):
            for k in range(7):
                cp = pltpu.make_async_remote_copy(
                    src_ref=g_refs[a].at[_slot(_peer(k))], dst_ref=z_refs[a].at[k],
                    send_sem=ss.at[7 * a + k], recv_sem=rs.at[7 * a + k], device_id=_peer(k), device_id_type=MESH)
                cp.wait_send()
                cp.wait_recv()

    out = pl.pallas_call(
        body, name=name,
        out_shape=tuple(pltpu.HBM(a.shape, a.dtype) for a in list(grads) + list(zones)),
        in_specs=[HBM] * (2 * n) + [SEM, SEM, ANY], out_specs=tuple([HBM] * (2 * n)),
        input_output_aliases={a: a for a in range(2 * n)},
        compiler_params=pltpu.CompilerParams(has_side_effects=EFFECT),
    )(*grads, *zones, send_sems, recv_sems, after)
    return list(out[:n]), list(out[n:])


def _cast_layer(w, l, me_idx, name):
    _, r, c = w.shape
    tr = _pick(r, [512, 256, 128, 64, 32, 16])

    def body(me_ref, w_ref, o_ref):
        o_ref[...] = w_ref[...].astype(BF16)

    return pl.pallas_call(
        body, name=name,
        grid_spec=pltpu.PrefetchScalarGridSpec(
            num_scalar_prefetch=1, grid=(r // tr,),
            in_specs=[pl.BlockSpec((None, tr, c), lambda i, me_ref: (l, i, 0))],
            out_specs=pl.BlockSpec((None, tr, c), lambda i, me_ref: (me_ref[0], i, 0))),
        out_shape=jax.ShapeDtypeStruct((N_DEV, r, c), BF16), compiler_params=_cp(1),
    )(me_idx, w)


def _adam_math(w, g, m, v):
    m2 = ADAM_B1 * m + (1.0 - ADAM_B1) * g
    v2 = ADAM_B2 * v + (1.0 - ADAM_B2) * (g * g)
    m_hat = m2 / (1.0 - ADAM_B1 ** ADAM_STEP)
    v_hat = v2 / (1.0 - ADAM_B2 ** ADAM_STEP)
    delta = -ADAM_LR * (m_hat / (jnp.sqrt(v_hat) + ADAM_EPS) + ADAM_WD * w)
    return delta, m2, v2


def _adamw_reduced(own, zone, me_idx, w, m, v, l, outs, name):
    _, r, c = own.shape
    tr = _pick(r, [256, 128, 64, 32, 16])
    if outs is None:
        outs = [lax.empty(w.shape, F32) for _ in range(4)]

    def body(me_ref, p_ref, z_ref, w_ref, m_ref, v_ref, *rest):
        g_out, d_out, m_out, v_out = rest[4:]
        g = p_ref[...].astype(F32)
        for k in range(7):
            g = g + z_ref[k].astype(F32)
        d, m2, v2 = _adam_math(w_ref[...], g, m_ref[...], v_ref[...])
        g_out[...] = g
        d_out[...] = d
        m_out[...] = m2
        v_out[...] = v2

    wspec = pl.BlockSpec((None, tr, c), lambda i, me_ref: (l, i, 0))
    return pl.pallas_call(
        body, name=name,
        grid_spec=pltpu.PrefetchScalarGridSpec(
            num_scalar_prefetch=1, grid=(r // tr,),
            in_specs=[pl.BlockSpec((None, tr, c), lambda i, me_ref: (me_ref[0], i, 0)),
                      pl.BlockSpec((7, tr, c), lambda i, me_ref: (0, i, 0)), wspec, wspec, wspec] + [ANY] * 4,
            out_specs=[wspec] * 4),
        out_shape=[jax.ShapeDtypeStruct(w.shape, F32)] * 4,
        input_output_aliases={6 + t: t for t in range(4)}, compiler_params=_cp(1),
    )(me_idx, own, zone, w, m, v, *outs)


def _adamw_plain(w, g, m, v, name):
    def body(w_ref, g_ref, m_ref, v_ref, d_out, m_out, v_out):
        d, m2, v2 = _adam_math(w_ref[...], g_ref[...], m_ref[...], v_ref[...])
        d_out[...] = d
        m_out[...] = m2
        v_out[...] = v2

    return pl.pallas_call(
        body, name=name, out_shape=[jax.ShapeDtypeStruct(w.shape, F32)] * 3,
        compiler_params=pltpu.CompilerParams(vmem_limit_bytes=VMEM_ELEMENTWISE),
    )(w, g, m, v)


def _adamw_tiled(w, g, m, v, name):
    lyr, r, c = w.shape
    tr = _pick(r, [256, 128, 64, 32, 16, 8])

    def body(w_ref, g_ref, m_ref, v_ref, d_out, m_out, v_out):
        d, m2, v2 = _adam_math(w_ref[...], g_ref[...], m_ref[...], v_ref[...])
        d_out[...] = d
        m_out[...] = m2
        v_out[...] = v2

    spec = pl.BlockSpec((None, tr, c), lambda l, i: (l, i, 0))
    return pl.pallas_call(
        body, name=name, grid=(lyr, r // tr), in_specs=[spec] * 4, out_specs=[spec] * 3,
        out_shape=[jax.ShapeDtypeStruct(w.shape, F32)] * 3, compiler_params=_cp(2),
    )(w, g, m, v)


def _sum8(gathered, rows, name):
    def body(g_ref, o_ref):
        acc = g_ref[0:rows, :]
        for d in range(1, N_DEV):
            acc = acc + g_ref[d * rows:(d + 1) * rows, :]
        o_ref[...] = acc

    return pl.pallas_call(
        body, name=name, out_shape=jax.ShapeDtypeStruct((rows, 128), F32),
        compiler_params=pltpu.CompilerParams(vmem_limit_bytes=VMEM_ELEMENTWISE),
    )(gathered)


MXU_COLS = 256


def _pairable(nb, nc):
    return nb % 2 == 0 and nc % MXU_COLS == MXU_COLS // 2 and nc > MXU_COLS // 2


def _mm_nn(a, b3, *, tm, tn, out_dtype, name):
    M, K = a.shape
    nb, _, nc = b3.shape
    q = nc // tn

    if _pairable(nb, nc) and tn == nc:
        cut = nc - 128

        def pair_body(a_ref, b_ref, o_ref):
            av = a_ref[...]
            mid = jnp.concatenate([b_ref[0, :, cut:nc], b_ref[1, :, 0:128]], axis=1)
            o_ref[:, 0:cut] = jnp.dot(av, b_ref[0, :, 0:cut], preferred_element_type=F32).astype(o_ref.dtype)
            o_ref[:, cut:nc + 128] = jnp.dot(av, mid, preferred_element_type=F32).astype(o_ref.dtype)
            o_ref[:, nc + 128:2 * nc] = jnp.dot(av, b_ref[1, :, 128:nc], preferred_element_type=F32).astype(o_ref.dtype)

        return pl.pallas_call(
            pair_body, name=name, grid=(nb // 2, M // tm),
            in_specs=[pl.BlockSpec((tm, K), lambda j, i: (i, 0)),
                      pl.BlockSpec((2, K, nc), lambda j, i: (j, 0, 0))],
            out_specs=pl.BlockSpec((tm, 2 * nc), lambda j, i: (i, j)),
            out_shape=jax.ShapeDtypeStruct((M, nb * nc), out_dtype), compiler_params=_cp(2, VMEM_MATMUL),
        )(a, b3)

    def body(a_ref, b_ref, o_ref):
        o_ref[...] = jnp.dot(a_ref[...], b_ref[...], preferred_element_type=F32).astype(o_ref.dtype)

    return pl.pallas_call(
        body, name=name, grid=(nb * q, M // tm),
        in_specs=[pl.BlockSpec((tm, K), lambda j, i: (i, 0)),
                  pl.BlockSpec((None, K, tn), lambda j, i: (j // q, 0, j % q))],
        out_specs=pl.BlockSpec((tm, tn), lambda j, i: (i, j)),
        out_shape=jax.ShapeDtypeStruct((M, nb * nc), out_dtype), compiler_params=_cp(2, VMEM_MATMUL),
    )(a, b3)


def _mm_nn_resid(a, b2, x_old, mod, gate_row, n_ctx, *, tm, tn, name):
    M, K = a.shape
    N = b2.shape[1]

    def body(a_ref, b_ref, x_ref, mod_ref, y_ref, xn_ref):
        y = jnp.dot(a_ref[...], b_ref[...], preferred_element_type=F32)
        is_ctx = _rows(pl.program_id(1), tm) < n_ctx
        g = jnp.where(is_ctx, mod_ref[0, gate_row:gate_row + 1, :], mod_ref[1, gate_row:gate_row + 1, :])
        y_ref[...] = y.astype(BF16)
        xn_ref[...] = x_ref[...] + g * y

    return pl.pallas_call(
        body, name=name, grid=(N // tn, M // tm),
        in_specs=[pl.BlockSpec((tm, K), lambda j, i: (i, 0)),
                  pl.BlockSpec((K, tn), lambda j, i: (0, j)),
                  pl.BlockSpec((tm, tn), lambda j, i: (i, j)),
                  pl.BlockSpec((2, 6, tn), lambda j, i: (0, 0, j))],
        out_specs=[pl.BlockSpec((tm, tn), lambda j, i: (i, j))] * 2,
        out_shape=[jax.ShapeDtypeStruct((M, N), BF16), jax.ShapeDtypeStruct((M, N), F32)],
        compiler_params=_cp(2, VMEM_MATMUL),
    )(a, b2, x_old, mod)


def _mm_nn_resid_norm(a, b2, x_old, mod, gate_row, norm_row0, n_ctx, *, tm, name):
    M, K = a.shape
    N = b2.shape[1]

    def body(a_ref, b_ref, x_ref, mod_ref, y_ref, xn_ref, h_ref, ht_ref):
        y = jnp.dot(a_ref[...], b_ref[...], preferred_element_type=F32)
        is_ctx = _rows(pl.program_id(0), tm) < n_ctx

        def row(k):
            return jnp.where(is_ctx, mod_ref[0, k:k + 1, :], mod_ref[1, k:k + 1, :])

        y_ref[...] = y.astype(BF16)
        xn = x_ref[...] + row(gate_row) * y
        xn_ref[...] = xn
        r = lax.rsqrt(jnp.mean(xn * xn, axis=-1, keepdims=True) + EPS)
        hv = xn * r * (1.0 + row(norm_row0 + 1)) + row(norm_row0)
        h_ref[...] = hv.astype(BF16)
        ht_ref[...] = hv.T.astype(BF16)

    rows = pl.BlockSpec((tm, N), lambda i: (i, 0))
    return pl.pallas_call(
        body, name=name, grid=(M // tm,),
        in_specs=[pl.BlockSpec((tm, K), lambda i: (i, 0)), pl.BlockSpec((K, N), lambda i: (0, 0)), rows,
                  pl.BlockSpec((2, 6, N), lambda i: (0, 0, 0))],
        out_specs=[rows, rows, rows, pl.BlockSpec((N, tm), lambda i: (0, i))],
        out_shape=[jax.ShapeDtypeStruct((M, N), BF16), jax.ShapeDtypeStruct((M, N), F32),
                   jax.ShapeDtypeStruct((M, N), BF16), jax.ShapeDtypeStruct((N, M), BF16)],
        compiler_params=_cp(1, VMEM_MATMUL),
    )(a, b2, x_old, mod)


def _norm_bwd_math(dh_v, xv, mod_ref, res, is_ctx, row0):
    D = xv.shape[1]
    r = lax.rsqrt(jnp.mean(xv * xv, axis=-1, keepdims=True) + EPS)
    xhat = xv * r
    sc = jnp.where(is_ctx, mod_ref[0, row0 + 1:row0 + 2, :], mod_ref[1, row0 + 1:row0 + 2, :])
    dxhat = dh_v * (1.0 + sc)
    dx = res + r * (dxhat - xhat * jnp.mean(dxhat * xhat, axis=-1, keepdims=True))
    dsc = dh_v * xhat
    zero = jnp.zeros_like(dh_v)
    sums = [jnp.sum(jnp.where(is_ctx, dh_v, zero), axis=0, keepdims=True),
            jnp.sum(jnp.where(is_ctx, dsc, zero), axis=0, keepdims=True),
            jnp.sum(jnp.where(is_ctx, zero, dh_v), axis=0, keepdims=True),
            jnp.sum(jnp.where(is_ctx, zero, dsc), axis=0, keepdims=True)]
    rid = lax.broadcasted_iota(I32, (8, D), 0)
    upd = jnp.zeros((8, D), F32)
    for k, s in enumerate(sums):
        upd = upd + jnp.where(rid == k, s, 0.0)
    return dx, upd


def _mm_nt_acc(parts, w3, *, tm, name, vmem=VMEM_MATMUL):
    n_parts = len(parts)
    M = parts[0].shape[0]
    nb, K, nc = w3.shape
    pair = _pairable(nb // n_parts, nc)
    g = 2 if pair else 1
    steps, per = nb // g, nb // n_parts // g
    cut = nc - 128

    def nt(d, w):
        return lax.dot_general(d, w, NT_DIMS, preferred_element_type=F32)

    def contribution(d_ref, w_ref):
        if not pair:
            return nt(d_ref[...], w_ref[0])
        mid = jnp.concatenate([w_ref[0, :, cut:nc], w_ref[1, :, 0:128]], axis=1)
        return (nt(d_ref[:, 0:cut], w_ref[0, :, 0:cut]) + nt(d_ref[:, cut:nc + 128], mid)
                + nt(d_ref[:, nc + 128:2 * nc], w_ref[1, :, 128:nc]))

    def body(*refs):
        dy_refs, w_ref = refs[:n_parts], refs[n_parts]
        o_ref, acc_ref = refs[n_parts + 1], refs[n_parts + 2]
        s = pl.program_id(1)

        @pl.when(s == 0)
        def _():
            acc_ref[...] = jnp.zeros_like(acc_ref)

        for p in range(n_parts):
            @pl.when(s // per == p)
            def _(p=p):
                acc_ref[...] += contribution(dy_refs[p], w_ref)

        @pl.when(s == steps - 1)
        def _():
            o_ref[...] = acc_ref[...].astype(o_ref.dtype)

    def part_spec(p):
        return pl.BlockSpec((tm, g * nc), lambda i, s: (i, jnp.clip(s - p * per, 0, per - 1)))

    return pl.pallas_call(
        body, name=name, grid=(M // tm, steps),
        in_specs=[part_spec(p) for p in range(n_parts)] + [pl.BlockSpec((g, K, nc), lambda i, s: (s, 0, 0))],
        out_specs=pl.BlockSpec((tm, K), lambda i, s: (i, 0)),
        out_shape=jax.ShapeDtypeStruct((M, K), BF16),
        scratch_shapes=[pltpu.VMEM((tm, K), F32)], compiler_params=_cp(2, vmem),
    )(*parts, w3)


def _norm_mod_bwd(dh, x, mod, dx_res, row0, n_ctx, name):
    S, D = x.shape
    tm = _pick(S, [256, 128])

    def body(dh_ref, x_ref, mod_ref, res_ref, dx_ref, acc_ref):
        i = pl.program_id(0)

        @pl.when(i == 0)
        def _():
            acc_ref[...] = jnp.zeros_like(acc_ref)

        dx, upd = _norm_bwd_math(dh_ref[...].astype(F32), x_ref[...], mod_ref, res_ref[...], _rows(i, tm) < n_ctx, row0)
        dx_ref[...] = dx
        acc_ref[...] += upd

    rows = pl.BlockSpec((tm, D), lambda i: (i, 0))
    return pl.pallas_call(
        body, name=name, grid=(S // tm,),
        in_specs=[rows, rows, pl.BlockSpec((2, 6, D), lambda i: (0, 0, 0)), rows],
        out_specs=[rows, pl.BlockSpec((8, D), lambda i: (0, 0))],
        out_shape=[jax.ShapeDtypeStruct((S, D), F32), jax.ShapeDtypeStruct((8, D), F32)],
        compiler_params=_cp(1),
    )(dh, x, mod, dx_res)


def _gate_dgrad(dx, y, mod, gate_row, n_ctx, w2, *, tm, tn, name):
    M, N = dx.shape
    K = w2.shape[0]

    def body(dx_ref, y_ref, mod_ref, w_ref, da_ref, dy_ref, acc_ref):
        i = pl.program_id(0)
        j = pl.program_id(1)

        @pl.when((i == 0) & (j == 0))
        def _():
            acc_ref[...] = jnp.zeros_like(acc_ref)

        @pl.when(j == 0)
        def _():
            dxv = dx_ref[...]
            is_ctx = _rows(i, tm) < n_ctx
            g = jnp.where(is_ctx, mod_ref[0, gate_row:gate_row + 1, :], mod_ref[1, gate_row:gate_row + 1, :])
            dy_ref[...] = (g * dxv).astype(BF16)
            prod = dxv * y_ref[...].astype(F32)
            zero = jnp.zeros_like(prod)
            s_ctx = jnp.sum(jnp.where(is_ctx, prod, zero), axis=0, keepdims=True)
            s_lat = jnp.sum(jnp.where(is_ctx, zero, prod), axis=0, keepdims=True)
            rid = lax.broadcasted_iota(I32, (8, N), 0)
            acc_ref[...] += jnp.where(rid == 0, s_ctx, 0.0) + jnp.where(rid == 1, s_lat, 0.0)

        da_ref[...] = lax.dot_general(dy_ref[...], w_ref[...], NT_DIMS, preferred_element_type=F32).astype(BF16)

    rows = pl.BlockSpec((tm, N), lambda i, j: (i, 0))
    return pl.pallas_call(
        body, name=name, grid=(M // tm, K // tn),
        in_specs=[rows, rows, pl.BlockSpec((2, 6, N), lambda i, j: (0, 0, 0)),
                  pl.BlockSpec((tn, N), lambda i, j: (j, 0))],
        out_specs=[pl.BlockSpec((tm, tn), lambda i, j: (i, j)), rows, pl.BlockSpec((8, N), lambda i, j: (0, 0))],
        out_shape=[jax.ShapeDtypeStruct((M, K), BF16), jax.ShapeDtypeStruct((M, N), BF16),
                   jax.ShapeDtypeStruct((8, N), F32)],
        compiler_params=_cp(2, VMEM_MATMUL),
    )(dx, y, mod, w2)


def _mm_tn(a, parts, *, nb, tka, tn, ts, name, a_transposed=False):
    n_parts = len(parts)
    S, Ka = a.shape[::-1] if a_transposed else a.shape
    N = sum(p.shape[1] for p in parts)
    nc = N // nb
    q = nc // tn
    nk = S // ts
    g = 2 if (tn == nc and n_parts == 1 and _pairable(nb, nc)) else 1
    per = nb * q // n_parts // g

    def body(a_ref, *rest):
        dy_refs, o_ref, acc_ref = rest[:n_parts], rest[n_parts], rest[n_parts + 1]
        j = pl.program_id(0)
        k = pl.program_id(2)

        @pl.when(k == 0)
        def _():
            acc_ref[...] = jnp.zeros_like(acc_ref)

        for p in range(n_parts):
            @pl.when(j // per == p)
            def _(p=p):
                if a_transposed:
                    acc_ref[...] += jnp.dot(a_ref[...], dy_refs[p][...], preferred_element_type=F32)
                else:
                    acc_ref[...] += lax.dot_general(a_ref[...], dy_refs[p][...], TN_DIMS, preferred_element_type=F32)

        @pl.when(k == nk - 1)
        def _():
            for t in range(g):
                o_ref[t] = acc_ref[:, t * tn:(t + 1) * tn].astype(o_ref.dtype)

    def part_spec(p):
        return pl.BlockSpec((ts, g * tn), lambda j, ia, k: (jnp.where(j // per == p, k, 0),
                                                            jnp.clip(j - p * per, 0, per - 1)))

    return pl.pallas_call(
        body, name=name, grid=(nb * q // g, Ka // tka, nk),
        in_specs=[pl.BlockSpec((tka, ts), lambda j, ia, k: (ia, k)) if a_transposed
                  else pl.BlockSpec((ts, tka), lambda j, ia, k: (k, ia))] + [part_spec(p) for p in range(n_parts)],
        out_specs=pl.BlockSpec((g, tka, tn), lambda j, ia, k: (j // q, ia, j % q)),
        out_shape=jax.ShapeDtypeStruct((nb, Ka, nc), BF16),
        scratch_shapes=[pltpu.VMEM((tka, g * tn), F32)], compiler_params=_cp(3, VMEM_MATMUL),
    )(a, *parts)


def _norm_mod(x, mod, row0, n_ctx, name):
    S, D = x.shape
    tm = _pick(S, [256, 128])

    def body(x_ref, mod_ref, h_ref, ht_ref):
        xv = x_ref[...]
        r = lax.rsqrt(jnp.mean(xv * xv, axis=-1, keepdims=True) + EPS)
        is_ctx = _rows(pl.program_id(0), tm) < n_ctx
        sh = jnp.where(is_ctx, mod_ref[0, row0:row0 + 1, :], mod_ref[1, row0:row0 + 1, :])
        sc = jnp.where(is_ctx, mod_ref[0, row0 + 1:row0 + 2, :], mod_ref[1, row0 + 1:row0 + 2, :])
        hv = xv * r * (1.0 + sc) + sh
        h_ref[...] = hv.astype(BF16)
        ht_ref[...] = hv.T.astype(BF16)

    return pl.pallas_call(
        body, name=name, grid=(S // tm,),
        in_specs=[pl.BlockSpec((tm, D), lambda i: (i, 0)), pl.BlockSpec((2, 6, D), lambda i: (0, 0, 0))],
        out_specs=[pl.BlockSpec((tm, D), lambda i: (i, 0)), pl.BlockSpec((D, tm), lambda i: (0, i))],
        out_shape=[jax.ShapeDtypeStruct((S, D), BF16), jax.ShapeDtypeStruct((D, S), BF16)], compiler_params=_cp(1),
    )(x, mod)


def _loss_grad(x, target, n_ctx, name):
    S, D = x.shape
    tm = _pick(n_ctx, [256, 128])
    nct = n_ctx // tm

    def body(x_ref, t_ref, dx_ref, tot_ref, acc_ref):
        i = pl.program_id(0)

        @pl.when(i == 0)
        def _():
            acc_ref[...] = jnp.zeros_like(acc_ref)

        @pl.when(i < nct)
        def _():
            dx_ref[...] = jnp.zeros_like(dx_ref)

        @pl.when(i >= nct)
        def _():
            err = x_ref[...] - t_ref[...]
            dx_ref[...] = err * (1.0 / D)
            acc_ref[...] += jnp.sum(err * err, axis=0, keepdims=True)

        @pl.when(i == S // tm - 1)
        def _():
            tot = jnp.sum(acc_ref[...], axis=1, keepdims=True) * (0.5 / D)
            tot_ref[...] = jnp.broadcast_to(tot, tot_ref.shape)

    return pl.pallas_call(
        body, name=name, grid=(S // tm,),
        in_specs=[pl.BlockSpec((tm, D), lambda i: (i, 0)),
                  pl.BlockSpec((tm, D), lambda i: (jnp.maximum(i - nct, 0), 0))],
        out_specs=[pl.BlockSpec((tm, D), lambda i: (i, 0)), pl.BlockSpec((1, 128), lambda i: (0, 0))],
        out_shape=[jax.ShapeDtypeStruct((S, D), F32), jax.ShapeDtypeStruct((1, 128), F32)],
        scratch_shapes=[pltpu.VMEM((1, D), F32)], compiler_params=_cp(1),
    )(x, target)


def _halo_specs(tm, tc, S, col_off):
    per = tm // HALO
    last = S // HALO - 1
    return [pl.BlockSpec((HALO, tc), lambda j, i: (jnp.maximum(i * per - 1, 0), j + col_off)),
            pl.BlockSpec((tm, tc), lambda j, i: (i, j + col_off)),
            pl.BlockSpec((HALO, tc), lambda j, i: (jnp.minimum((i + 1) * per, last), j + col_off))]


def _ext(p_ref, m_ref, n_ref):
    return jnp.concatenate([p_ref[...], m_ref[...], n_ref[...]], axis=0).astype(F32)


def _links(i, tm, S, n_ctx):
    n = tm + 2 * HALO
    rid = i * tm - HALO + lax.broadcasted_iota(I32, (n, 1), 0)
    has_prev = (rid != 0) & (rid != n_ctx)
    has_next = (rid != n_ctx - 1) & (rid != S - 1)
    return has_prev, has_next


def _edge_tile(i, tm, S, n_ctx):
    tiles = sorted({t for b in (0, n_ctx - 1, n_ctx, S - 1) for t in range(S // tm)
                    if t * tm - HALO <= b < (t + 1) * tm + HALO})
    edge = i == tiles[0]
    for t in tiles[1:]:
        edge = edge | (i == t)
    return edge


def _up(x, keep=None):
    y = pltpu.roll(x, 1, 0)
    return y if keep is None else jnp.where(keep, y, 0.0)


def _dn(x, keep=None):
    y = pltpu.roll(x, x.shape[0] - 1, 0)
    return y if keep is None else jnp.where(keep, y, 0.0)


def _conv3(x, w_ref, has_prev, has_next):
    return w_ref[0:1, :] * _up(x, has_prev) + w_ref[1:2, :] * x + w_ref[2:3, :] * _dn(x, has_next)


def _conv3_t(d, w_ref, has_prev, has_next):
    return w_ref[0:1, :] * _dn(d, has_next) + w_ref[1:2, :] * d + w_ref[2:3, :] * _up(d, has_prev)


def _conv3_wgrad(d, x, has_prev, has_next, extra=None):
    c = slice(HALO, d.shape[0] - HALO)
    taps = [_up(x, has_prev), x, _dn(x, has_next)]
    sums = [jnp.sum((d * t)[c], axis=0, keepdims=True) for t in taps]
    if extra is not None:
        sums.append(jnp.sum(extra[c], axis=0, keepdims=True))
    rid = lax.broadcasted_iota(I32, (8, d.shape[1]), 0)
    upd = jnp.zeros((8, d.shape[1]), F32)
    for k, s in enumerate(sums):
        upd = upd + jnp.where(rid == k, s, 0.0)
    return upd


def _sigmoid(x):
    return 1.0 / (1.0 + jnp.exp(-x))


def _ffn_act(u, conv_w, conv_b, l, n_ctx, name):
    S, F2 = u.shape
    F = F2 // 2
    tm = _pick(S, [384, 256, 128])
    tc = _pick(F, [1408, 512, 256, 128])
    nj = F // tc

    def body(gp, gm, gn, v_ref, w_ref, b_ref, a_ref):
        i = pl.program_id(1)

        def compute(has_prev, has_next):
            gc = _conv3(_ext(gp, gm, gn), w_ref, has_prev, has_next)[HALO:HALO + tm] + b_ref[...]
            a_ref[...] = (gc * _sigmoid(gc) * v_ref[...].astype(F32)).astype(BF16)

        edge = _edge_tile(i, tm, S, n_ctx)
        pl.when(edge)(lambda: compute(*_links(i, tm, S, n_ctx)))
        pl.when(jnp.logical_not(edge))(lambda: compute(None, None))

    return pl.pallas_call(
        body, name=name, grid=(nj, S // tm),
        in_specs=_halo_specs(tm, tc, S, 0) + [
            pl.BlockSpec((tm, tc), lambda j, i: (i, j + nj)),
            pl.BlockSpec((None, 3, tc), lambda j, i: (l, 0, j)),
            pl.BlockSpec((None, 1, tc), lambda j, i: (l, 0, j))],
        out_specs=pl.BlockSpec((tm, tc), lambda j, i: (i, j)),
        out_shape=jax.ShapeDtypeStruct((S, F), BF16), compiler_params=_cp(2),
    )(u, u, u, u, conv_w, conv_b)


def _ffn_act_bwd(u, da, conv_w, conv_b, l, n_ctx, name):
    S, F2 = u.shape
    F = F2 // 2
    tm = _pick(S, [384, 256, 128])
    tc = _pick(F, [1408, 512, 256, 128])
    nj = F // tc

    def body(gp, gm, gn, vp, vm, vn, dp, dm, dn_, w_ref, b_ref, dg_ref, dv_ref, acc_ref):
        i = pl.program_id(1)

        @pl.when(i == 0)
        def _():
            acc_ref[...] = jnp.zeros_like(acc_ref)

        def compute(has_prev, has_next):
            g = _ext(gp, gm, gn)
            val = _ext(vp, vm, vn)
            d_a = _ext(dp, dm, dn_)
            gc = _conv3(g, w_ref, has_prev, has_next) + b_ref[...]
            sg = _sigmoid(gc)
            dgc = d_a * val * (sg * (1.0 + gc * (1.0 - sg)))
            c = slice(HALO, HALO + tm)
            dv_ref[...] = (d_a * gc * sg)[c].astype(BF16)
            dg_ref[...] = _conv3_t(dgc, w_ref, has_prev, has_next)[c].astype(BF16)
            acc_ref[...] += _conv3_wgrad(dgc, g, has_prev, has_next, extra=dgc)

        edge = _edge_tile(i, tm, S, n_ctx)
        pl.when(edge)(lambda: compute(*_links(i, tm, S, n_ctx)))
        pl.when(jnp.logical_not(edge))(lambda: compute(None, None))

    return pl.pallas_call(
        body, name=name, grid=(nj, S // tm),
        in_specs=_halo_specs(tm, tc, S, 0) + _halo_specs(tm, tc, S, nj) + _halo_specs(tm, tc, S, 0) + [
            pl.BlockSpec((None, 3, tc), lambda j, i: (l, 0, j)),
            pl.BlockSpec((None, 1, tc), lambda j, i: (l, 0, j))],
        out_specs=[pl.BlockSpec((tm, tc), lambda j, i: (i, j))] * 2 + [pl.BlockSpec((8, tc), lambda j, i: (0, j))],
        out_shape=[jax.ShapeDtypeStruct((S, F), BF16)] * 2 + [jax.ShapeDtypeStruct((8, F), F32)],
        compiler_params=_cp(2),
    )(u, u, u, u, u, u, da, da, da, conv_w, conv_b)


def _sc_act(u, conv_w, l, n_ctx, name):
    S, D3 = u.shape
    D = D3 // 3
    tm = _pick(S, [384, 256, 128])
    tc = _pick(D, [1024, 512, 256, 128])
    nj = D // tc

    def body(b_ref, cp, cm, cn, vp, vm, vn, w_ref, z_ref):
        i = pl.program_id(1)

        def compute(has_prev, has_next):
            t = _ext(cp, cm, cn) * _ext(vp, vm, vn)
            cv = _conv3(t, w_ref, has_prev, has_next)[HALO:HALO + tm]
            z_ref[...] = (b_ref[...].astype(F32) * cv).astype(BF16)

        edge = _edge_tile(i, tm, S, n_ctx)
        pl.when(edge)(lambda: compute(*_links(i, tm, S, n_ctx)))
        pl.when(jnp.logical_not(edge))(lambda: compute(None, None))

    return pl.pallas_call(
        body, name=name, grid=(nj, S // tm),
        in_specs=[pl.BlockSpec((tm, tc), lambda j, i: (i, j))] + _halo_specs(tm, tc, S, nj)
        + _halo_specs(tm, tc, S, 2 * nj) + [pl.BlockSpec((None, 3, tc), lambda j, i: (l, 0, j))],
        out_specs=pl.BlockSpec((tm, tc), lambda j, i: (i, j)),
        out_shape=jax.ShapeDtypeStruct((S, D), BF16), compiler_params=_cp(2),
    )(u, u, u, u, u, u, u, conv_w)


def _sc_act_bwd(u, dz, conv_w, l, n_ctx, name):
    S, D3 = u.shape
    D = D3 // 3
    tm = _pick(S, [128])
    tc = D
    nj = 1

    def body(bp, bm, bn, cp, cm, cn, vp, vm, vn, zp, zm, zn, w_ref, du_ref, acc_ref):
        db_ref, dc_ref, dv_ref = du_ref.at[:, 0:D], du_ref.at[:, D:2 * D], du_ref.at[:, 2 * D:3 * D]
        i = pl.program_id(1)

        @pl.when(i == 0)
        def _():
            acc_ref[...] = jnp.zeros_like(acc_ref)

        def compute(has_prev, has_next):
            gb = _ext(bp, bm, bn)
            gcv = _ext(cp, cm, cn)
            val = _ext(vp, vm, vn)
            d_z = _ext(zp, zm, zn)
            t = gcv * val
            c = slice(HALO, HALO + tm)
            db_ref[...] = (d_z * _conv3(t, w_ref, has_prev, has_next))[c].astype(BF16)
            dcv = d_z * gb
            dt = _conv3_t(dcv, w_ref, has_prev, has_next)
            dc_ref[...] = (dt * val)[c].astype(BF16)
            dv_ref[...] = (dt * gcv)[c].astype(BF16)
            acc_ref[...] += _conv3_wgrad(dcv, t, has_prev, has_next)

        edge = _edge_tile(i, tm, S, n_ctx)
        pl.when(edge)(lambda: compute(*_links(i, tm, S, n_ctx)))
        pl.when(jnp.logical_not(edge))(lambda: compute(None, None))

    return pl.pallas_call(
        body, name=name, grid=(nj, S // tm),
        in_specs=_halo_specs(tm, tc, S, 0) + _halo_specs(tm, tc, S, nj) + _halo_specs(tm, tc, S, 2 * nj)
        + _halo_specs(tm, tc, S, 0) + [pl.BlockSpec((None, 3, tc), lambda j, i: (l, 0, j))],
        out_specs=[pl.BlockSpec((tm, D3), lambda j, i: (i, 0)), pl.BlockSpec((8, tc), lambda j, i: (0, j))],
        out_shape=[jax.ShapeDtypeStruct((S, D3), BF16), jax.ShapeDtypeStruct((8, D), F32)],
        compiler_params=_cp(2),
    )(u, u, u, u, u, u, u, u, u, dz, dz, dz, conv_w)


def _rope_tables(T, n_ctx):
    rows = T // GRID_W
    pairs = HEAD_DIM // 4
    row = jnp.repeat(jnp.arange(rows), GRID_W).astype(F32)
    col = jnp.tile(jnp.arange(GRID_W), rows).astype(F32)
    inv = ROPE_BASE ** (-jnp.arange(pairs, dtype=F32) / pairs)
    ang = jnp.concatenate([row[:, None] * inv, row[:, None] * inv, col[:, None] * inv, col[:, None] * inv], axis=1)
    cos, sin = jnp.cos(ang), jnp.sin(ang)
    first = (jnp.arange(HEAD_DIM) % (2 * pairs)) < pairs
    sin_a = jnp.where(first, -sin, 0.0)
    sin_b = jnp.where(first, 0.0, sin)
    pad = jnp.zeros((n_ctx, HEAD_DIM), F32)
    return (jnp.concatenate([pad + 1.0, cos], axis=0), jnp.concatenate([pad, sin_a], axis=0),
            jnp.concatenate([pad, sin_b], axis=0))


def _qk_prep(qkv, tabs, gains, l, n_q, n_kv, name):
    S = qkv.shape[0]
    W = qkv.shape[1]
    tm = _pick(S, [256, 128])
    cos, sin_a, sin_b = tabs

    def body(x_ref, cos_ref, sa_ref, sb_ref, g_ref, q_ref, k_ref):
        cs, sa, sb = cos_ref[...], sa_ref[...], sb_ref[...]
        for h in range(n_q + n_kv):
            xv = x_ref[:, h * 128:(h + 1) * 128].astype(F32)
            r = lax.rsqrt(jnp.mean(xv * xv, axis=-1, keepdims=True) + EPS)
            gain = g_ref[0:1, :] if h < n_q else g_ref[1:2, :]
            y = xv * r * gain
            out = (y * cs + pltpu.roll(y, 96, 1) * sa + pltpu.roll(y, 32, 1) * sb).astype(BF16)
            if h < n_q:
                q_ref[:, h * 128:(h + 1) * 128] = out
            else:
                k_ref[:, (h - n_q) * 128:(h - n_q + 1) * 128] = out

    tspec = pl.BlockSpec((tm, 128), lambda i: (i, 0))
    return pl.pallas_call(
        body, name=name, grid=(S // tm,),
        in_specs=[pl.BlockSpec((tm, W), lambda i: (i, 0)), tspec, tspec, tspec,
                  pl.BlockSpec((None, 2, 128), lambda i: (l, 0, 0))],
        out_specs=[pl.BlockSpec((tm, n_q * 128), lambda i: (i, 0)), pl.BlockSpec((tm, n_kv * 128), lambda i: (i, 0))],
        out_shape=[jax.ShapeDtypeStruct((S, n_q * 128), BF16), jax.ShapeDtypeStruct((S, n_kv * 128), BF16)],
        compiler_params=_cp(1),
    )(qkv, cos, sin_a, sin_b, gains)


def _qk_prep_bwd(dq, dk, dv, qkv, tabs, gains, l, n_q, n_kv, name):
    S, W = qkv.shape
    tm = _pick(S, [256, 128])
    cos, sin_a, sin_b = tabs

    def body(dq_ref, dk_ref, dv_ref, x_ref, cos_ref, sa_ref, sb_ref, g_ref, o_ref, acc_ref):
        @pl.when(pl.program_id(0) == 0)
        def _():
            acc_ref[...] = jnp.zeros_like(acc_ref)

        cs, sa, sb = cos_ref[...], sa_ref[...], sb_ref[...]
        dgq = jnp.zeros((1, 128), F32)
        dgk = jnp.zeros((1, 128), F32)
        for h in range(n_q + n_kv):
            if h < n_q:
                d_out = dq_ref[:, h * 128:(h + 1) * 128]
                gain = g_ref[0:1, :]
            else:
                d_out = dk_ref[:, (h - n_q) * 128:(h - n_q + 1) * 128]
                gain = g_ref[1:2, :]
            dy = d_out * cs + pltpu.roll(d_out * sa, 32, 1) + pltpu.roll(d_out * sb, 96, 1)
            xv = x_ref[:, h * 128:(h + 1) * 128].astype(F32)
            r = lax.rsqrt(jnp.mean(xv * xv, axis=-1, keepdims=True) + EPS)
            xhat = xv * r
            dg = jnp.sum(dy * xhat, axis=0, keepdims=True)
            if h < n_q:
                dgq = dgq + dg
            else:
                dgk = dgk + dg
            dxhat = dy * gain
            dx = r * (dxhat - xhat * jnp.mean(dxhat * xhat, axis=-1, keepdims=True))
            o_ref[:, h * 128:(h + 1) * 128] = dx.astype(BF16)
        v0 = (n_q + n_kv) * 128
        o_ref[:, v0:] = dv_ref[...].astype(BF16)
        rid = lax.broadcasted_iota(I32, (8, 128), 0)
        acc_ref[...] += jnp.where(rid == 0, dgq, 0.0) + jnp.where(rid == 1, dgk, 0.0)

    tspec = pl.BlockSpec((tm, 128), lambda i: (i, 0))
    return pl.pallas_call(
        body, name=name, grid=(S // tm,),
        in_specs=[pl.BlockSpec((tm, n_q * 128), lambda i: (i, 0)), pl.BlockSpec((tm, n_kv * 128), lambda i: (i, 0)),
                  pl.BlockSpec((tm, n_kv * 128), lambda i: (i, 0)), pl.BlockSpec((tm, W), lambda i: (i, 0)),
                  tspec, tspec, tspec, pl.BlockSpec((None, 2, 128), lambda i: (l, 0, 0))],
        out_specs=[pl.BlockSpec((tm, W), lambda i: (i, 0)), pl.BlockSpec((8, 128), lambda i: (0, 0))],
        out_shape=[jax.ShapeDtypeStruct((S, W), BF16), jax.ShapeDtypeStruct((8, 128), F32)],
        compiler_params=_cp(1),
    )(dq, dk, dv, qkv, cos, sin_a, sin_b, gains)


def _band_specs(width, col, nb, n_ctx):
    return [pl.BlockSpec((BLK, width), lambda i: (jnp.maximum(i - 1, 0), col)),
            pl.BlockSpec((BLK, width), lambda i: (i, col)),
            pl.BlockSpec((BLK, width), lambda i: (jnp.minimum(i + 1, nb - 1), col)),
            pl.BlockSpec((n_ctx, width), lambda i: (0, col))]


def _q_side_mask(i, S, n_ctx):
    shape = (GROUP * BLK, 3 * BLK + n_ctx)
    a = lax.broadcasted_iota(I32, shape, 0) & (BLK - 1)
    kk = lax.broadcasted_iota(I32, shape, 1)
    rq = i * BLK + a
    rk = (i - 1) * BLK + kk
    band = (rq >= n_ctx) & (rk >= n_ctx) & (rk < S) & (jnp.abs(rq - rk) <= WINDOW)
    return (kk >= 3 * BLK) | band


def _stack_heads(ref, g):
    return jnp.concatenate([ref[:, (GROUP * g + hh) * 128:(GROUP * g + hh + 1) * 128] for hh in range(GROUP)], axis=0)


def _stack_cols(ref, g):
    return jnp.concatenate([ref[:, GROUP * g + hh:GROUP * g + hh + 1] for hh in range(GROUP)], axis=0)


def _sink_col(sink_ref, l, g):
    return jnp.concatenate([jnp.full((BLK, 1), sink_ref[l, GROUP * g + hh], F32) for hh in range(GROUP)], axis=0)


def _attn_fwd(q, k, qkv, sink, l, n_ctx, name):
    S, DQ = q.shape
    DK = k.shape[1]
    n_kv = DK // 128
    nb = S // BLK
    vcol = (DQ + DK) // DK
    scale = HEAD_DIM ** -0.5

    def body(sink_ref, q_ref, kp, kc, kn, kx, vp, vc, vn, vx, o_ref, lse_ref):
        i = pl.program_id(0)
        mask = _q_side_mask(i, S, n_ctx)
        lane = lax.broadcasted_iota(I32, (BLK, 128), 1)
        lse_tile = jnp.zeros((BLK, 128), F32)
        outs = []
        for g in range(n_kv):
            sl = slice(g * 128, (g + 1) * 128)
            kcat = jnp.concatenate([kp[:, sl], kc[:, sl], kn[:, sl], kx[:, sl]], axis=0)
            vcat = jnp.concatenate([vp[:, sl], vc[:, sl], vn[:, sl], vx[:, sl]], axis=0)
            s = lax.dot_general(_stack_heads(q_ref, g), kcat, NT_DIMS, preferred_element_type=F32) * scale
            s = jnp.where(mask, s, NEG)
            sk = _sink_col(sink_ref, l, g)
            m = jnp.maximum(jnp.max(s, axis=1, keepdims=True), sk)
            e = jnp.exp(s - m)
            den = jnp.sum(e, axis=1, keepdims=True) + jnp.exp(sk - m)
            o = jnp.dot(e.astype(BF16), vcat, preferred_element_type=F32) * (1.0 / den)
            lse = m + jnp.log(den)
            for hh in range(GROUP):
                h = GROUP * g + hh
                outs.append(o[hh * BLK:(hh + 1) * BLK].astype(BF16))
                lse_tile = jnp.where(lane == h, lse[hh * BLK:(hh + 1) * BLK], lse_tile)
        o_ref[...] = jnp.concatenate(outs, axis=1)
        lse_ref[...] = lse_tile

    return pl.pallas_call(
        body, name=name, grid=(nb,),
        in_specs=[pl.BlockSpec(memory_space=pltpu.SMEM), pl.BlockSpec((BLK, DQ), lambda i: (i, 0))]
        + _band_specs(DK, 0, nb, n_ctx) + _band_specs(DK, vcol, nb, n_ctx),
        out_specs=[pl.BlockSpec((BLK, DQ), lambda i: (i, 0)), pl.BlockSpec((BLK, 128), lambda i: (i, 0))],
        out_shape=[jax.ShapeDtypeStruct((S, DQ), BF16), jax.ShapeDtypeStruct((S, 128), F32)],
        compiler_params=_cp(1),
    )(sink, q, k, k, k, k, qkv, qkv, qkv, qkv)


def _attn_bwd_q(q, k, qkv, o, do, lse, sink, l, n_ctx, name):
    S, DQ = q.shape
    DK = k.shape[1]
    n_kv = DK // 128
    nb = S // BLK
    vcol = (DQ + DK) // DK
    scale = HEAD_DIM ** -0.5

    def body(sink_ref, q_ref, kp, kc, kn, kx, vp, vc, vn, vx, o_ref, do_ref, lse_ref,
             dq_ref, delta_ref, dkx_ref, dvx_ref, dsink_ref):
        i = pl.program_id(0)

        @pl.when(i == 0)
        def _():
            dkx_ref[...] = jnp.zeros_like(dkx_ref)
            dvx_ref[...] = jnp.zeros_like(dvx_ref)
            dsink_ref[...] = jnp.zeros_like(dsink_ref)

        mask = _q_side_mask(i, S, n_ctx)
        lane = lax.broadcasted_iota(I32, (BLK, 128), 1)
        lane8 = lax.broadcasted_iota(I32, (8, 128), 1)
        row8 = lax.broadcasted_iota(I32, (8, 128), 0)
        delta_tile = jnp.zeros((BLK, 128), F32)
        dsink_upd = jnp.zeros((8, 128), F32)
        dqs, dkx_upd, dvx_upd = [], [], []
        for g in range(n_kv):
            sl = slice(g * 128, (g + 1) * 128)
            kcat = jnp.concatenate([kp[:, sl], kc[:, sl], kn[:, sl], kx[:, sl]], axis=0)
            vcat = jnp.concatenate([vp[:, sl], vc[:, sl], vn[:, sl], vx[:, sl]], axis=0)
            qg = _stack_heads(q_ref, g)
            dog = _stack_heads(do_ref, g)
            delta = jnp.sum(dog.astype(F32) * _stack_heads(o_ref, g).astype(F32), axis=1, keepdims=True)
            lse_g = _stack_cols(lse_ref, g)
            s = lax.dot_general(qg, kcat, NT_DIMS, preferred_element_type=F32) * scale
            p = jnp.exp(jnp.where(mask, s - lse_g, NEG))
            dp = lax.dot_general(dog, vcat, NT_DIMS, preferred_element_type=F32)
            ds = (p * (dp - delta) * scale).astype(BF16)
            dqg = jnp.dot(ds, kcat, preferred_element_type=F32)
            dkx_upd.append(lax.dot_general(ds[:, 3 * BLK:], qg, TN_DIMS, preferred_element_type=F32))
            dvx_upd.append(lax.dot_general(p.astype(BF16)[:, 3 * BLK:], dog, TN_DIMS, preferred_element_type=F32))
            dsk = -jnp.exp(_sink_col(sink_ref, l, g) - lse_g) * delta
            for hh in range(GROUP):
                h = GROUP * g + hh
                rs = slice(hh * BLK, (hh + 1) * BLK)
                dqs.append(dqg[rs])
                delta_tile = jnp.where(lane == h, delta[rs], delta_tile)
                tot = jnp.sum(dsk[rs], axis=0, keepdims=True)
                dsink_upd = dsink_upd + jnp.where((lane8 == h) & (row8 == 0), tot, 0.0)
        dq_ref[...] = jnp.concatenate(dqs, axis=1)
        dkx_ref[...] += jnp.concatenate(dkx_upd, axis=1)
        dvx_ref[...] += jnp.concatenate(dvx_upd, axis=1)
        delta_ref[...] = delta_tile
        dsink_ref[...] += dsink_upd

    blk = pl.BlockSpec((BLK, DQ), lambda i: (i, 0))
    stat = pl.BlockSpec((BLK, 128), lambda i: (i, 0))
    return pl.pallas_call(
        body, name=name, grid=(nb,),
        in_specs=[pl.BlockSpec(memory_space=pltpu.SMEM), blk] + _band_specs(DK, 0, nb, n_ctx)
        + _band_specs(DK, vcol, nb, n_ctx) + [blk, blk, stat],
        out_specs=[blk, stat, pl.BlockSpec((n_ctx, DK), lambda i: (0, 0)), pl.BlockSpec((n_ctx, DK), lambda i: (0, 0)),
                   pl.BlockSpec((8, 128), lambda i: (0, 0))],
        out_shape=[jax.ShapeDtypeStruct((S, DQ), F32), jax.ShapeDtypeStruct((S, 128), F32),
                   jax.ShapeDtypeStruct((n_ctx, DK), F32), jax.ShapeDtypeStruct((n_ctx, DK), F32),
                   jax.ShapeDtypeStruct((8, 128), F32)],
        compiler_params=_cp(1),
    )(sink, q, k, k, k, k, qkv, qkv, qkv, qkv, o, do, lse)


def _attn_bwd_kv(q, k, qkv, do, lse, delta, dkx, dvx, n_ctx, name):
    S, DQ = q.shape
    DK = k.shape[1]
    n_kv = DK // 128
    nb = S // BLK
    nctx_b = n_ctx // BLK
    vcol = (DQ + DK) // DK
    scale = HEAD_DIM ** -0.5

    def three(width):
        return [pl.BlockSpec((BLK, width), lambda j: (jnp.maximum(j - 1, 0), 0)),
                pl.BlockSpec((BLK, width), lambda j: (j, 0)),
                pl.BlockSpec((BLK, width), lambda j: (jnp.minimum(j + 1, nb - 1), 0))]

    def body(k_ref, v_ref, qp, qc, qn, dop, doc, don, lp, lc, ln, dlp, dlc, dln, dkx_ref, dvx_ref, dk_ref, dv_ref):
        j = pl.program_id(0)

        @pl.when(j < nctx_b)
        def _():
            dk_ref[...] = dkx_ref[...]
            dv_ref[...] = dvx_ref[...]

        @pl.when(j >= nctx_b)
        def _():
            shape = (3 * GROUP * BLK, BLK)
            t = lax.broadcasted_iota(I32, shape, 0)
            rq = (j - 1 + t // (GROUP * BLK)) * BLK + (t & (BLK - 1))
            rk = j * BLK + lax.broadcasted_iota(I32, shape, 1)
            valid = (rq >= n_ctx) & (rq < S) & (jnp.abs(rq - rk) <= WINDOW)
            dks, dvs = [], []
            for g in range(n_kv):
                sl = slice(g * 128, (g + 1) * 128)
                qcat = jnp.concatenate([_stack_heads(r, g) for r in (qp, qc, qn)], axis=0)
                docat = jnp.concatenate([_stack_heads(r, g) for r in (dop, doc, don)], axis=0)
                lse_c = jnp.concatenate([_stack_cols(r, g) for r in (lp, lc, ln)], axis=0)
                delta_c = jnp.concatenate([_stack_cols(r, g) for r in (dlp, dlc, dln)], axis=0)
                s = lax.dot_general(qcat, k_ref[:, sl], NT_DIMS, preferred_element_type=F32) * scale
                p = jnp.exp(jnp.where(valid, s - lse_c, NEG))
                dp = lax.dot_general(docat, v_ref[:, sl], NT_DIMS, preferred_element_type=F32)
                ds = (p * (dp - delta_c) * scale).astype(BF16)
                dks.append(lax.dot_general(ds, qcat, TN_DIMS, preferred_element_type=F32))
                dvs.append(lax.dot_general(p.astype(BF16), docat, TN_DIMS, preferred_element_type=F32))
            dk_ref[...] = jnp.concatenate(dks, axis=1)
            dv_ref[...] = jnp.concatenate(dvs, axis=1)

    cspec = pl.BlockSpec((BLK, DK), lambda j: (jnp.minimum(j, nctx_b - 1), 0))
    return pl.pallas_call(
        body, name=name, grid=(nb,),
        in_specs=[pl.BlockSpec((BLK, DK), lambda j: (j, 0)), pl.BlockSpec((BLK, DK), lambda j: (j, vcol))]
        + three(DQ) + three(DQ) + three(128) + three(128) + [cspec, cspec],
        out_specs=[pl.BlockSpec((BLK, DK), lambda j: (j, 0))] * 2,
        out_shape=[jax.ShapeDtypeStruct((S, DK), F32)] * 2, compiler_params=_cp(1),
    )(k, qkv, q, q, q, do, do, do, lse, lse, lse, delta, delta, delta, dkx, dvx)


def _ada_fwd(cond, w_ada, b_cols, name):
    lyr, D, C = w_ada.shape
    tc = _pick(C, [512, 384, 256, 128])

    def body(c_ref, w_ref, b_ref, o_ref):
        cv = c_ref[...]
        act = cv * _sigmoid(cv)
        o_ref[...] = jnp.dot(act, w_ref[...], preferred_element_type=F32,
                             precision=lax.Precision.HIGHEST) + b_ref[...]

    return pl.pallas_call(
        body, name=name, grid=(lyr, C // tc),
        in_specs=[pl.BlockSpec((16, D), lambda l, j: (0, 0)),
                  pl.BlockSpec((None, D, tc), lambda l, j: (l, 0, j)),
                  pl.BlockSpec((None, 1, tc), lambda l, j: (l, 0, j))],
        out_specs=pl.BlockSpec((None, 16, tc), lambda l, j: (l, 0, j)),
        out_shape=jax.ShapeDtypeStruct((lyr, 16, C), F32), compiler_params=_cp(2),
    )(cond, w_ada, b_cols)


def _ada_bwd(cond, d_out, w_ada, name):
    lyr, D, C = w_ada.shape
    tc = _pick(C, [512, 384, 256, 128])

    def body(c_ref, d_ref, w_ref, gw_ref, dc_ref):
        @pl.when((pl.program_id(0) == 0) & (pl.program_id(1) == 0))
        def _():
            dc_ref[...] = jnp.zeros_like(dc_ref)

        cv = c_ref[...]
        act = cv * _sigmoid(cv)
        dv = d_ref[...]
        gw_ref[...] = lax.dot_general(act, dv, TN_DIMS, preferred_element_type=F32, precision=lax.Precision.HIGHEST)
        dc_ref[...] += lax.dot_general(dv, w_ref[...], NT_DIMS, preferred_element_type=F32,
                                       precision=lax.Precision.HIGHEST)

    return pl.pallas_call(
        body, name=name, grid=(lyr, C // tc),
        in_specs=[pl.BlockSpec((16, D), lambda l, j: (0, 0)),
                  pl.BlockSpec((None, 16, tc), lambda l, j: (l, 0, j)),
                  pl.BlockSpec((None, D, tc), lambda l, j: (l, 0, j))],
        out_specs=[pl.BlockSpec((None, D, tc), lambda l, j: (l, 0, j)), pl.BlockSpec((16, D), lambda l, j: (0, 0))],
        out_shape=[jax.ShapeDtypeStruct((lyr, D, C), F32), jax.ShapeDtypeStruct((16, D), F32)],
        compiler_params=_cp(2),
    )(cond, d_out, w_ada)


def _sum_rows(d_rows, name):
    lyr, r, C = d_rows.shape

    def body(d_ref, o_ref):
        o_ref[...] = jnp.sum(d_ref[...], axis=0, keepdims=True)

    return pl.pallas_call(
        body, name=name, grid=(lyr,),
        in_specs=[pl.BlockSpec((None, r, C), lambda l: (l, 0, 0))],
        out_specs=pl.BlockSpec((None, 1, C), lambda l: (l, 0, 0)),
        out_shape=jax.ShapeDtypeStruct((lyr, 1, C), F32), compiler_params=_cp(1),
    )(d_rows)


def _cctx_grad(gathered, c_ctx_row, name):
    D = gathered.shape[1]

    def body(g_ref, c_ref, o_ref):
        acc = g_ref[0:16, :]
        for d in range(1, N_DEV):
            acc = acc + g_ref[16 * d:16 * (d + 1), :]
        cv = c_ref[...]
        sg = _sigmoid(cv)
        o_ref[...] = acc[8:16] * (sg * (1.0 + cv * (1.0 - sg)))

    return pl.pallas_call(
        body, name=name, out_shape=jax.ShapeDtypeStruct((8, D), F32),
        compiler_params=pltpu.CompilerParams(vmem_limit_bytes=VMEM_ELEMENTWISE),
    )(gathered, c_ctx_row)


def _pad_rows(a, rows):
    return jnp.concatenate([a, jnp.zeros((rows - a.shape[0],) + a.shape[1:], a.dtype)], axis=0)


def kernel(x, c, ctx, c_ctx, w_ada, b_ada, attn_w_qkv, attn_w_o, attn_q_gain, attn_k_gain, attn_sink, sc_w_in, sc_conv, sc_w_out, ffn_w_up, ffn_conv, ffn_conv_b, ffn_w_down, loss_target, m_c_ctx, m_w_ada, m_b_ada, m_attn_w_qkv, m_attn_w_o, m_attn_q_gain, m_attn_k_gain, m_attn_sink, m_sc_w_in, m_sc_conv, m_sc_w_out, m_ffn_w_up, m_ffn_conv, m_ffn_conv_b, m_ffn_w_down, v_c_ctx, v_w_ada, v_b_ada, v_attn_w_qkv, v_attn_w_o, v_attn_q_gain, v_attn_k_gain, v_attn_sink, v_sc_w_in, v_sc_conv, v_sc_w_out, v_ffn_w_up, v_ffn_conv, v_ffn_conv_b, v_ffn_w_down):
    T, D = x.shape[1], x.shape[2]
    L = ctx.shape[1]
    S = L + T
    depth = w_ada.shape[0]
    F = ffn_conv_b.shape[1]
    n_q = D // HEAD_DIM
    n_kv = n_q // GROUP
    ada_c = w_ada.shape[2]
    assert L % BLK == 0 and T % BLK == 0 and ada_c * N_DEV == 6 * D

    px, py, pc = _my_pos()
    me = 4 * px + 2 * py + pc
    me_idx = jnp.reshape(me, (1,)).astype(I32)

    tm_mm = _pick(S, [768, 704, 384, 256, 128])
    ts_tn = _pick(S, [2112, 1056, 768, 384, 256, 128])
    tm_half = _pick(S, [384, 256, 128])
    ts_lane = _pick(S, [2816, 768, 384, 256, 128])

    c_all = _gather_small(_pad_rows(c, 8), "gather_cond")
    cond = jnp.concatenate([c_all[0::8], c_ctx[None, :], jnp.zeros((7, D), F32)], axis=0)
    b_cols = lax.dynamic_slice_in_dim(b_ada, me * ada_c, ada_c, axis=1)[:, None, :]
    ada_mine = _ada_fwd(cond, w_ada, b_cols, "ada_fwd")
    ada_all = _gather_small(ada_mine.reshape(depth * 16, ada_c), "gather_ada")
    ada_all = ada_all.reshape(N_DEV, depth, 16, ada_c)
    ada_rows = jnp.transpose(ada_all, (1, 2, 0, 3)).reshape(depth, 16, 6, D)
    mod_lat = lax.dynamic_index_in_dim(ada_rows, me, axis=1, keepdims=False)
    mods = jnp.stack([ada_rows[:, 8], mod_lat], axis=1)

    gathered = [None] * depth
    tabs = _rope_tables(T, L)
    gains = jnp.stack([attn_q_gain, attn_k_gain], axis=1)
    conv_b3 = ffn_conv_b[:, None, :]
    sc_conv_all = _gather_small(_pad_rows(sc_conv.reshape(-1, sc_conv.shape[2]), 8), "gather_scconv")
    ffn_conv_all = _gather_small(_pad_rows(ffn_conv.reshape(-1, ffn_conv.shape[2]), 16), "gather_ffnconv")
    n_sc = sc_conv.shape[0]
    sc_conv_full = jnp.transpose(sc_conv_all.reshape(N_DEV, 8, -1)[:, :n_sc * 3], (1, 0, 2)).reshape(n_sc, 3, D)
    ffn_conv_full = jnp.transpose(ffn_conv_all.reshape(N_DEV, 16, -1)[:, :depth * 3], (1, 0, 2)).reshape(depth, 3, F)

    def start_weights(l, tag, after):
        if tag == "ffn":
            ws = [(ffn_w_up, l), (ffn_w_down, l)]
        else:
            ws = [(attn_w_qkv, l // 2), (attn_w_o, l // 2)] if l % 2 == 0 else [(sc_w_in, l // 2), (sc_w_out, l // 2)]
        lands = [_cast_layer(w, j, me_idx, f"cast_{tag}{k}_{l}") for k, (w, j) in enumerate(ws)]
        return _gather_start(lands, after, f"gather_start_{tag}{l}")

    def wait_weights(flight, after, name):
        send_sems, recv_sems, lands, _ = flight
        return _gather_wait(lands, send_sems, recv_sems, after, name)

    flight_mix = start_weights(0, "mix", [mods, sc_conv_full, ffn_conv_full])
    mods = mods + flight_mix[3][0, 0]

    xs = jnp.concatenate([ctx[0], x[0]], axis=0)
    saved = []
    for l in range(depth):
        j = l // 2
        mod = mods[l]
        if l == 0:
            w_a, w_b = wait_weights(flight_mix, mods, "gather_wait_mix0")
            flight_ffn = start_weights(0, "ffn", [w_a])
            mod = mod + flight_ffn[3][0, 0]
            h, h_t = _norm_mod(xs, mod, 0, L, "norm_m0")
        else:
            h, h_t = _norm_mod(xs, mod, 0, L, f"norm_m{l}")
            w_a, w_b = wait_weights(flight_mix, h, f"gather_wait_mix{l}")
        if l % 2 == 0:
            qkv = _mm_nn(h, w_a, tm=tm_mm, tn=w_a.shape[2], out_dtype=BF16, name=f"qkv{l}")
            qr, kr = _qk_prep(qkv, tabs, gains, j, n_q, n_kv, f"qk_prep{l}")
            z, lse = _attn_fwd(qr, kr, qkv, attn_sink, j, L, f"attn{l}")
            mix = (qkv, qr, kr, lse)
        else:
            u = _mm_nn(h, w_a, tm=tm_mm, tn=w_a.shape[2], out_dtype=BF16, name=f"scin{l}")
            z = _sc_act(u, sc_conv_full, j, L, f"sc_act{l}")
            mix = (u,)
        if l + 1 < depth:
            flight_mix = start_weights(l + 1, "mix", [z])
            mod = mod + flight_mix[3][0, 0]
        y_m, x1, h2, h2_t = _mm_nn_resid_norm(z, w_b.reshape(D, D), xs, mod, 2, 3, L, tm=tm_half, name=f"mixout{l}")
        w_up, w_down = wait_weights(flight_ffn, x1, f"gather_wait_ffn{l}")
        gathered[l] = (w_a, w_b, w_up, w_down)
        u_f = _mm_nn(h2, w_up, tm=tm_mm, tn=w_up.shape[2], out_dtype=BF16, name=f"up{l}")
        a_f = _ffn_act(u_f, ffn_conv_full, conv_b3, l, L, f"ffn_act{l}")
        if l + 1 < depth:
            flight_ffn = start_weights(l + 1, "ffn", [a_f])
            mod = mod + flight_ffn[3][0, 0]
        y_f, x2 = _mm_nn_resid(a_f, w_down.reshape(F, D), x1, mod, 5, L, tm=tm_mm, tn=_pick(D, [512]), name=f"down{l}")
        saved.append((xs, h_t, mix, z, y_m, x1, h2_t, u_f, a_f, y_f))
        xs = x2

    dx, sq = _loss_grad(xs, loss_target[0], L, "loss")
    loss = lax.psum(sq[0, 0], ("x", "y", "c"))

    dmods = [None] * depth
    g_conv_b, g_ffn_conv, g_sc_conv = [None] * depth, [None] * depth, [None] * n_sc
    g_gain, g_sink = [None] * (depth - n_sc), [None] * (depth - n_sc)
    rs_flight = [None] * depth
    sent = jnp.zeros((), F32)
    for l in reversed(range(depth)):
        j = l // 2
        w_a, w_b, w_up, w_down = gathered[l]
        mod = mods[l] + sent
        x0, h_t, mix, z, y_m, x1, h2_t, u_f, a_f, y_f = saved[l]
        da, dy, s_gf = _gate_dgrad(dx, y_f, mod, 5, L, w_down.reshape(F, D), tm=tm_mm, tn=_pick(F, [1408, 512]),
                                   name=f"down_dgrad{l}")
        gw_down = _mm_tn(a_f, [dy], nb=1, tka=_pick(F, [1408, 512]), tn=_pick(D, [1024, 512]), ts=ts_tn, name=f"down_wgrad{l}")
        dgate, dval, s_conv = _ffn_act_bwd(u_f, da, ffn_conv_full, conv_b3, l, L, f"ffn_act_bwd{l}")
        dh2 = _mm_nt_acc([dgate, dval], w_up, tm=tm_mm, name=f"up_dgrad{l}", vmem=VMEM_NEAR_FULL)
        dx1, s_nf = _norm_mod_bwd(dh2, x1, mod, dx, 3, L, f"norm_f_bwd{l}")
        gw_up = _mm_tn(h2_t, [dgate, dval], nb=N_DEV, tka=_pick(D, [512]), tn=w_up.shape[2], ts=ts_lane,
                       name=f"up_wgrad{l}", a_transposed=True)
        g_ffn_conv[l], g_conv_b[l] = s_conv[0:3], s_conv[3]
        rs_ffn = _rs_start([gw_up, gw_down.reshape(N_DEV, -1, D)], f"rs_start_ffn{l}")
        mod = mods[l] + rs_ffn[4][0, 0]
        dz, dy, s_gm = _gate_dgrad(dx1, y_m, mod, 2, L, w_b.reshape(D, D), tm=tm_mm, tn=_pick(D, [1024, 512]),
                                   name=f"mixout_dgrad{l}")
        gw_b = _mm_tn(z, [dy], nb=1, tka=_pick(D, [1024, 512]), tn=_pick(D, [1024, 512]), ts=ts_tn, name=f"mixout_wgrad{l}")
        if l % 2 == 0:
            qkv, qr, kr, lse = mix
            dq, delta, dkx, dvx, s_sink = _attn_bwd_q(qr, kr, qkv, z, dz, lse, attn_sink, j, L, f"attn_bwd_q{l}")
            dk, dv = _attn_bwd_kv(qr, kr, qkv, dz, lse, delta, dkx, dvx, L, f"attn_bwd_kv{l}")
            du_m, s_gain = _qk_prep_bwd(dq, dk, dv, qkv, tabs, gains, j, n_q, n_kv, f"qk_prep_bwd{l}")
            g_gain[j], g_sink[j] = s_gain[0:2], s_sink[0]
        else:
            (u,) = mix
            du_m, s_scconv = _sc_act_bwd(u, dz, sc_conv_full, j, L, f"sc_act_bwd{l}")
            g_sc_conv[j] = s_scconv[0:3]
        dh = _mm_nt_acc([du_m], w_a, tm=tm_mm, name=f"mixin_dgrad{l}")
        gw_a = _mm_tn(h_t, [du_m], nb=N_DEV, tka=_pick(D, [1024, 512]), tn=w_a.shape[2], ts=ts_lane,
                      name=f"mixin_wgrad{l}", a_transposed=True)
        dx, s_nm = _norm_mod_bwd(dh, x0, mod, dx1, 0, L, f"norm_m_bwd{l}")
        dmods[l] = jnp.stack([jnp.stack([s_nm[2 * k], s_nm[2 * k + 1], s_gm[k], s_nf[2 * k], s_nf[2 * k + 1], s_gf[k]])
                              for k in range(2)])
        rs_mix = _rs_start([gw_a, gw_b.reshape(N_DEV, -1, D)], f"rs_start_mix{l}")
        sent = rs_mix[4][0, 0]
        rs_flight[l] = (rs_mix, rs_ffn)

    grad_x = dx[L:][None]

    big_w = {"qkv": (attn_w_qkv, m_attn_w_qkv, v_attn_w_qkv), "wo": (attn_w_o, m_attn_w_o, v_attn_w_o),
             "scin": (sc_w_in, m_sc_w_in, v_sc_w_in), "scout": (sc_w_out, m_sc_w_out, v_sc_w_out),
             "up": (ffn_w_up, m_ffn_w_up, v_ffn_w_up), "down": (ffn_w_down, m_ffn_w_down, v_ffn_w_down)}
    big_res = {k: None for k in big_w}
    for l in reversed(range(depth)):
        j = l // 2
        groups = [(["qkv", "wo"] if l % 2 == 0 else ["scin", "scout"], [j, j]), (["up", "down"], [l, l])]
        for (names, idxs), flight, tag in reversed(list(zip(groups, rs_flight[l], ("mix", "ffn")))):
            send_sems, recv_sems, own, zones, _ = flight
            own, zones = _rs_wait(own, zones, send_sems, recv_sems, dx, f"rs_wait_{tag}{l}")
            for n, li, p, z in zip(names, idxs, own, zones):
                w, m, v = big_w[n]
                big_res[n] = _adamw_reduced(p, z, me_idx, w, m, v, li, big_res[n], f"adamw_{n}{l}")

    n_attn = depth - n_sc
    pack = [jnp.stack(dmods)[:, 0].reshape(-1, 128), jnp.stack(dmods)[:, 1].reshape(-1, 128),
            jnp.stack(g_gain).reshape(-1, 128), jnp.stack(g_sink),
            jnp.stack(g_conv_b).reshape(-1, 128), jnp.stack(g_ffn_conv).reshape(-1, 128),
            jnp.stack(g_sc_conv).reshape(-1, 128)]
    used = [p.shape[0] for p in pack]
    pack = [_pad_rows(p, -(-p.shape[0] // 8) * 8) for p in pack]
    sizes = [p.shape[0] for p in pack]
    flat = jnp.concatenate(pack, axis=0)
    rows = flat.shape[0]
    small_all = _gather_small(flat, "gather_small_grads")
    small_sum = _sum8(small_all, rows, "sum_small_grads")
    offs = [sum(sizes[:k]) for k in range(len(sizes))]
    seg = lambda a, k: a[offs[k]:offs[k] + used[k]]
    dmod_ctx = seg(small_sum, 0).reshape(depth, 6 * D)
    dmod_lat = small_all.reshape(N_DEV, rows, 128)[:, offs[1]:offs[1] + used[1]].reshape(N_DEV, depth, 6 * D)
    g_gain_sum = seg(small_sum, 2).reshape(n_attn, 2, 128)
    g_sink_sum = seg(small_sum, 3)[:n_attn, :n_q]
    g_conv_b_sum = seg(small_sum, 4).reshape(depth, F)
    g_ffn_conv_sum = seg(small_sum, 5).reshape(depth, 3, F)
    g_sc_conv_sum = seg(small_sum, 6).reshape(n_sc, 3, D)

    d_rows = jnp.concatenate([jnp.transpose(dmod_lat, (1, 0, 2)), dmod_ctx[:, None, :],
                              jnp.zeros((depth, 7, 6 * D), F32)], axis=1)
    d_cols = lax.dynamic_slice_in_dim(d_rows, me * ada_c, ada_c, axis=2)
    g_w_ada, dcond_part = _ada_bwd(cond, d_cols, w_ada, "ada_bwd")
    dcond_all = _gather_small(dcond_part, "gather_dcond")
    g_c_ctx = _cctx_grad(dcond_all, jnp.broadcast_to(c_ctx[None, :], (8, D)), "cctx_grad")[0]
    g_b_ada = _sum_rows(d_rows, "b_ada_grad")[:, 0]

    def small_adam(w, g, m, v, name):
        w2 = w.reshape(-1, w.shape[-1])
        d, m2, v2 = _adamw_plain(w2, g.reshape(w2.shape), m.reshape(w2.shape), v.reshape(w2.shape), name)
        return g.reshape(w.shape), d.reshape(w.shape), m2.reshape(w.shape), v2.reshape(w.shape)

    g_sc_conv_mine = lax.dynamic_slice_in_dim(g_sc_conv_sum, me * sc_conv.shape[2], sc_conv.shape[2], axis=2)
    g_ffn_conv_mine = lax.dynamic_slice_in_dim(g_ffn_conv_sum, me * ffn_conv.shape[2], ffn_conv.shape[2], axis=2)
    res = {
        "c_ctx": small_adam(c_ctx[None, :], g_c_ctx[None, :], m_c_ctx[None, :], v_c_ctx[None, :], "adamw_c_ctx"),
        "b_ada": small_adam(b_ada, g_b_ada, m_b_ada, v_b_ada, "adamw_b_ada"),
        "attn_q_gain": small_adam(attn_q_gain, g_gain_sum[:, 0], m_attn_q_gain, v_attn_q_gain, "adamw_q_gain"),
        "attn_k_gain": small_adam(attn_k_gain, g_gain_sum[:, 1], m_attn_k_gain, v_attn_k_gain, "adamw_k_gain"),
        "attn_sink": small_adam(attn_sink, g_sink_sum, m_attn_sink, v_attn_sink, "adamw_sink"),
        "sc_conv": small_adam(sc_conv, g_sc_conv_mine, m_sc_conv, v_sc_conv, "adamw_sc_conv"),
        "ffn_conv": small_adam(ffn_conv, g_ffn_conv_mine, m_ffn_conv, v_ffn_conv, "adamw_ffn_conv"),
        "ffn_conv_b": small_adam(ffn_conv_b, g_conv_b_sum, m_ffn_conv_b, v_ffn_conv_b, "adamw_conv_b"),
    }
    res["c_ctx"] = tuple(t[0] for t in res["c_ctx"])
    res["w_ada"] = (g_w_ada,) + tuple(_adamw_tiled(w_ada, g_w_ada, m_w_ada, v_w_ada, "adamw_w_ada"))
    res["attn_w_qkv"], res["attn_w_o"] = big_res["qkv"], big_res["wo"]
    res["sc_w_in"], res["sc_w_out"] = big_res["scin"], big_res["scout"]
    res["ffn_w_up"], res["ffn_w_down"] = big_res["up"], big_res["down"]

    order = ["c_ctx", "w_ada", "b_ada", "attn_w_qkv", "attn_w_o", "attn_q_gain", "attn_k_gain", "attn_sink",
             "sc_w_in", "sc_conv", "sc_w_out", "ffn_w_up", "ffn_conv", "ffn_conv_b", "ffn_w_down"]
    outs = [loss, grad_x]
    for t in range(4):
        outs += [res[n][t] for n in order]
    return tuple(outs)
```
